```python
import jax, jax.numpy as jnp
from jax import lax
import numpy as np

D_MODEL = 1024
BATCH = 8
SEQ = 4096
DEPTH = 1

CHUNK = 64
N_MEM = 256
NORM_EPS = 1e-6
SSD_WIDTH = D_MODEL
SSD_HEAD_DIM = 64
SSD_HEADS = SSD_WIDTH // SSD_HEAD_DIM
SSD_GROUPS = 4
SSD_STATE = 128
SSD_CONV = 4
SSD_CONV_DIM = SSD_WIDTH + 2 * SSD_GROUPS * SSD_STATE
LRU_WIDTH = D_MODEL
LRU_BLOCKS = 16
LRU_BLOCK_W = LRU_WIDTH // LRU_BLOCKS
LRU_CONV = 4
RG_C = 8.0
MIX_WIDTH = SSD_WIDTH + LRU_WIDTH
IN_PROJ = SSD_WIDTH + SSD_CONV_DIM + SSD_HEADS + 2 * LRU_WIDTH
X_HEADS = 4
X_HEAD_DIM = D_MODEL // X_HEADS
N_EXPERTS = 32
TOP_K = 4
D_FF = D_MODEL
SWIGLU_LIMIT = 7.0
SWIGLU_ALPHA = 1.702
MOE_BLOCK = 128

kernel_name = 'hybrid_ssd_rglru_memxattn_moe'


def rmsnorm(x, g):
    xf = x.astype(jnp.float32)
    y = xf * lax.rsqrt(jnp.mean(xf * xf, axis=-1, keepdims=True) + NORM_EPS)
    return (y * g.astype(jnp.float32)).astype(x.dtype)


def gated_rmsnorm(y, z, g):
    b, s, w = y.shape
    u = (y * jax.nn.silu(z)).astype(jnp.float32).reshape(b, s, SSD_GROUPS, w // SSD_GROUPS)
    u = u * lax.rsqrt(jnp.mean(u * u, axis=-1, keepdims=True) + NORM_EPS)
    return (u.reshape(b, s, w) * g.astype(jnp.float32)).astype(y.dtype)


def causal_dwconv(x, w, bias):
    k, c = w.shape
    y = lax.conv_general_dilated(x, w[:, None, :], window_strides=(1,), padding=[(k - 1, 0)],
                                 dimension_numbers=('NWC', 'WIO', 'NWC'), feature_group_count=c)
    return y + bias


def ssd_chunked(xh, dt, a, bm, cm):
    b, s, h, p = xh.shape
    g, n = bm.shape[2], bm.shape[3]
    k = h // g
    c = s // CHUNK
    xdt = (xh * dt[..., None]).reshape(b, c, CHUNK, g, k, p)
    a_cs = jnp.cumsum((dt * a).reshape(b, c, CHUNK, g, k), axis=2)
    bc = bm.reshape(b, c, CHUNK, g, n)
    cc = cm.reshape(b, c, CHUNK, g, n)
    causal = jnp.tril(jnp.ones((CHUNK, CHUNK), dtype=bool))[None, None, :, :, None, None]
    seg = a_cs[:, :, :, None] - a_cs[:, :, None, :]
    decay = jnp.exp(jnp.where(causal, seg, -jnp.inf))
    cb = jnp.einsum('bclgn,bcsgn->bclsg', cc, bc)
    y_diag = jnp.einsum('bclsgk,bcsgkp->bclgkp', cb[..., None] * decay, xdt)
    decay_end = jnp.exp(a_cs[:, :, -1:] - a_cs)
    states = jnp.einsum('bclgn,bclgkp->bcgkpn', bc, decay_end[..., None] * xdt)
    chunk_decay = jnp.exp(a_cs[:, :, -1])

    def step(hs, inp):
        dec, st = inp
        return dec[..., None, None] * hs + st, hs

    _, prev = lax.scan(step, jnp.zeros_like(states[:, 0]),
                       (jnp.moveaxis(chunk_decay, 1, 0), jnp.moveaxis(states, 1, 0)))
    prev = jnp.moveaxis(prev, 0, 1)
    y_off = jnp.einsum('bclgn,bcgkpn->bclgkp', cc, prev) * jnp.exp(a_cs)[..., None]
    return (y_diag + y_off).reshape(b, s, h, p)


def rg_lru(xr, wa, ba, wx, bx, lam):
    b, s, w = xr.shape
    nb, bw, _ = wa.shape
    xb = xr.reshape(b, s, nb, bw)
    r = jax.nn.sigmoid(jnp.einsum('bsni,nij->bsnj', xb, wa).reshape(b, s, w) + ba)
    i = jax.nn.sigmoid(jnp.einsum('bsni,nij->bsnj', xb, wx).reshape(b, s, w) + bx)
    log_a = -RG_C * r * jax.nn.softplus(-lam)
    a = jnp.exp(log_a)
    u = xr * i * jnp.sqrt(-jnp.expm1(2.0 * log_a))

    def combine(left, right):
        a1, b1 = left
        a2, b2 = right
        return a1 * a2, a2 * b1 + b2

    _, hs = lax.associative_scan(combine, (a, u), axis=1)
    return hs


def hybrid_mixer(hn, w_in, ssd_conv_w, ssd_conv_b, ssd_dt_bias, ssd_a_log, ssd_d, ssd_norm,
                 lru_conv_w, lru_conv_b, lru_wa, lru_ba, lru_wx, lru_bx, lru_lambda, w_out):
    b, s, _ = hn.shape
    proj = hn @ w_in
    o1 = SSD_WIDTH
    o2 = o1 + SSD_CONV_DIM
    o3 = o2 + SSD_HEADS
    o4 = o3 + LRU_WIDTH
    z, xbc, dt_raw, xr, gr = proj[..., :o1], proj[..., o1:o2], proj[..., o2:o3], proj[..., o3:o4], proj[..., o4:]
    xbc = jax.nn.silu(causal_dwconv(xbc, ssd_conv_w, ssd_conv_b)).astype(jnp.float32)
    nbc = SSD_GROUPS * SSD_STATE
    xs = xbc[..., :SSD_WIDTH].reshape(b, s, SSD_HEADS, SSD_HEAD_DIM)
    bm = xbc[..., SSD_WIDTH:SSD_WIDTH + nbc].reshape(b, s, SSD_GROUPS, SSD_STATE)
    cm = xbc[..., SSD_WIDTH + nbc:].reshape(b, s, SSD_GROUPS, SSD_STATE)
    dt = jax.nn.softplus(dt_raw.astype(jnp.float32) + ssd_dt_bias.astype(jnp.float32))
    a = -jnp.exp(ssd_a_log.astype(jnp.float32))
    y = ssd_chunked(xs, dt, a, bm, cm) + xs * ssd_d.astype(jnp.float32)[:, None]
    y_ssd = gated_rmsnorm(y.reshape(b, s, SSD_WIDTH).astype(hn.dtype), z, ssd_norm)
    xr = causal_dwconv(xr, lru_conv_w, lru_conv_b).astype(jnp.float32)
    hl = rg_lru(xr, lru_wa.astype(jnp.float32), lru_ba.astype(jnp.float32), lru_wx.astype(jnp.float32),
                lru_bx.astype(jnp.float32), lru_lambda.astype(jnp.float32)).astype(hn.dtype)
    y_lru = hl * jax.nn.gelu(gr)
    return jnp.concatenate([y_ssd, y_lru], axis=-1) @ w_out


def memory_cross_attention(hn, mn, w_q, w_kv, w_o):
    b, s, d = hn.shape
    m = mn.shape[1]
    q = (hn @ w_q).reshape(b, s, X_HEADS, X_HEAD_DIM)
    kv = mn @ w_kv
    k = kv[..., :d].reshape(b, m, X_HEADS, X_HEAD_DIM)
    v = kv[..., d:].reshape(b, m, X_HEADS, X_HEAD_DIM)
    scores = jnp.einsum('bshd,bmhd->bhsm', q, k).astype(jnp.float32) * (X_HEAD_DIM ** -0.5)
    probs = jax.nn.softmax(scores, axis=-1).astype(v.dtype)
    o = jnp.einsum('bhsm,bmhd->bshd', probs, v).reshape(b, s, d)
    return o @ w_o


def moe_ffn(hn, w_router, b_router, w_gate_up, b_gate_up, w_down, b_down):
    b, s, d = hn.shape
    t = b * s
    xt = hn.reshape(t, d)
    logits = (xt @ w_router + b_router).astype(jnp.float32)
    top_val, top_idx = lax.top_k(logits, TOP_K)
    gate = jax.nn.softmax(top_val, axis=-1)
    n_pairs = t * TOP_K
    flat_e = top_idx.reshape(-1).astype(jnp.int32)
    flat_tok = jnp.arange(n_pairs, dtype=jnp.int32) // TOP_K
    flat_gate = gate.reshape(-1)
    order = jnp.argsort(flat_e)
    se, stok, sg = flat_e[order], flat_tok[order], flat_gate[order]
    counts = jnp.bincount(flat_e, length=N_EXPERTS).astype(jnp.int32)
    pcounts = (counts + MOE_BLOCK - 1) // MOE_BLOCK * MOE_BLOCK
    start = jnp.cumsum(counts) - counts
    pend = jnp.cumsum(pcounts)
    pstart = pend - pcounts
    dest = pstart[se] + (jnp.arange(n_pairs, dtype=jnp.int32) - start[se])
    n_blocks = (n_pairs + N_EXPERTS * (MOE_BLOCK - 1) + MOE_BLOCK - 1) // MOE_BLOCK
    rows = n_blocks * MOE_BLOCK
    row_tok = jnp.full((rows,), t, dtype=jnp.int32).at[dest].set(stok)
    row_gate = jnp.zeros((rows,), jnp.float32).at[dest].set(sg)
    block_e = jnp.searchsorted(pend, jnp.arange(n_blocks, dtype=jnp.int32) * MOE_BLOCK, side='right')
    block_e = jnp.minimum(block_e, N_EXPERTS - 1).astype(jnp.int32)
    xpad = jnp.concatenate([xt, jnp.zeros((1, d), xt.dtype)], axis=0)
    xrows = xpad[row_tok].reshape(n_blocks, MOE_BLOCK, d)

    def expert_block(args):
        xb, e = args
        gu = xb @ w_gate_up[e] + b_gate_up[e]
        g_ = jnp.minimum(gu[..., ::2], SWIGLU_LIMIT)
        u_ = jnp.clip(gu[..., 1::2], -SWIGLU_LIMIT, SWIGLU_LIMIT)
        act = (u_ + 1.0) * (g_ * jax.nn.sigmoid(SWIGLU_ALPHA * g_))
        return act @ w_down[e] + b_down[e]

    yrows = lax.map(expert_block, (xrows, block_e)).reshape(rows, d)
    yrows = yrows * row_gate[:, None].astype(yrows.dtype)
    out = jax.ops.segment_sum(yrows, row_tok, num_segments=t + 1)[:t]
    return out.reshape(b, s, d)


def setup_inputs(seed: int = 0) -> dict:
    key = jax.random.key(seed)
    ks = iter(jax.random.split(key, 40))
    L = DEPTH

    def nrm(shape, scale):
        return jax.random.normal(next(ks), shape, jnp.float32) * scale

    def gain(shape):
        return 1.0 + nrm(shape, 0.02)

    def unif(shape, lo, hi):
        return jax.random.uniform(next(ks), shape, jnp.float32, lo, hi)

    x = nrm((BATCH, SEQ, D_MODEL), 1.0)
    mem = nrm((BATCH, N_MEM, D_MODEL), 1.0)
    norm_mix = gain((L, D_MODEL))
    w_in = nrm((L, D_MODEL, IN_PROJ), D_MODEL ** -0.5)
    ssd_conv_w = nrm((L, SSD_CONV, SSD_CONV_DIM), SSD_CONV ** -0.5)
    ssd_conv_b = nrm((L, SSD_CONV_DIM), 0.01)
    dt0 = jnp.exp(unif((L, SSD_HEADS), float(np.log(1e-3)), float(np.log(1e-1))))
    ssd_dt_bias = dt0 + jnp.log(-jnp.expm1(-dt0))
    ssd_a_log = jnp.log(unif((L, SSD_HEADS), 1.0, 16.0))
    ssd_d = gain((L, SSD_HEADS))
    ssd_norm = gain((L, SSD_WIDTH))
    lru_conv_w = nrm((L, LRU_CONV, LRU_WIDTH), LRU_CONV ** -0.5)
    lru_conv_b = nrm((L, LRU_WIDTH), 0.01)
    lru_wa = nrm((L, LRU_BLOCKS, LRU_BLOCK_W, LRU_BLOCK_W), LRU_BLOCK_W ** -0.5)
    lru_ba = nrm((L, LRU_WIDTH), 0.01)
    lru_wx = nrm((L, LRU_BLOCKS, LRU_BLOCK_W, LRU_BLOCK_W), LRU_BLOCK_W ** -0.5)
    lru_bx = nrm((L, LRU_WIDTH), 0.01)
    a0 = unif((L, LRU_WIDTH), 0.9, 0.999) ** (1.0 / RG_C)
    lru_lambda = jnp.log(a0) - jnp.log1p(-a0)
    w_out = nrm((L, MIX_WIDTH, D_MODEL), MIX_WIDTH ** -0.5)
    norm_xattn = gain((L, D_MODEL))
    norm_mem = gain((L, D_MODEL))
    w_q = nrm((L, D_MODEL, D_MODEL), D_MODEL ** -0.5)
    w_kv = nrm((L, D_MODEL, 2 * D_MODEL), D_MODEL ** -0.5)
    w_o = nrm((L, D_MODEL, D_MODEL), D_MODEL ** -0.5)
    norm_moe = gain((L, D_MODEL))
    w_router = nrm((L, D_MODEL, N_EXPERTS), D_MODEL ** -0.5)
    b_router = nrm((L, N_EXPERTS), 0.01)
    w_gate_up = nrm((L, N_EXPERTS, D_MODEL, 2 * D_FF), D_MODEL ** -0.5)
    b_gate_up = nrm((L, N_EXPERTS, 2 * D_FF), 0.01)
    w_down = nrm((L, N_EXPERTS, D_FF, D_MODEL), D_FF ** -0.5)
    b_down = nrm((L, N_EXPERTS, D_MODEL), 0.01)
    norm_final = gain((D_MODEL,))
    return {'x': x, 'mem': mem, 'norm_mix': norm_mix, 'w_in': w_in,
            'ssd_conv_w': ssd_conv_w, 'ssd_conv_b': ssd_conv_b, 'ssd_dt_bias': ssd_dt_bias,
            'ssd_a_log': ssd_a_log, 'ssd_d': ssd_d, 'ssd_norm': ssd_norm,
            'lru_conv_w': lru_conv_w, 'lru_conv_b': lru_conv_b, 'lru_wa': lru_wa, 'lru_ba': lru_ba,
            'lru_wx': lru_wx, 'lru_bx': lru_bx, 'lru_lambda': lru_lambda, 'w_out': w_out,
            'norm_xattn': norm_xattn, 'norm_mem': norm_mem, 'w_q': w_q, 'w_kv': w_kv, 'w_o': w_o,
            'norm_moe': norm_moe, 'w_router': w_router, 'b_router': b_router,
            'w_gate_up': w_gate_up, 'b_gate_up': b_gate_up, 'w_down': w_down, 'b_down': b_down,
            'norm_final': norm_final}


def reference(x, mem, norm_mix, w_in, ssd_conv_w, ssd_conv_b, ssd_dt_bias, ssd_a_log, ssd_d, ssd_norm,
              lru_conv_w, lru_conv_b, lru_wa, lru_ba, lru_wx, lru_bx, lru_lambda, w_out,
              norm_xattn, norm_mem, w_q, w_kv, w_o, norm_moe, w_router, b_router,
              w_gate_up, b_gate_up, w_down, b_down, norm_final):
    h = x
    for l in range(DEPTH):
        h = h + hybrid_mixer(rmsnorm(h, norm_mix[l]), w_in[l], ssd_conv_w[l], ssd_conv_b[l], ssd_dt_bias[l],
                             ssd_a_log[l], ssd_d[l], ssd_norm[l], lru_conv_w[l], lru_conv_b[l], lru_wa[l],
                             lru_ba[l], lru_wx[l], lru_bx[l], lru_lambda[l], w_out[l])
        h = h + memory_cross_attention(rmsnorm(h, norm_xattn[l]), rmsnorm(mem, norm_mem[l]),
                                       w_q[l], w_kv[l], w_o[l])
        h = h + moe_ffn(rmsnorm(h, norm_moe[l]), w_router[l], b_router[l], w_gate_up[l], b_gate_up[l],
                        w_down[l], b_down[l])
    return rmsnorm(h, norm_final)
```

```python
import functools

import jax
import jax.numpy as jnp
from jax import lax
from jax.experimental import pallas as pl
from jax.experimental.pallas import tpu as pltpu

F32 = jnp.float32
BF16 = jnp.bfloat16

NORM_EPS = 1e-6
LANES = 128
SUBLANES = 8
SSD_HEAD_DIM = 64
SSD_HEADS = 16
SSD_GROUPS = 4
SSD_STATE = 128
CONV_K = 4
LRU_BLOCKS = 16
RG_C = 8.0
X_HEADS = 4
N_EXPERTS = 32
TOP_K = 4
SWIGLU_LIMIT = 7.0
SWIGLU_ALPHA = 1.702

VMEM_LIMIT = 56 * 1024 * 1024

IN_TILE = 512
SSD_CHUNK = 256
LRU_TILE = 256
MID_TILE = 256
MOE_BLOCK = 256
COMB_TILE = 256


def _cparams(sem):
    return pltpu.CompilerParams(dimension_semantics=sem, vmem_limit_bytes=VMEM_LIMIT)


def _rms(x, g):
    ms = jnp.mean(x * x, axis=-1, keepdims=True)
    return x * lax.rsqrt(ms + NORM_EPS) * g


def _sigmoid(x):
    return 1.0 / (1.0 + jnp.exp(-x))


def _softplus(x):
    return jnp.maximum(x, 0.0) + jnp.log(1.0 + jnp.exp(-jnp.abs(x)))


def _split3(x):
    a = x.astype(BF16)
    r = x - a.astype(F32)
    b = r.astype(BF16)
    c = (r - b.astype(F32)).astype(BF16)
    return a, b, c


def _dot(a, b):
    return jnp.dot(a, b, preferred_element_type=F32)


def _dot_nt(a, b):
    return lax.dot_general(a, b, (((1,), (1,)), ((), ())), preferred_element_type=F32)


def _dot01_right(x, m01):
    a, b, c = _split3(x)
    return _dot(a, m01) + _dot(b, m01) + _dot(c, m01)


def _dot01_left(m01, x):
    a, b, c = _split3(x)
    return _dot(m01, a) + _dot(m01, b) + _dot(m01, c)


def _dot_hilo(x, w_hi, w_lo):
    xh = x.astype(BF16)
    xl = (x - xh.astype(F32)).astype(BF16)
    return _dot(xh, w_hi) + _dot(xl, w_hi) + _dot(xh, w_lo)


def _hilo(w):
    hi = w.astype(BF16)
    lo = (w - hi.astype(F32)).astype(BF16)
    return hi, lo


def _in_proj_kernel(x_ref, g_ref, wzx_ref, wdth_ref, wdtl_ref, wxr_ref, wgr_ref,
                    z_ref, xbc_ref, dt_ref, xr_ref, gr_ref):
    d = x_ref.shape[1]
    hn = _rms(x_ref[...], g_ref[...])
    hb = hn.astype(BF16)
    z_ref[...] = _dot(hb, wzx_ref[:, :d]).astype(BF16)
    xbc_ref[...] = _dot(hb, wzx_ref[:, d:]).astype(BF16)
    dt_ref[...] = _dot_hilo(hn, wdth_ref[...], wdtl_ref[...])
    xr_ref[...] = _dot(hb, wxr_ref[...]).astype(BF16)
    gr_ref[...] = _dot(hb, wgr_ref[...]).astype(BF16)


def _in_proj(x2, g, wzx, wdt_hi, wdt_lo, wxr, wgr):
    t, d = x2.shape
    tm = IN_TILE
    nzx = wzx.shape[1]
    const = lambda i: (0, 0)
    row = lambda i: (i, 0)
    return pl.pallas_call(
        _in_proj_kernel,
        grid=(t // tm,),
        in_specs=[
            pl.BlockSpec((tm, d), row),
            pl.BlockSpec((1, d), const),
            pl.BlockSpec((d, nzx), const),
            pl.BlockSpec((d, LANES), const),
            pl.BlockSpec((d, LANES), const),
            pl.BlockSpec((d, d), const),
            pl.BlockSpec((d, d), const),
        ],
        out_specs=[
            pl.BlockSpec((tm, d), row),
            pl.BlockSpec((tm, nzx - d), row),
            pl.BlockSpec((tm, LANES), row),
            pl.BlockSpec((tm, d), row),
            pl.BlockSpec((tm, d), row),
        ],
        out_shape=[
            jax.ShapeDtypeStruct((t, d), BF16),
            jax.ShapeDtypeStruct((t, nzx - d), BF16),
            jax.ShapeDtypeStruct((t, LANES), F32),
            jax.ShapeDtypeStruct((t, d), BF16),
            jax.ShapeDtypeStruct((t, d), BF16),
        ],
        compiler_params=_cparams(("arbitrary",)),
        name="in_proj",
    )(x2, g, wzx, wdt_hi, wdt_lo, wxr, wgr)


def _causal_conv(ext_ref, x_f32, w_ref, b_ref, first):
    n = x_f32.shape[0]
    pad = SUBLANES

    @pl.when(first)
    def _():
        ext_ref[0:pad, :] = jnp.zeros((pad, ext_ref.shape[1]), F32)

    ext_ref[pad:pad + n, :] = x_f32
    acc = b_ref[...] + w_ref[0:1, :] * ext_ref[pad - 3:pad - 3 + n, :]
    for j in range(1, CONV_K):
        acc = acc + w_ref[j:j + 1, :] * ext_ref[pad - 3 + j:pad - 3 + j + n, :]
    ext_ref[0:pad, :] = ext_ref[n:n + pad, :]
    return acc


def _ssd_kernel(xbc_ref, dt_ref, z_ref, cw_ref, cb_ref, dtb_ref, alog_ref, dskip_ref, gn_ref, e_ref,
                y_ref, ext_ref, st_ref):
    n = xbc_ref.shape[1]
    w = z_ref.shape[2]
    gw = w // SSD_GROUPS
    first = pl.program_id(1) == 0

    @pl.when(first)
    def _():
        st_ref[...] = jnp.zeros(st_ref.shape, F32)

    conv = _causal_conv(ext_ref, xbc_ref[0].astype(F32), cw_ref, cb_ref, first)
    xc = conv * _sigmoid(conv)
    xs = xc[:, :w]

    dt = _softplus(dt_ref[0] + dtb_ref[...])
    a = -jnp.exp(alog_ref[...])
    da = dt * a
    ri = lax.broadcasted_iota(jnp.int32, (n, n), 0)
    ci = lax.broadcasted_iota(jnp.int32, (n, n), 1)
    causal = ri >= ci
    tril = jnp.where(causal, 1.0, 0.0).astype(BF16)
    a_cs = _dot01_left(tril, da)
    a_cs_t = a_cs.T

    e01 = e_ref[...]
    dt_x = _dot01_right(dt, e01)
    acs_x = _dot01_right(a_cs, e01)
    last_x = acs_x[n - 1:n, :]
    xdt = xs * dt_x
    xdt_b = xdt.astype(BF16)
    xdt_end = (xdt * jnp.exp(last_x - acs_x)).astype(BF16)
    exp_acs = jnp.exp(acs_x)
    chunk_decay = jnp.exp(last_x)
    lane = lax.broadcasted_iota(jnp.int32, (n, gw), 1)

    for g in range(SSD_GROUPS):
        lo = g * gw
        bg = xc[:, w + g * SSD_STATE:w + (g + 1) * SSD_STATE].astype(BF16)
        cg = xc[:, w + (SSD_GROUPS + g) * SSD_STATE:w + (SSD_GROUPS + g + 1) * SSD_STATE].astype(BF16)
        cb = _dot_nt(cg, bg)
        prev = st_ref[g]
        acc = _dot(cg, prev.astype(BF16)) * exp_acs[:, lo:lo + gw]
        new = lax.dot_general(bg, xdt_end[:, lo:lo + gw], (((0,), (0,)), ((), ())),
                              preferred_element_type=F32)
        st_ref[g] = chunk_decay[:, lo:lo + gw] * prev + new
        xg = xdt_b[:, lo:lo + gw]
        for k in range(SSD_HEADS // SSD_GROUPS):
            h = g * (SSD_HEADS // SSD_GROUPS) + k
            seg = a_cs[:, h:h + 1] - a_cs_t[h:h + 1, :]
            dec = jnp.exp(jnp.where(causal, seg, -jnp.inf))
            m = (cb * dec).astype(BF16)
            in_head = (lane >= k * SSD_HEAD_DIM) & (lane < (k + 1) * SSD_HEAD_DIM)
            acc = acc + _dot(m, jnp.where(in_head, xg, jnp.zeros_like(xg)))
        yg = acc + xs[:, lo:lo + gw] * dskip_ref[:, lo:lo + gw]
        zg = z_ref[0, :, lo:lo + gw].astype(F32)
        u = yg * (zg * _sigmoid(zg))
        u = u * lax.rsqrt(jnp.mean(u * u, axis=-1, keepdims=True) + NORM_EPS)
        y_ref[0, :, lo:lo + gw] = (u * gn_ref[:, lo:lo + gw]).astype(BF16)


def _ssd(xbc, dt, z, cw, cb, dtb, alog, dskip, gn, e01):
    b, s, cdim = xbc.shape
    w = z.shape[2]
    n = SSD_CHUNK
    tile = lambda i, j: (i, j, 0)
    const = lambda i, j: (0, 0)
    return pl.pallas_call(
        _ssd_kernel,
        grid=(b, s // n),
        in_specs=[
            pl.BlockSpec((1, n, cdim), tile),
            pl.BlockSpec((1, n, LANES), tile),
            pl.BlockSpec((1, n, w), tile),
            pl.BlockSpec((CONV_K, cdim), const),
            pl.BlockSpec((1, cdim), const),
            pl.BlockSpec((1, LANES), const),
            pl.BlockSpec((1, LANES), const),
            pl.BlockSpec((1, w), const),
            pl.BlockSpec((1, w), const),
            pl.BlockSpec((LANES, w), const),
        ],
        out_specs=pl.BlockSpec((1, n, w), tile),
        out_shape=jax.ShapeDtypeStruct((b, s, w), BF16),
        scratch_shapes=[
            pltpu.VMEM((n + SUBLANES, cdim), F32),
            pltpu.VMEM((SSD_GROUPS, SSD_STATE, w // SSD_GROUPS), F32),
        ],
        compiler_params=_cparams(("arbitrary", "arbitrary")),
        name="ssd",
    )(xbc, dt, z, cw, cb, dtb, alog, dskip, gn, e01)


def _gelu_tanh(x):
    c = 0.7978845608028654
    return 0.5 * x * (1.0 + jnp.tanh(c * (x + 0.044715 * (x * x * x))))


def _lru_kernel(xr_ref, gr_ref, cw_ref, cb_ref, wa_ref, ba_ref, wx_ref, bx_ref, lam_ref,
                y_ref, ext_ref, car_ref, h_ref):
    n = xr_ref.shape[1]
    w = xr_ref.shape[2]
    first = pl.program_id(1) == 0

    @pl.when(first)
    def _():
        car_ref[...] = jnp.zeros(car_ref.shape, F32)

    xc = _causal_conv(ext_ref, xr_ref[0].astype(F32), cw_ref, cb_ref, first)
    xb = xc.astype(BF16)
    nq = wa_ref.shape[0]
    qw = w // nq
    r_parts, i_parts = [], []
    for q in range(nq):
        xq = xb[:, q * qw:(q + 1) * qw]
        r_parts.append(_dot(xq, wa_ref[q]))
        i_parts.append(_dot(xq, wx_ref[q]))
    r = _sigmoid(jnp.concatenate(r_parts, axis=1) + ba_ref[...])
    gi = _sigmoid(jnp.concatenate(i_parts, axis=1) + bx_ref[...])
    log_a = (-RG_C) * r * _softplus(-lam_ref[...])
    a = jnp.exp(log_a)
    u = xc * gi * jnp.sqrt(1.0 - a * a)

    row = lax.broadcasted_iota(jnp.int32, (n, w), 0) & (SUBLANES - 1)
    ap, bp = a, u
    for d in (1, 2, 4):
        a_s = pltpu.roll(ap, d, 0)
        b_s = pltpu.roll(bp, d, 0)
        m = row >= d
        bp = jnp.where(m, ap * b_s + bp, bp)
        ap = jnp.where(m, ap * a_s, ap)
    carry = car_ref[...]
    for g in range(n // SUBLANES):
        sl = slice(g * SUBLANES, (g + 1) * SUBLANES)
        hb = bp[sl] + ap[sl] * carry
        h_ref[sl, :] = hb
        carry = jnp.broadcast_to(hb[SUBLANES - 1:SUBLANES, :], (SUBLANES, w))
    car_ref[...] = carry
    y_ref[0] = (h_ref[...] * _gelu_tanh(gr_ref[0].astype(F32))).astype(BF16)


def _lru(xr, gr, cw, cb, wa, ba, wx, bx, lam):
    b, s, w = xr.shape
    n = LRU_TILE
    nq, qw, _ = wa.shape
    tile = lambda i, j: (i, j, 0)
    const = lambda i, j: (0, 0)
    const3 = lambda i, j: (0, 0, 0)
    return pl.pallas_call(
        _lru_kernel,
        grid=(b, s // n),
        in_specs=[
            pl.BlockSpec((1, n, w), tile),
            pl.BlockSpec((1, n, w), tile),
            pl.BlockSpec((CONV_K, w), const),
            pl.BlockSpec((1, w), const),
            pl.BlockSpec((nq, qw, qw), const3),
            pl.BlockSpec((1, w), const),
            pl.BlockSpec((nq, qw, qw), const3),
            pl.BlockSpec((1, w), const),
            pl.BlockSpec((1, w), const),
        ],
        out_specs=pl.BlockSpec((1, n, w), tile),
        out_shape=jax.ShapeDtypeStruct((b, s, w), BF16),
        scratch_shapes=[
            pltpu.VMEM((n + SUBLANES, w), F32),
            pltpu.VMEM((SUBLANES, w), F32),
            pltpu.VMEM((n, w), F32),
        ],
        compiler_params=_cparams(("arbitrary", "arbitrary")),
        name="lru",
    )(xr, gr, cw, cb, wa, ba, wx, bx, lam)


def _kv_kernel(m_ref, g_ref, w_ref, k_ref, v_ref):
    d = m_ref.shape[1]
    mn = _rms(m_ref[...], g_ref[...]).astype(BF16)
    k_ref[...] = _dot(mn, w_ref[:, :d]).astype(BF16)
    v_ref[...] = _dot(mn, w_ref[:, d:]).astype(BF16)


def _kv(mem2, g, wkv):
    t, d = mem2.shape
    tm = min(t, 512)
    row = lambda i: (i, 0)
    const = lambda i: (0, 0)
    return pl.pallas_call(
        _kv_kernel,
        grid=(t // tm,),
        in_specs=[pl.BlockSpec((tm, d), row), pl.BlockSpec((1, d), const), pl.BlockSpec((d, 2 * d), const)],
        out_specs=[pl.BlockSpec((tm, d), row), pl.BlockSpec((tm, d), row)],
        out_shape=[jax.ShapeDtypeStruct((t, d), BF16), jax.ShapeDtypeStruct((t, d), BF16)],
        compiler_params=_cparams(("arbitrary",)),
        name="kv",
    )(mem2, g, wkv)


def _mid_kernel(x_ref, ys_ref, yl_ref, k_ref, v_ref, wo1_ref, wo2_ref, gx_ref, wq_ref, wo_ref,
                gm_ref, wrh_ref, wrl_ref, br_ref,
                h_ref, hn_ref, idx_ref, rank_ref, gate_ref, cnt_ref, car_ref):
    tm, d = x_ref.shape
    hd = d // X_HEADS

    @pl.when(pl.program_id(0) == 0)
    def _():
        car_ref[...] = jnp.zeros(car_ref.shape, F32)

    h1 = x_ref[...] + _dot(ys_ref[...], wo1_ref[...]) + _dot(yl_ref[...], wo2_ref[...])

    q = _dot(_rms(h1, gx_ref[...]).astype(BF16), wq_ref[...]).astype(BF16)
    o_parts = []
    for hh in range(X_HEADS):
        sl = slice(hh * hd, (hh + 1) * hd)
        sc = _dot_nt(q[:, sl], k_ref[0, :, sl]) * (hd ** -0.5)
        sc = sc - jnp.max(sc, axis=-1, keepdims=True)
        p = jnp.exp(sc)
        p = p / jnp.sum(p, axis=-1, keepdims=True)
        o_parts.append(_dot(p.astype(BF16), v_ref[0, :, sl]))
    o = jnp.concatenate(o_parts, axis=1).astype(BF16)
    h2 = h1 + _dot(o, wo_ref[...])
    h_ref[...] = h2

    hn = _rms(h2, gm_ref[...])
    hn_ref[...] = hn
    logits = _dot_hilo(hn, wrh_ref[...], wrl_ref[...]) + br_ref[...]

    lane = lax.broadcasted_iota(jnp.int32, (tm, LANES), 1)
    picked = jnp.zeros((tm, LANES), F32)
    vals, idxs = [], []
    l = logits
    for _ in range(TOP_K):
        m = jnp.max(l, axis=-1, keepdims=True)
        idx = jnp.min(jnp.where(l == m, lane, LANES), axis=-1, keepdims=True)
        sel = lane == idx
        vals.append(m)
        idxs.append(idx)
        picked = jnp.where(sel, 1.0, picked)
        l = jnp.where(sel, -jnp.inf, l)
    ex = [jnp.exp(v - vals[0]) for v in vals]
    den = ex[0] + ex[1] + ex[2] + ex[3]

    ri = lax.broadcasted_iota(jnp.int32, (tm, tm), 0)
    ci = lax.broadcasted_iota(jnp.int32, (tm, tm), 1)
    strict = jnp.where(ri > ci, 1.0, 0.0).astype(BF16)
    before = _dot(strict, picked.astype(BF16)) + car_ref[0:1, :]
    idx_out = jnp.zeros((tm, LANES), jnp.int32)
    rank_out = jnp.zeros((tm, LANES), jnp.int32)
    gate_out = jnp.zeros((tm, LANES), F32)
    for k in range(TOP_K):
        rk = jnp.sum(jnp.where(lane == idxs[k], before, 0.0), axis=-1, keepdims=True)
        at_k = lane == k
        idx_out = jnp.where(at_k, idxs[k], idx_out)
        rank_out = jnp.where(at_k, rk.astype(jnp.int32), rank_out)
        gate_out = jnp.where(at_k, ex[k] / den, gate_out)
    idx_ref[...] = idx_out
    rank_ref[...] = rank_out
    gate_ref[...] = gate_out
    total = car_ref[...] + jnp.sum(picked, axis=0, keepdims=True)
    car_ref[...] = total
    cnt_ref[...] = total.astype(jnp.int32)


def _mid(x2, ys, yl, kk, vv, wo1, wo2, gx, wq, wo, gm, wr_hi, wr_lo, br, seq):
    t, d = x2.shape
    tm = MID_TILE
    m = kk.shape[1]
    per_b = seq // tm
    row = lambda i: (i, 0)
    const = lambda i: (0, 0)
    kvmap = lambda i: (i // per_b, 0, 0)
    wspec = pl.BlockSpec((d, d), const)
    vspec = pl.BlockSpec((1, d), const)
    return pl.pallas_call(
        _mid_kernel,
        grid=(t // tm,),
        in_specs=[
            pl.BlockSpec((tm, d), row), pl.BlockSpec((tm, d), row), pl.BlockSpec((tm, d), row),
            pl.BlockSpec((1, m, d), kvmap), pl.BlockSpec((1, m, d), kvmap),
            wspec, wspec, vspec, wspec, wspec, vspec,
            pl.BlockSpec((d, LANES), const), pl.BlockSpec((d, LANES), const), pl.BlockSpec((1, LANES), const),
        ],
        out_specs=[
            pl.BlockSpec((tm, d), row), pl.BlockSpec((tm, d), row),
            pl.BlockSpec((tm, LANES), row), pl.BlockSpec((tm, LANES), row), pl.BlockSpec((tm, LANES), row),
            pl.BlockSpec((SUBLANES, LANES), const),
        ],
        out_shape=[
            jax.ShapeDtypeStruct((t, d), F32), jax.ShapeDtypeStruct((t, d), F32),
            jax.ShapeDtypeStruct((t, LANES), jnp.int32), jax.ShapeDtypeStruct((t, LANES), jnp.int32),
            jax.ShapeDtypeStruct((t, LANES), F32),
            jax.ShapeDtypeStruct((SUBLANES, LANES), jnp.int32),
        ],
        scratch_shapes=[pltpu.VMEM((SUBLANES, LANES), F32)],
        compiler_params=_cparams(("arbitrary",)),
        name="mid",
    )(x2, ys, yl, kk, vv, wo1, wo2, gx, wq, wo, gm, wr_hi, wr_lo, br)


def _row_copy(src_hbm, row, dst_vmem, slot, r, sem):
    return pltpu.make_async_copy(src_hbm.at[pl.ds(row, 1), :], dst_vmem.at[slot, pl.ds(r, 1), :], sem.at[slot])


def _start_rows(src_hbm, idx_ref, dst_vmem, slot, sem, n):
    def body(r, c):
        _row_copy(src_hbm, idx_ref[0, 0, r], dst_vmem, slot, r, sem).start()
        return c

    lax.fori_loop(0, n, body, 0, unroll=8)


def _wait_rows(src_hbm, dst_vmem, slot, sem, n):
    pltpu.make_async_copy(src_hbm.at[pl.ds(0, n), :], dst_vmem.at[slot], sem.at[slot]).wait()


def _moe_kernel(be_ref, nvb_ref, tok_ref, tokn_ref, gate_ref, hn_hbm, wg_ref, bg_ref, wu_ref, bu_ref,
                wd_ref, bd_ref, y_ref, xbuf, sem):
    i = pl.program_id(0)
    bm = y_ref.shape[0]
    nvb = nvb_ref[0]
    slot = i % 2

    @pl.when(jnp.logical_and(i == 0, nvb > 0))
    def _():
        _start_rows(hn_hbm, tok_ref, xbuf, 0, sem, bm)

    @pl.when(i + 1 < nvb)
    def _():
        _start_rows(hn_hbm, tokn_ref, xbuf, 1 - slot, sem, bm)

    @pl.when(i < nvb)
    def _():
        _wait_rows(hn_hbm, xbuf, slot, sem, bm)
        xb = xbuf[slot].astype(BF16)
        g = _dot(xb, wg_ref[0]) + bg_ref[0]
        u = _dot(xb, wu_ref[0]) + bu_ref[0]
        g = jnp.minimum(g, SWIGLU_LIMIT)
        u = jnp.clip(u, -SWIGLU_LIMIT, SWIGLU_LIMIT)
        act = (u + 1.0) * (g * _sigmoid(SWIGLU_ALPHA * g))
        y = _dot(act.astype(BF16), wd_ref[0]) + bd_ref[0]
        y_ref[...] = y * gate_ref[...]

    @pl.when(i >= nvb)
    def _():
        y_ref[...] = jnp.zeros(y_ref.shape, F32)


def _moe(block_e, nvb, row_tok3, row_gate, hn, wg, bg, wu, bu, wd, bd):
    nb = row_tok3.shape[0]
    bm = MOE_BLOCK
    d = hn.shape[1]
    f = wg.shape[2]
    emap = lambda i, be, nv: (be[i], 0, 0)
    grid_spec = pltpu.PrefetchScalarGridSpec(
        num_scalar_prefetch=2,
        grid=(nb,),
        in_specs=[
            pl.BlockSpec((1, 1, bm), lambda i, be, nv: (i, 0, 0), memory_space=pltpu.SMEM),
            pl.BlockSpec((1, 1, bm), lambda i, be, nv: (jnp.minimum(i + 1, nb - 1), 0, 0),
                         memory_space=pltpu.SMEM),
            pl.BlockSpec((bm, 1), lambda i, be, nv: (i, 0)),
            pl.BlockSpec(memory_space=pl.ANY),
            pl.BlockSpec((1, d, f), emap), pl.BlockSpec((1, 1, f), emap),
            pl.BlockSpec((1, d, f), emap), pl.BlockSpec((1, 1, f), emap),
            pl.BlockSpec((1, f, d), emap), pl.BlockSpec((1, 1, d), emap),
        ],
        out_specs=pl.BlockSpec((bm, d), lambda i, be, nv: (i, 0)),
        scratch_shapes=[pltpu.VMEM((2, bm, d), F32), pltpu.SemaphoreType.DMA((2,))],
    )
    return pl.pallas_call(
        _moe_kernel,
        grid_spec=grid_spec,
        out_shape=jax.ShapeDtypeStruct((nb * bm, d), F32),
        compiler_params=_cparams(("arbitrary",)),
        name="moe",
    )(block_e, nvb, row_tok3, row_tok3, row_gate, hn, wg, bg, wu, bu, wd, bd)


def _comb_kernel(dst_ref, dstn_ref, h_ref, y_hbm, g_ref, o_ref, ybuf, sem):
    i = pl.program_id(0)
    n = pl.num_programs(0)
    tc = h_ref.shape[0]
    rows = TOP_K * tc
    slot = i % 2

    @pl.when(i == 0)
    def _():
        _start_rows(y_hbm, dst_ref, ybuf, 0, sem, rows)

    @pl.when(i + 1 < n)
    def _():
        _start_rows(y_hbm, dstn_ref, ybuf, 1 - slot, sem, rows)

    _wait_rows(y_hbm, ybuf, slot, sem, rows)
    acc = h_ref[...]
    for k in range(TOP_K):
        acc = acc + ybuf[slot, k * tc:(k + 1) * tc, :]
    o_ref[...] = _rms(acc, g_ref[...])


def _combine(dest3, h2, y, g):
    t, d = h2.shape
    tc = COMB_TILE
    nt = t // tc
    return pl.pallas_call(
        _comb_kernel,
        grid=(nt,),
        in_specs=[
            pl.BlockSpec((1, 1, TOP_K * tc), lambda i: (i, 0, 0), memory_space=pltpu.SMEM),
            pl.BlockSpec((1, 1, TOP_K * tc), lambda i: (jnp.minimum(i + 1, nt - 1), 0, 0),
                         memory_space=pltpu.SMEM),
            pl.BlockSpec((tc, d), lambda i: (i, 0)),
            pl.BlockSpec(memory_space=pl.ANY),
            pl.BlockSpec((1, d), lambda i: (0, 0)),
        ],
        out_specs=pl.BlockSpec((tc, d), lambda i: (i, 0)),
        out_shape=jax.ShapeDtypeStruct((t, d), F32),
        scratch_shapes=[pltpu.VMEM((2, TOP_K * tc, d), F32), pltpu.SemaphoreType.DMA((2,))],
        compiler_params=_cparams(("arbitrary",)),
        name="combine",
    )(dest3, dest3, h2, y, g)


def _block_diag(wb, per):
    nb, bw, _ = wb.shape
    wq = wb.reshape(nb // per, per, bw, bw)
    eye = jnp.eye(per, dtype=wb.dtype)
    out = jnp.einsum('qaij,ab->qaibj', wq, eye)
    return out.reshape(nb // per, per * bw, per * bw)


def _pad_lanes(v, fill=0.0):
    return jnp.pad(v, (0, LANES - v.shape[0]), constant_values=fill).reshape(1, LANES)


def kernel(x, mem, norm_mix, w_in, ssd_conv_w, ssd_conv_b, ssd_dt_bias, ssd_a_log, ssd_d, ssd_norm, lru_conv_w, lru_conv_b, lru_wa, lru_ba, lru_wx, lru_bx, lru_lambda, w_out, norm_xattn, norm_mem, w_q, w_kv, w_o, norm_moe, w_router, b_router, w_gate_up, b_gate_up, w_down, b_down, norm_final):
    b, s, d = x.shape
    t = b * s
    n_mem = mem.shape[1]
    w = d
    cdim = w + 2 * SSD_GROUPS * SSD_STATE
    o1, o2, o3, o4 = w, w + cdim, w + cdim + SSD_HEADS, w + cdim + SSD_HEADS + w

    wi = w_in[0]
    wzx = wi[:, :o2].astype(BF16)
    wdt = jnp.pad(wi[:, o2:o3], ((0, 0), (0, LANES - SSD_HEADS)))
    wdt_hi, wdt_lo = _hilo(wdt)
    wxr = wi[:, o3:o4].astype(BF16)
    wgr = wi[:, o4:].astype(BF16)
    e01 = (jnp.arange(LANES)[:, None] == (jnp.arange(w)[None, :] // SSD_HEAD_DIM)).astype(BF16)
    dskip = jnp.repeat(ssd_d[0], SSD_HEAD_DIM).reshape(1, w)
    per = 256 // (w // LRU_BLOCKS)
    wa_bd = _block_diag(lru_wa[0], per).astype(BF16)
    wx_bd = _block_diag(lru_wx[0], per).astype(BF16)
    wr = jnp.pad(w_router[0], ((0, 0), (0, LANES - N_EXPERTS)))
    wr_hi, wr_lo = _hilo(wr)
    br = _pad_lanes(b_router[0], fill=-1e30)
    wgu = w_gate_up[0]
    wg = wgu[:, :, 0::2].astype(BF16)
    wu = wgu[:, :, 1::2].astype(BF16)
    bgu = b_gate_up[0]
    bg = bgu[:, None, 0::2]
    bu = bgu[:, None, 1::2]
    wd = w_down[0].astype(BF16)
    bd = b_down[0][:, None, :]

    x2 = x.reshape(t, d)
    z, xbc, dt, xr, gr = _in_proj(x2, norm_mix[0].reshape(1, d), wzx, wdt_hi, wdt_lo, wxr, wgr)

    y_ssd = _ssd(xbc.reshape(b, s, cdim), dt.reshape(b, s, LANES), z.reshape(b, s, w),
                 ssd_conv_w[0], ssd_conv_b[0].reshape(1, cdim), _pad_lanes(ssd_dt_bias[0]),
                 _pad_lanes(ssd_a_log[0]), dskip, ssd_norm[0].reshape(1, w), e01)
    y_lru = _lru(xr.reshape(b, s, w), gr.reshape(b, s, w), lru_conv_w[0], lru_conv_b[0].reshape(1, w),
                 wa_bd, lru_ba[0].reshape(1, w), wx_bd, lru_bx[0].reshape(1, w), lru_lambda[0].reshape(1, w))

    kk, vv = _kv(mem.reshape(b * n_mem, d), norm_mem[0].reshape(1, d), w_kv[0].astype(BF16))
    wo_mix = w_out[0].astype(BF16)
    h2, hn, idx_m, rank_m, gate_m, cnt = _mid(
        x2, y_ssd.reshape(t, w), y_lru.reshape(t, w), kk.reshape(b, n_mem, d), vv.reshape(b, n_mem, d),
        wo_mix[:w], wo_mix[w:], norm_xattn[0].reshape(1, d), w_q[0].astype(BF16), w_o[0].astype(BF16),
        norm_moe[0].reshape(1, d), wr_hi, wr_lo, br, s)

    bm = MOE_BLOCK
    counts = cnt[0, :N_EXPERTS]
    pcounts = (counts + bm - 1) // bm * bm
    pend = jnp.cumsum(pcounts)
    pstart = pend - pcounts
    idx = idx_m[:, :TOP_K]
    dest = pstart[idx] + rank_m[:, :TOP_K]
    n_pairs = t * TOP_K
    nb = (n_pairs + N_EXPERTS * (bm - 1) + bm - 1) // bm
    rows = nb * bm
    tok = jnp.broadcast_to(jnp.arange(t, dtype=jnp.int32)[:, None], (t, TOP_K))
    row_tok = jnp.zeros((rows,), jnp.int32).at[dest.reshape(-1)].set(tok.reshape(-1))
    row_gate = jnp.zeros((rows,), F32).at[dest.reshape(-1)].set(gate_m[:, :TOP_K].reshape(-1))
    block_e = jnp.searchsorted(pend, jnp.arange(nb, dtype=jnp.int32) * bm, side='right')
    block_e = jnp.minimum(block_e, N_EXPERTS - 1).astype(jnp.int32)
    nvb = (pend[-1] // bm).astype(jnp.int32).reshape(1)

    y = _moe(block_e, nvb, row_tok.reshape(nb, 1, bm), row_gate.reshape(rows, 1), hn, wg, bg, wu, bu, wd, bd)

    tc = COMB_TILE
    dest3 = dest.reshape(t // tc, tc, TOP_K).transpose(0, 2, 1).reshape(t // tc, 1, TOP_K * tc)
    out = _combine(dest3.astype(jnp.int32), h2, y, norm_final.reshape(1, d))
    return out.reshape(b, s, d)
```

```python
import functools

import jax
import jax.numpy as jnp
from jax import lax
from jax.experimental import pallas as pl
from jax.experimental.pallas import tpu as pltpu

F32 = jnp.float32
BF16 = jnp.bfloat16

NORM_EPS = 1e-6
LANES = 128
SUBLANES = 8
SSD_HEAD_DIM = 64
SSD_HEADS = 16
SSD_GROUPS = 4
SSD_STATE = 128
CONV_K = 4
LRU_BLOCKS = 16
RG_C = 8.0
X_HEADS = 4
N_EXPERTS = 32
TOP_K = 4
SWIGLU_LIMIT = 7.0
SWIGLU_ALPHA = 1.702

VMEM_LIMIT = 56 * 1024 * 1024

IN_TILE = 512
SSD_CHUNK = 256
LRU_TILE = 256
MID_TILE = 256
MOE_BLOCK = 256
COMB_TILE = 256


def _cparams(sem):
    return pltpu.CompilerParams(dimension_semantics=sem, vmem_limit_bytes=VMEM_LIMIT)


def _rms(x, g):
    ms = jnp.mean(x * x, axis=-1, keepdims=True)
    return x * lax.rsqrt(ms + NORM_EPS) * g


def _sigmoid(x):
    return 1.0 / (1.0 + jnp.exp(-x))


def _softplus(x):
    return jnp.maximum(x, 0.0) + jnp.log(1.0 + jnp.exp(-jnp.abs(x)))


def _split3(x):
    a = x.astype(BF16)
    r = x - a.astype(F32)
    b = r.astype(BF16)
    c = (r - b.astype(F32)).astype(BF16)
    return a, b, c


def _dot(a, b):
    return jnp.dot(a, b, preferred_element_type=F32)


def _dot_nt(a, b):
    return lax.dot_general(a, b, (((1,), (1,)), ((), ())), preferred_element_type=F32)


def _dot01_right(x, m01):
    a, b, c = _split3(x)
    return _dot(a, m01) + _dot(b, m01) + _dot(c, m01)


def _dot01_left(m01, x):
    a, b, c = _split3(x)
    return _dot(m01, a) + _dot(m01, b) + _dot(m01, c)


def _dot_hilo(x, w_hi, w_lo):
    xh = x.astype(BF16)
    xl = (x - xh.astype(F32)).astype(BF16)
    return _dot(xh, w_hi) + _dot(xl, w_hi) + _dot(xh, w_lo)


def _hilo(w):
    hi = w.astype(BF16)
    lo = (w - hi.astype(F32)).astype(BF16)
    return hi, lo


def _in_proj_kernel(x_ref, g_ref, wzx_ref, wdth_ref, wdtl_ref, wxr_ref, wgr_ref,
                    z_ref, xbc_ref, dt_ref, xr_ref, gr_ref):
    d = x_ref.shape[1]
    hn = _rms(x_ref[...], g_ref[...])
    hb = hn.astype(BF16)
    z_ref[...] = _dot(hb, wzx_ref[:, :d]).astype(BF16)
    xbc_ref[...] = _dot(hb, wzx_ref[:, d:]).astype(BF16)
    dt_ref[...] = _dot_hilo(hn, wdth_ref[...], wdtl_ref[...])
    xr_ref[...] = _dot(hb, wxr_ref[...]).astype(BF16)
    gr_ref[...] = _dot(hb, wgr_ref[...]).astype(BF16)


def _in_proj(x2, g, wzx, wdt_hi, wdt_lo, wxr, wgr):
    t, d = x2.shape
    tm = IN_TILE
    nzx = wzx.shape[1]
    const = lambda i: (0, 0)
    row = lambda i: (i, 0)
    return pl.pallas_call(
        _in_proj_kernel,
        grid=(t // tm,),
        in_specs=[
            pl.BlockSpec((tm, d), row),
            pl.BlockSpec((1, d), const),
            pl.BlockSpec((d, nzx), const),
            pl.BlockSpec((d, LANES), const),
            pl.BlockSpec((d, LANES), const),
            pl.BlockSpec((d, d), const),
            pl.BlockSpec((d, d), const),
        ],
        out_specs=[
            pl.BlockSpec((tm, d), row),
            pl.BlockSpec((tm, nzx - d), row),
            pl.BlockSpec((tm, LANES), row),
            pl.BlockSpec((tm, d), row),
            pl.BlockSpec((tm, d), row),
        ],
        out_shape=[
            jax.ShapeDtypeStruct((t, d), BF16),
            jax.ShapeDtypeStruct((t, nzx - d), BF16),
            jax.ShapeDtypeStruct((t, LANES), F32),
            jax.ShapeDtypeStruct((t, d), BF16),
            jax.ShapeDtypeStruct((t, d), BF16),
        ],
        compiler_params=_cparams(("arbitrary",)),
        name="in_proj",
    )(x2, g, wzx, wdt_hi, wdt_lo, wxr, wgr)


def _causal_conv(ext_ref, x_f32, w_ref, b_ref, first):
    n = x_f32.shape[0]
    pad = SUBLANES

    @pl.when(first)
    def _():
        ext_ref[0:pad, :] = jnp.zeros((pad, ext_ref.shape[1]), F32)

    ext_ref[pad:pad + n, :] = x_f32
    acc = b_ref[...] + w_ref[0:1, :] * ext_ref[pad - 3:pad - 3 + n, :]
    for j in range(1, CONV_K):
        acc = acc + w_ref[j:j + 1, :] * ext_ref[pad - 3 + j:pad - 3 + j + n, :]
    ext_ref[0:pad, :] = ext_ref[n:n + pad, :]
    return acc


def _ssd_kernel(xbc_ref, dt_ref, z_ref, cw_ref, cb_ref, dtb_ref, alog_ref, dskip_ref, gn_ref, e_ref,
                y_ref, ext_ref, st_ref):
    n = xbc_ref.shape[1]
    w = z_ref.shape[2]
    gw = w // SSD_GROUPS
    first = pl.program_id(1) == 0

    @pl.when(first)
    def _():
        st_ref[...] = jnp.zeros(st_ref.shape, F32)

    conv = _causal_conv(ext_ref, xbc_ref[0].astype(F32), cw_ref, cb_ref, first)
    xc = conv * _sigmoid(conv)
    xs = xc[:, :w]

    dt = _softplus(dt_ref[0] + dtb_ref[...])
    a = -jnp.exp(alog_ref[...])
    da = dt * a
    ri = lax.broadcasted_iota(jnp.int32, (n, n), 0)
    ci = lax.broadcasted_iota(jnp.int32, (n, n), 1)
    causal = ri >= ci
    tril = jnp.where(causal, 1.0, 0.0).astype(BF16)
    a_cs = _dot01_left(tril, da)
    a_cs_t = a_cs.T

    e01 = e_ref[...]
    dt_x = _dot01_right(dt, e01)
    acs_x = _dot01_right(a_cs, e01)
    last_x = acs_x[n - 1:n, :]
    xdt = xs * dt_x
    xdt_b = xdt.astype(BF16)
    xdt_end = (xdt * jnp.exp(last_x - acs_x)).astype(BF16)
    exp_acs = jnp.exp(acs_x)
    chunk_decay = jnp.exp(last_x)
    lane = lax.broadcasted_iota(jnp.int32, (n, gw), 1)

    for g in range(SSD_GROUPS):
        lo = g * gw
        bg = xc[:, w + g * SSD_STATE:w + (g + 1) * SSD_STATE].astype(BF16)
        cg = xc[:, w + (SSD_GROUPS + g) * SSD_STATE:w + (SSD_GROUPS + g + 1) * SSD_STATE].astype(BF16)
        cb = _dot_nt(cg, bg)
        prev = st_ref[g]
        acc = _dot(cg, prev.astype(BF16)) * exp_acs[:, lo:lo + gw]
        new = lax.dot_general(bg, xdt_end[:, lo:lo + gw], (((0,), (0,)), ((), ())),
                              preferred_element_type=F32)
        st_ref[g] = chunk_decay[:, lo:lo + gw] * prev + new
        xg = xdt_b[:, lo:lo + gw]
        for k in range(SSD_HEADS // SSD_GROUPS):
            h = g * (SSD_HEADS // SSD_GROUPS) + k
            seg = a_cs[:, h:h + 1] - a_cs_t[h:h + 1, :]
            dec = jnp.exp(jnp.where(causal, seg, -jnp.inf))
            m = (cb * dec).astype(BF16)
            in_head = (lane >= k * SSD_HEAD_DIM) & (lane < (k + 1) * SSD_HEAD_DIM)
            acc = acc + _dot(m, jnp.where(in_head, xg, jnp.zeros_like(xg)))
        yg = acc + xs[:, lo:lo + gw] * dskip_ref[:, lo:lo + gw]
        zg = z_ref[0, :, lo:lo + gw].astype(F32)
        u = yg * (zg * _sigmoid(zg))
        u = u * lax.rsqrt(jnp.mean(u * u, axis=-1, keepdims=True) + NORM_EPS)
        y_ref[0, :, lo:lo + gw] = (u * gn_ref[:, lo:lo + gw]).astype(BF16)


def _ssd(xbc, dt, z, cw, cb, dtb, alog, dskip, gn, e01):
    b, s, cdim = xbc.shape
    w = z.shape[2]
    n = SSD_CHUNK
    tile = lambda i, j: (i, j, 0)
    const = lambda i, j: (0, 0)
    return pl.pallas_call(
        _ssd_kernel,
        grid=(b, s // n),
        in_specs=[
            pl.BlockSpec((1, n, cdim), tile),
            pl.BlockSpec((1, n, LANES), tile),
            pl.BlockSpec((1, n, w), tile),
            pl.BlockSpec((CONV_K, cdim), const),
            pl.BlockSpec((1, cdim), const),
            pl.BlockSpec((1, LANES), const),
            pl.BlockSpec((1, LANES), const),
            pl.BlockSpec((1, w), const),
            pl.BlockSpec((1, w), const),
            pl.BlockSpec((LANES, w), const),
        ],
        out_specs=pl.BlockSpec((1, n, w), tile),
        out_shape=jax.ShapeDtypeStruct((b, s, w), BF16),
        scratch_shapes=[
            pltpu.VMEM((n + SUBLANES, cdim), F32),
            pltpu.VMEM((SSD_GROUPS, SSD_STATE, w // SSD_GROUPS), F32),
        ],
        compiler_params=_cparams(("arbitrary", "arbitrary")),
        name="ssd",
    )(xbc, dt, z, cw, cb, dtb, alog, dskip, gn, e01)


def _gelu_tanh(x):
    c = 0.7978845608028654
    return 0.5 * x * (1.0 + jnp.tanh(c * (x + 0.044715 * (x * x * x))))


def _lru_kernel(xr_ref, gr_ref, cw_ref, cb_ref, wa_ref, ba_ref, wx_ref, bx_ref, lam_ref,
                y_ref, ext_ref, car_ref, h_ref):
    n = xr_ref.shape[1]
    w = xr_ref.shape[2]
    first = pl.program_id(1) == 0

    @pl.when(first)
    def _():
        car_ref[...] = jnp.zeros(car_ref.shape, F32)

    xc = _causal_conv(ext_ref, xr_ref[0].astype(F32), cw_ref, cb_ref, first)
    xb = xc.astype(BF16)
    nq = wa_ref.shape[0]
    qw = w // nq
    r_parts, i_parts = [], []
    for q in range(nq):
        xq = xb[:, q * qw:(q + 1) * qw]
        r_parts.append(_dot(xq, wa_ref[q]))
        i_parts.append(_dot(xq, wx_ref[q]))
    r = _sigmoid(jnp.concatenate(r_parts, axis=1) + ba_ref[...])
    gi = _sigmoid(jnp.concatenate(i_parts, axis=1) + bx_ref[...])
    log_a = (-RG_C) * r * _softplus(-lam_ref[...])
    a = jnp.exp(log_a)
    u = xc * gi * jnp.sqrt(1.0 - a * a)

    row = lax.broadcasted_iota(jnp.int32, (n, w), 0) & (SUBLANES - 1)
    ap, bp = a, u
    for d in (1, 2, 4):
        a_s = pltpu.roll(ap, d, 0)
        b_s = pltpu.roll(bp, d, 0)
        m = row >= d
        bp = jnp.where(m, ap * b_s + bp, bp)
        ap = jnp.where(m, ap * a_s, ap)
    carry = car_ref[...]
    for g in range(n // SUBLANES):
        sl = slice(g * SUBLANES, (g + 1) * SUBLANES)
        hb = bp[sl] + ap[sl] * carry
        h_ref[sl, :] = hb
        carry = jnp.broadcast_to(hb[SUBLANES - 1:SUBLANES, :], (SUBLANES, w))
    car_ref[...] = carry
    y_ref[0] = (h_ref[...] * _gelu_tanh(gr_ref[0].astype(F32))).astype(BF16)


def _lru(xr, gr, cw, cb, wa, ba, wx, bx, lam):
    b, s, w = xr.shape
    n = LRU_TILE
    nq, qw, _ = wa.shape
    tile = lambda i, j: (i, j, 0)
    const = lambda i, j: (0, 0)
    const3 = lambda i, j: (0, 0, 0)
    return pl.pallas_call(
        _lru_kernel,
        grid=(b, s // n),
        in_specs=[
            pl.BlockSpec((1, n, w), tile),
            pl.BlockSpec((1, n, w), tile),
            pl.BlockSpec((CONV_K, w), const),
            pl.BlockSpec((1, w), const),
            pl.BlockSpec((nq, qw, qw), const3),
            pl.BlockSpec((1, w), const),
            pl.BlockSpec((nq, qw, qw), const3),
            pl.BlockSpec((1, w), const),
            pl.BlockSpec((1, w), const),
        ],
        out_specs=pl.BlockSpec((1, n, w), tile),
        out_shape=jax.ShapeDtypeStruct((b, s, w), BF16),
        scratch_shapes=[
            pltpu.VMEM((n + SUBLANES, w), F32),
            pltpu.VMEM((SUBLANES, w), F32),
            pltpu.VMEM((n, w), F32),
        ],
        compiler_params=_cparams(("arbitrary", "arbitrary")),
        name="lru",
    )(xr, gr, cw, cb, wa, ba, wx, bx, lam)


def _kv_kernel(m_ref, g_ref, w_ref, k_ref, v_ref):
    d = m_ref.shape[1]
    mn = _rms(m_ref[...], g_ref[...]).astype(BF16)
    k_ref[...] = _dot(mn, w_ref[:, :d]).astype(BF16)
    v_ref[...] = _dot(mn, w_ref[:, d:]).astype(BF16)


def _kv(mem2, g, wkv):
    t, d = mem2.shape
    tm = min(t, 512)
    row = lambda i: (i, 0)
    const = lambda i: (0, 0)
    return pl.pallas_call(
        _kv_kernel,
        grid=(t // tm,),
        in_specs=[pl.BlockSpec((tm, d), row), pl.BlockSpec((1, d), const), pl.BlockSpec((d, 2 * d), const)],
        out_specs=[pl.BlockSpec((tm, d), row), pl.BlockSpec((tm, d), row)],
        out_shape=[jax.ShapeDtypeStruct((t, d), BF16), jax.ShapeDtypeStruct((t, d), BF16)],
        compiler_params=_cparams(("arbitrary",)),
        name="kv",
    )(mem2, g, wkv)


def _mid_kernel(x_ref, ys_ref, yl_ref, k_ref, v_ref, wo1_ref, wo2_ref, gx_ref, wq_ref, wo_ref,
                gm_ref, wrh_ref, wrl_ref, br_ref,
                h_ref, hn_ref, idx_ref, rank_ref, gate_ref, cnt_ref, car_ref):
    tm, d = x_ref.shape
    hd = d // X_HEADS

    @pl.when(pl.program_id(0) == 0)
    def _():
        car_ref[...] = jnp.zeros(car_ref.shape, F32)

    h1 = x_ref[...] + _dot(ys_ref[...], wo1_ref[...]) + _dot(yl_ref[...], wo2_ref[...])

    q = _dot(_rms(h1, gx_ref[...]).astype(BF16), wq_ref[...]).astype(BF16)
    o_parts = []
    for hh in range(X_HEADS):
        sl = slice(hh * hd, (hh + 1) * hd)
        sc = _dot_nt(q[:, sl], k_ref[0, :, sl]) * (hd ** -0.5)
        sc = sc - jnp.max(sc, axis=-1, keepdims=True)
        p = jnp.exp(sc)
        p = p / jnp.sum(p, axis=-1, keepdims=True)
        o_parts.append(_dot(p.astype(BF16), v_ref[0, :, sl]))
    o = jnp.concatenate(o_parts, axis=1).astype(BF16)
    h2 = h1 + _dot(o, wo_ref[...])
    h_ref[...] = h2

    hn = _rms(h2, gm_ref[...])
    hn_ref[...] = hn
    logits = _dot_hilo(hn, wrh_ref[...], wrl_ref[...]) + br_ref[...]

    lane = lax.broadcasted_iota(jnp.int32, (tm, LANES), 1)
    picked = jnp.zeros((tm, LANES), F32)
    vals, idxs = [], []
    l = logits
    for _ in range(TOP_K):
        m = jnp.max(l, axis=-1, keepdims=True)
        idx = jnp.min(jnp.where(l == m, lane, LANES), axis=-1, keepdims=True)
        sel = lane == idx
        vals.append(m)
        idxs.append(idx)
        picked = jnp.where(sel, 1.0, picked)
        l = jnp.where(sel, -jnp.inf, l)
    ex = [jnp.exp(v - vals[0]) for v in vals]
    den = ex[0] + ex[1] + ex[2] + ex[3]

    ri = lax.broadcasted_iota(jnp.int32, (tm, tm), 0)
    ci = lax.broadcasted_iota(jnp.int32, (tm, tm), 1)
    strict = jnp.where(ri > ci, 1.0, 0.0).astype(BF16)
    before = _dot(strict, picked.astype(BF16)) + car_ref[0:1, :]
    idx_out = jnp.zeros((tm, LANES), jnp.int32)
    rank_out = jnp.zeros((tm, LANES), jnp.int32)
    gate_out = jnp.zeros((tm, LANES), F32)
    for k in range(TOP_K):
        rk = jnp.sum(jnp.where(lane == idxs[k], before, 0.0), axis=-1, keepdims=True)
        at_k = lane == k
        idx_out = jnp.where(at_k, idxs[k], idx_out)
        rank_out = jnp.where(at_k, rk.astype(jnp.int32), rank_out)
        gate_out = jnp.where(at_k, ex[k] / den, gate_out)
    idx_ref[...] = idx_out
    rank_ref[...] = rank_out
    gate_ref[...] = gate_out
    total = car_ref[...] + jnp.sum(picked, axis=0, keepdims=True)
    car_ref[...] = total
    cnt_ref[...] = total.astype(jnp.int32)


def _mid(x2, ys, yl, kk, vv, wo1, wo2, gx, wq, wo, gm, wr_hi, wr_lo, br, seq):
    t, d = x2.shape
    tm = MID_TILE
    m = kk.shape[1]
    per_b = seq // tm
    row = lambda i: (i, 0)
    const = lambda i: (0, 0)
    kvmap = lambda i: (i // per_b, 0, 0)
    wspec = pl.BlockSpec((d, d), const)
    vspec = pl.BlockSpec((1, d), const)
    return pl.pallas_call(
        _mid_kernel,
        grid=(t // tm,),
        in_specs=[
            pl.BlockSpec((tm, d), row), pl.BlockSpec((tm, d), row), pl.BlockSpec((tm, d), row),
            pl.BlockSpec((1, m, d), kvmap), pl.BlockSpec((1, m, d), kvmap),
            wspec, wspec, vspec, wspec, wspec, vspec,
            pl.BlockSpec((d, LANES), const), pl.BlockSpec((d, LANES), const), pl.BlockSpec((1, LANES), const),
        ],
        out_specs=[
            pl.BlockSpec((tm, d), row), pl.BlockSpec((tm, d), row),
            pl.BlockSpec((tm, LANES), row), pl.BlockSpec((tm, LANES), row), pl.BlockSpec((tm, LANES), row),
            pl.BlockSpec((SUBLANES, LANES), const),
        ],
        out_shape=[
            jax.ShapeDtypeStruct((t, d), F32), jax.ShapeDtypeStruct((t, d), F32),
            jax.ShapeDtypeStruct((t, LANES), jnp.int32), jax.ShapeDtypeStruct((t, LANES), jnp.int32),
            jax.ShapeDtypeStruct((t, LANES), F32),
            jax.ShapeDtypeStruct((SUBLANES, LANES), jnp.int32),
        ],
        scratch_shapes=[pltpu.VMEM((SUBLANES, LANES), F32)],
        compiler_params=_cparams(("arbitrary",)),
        name="mid",
    )(x2, ys, yl, kk, vv, wo1, wo2, gx, wq, wo, gm, wr_hi, wr_lo, br)


def _row_copy(src_hbm, row, dst_vmem, slot, r, sem):
    return pltpu.make_async_copy(src_hbm.at[pl.ds(row, 1), :], dst_vmem.at[slot, pl.ds(r, 1), :], sem.at[slot])


def _start_rows(src_hbm, idx_ref, dst_vmem, slot, sem, n):
    def body(r, c):
        _row_copy(src_hbm, idx_ref[0, 0, r], dst_vmem, slot, r, sem).start()
        return c

    lax.fori_loop(0, n, body, 0, unroll=8)


def _wait_rows(src_hbm, dst_vmem, slot, sem, n):
    pltpu.make_async_copy(src_hbm.at[pl.ds(0, n), :], dst_vmem.at[slot], sem.at[slot]).wait()


def _dispatch_kernel(pend_ref, dst_ref, hn_ref, x_hbm, zero_ref, sem):
    i = pl.program_id(0)
    tc = hn_ref.shape[0]
    bm = zero_ref.shape[0]
    rows = TOP_K * tc

    @pl.when(i == 0)
    def _():
        zero_ref[...] = jnp.zeros(zero_ref.shape, F32)

        def fill(start):
            return pltpu.make_async_copy(zero_ref, x_hbm.at[pl.ds(pl.multiple_of(start, bm), bm), :], sem.at[1])

        for e in range(N_EXPERTS):
            fill(jnp.maximum(pend_ref[e] - bm, 0)).start()
        for e in range(N_EXPERTS):
            fill(0).wait()

        def tail(j, c):
            fill(j * bm).start()
            fill(0).wait()
            return c

        lax.fori_loop(pend_ref[N_EXPERTS - 1] // bm, x_hbm.shape[0] // bm, tail, 0)

    def body(r, c):
        src = lax.rem(r, tc)
        pltpu.make_async_copy(hn_ref.at[pl.ds(src, 1), :], x_hbm.at[pl.ds(dst_ref[0, 0, r], 1), :],
                              sem.at[0]).start()
        return c

    lax.fori_loop(0, rows, body, 0, unroll=8)
    pltpu.make_async_copy(x_hbm.at[pl.ds(0, rows), :], x_hbm.at[pl.ds(0, rows), :], sem.at[0]).wait()


def _dispatch(pend, dest3, hn, n_rows):
    t, d = hn.shape
    tc = COMB_TILE
    grid_spec = pltpu.PrefetchScalarGridSpec(
        num_scalar_prefetch=1,
        grid=(t // tc,),
        in_specs=[
            pl.BlockSpec((1, 1, TOP_K * tc), lambda i, pe: (i, 0, 0), memory_space=pltpu.SMEM),
            pl.BlockSpec((tc, d), lambda i, pe: (i, 0)),
        ],
        out_specs=pl.BlockSpec(memory_space=pl.ANY),
        scratch_shapes=[pltpu.VMEM((MOE_BLOCK, d), F32), pltpu.SemaphoreType.DMA((2,))],
    )
    return pl.pallas_call(
        _dispatch_kernel,
        grid_spec=grid_spec,
        out_shape=jax.ShapeDtypeStruct((n_rows, d), F32),
        compiler_params=_cparams(("arbitrary",)),
        name="dispatch",
    )(pend, dest3, hn)


def _moe_kernel(be_ref, nvb_ref, x_ref, wgu_ref, bg_ref, bu_ref, wd_ref, bd_ref, perm_ref,
                y_ref, wgu_s, wd_s, act_s):
    i = pl.program_id(0)
    nvb = nvb_ref[0]
    f = wd_ref.shape[1]
    pw = perm_ref.shape[0]
    half = pw // 2

    changed = jnp.logical_or(i == 0, be_ref[i] != be_ref[jnp.maximum(i - 1, 0)])

    @pl.when(jnp.logical_and(changed, i < nvb))
    def _():
        for c in range(2 * f // pw):
            wc = wgu_ref[0, :, c * pw:(c + 1) * pw].astype(BF16)
            wgu_s[:, c * pw:(c + 1) * pw] = _dot(wc, perm_ref[...]).astype(BF16)
        wd_s[...] = wd_ref[0].astype(BF16)

    @pl.when(i < nvb)
    def _():
        xb = x_ref[...].astype(BF16)
        for c in range(2 * f // pw):
            gu = _dot(xb, wgu_s[:, c * pw:(c + 1) * pw])
            g = gu[:, :half] + bg_ref[0, :, c * half:(c + 1) * half]
            u = gu[:, half:] + bu_ref[0, :, c * half:(c + 1) * half]
            g = jnp.minimum(g, SWIGLU_LIMIT)
            u = jnp.clip(u, -SWIGLU_LIMIT, SWIGLU_LIMIT)
            act = (u + 1.0) * (g * _sigmoid(SWIGLU_ALPHA * g))
            act_s[:, c * half:(c + 1) * half] = act.astype(BF16)
        y_ref[...] = _dot(act_s[...], wd_s[...]) + bd_ref[0]

    @pl.when(i >= nvb)
    def _():
        y_ref[...] = jnp.zeros(y_ref.shape, F32)


def _moe(block_e, nvb, xrows, wgu, bg, bu, wd, bd, perm):
    rows, d = xrows.shape
    bm = MOE_BLOCK
    nb = rows // bm
    f = wd.shape[1]
    emap = lambda i, be, nv: (be[i], 0, 0)
    grid_spec = pltpu.PrefetchScalarGridSpec(
        num_scalar_prefetch=2,
        grid=(nb,),
        in_specs=[
            pl.BlockSpec((bm, d), lambda i, be, nv: (jnp.minimum(i, jnp.maximum(nv[0] - 1, 0)), 0)),
            pl.BlockSpec((1, d, 2 * f), emap),
            pl.BlockSpec((1, 1, f), emap), pl.BlockSpec((1, 1, f), emap),
            pl.BlockSpec((1, f, d), emap), pl.BlockSpec((1, 1, d), emap),
            pl.BlockSpec(perm.shape, lambda i, be, nv: (0, 0)),
        ],
        out_specs=pl.BlockSpec((bm, d), lambda i, be, nv: (i, 0)),
        scratch_shapes=[pltpu.VMEM((d, 2 * f), BF16), pltpu.VMEM((f, d), BF16), pltpu.VMEM((bm, f), BF16)],
    )
    return pl.pallas_call(
        _moe_kernel,
        grid_spec=grid_spec,
        out_shape=jax.ShapeDtypeStruct((rows, d), F32),
        compiler_params=_cparams(("arbitrary",)),
        name="moe",
    )(block_e, nvb, xrows, wgu, bg, bu, wd, bd, perm)


def _comb_kernel(dst_ref, dstn_ref, h_ref, gate_ref, y_hbm, g_ref, o_ref, ybuf, sem):
    i = pl.program_id(0)
    n = pl.num_programs(0)
    tc = h_ref.shape[0]
    rows = TOP_K * tc
    slot = i % 2

    @pl.when(i == 0)
    def _():
        _start_rows(y_hbm, dst_ref, ybuf, 0, sem, rows)

    @pl.when(i + 1 < n)
    def _():
        _start_rows(y_hbm, dstn_ref, ybuf, 1 - slot, sem, rows)

    _wait_rows(y_hbm, ybuf, slot, sem, rows)
    acc = h_ref[...]
    for k in range(TOP_K):
        acc = acc + ybuf[slot, k * tc:(k + 1) * tc, :] * gate_ref[:, k:k + 1]
    o_ref[...] = _rms(acc, g_ref[...])


def _combine(dest3, h2, gate, y, g):
    t, d = h2.shape
    tc = COMB_TILE
    nt = t // tc
    return pl.pallas_call(
        _comb_kernel,
        grid=(nt,),
        in_specs=[
            pl.BlockSpec((1, 1, TOP_K * tc), lambda i: (i, 0, 0), memory_space=pltpu.SMEM),
            pl.BlockSpec((1, 1, TOP_K * tc), lambda i: (jnp.minimum(i + 1, nt - 1), 0, 0),
                         memory_space=pltpu.SMEM),
            pl.BlockSpec((tc, d), lambda i: (i, 0)),
            pl.BlockSpec((tc, LANES), lambda i: (i, 0)),
            pl.BlockSpec(memory_space=pl.ANY),
            pl.BlockSpec((1, d), lambda i: (0, 0)),
        ],
        out_specs=pl.BlockSpec((tc, d), lambda i: (i, 0)),
        out_shape=jax.ShapeDtypeStruct((t, d), F32),
        scratch_shapes=[pltpu.VMEM((2, TOP_K * tc, d), F32), pltpu.SemaphoreType.DMA((2,))],
        compiler_params=_cparams(("arbitrary",)),
        name="combine",
    )(dest3, dest3, h2, gate, y, g)


def _block_diag(wb, per):
    nb, bw, _ = wb.shape
    wq = wb.reshape(nb // per, per, bw, bw)
    eye = jnp.eye(per, dtype=wb.dtype)
    out = jnp.einsum('qaij,ab->qaibj', wq, eye)
    return out.reshape(nb // per, per * bw, per * bw)


def _pad_lanes(v, fill=0.0):
    return jnp.pad(v, (0, LANES - v.shape[0]), constant_values=fill).reshape(1, LANES)


def kernel(x, mem, norm_mix, w_in, ssd_conv_w, ssd_conv_b, ssd_dt_bias, ssd_a_log, ssd_d, ssd_norm, lru_conv_w, lru_conv_b, lru_wa, lru_ba, lru_wx, lru_bx, lru_lambda, w_out, norm_xattn, norm_mem, w_q, w_kv, w_o, norm_moe, w_router, b_router, w_gate_up, b_gate_up, w_down, b_down, norm_final):
    b, s, d = x.shape
    t = b * s
    n_mem = mem.shape[1]
    w = d
    cdim = w + 2 * SSD_GROUPS * SSD_STATE
    o1, o2, o3, o4 = w, w + cdim, w + cdim + SSD_HEADS, w + cdim + SSD_HEADS + w

    wi = w_in[0]
    wzx = wi[:, :o2].astype(BF16)
    wdt = jnp.pad(wi[:, o2:o3], ((0, 0), (0, LANES - SSD_HEADS)))
    wdt_hi, wdt_lo = _hilo(wdt)
    wxr = wi[:, o3:o4].astype(BF16)
    wgr = wi[:, o4:].astype(BF16)
    e01 = (jnp.arange(LANES)[:, None] == (jnp.arange(w)[None, :] // SSD_HEAD_DIM)).astype(BF16)
    dskip = jnp.repeat(ssd_d[0], SSD_HEAD_DIM).reshape(1, w)
    per = 256 // (w // LRU_BLOCKS)
    wa_bd = _block_diag(lru_wa[0], per).astype(BF16)
    wx_bd = _block_diag(lru_wx[0], per).astype(BF16)
    wr = jnp.pad(w_router[0], ((0, 0), (0, LANES - N_EXPERTS)))
    wr_hi, wr_lo = _hilo(wr)
    br = _pad_lanes(b_router[0], fill=-1e30)
    bgu = b_gate_up[0]
    bg = bgu[:, None, 0::2]
    bu = bgu[:, None, 1::2]
    bd = b_down[0][:, None, :]
    pw = 2 * LANES
    col = jnp.arange(pw)
    src_col = jnp.where(col < LANES, 2 * col, 2 * (col - LANES) + 1)
    perm = (jnp.arange(pw)[:, None] == src_col[None, :]).astype(BF16)

    x2 = x.reshape(t, d)
    z, xbc, dt, xr, gr = _in_proj(x2, norm_mix[0].reshape(1, d), wzx, wdt_hi, wdt_lo, wxr, wgr)

    y_ssd = _ssd(xbc.reshape(b, s, cdim), dt.reshape(b, s, LANES), z.reshape(b, s, w),
                 ssd_conv_w[0], ssd_conv_b[0].reshape(1, cdim), _pad_lanes(ssd_dt_bias[0]),
                 _pad_lanes(ssd_a_log[0]), dskip, ssd_norm[0].reshape(1, w), e01)
    y_lru = _lru(xr.reshape(b, s, w), gr.reshape(b, s, w), lru_conv_w[0], lru_conv_b[0].reshape(1, w),
                 wa_bd, lru_ba[0].reshape(1, w), wx_bd, lru_bx[0].reshape(1, w), lru_lambda[0].reshape(1, w))

    kk, vv = _kv(mem.reshape(b * n_mem, d), norm_mem[0].reshape(1, d), w_kv[0].astype(BF16))
    wo_mix = w_out[0].astype(BF16)
    h2, hn, idx_m, rank_m, gate_m, cnt = _mid(
        x2, y_ssd.reshape(t, w), y_lru.reshape(t, w), kk.reshape(b, n_mem, d), vv.reshape(b, n_mem, d),
        wo_mix[:w], wo_mix[w:], norm_xattn[0].reshape(1, d), w_q[0].astype(BF16), w_o[0].astype(BF16),
        norm_moe[0].reshape(1, d), wr_hi, wr_lo, br, s)

    bm = MOE_BLOCK
    counts = cnt[0, :N_EXPERTS]
    pcounts = (counts + bm - 1) // bm * bm
    pend = jnp.cumsum(pcounts).astype(jnp.int32)
    pstart = pend - pcounts
    idx = idx_m[:, :TOP_K]
    onehot = idx[:, :, None] == jnp.arange(N_EXPERTS, dtype=jnp.int32)[None, None, :]
    dest = jnp.sum(jnp.where(onehot, pstart[None, None, :], 0), axis=-1) + rank_m[:, :TOP_K]
    n_pairs = t * TOP_K
    nb = (n_pairs + N_EXPERTS * (bm - 1) + bm - 1) // bm
    blk0 = jnp.arange(nb, dtype=jnp.int32) * bm
    block_e = jnp.minimum(jnp.sum(pend[None, :] <= blk0[:, None], axis=1), N_EXPERTS - 1).astype(jnp.int32)
    nvb = (pend[-1] // bm).astype(jnp.int32).reshape(1)
    tc = COMB_TILE
    dest3 = dest.astype(jnp.int32).reshape(t // tc, tc, TOP_K).transpose(0, 2, 1).reshape(t // tc, 1, TOP_K * tc)

    xrows = _dispatch(pend, dest3, hn, nb * bm)
    y = _moe(block_e, nvb, xrows, w_gate_up[0], bg, bu, w_down[0], bd, perm)
    out = _combine(dest3, h2, gate_m, y, norm_final.reshape(1, d))
    return out.reshape(b, s, d)
```

```python
import functools

import jax
import jax.numpy as jnp
from jax import lax
from jax.experimental import pallas as pl
from jax.experimental.pallas import tpu as pltpu

F32 = jnp.float32
BF16 = jnp.bfloat16

NORM_EPS = 1e-6
LANES = 128
SUBLANES = 8
SSD_HEAD_DIM = 64
SSD_HEADS = 16
SSD_GROUPS = 4
SSD_STATE = 128
CONV_K = 4
LRU_BLOCKS = 16
RG_C = 8.0
X_HEADS = 4
N_EXPERTS = 32
TOP_K = 4
SWIGLU_LIMIT = 7.0
SWIGLU_ALPHA = 1.702

VMEM_LIMIT = 56 * 1024 * 1024

IN_TILE = 512
SSD_CHUNK = 256
LRU_TILE = 256
MID_TILE = 256
MOE_BLOCK = 256
COMB_TILE = 256


def _cparams(sem):
    return pltpu.CompilerParams(dimension_semantics=sem, vmem_limit_bytes=VMEM_LIMIT)


def _rms(x, g):
    ms = jnp.mean(x * x, axis=-1, keepdims=True)
    return x * lax.rsqrt(ms + NORM_EPS) * g


def _sigmoid(x):
    return 1.0 / (1.0 + jnp.exp(-x))


def _softplus(x):
    return jnp.maximum(x, 0.0) + jnp.log(1.0 + jnp.exp(-jnp.abs(x)))


def _split3(x):
    a = x.astype(BF16)
    r = x - a.astype(F32)
    b = r.astype(BF16)
    c = (r - b.astype(F32)).astype(BF16)
    return a, b, c


def _dot(a, b):
    return jnp.dot(a, b, preferred_element_type=F32)


def _dot_nt(a, b):
    return lax.dot_general(a, b, (((1,), (1,)), ((), ())), preferred_element_type=F32)


def _dot01_right(x, m01):
    a, b, c = _split3(x)
    return _dot(a, m01) + _dot(b, m01) + _dot(c, m01)


def _dot01_left(m01, x):
    a, b, c = _split3(x)
    return _dot(m01, a) + _dot(m01, b) + _dot(m01, c)


def _dot_hilo(x, w_hi, w_lo):
    xh = x.astype(BF16)
    xl = (x - xh.astype(F32)).astype(BF16)
    return _dot(xh, w_hi) + _dot(xl, w_hi) + _dot(xh, w_lo)


def _hilo(w):
    hi = w.astype(BF16)
    lo = (w - hi.astype(F32)).astype(BF16)
    return hi, lo


def _store_token_major(ref, val, base=0):
    n, d = val.shape
    parts = d // LANES
    for s in range(parts):
        ref[pl.ds(base + s, n, stride=parts), :] = val[:, s * LANES:(s + 1) * LANES]


def _load_token_major(ref, n, d, base=0):
    parts = d // LANES
    return [ref[pl.ds(base + s, n, stride=parts), :] for s in range(parts)]


def _in_proj_kernel(x_ref, g_ref, wzx_ref, wdth_ref, wdtl_ref, wxr_ref, wgr_ref,
                    z_ref, xbc_ref, dt_ref, xr_ref, gr_ref):
    d = x_ref.shape[1]
    hn = _rms(x_ref[...], g_ref[...])
    hb = hn.astype(BF16)
    z_ref[...] = _dot(hb, wzx_ref[:, :d]).astype(BF16)
    xbc_ref[...] = _dot(hb, wzx_ref[:, d:]).astype(BF16)
    dt_ref[...] = _dot_hilo(hn, wdth_ref[...], wdtl_ref[...])
    xr_ref[...] = _dot(hb, wxr_ref[...]).astype(BF16)
    gr_ref[...] = _dot(hb, wgr_ref[...]).astype(BF16)


def _in_proj(x2, g, wzx, wdt_hi, wdt_lo, wxr, wgr):
    t, d = x2.shape
    tm = IN_TILE
    nzx = wzx.shape[1]
    const = lambda i: (0, 0)
    row = lambda i: (i, 0)
    return pl.pallas_call(
        _in_proj_kernel,
        grid=(t // tm,),
        in_specs=[
            pl.BlockSpec((tm, d), row),
            pl.BlockSpec((1, d), const),
            pl.BlockSpec((d, nzx), const),
            pl.BlockSpec((d, LANES), const),
            pl.BlockSpec((d, LANES), const),
            pl.BlockSpec((d, d), const),
            pl.BlockSpec((d, d), const),
        ],
        out_specs=[
            pl.BlockSpec((tm, d), row),
            pl.BlockSpec((tm, nzx - d), row),
            pl.BlockSpec((tm, LANES), row),
            pl.BlockSpec((tm, d), row),
            pl.BlockSpec((tm, d), row),
        ],
        out_shape=[
            jax.ShapeDtypeStruct((t, d), BF16),
            jax.ShapeDtypeStruct((t, nzx - d), BF16),
            jax.ShapeDtypeStruct((t, LANES), F32),
            jax.ShapeDtypeStruct((t, d), BF16),
            jax.ShapeDtypeStruct((t, d), BF16),
        ],
        compiler_params=_cparams(("arbitrary",)),
        name="in_proj",
    )(x2, g, wzx, wdt_hi, wdt_lo, wxr, wgr)


def _causal_conv(ext_ref, x_f32, w_ref, b_ref, first):
    n = x_f32.shape[0]
    pad = SUBLANES

    @pl.when(first)
    def _():
        ext_ref[0:pad, :] = jnp.zeros((pad, ext_ref.shape[1]), F32)

    ext_ref[pad:pad + n, :] = x_f32
    acc = b_ref[...] + w_ref[0:1, :] * ext_ref[pad - 3:pad - 3 + n, :]
    for j in range(1, CONV_K):
        acc = acc + w_ref[j:j + 1, :] * ext_ref[pad - 3 + j:pad - 3 + j + n, :]
    ext_ref[0:pad, :] = ext_ref[n:n + pad, :]
    return acc


def _ssd_kernel(xbc_ref, dt_ref, z_ref, cw_ref, cb_ref, dtb_ref, alog_ref, dskip_ref, gn_ref, e_ref,
                y_ref, ext_ref, st_ref):
    n = xbc_ref.shape[1]
    w = z_ref.shape[2]
    gw = w // SSD_GROUPS
    first = pl.program_id(1) == 0

    @pl.when(first)
    def _():
        st_ref[...] = jnp.zeros(st_ref.shape, F32)

    conv = _causal_conv(ext_ref, xbc_ref[0].astype(F32), cw_ref, cb_ref, first)
    xc = conv * _sigmoid(conv)
    xs = xc[:, :w]

    dt = _softplus(dt_ref[0] + dtb_ref[...])
    a = -jnp.exp(alog_ref[...])
    da = dt * a
    ri = lax.broadcasted_iota(jnp.int32, (n, n), 0)
    ci = lax.broadcasted_iota(jnp.int32, (n, n), 1)
    causal = ri >= ci
    tril = jnp.where(causal, 1.0, 0.0).astype(BF16)
    a_cs = _dot01_left(tril, da)
    a_cs_t = a_cs.T

    e01 = e_ref[...]
    dt_x = _dot01_right(dt, e01)
    acs_x = _dot01_right(a_cs, e01)
    last_x = acs_x[n - 1:n, :]
    xdt = xs * dt_x
    xdt_b = xdt.astype(BF16)
    xdt_end = (xdt * jnp.exp(last_x - acs_x)).astype(BF16)
    exp_acs = jnp.exp(acs_x)
    chunk_decay = jnp.exp(last_x)
    lane = lax.broadcasted_iota(jnp.int32, (n, gw), 1)

    for g in range(SSD_GROUPS):
        lo = g * gw
        bg = xc[:, w + g * SSD_STATE:w + (g + 1) * SSD_STATE].astype(BF16)
        cg = xc[:, w + (SSD_GROUPS + g) * SSD_STATE:w + (SSD_GROUPS + g + 1) * SSD_STATE].astype(BF16)
        cb = _dot_nt(cg, bg)
        prev = st_ref[g]
        acc = _dot(cg, prev.astype(BF16)) * exp_acs[:, lo:lo + gw]
        new = lax.dot_general(bg, xdt_end[:, lo:lo + gw], (((0,), (0,)), ((), ())),
                              preferred_element_type=F32)
        st_ref[g] = chunk_decay[:, lo:lo + gw] * prev + new
        xg = xdt_b[:, lo:lo + gw]
        for k in range(SSD_HEADS // SSD_GROUPS):
            h = g * (SSD_HEADS // SSD_GROUPS) + k
            seg = a_cs[:, h:h + 1] - a_cs_t[h:h + 1, :]
            dec = jnp.exp(jnp.where(causal, seg, -jnp.inf))
            m = (cb * dec).astype(BF16)
            in_head = (lane >= k * SSD_HEAD_DIM) & (lane < (k + 1) * SSD_HEAD_DIM)
            acc = acc + _dot(m, jnp.where(in_head, xg, jnp.zeros_like(xg)))
        yg = acc + xs[:, lo:lo + gw] * dskip_ref[:, lo:lo + gw]
        zg = z_ref[0, :, lo:lo + gw].astype(F32)
        u = yg * (zg * _sigmoid(zg))
        u = u * lax.rsqrt(jnp.mean(u * u, axis=-1, keepdims=True) + NORM_EPS)
        y_ref[0, :, lo:lo + gw] = (u * gn_ref[:, lo:lo + gw]).astype(BF16)


def _ssd(xbc, dt, z, cw, cb, dtb, alog, dskip, gn, e01):
    b, s, cdim = xbc.shape
    w = z.shape[2]
    n = SSD_CHUNK
    tile = lambda i, j: (i, j, 0)
    const = lambda i, j: (0, 0)
    return pl.pallas_call(
        _ssd_kernel,
        grid=(b, s // n),
        in_specs=[
            pl.BlockSpec((1, n, cdim), tile),
            pl.BlockSpec((1, n, LANES), tile),
            pl.BlockSpec((1, n, w), tile),
            pl.BlockSpec((CONV_K, cdim), const),
            pl.BlockSpec((1, cdim), const),
            pl.BlockSpec((1, LANES), const),
            pl.BlockSpec((1, LANES), const),
            pl.BlockSpec((1, w), const),
            pl.BlockSpec((1, w), const),
            pl.BlockSpec((LANES, w), const),
        ],
        out_specs=pl.BlockSpec((1, n, w), tile),
        out_shape=jax.ShapeDtypeStruct((b, s, w), BF16),
        scratch_shapes=[
            pltpu.VMEM((n + SUBLANES, cdim), F32),
            pltpu.VMEM((SSD_GROUPS, SSD_STATE, w // SSD_GROUPS), F32),
        ],
        compiler_params=_cparams(("arbitrary", "arbitrary")),
        name="ssd",
    )(xbc, dt, z, cw, cb, dtb, alog, dskip, gn, e01)


def _gelu_tanh(x):
    c = 0.7978845608028654
    return 0.5 * x * (1.0 + jnp.tanh(c * (x + 0.044715 * (x * x * x))))


def _lru_kernel(xr_ref, gr_ref, cw_ref, cb_ref, wa_ref, ba_ref, wx_ref, bx_ref, lam_ref,
                y_ref, ext_ref, car_ref, h_ref):
    n = xr_ref.shape[1]
    w = xr_ref.shape[2]
    first = pl.program_id(1) == 0

    @pl.when(first)
    def _():
        car_ref[...] = jnp.zeros(car_ref.shape, F32)

    xc = _causal_conv(ext_ref, xr_ref[0].astype(F32), cw_ref, cb_ref, first)
    xb = xc.astype(BF16)
    nq = wa_ref.shape[0]
    qw = w // nq
    r_parts, i_parts = [], []
    for q in range(nq):
        xq = xb[:, q * qw:(q + 1) * qw]
        r_parts.append(_dot(xq, wa_ref[q]))
        i_parts.append(_dot(xq, wx_ref[q]))
    r = _sigmoid(jnp.concatenate(r_parts, axis=1) + ba_ref[...])
    gi = _sigmoid(jnp.concatenate(i_parts, axis=1) + bx_ref[...])
    log_a = (-RG_C) * r * _softplus(-lam_ref[...])
    a = jnp.exp(log_a)
    u = xc * gi * jnp.sqrt(1.0 - a * a)

    row = lax.broadcasted_iota(jnp.int32, (n, w), 0) & (SUBLANES - 1)
    ap, bp = a, u
    for d in (1, 2, 4):
        a_s = pltpu.roll(ap, d, 0)
        b_s = pltpu.roll(bp, d, 0)
        m = row >= d
        bp = jnp.where(m, ap * b_s + bp, bp)
        ap = jnp.where(m, ap * a_s, ap)
    carry = car_ref[...]
    for g in range(n // SUBLANES):
        sl = slice(g * SUBLANES, (g + 1) * SUBLANES)
        hb = bp[sl] + ap[sl] * carry
        h_ref[sl, :] = hb
        carry = jnp.broadcast_to(hb[SUBLANES - 1:SUBLANES, :], (SUBLANES, w))
    car_ref[...] = carry
    y_ref[0] = (h_ref[...] * _gelu_tanh(gr_ref[0].astype(F32))).astype(BF16)


def _lru(xr, gr, cw, cb, wa, ba, wx, bx, lam):
    b, s, w = xr.shape
    n = LRU_TILE
    nq, qw, _ = wa.shape
    tile = lambda i, j: (i, j, 0)
    const = lambda i, j: (0, 0)
    const3 = lambda i, j: (0, 0, 0)
    return pl.pallas_call(
        _lru_kernel,
        grid=(b, s // n),
        in_specs=[
            pl.BlockSpec((1, n, w), tile),
            pl.BlockSpec((1, n, w), tile),
            pl.BlockSpec((CONV_K, w), const),
            pl.BlockSpec((1, w), const),
            pl.BlockSpec((nq, qw, qw), const3),
            pl.BlockSpec((1, w), const),
            pl.BlockSpec((nq, qw, qw), const3),
            pl.BlockSpec((1, w), const),
            pl.BlockSpec((1, w), const),
        ],
        out_specs=pl.BlockSpec((1, n, w), tile),
        out_shape=jax.ShapeDtypeStruct((b, s, w), BF16),
        scratch_shapes=[
            pltpu.VMEM((n + SUBLANES, w), F32),
            pltpu.VMEM((SUBLANES, w), F32),
            pltpu.VMEM((n, w), F32),
        ],
        compiler_params=_cparams(("arbitrary", "arbitrary")),
        name="lru",
    )(xr, gr, cw, cb, wa, ba, wx, bx, lam)


def _kv_kernel(m_ref, g_ref, w_ref, k_ref, v_ref):
    d = m_ref.shape[1]
    mn = _rms(m_ref[...], g_ref[...]).astype(BF16)
    k_ref[...] = _dot(mn, w_ref[:, :d]).astype(BF16)
    v_ref[...] = _dot(mn, w_ref[:, d:]).astype(BF16)


def _kv(mem2, g, wkv):
    t, d = mem2.shape
    tm = min(t, 512)
    row = lambda i: (i, 0)
    const = lambda i: (0, 0)
    return pl.pallas_call(
        _kv_kernel,
        grid=(t // tm,),
        in_specs=[pl.BlockSpec((tm, d), row), pl.BlockSpec((1, d), const), pl.BlockSpec((d, 2 * d), const)],
        out_specs=[pl.BlockSpec((tm, d), row), pl.BlockSpec((tm, d), row)],
        out_shape=[jax.ShapeDtypeStruct((t, d), BF16), jax.ShapeDtypeStruct((t, d), BF16)],
        compiler_params=_cparams(("arbitrary",)),
        name="kv",
    )(mem2, g, wkv)


def _mid_kernel(x_ref, ys_ref, yl_ref, k_ref, v_ref, wo1_ref, wo2_ref, gx_ref, wq_ref, wo_ref,
                gm_ref, wrh_ref, wrl_ref, br_ref,
                h_ref, hn_ref, idx_ref, rank_ref, gate_ref, cnt_ref, car_ref):
    tm, d = x_ref.shape
    hd = d // X_HEADS

    @pl.when(pl.program_id(0) == 0)
    def _():
        car_ref[...] = jnp.zeros(car_ref.shape, F32)

    h1 = x_ref[...] + _dot(ys_ref[...], wo1_ref[...]) + _dot(yl_ref[...], wo2_ref[...])

    q = _dot(_rms(h1, gx_ref[...]).astype(BF16), wq_ref[...]).astype(BF16)
    o_parts = []
    for hh in range(X_HEADS):
        sl = slice(hh * hd, (hh + 1) * hd)
        sc = _dot_nt(q[:, sl], k_ref[0, :, sl]) * (hd ** -0.5)
        sc = sc - jnp.max(sc, axis=-1, keepdims=True)
        p = jnp.exp(sc)
        p = p / jnp.sum(p, axis=-1, keepdims=True)
        o_parts.append(_dot(p.astype(BF16), v_ref[0, :, sl]))
    o = jnp.concatenate(o_parts, axis=1).astype(BF16)
    h2 = h1 + _dot(o, wo_ref[...])
    h_ref[...] = h2

    hn = _rms(h2, gm_ref[...])
    _store_token_major(hn_ref, hn)
    logits = _dot_hilo(hn, wrh_ref[...], wrl_ref[...]) + br_ref[...]

    lane = lax.broadcasted_iota(jnp.int32, (tm, LANES), 1)
    picked = jnp.zeros((tm, LANES), F32)
    vals, idxs = [], []
    l = logits
    for _ in range(TOP_K):
        m = jnp.max(l, axis=-1, keepdims=True)
        idx = jnp.min(jnp.where(l == m, lane, LANES), axis=-1, keepdims=True)
        sel = lane == idx
        vals.append(m)
        idxs.append(idx)
        picked = jnp.where(sel, 1.0, picked)
        l = jnp.where(sel, -jnp.inf, l)
    ex = [jnp.exp(v - vals[0]) for v in vals]
    den = ex[0] + ex[1] + ex[2] + ex[3]

    ri = lax.broadcasted_iota(jnp.int32, (tm, tm), 0)
    ci = lax.broadcasted_iota(jnp.int32, (tm, tm), 1)
    strict = jnp.where(ri > ci, 1.0, 0.0).astype(BF16)
    before = _dot(strict, picked.astype(BF16)) + car_ref[0:1, :]
    idx_out = jnp.zeros((tm, LANES), jnp.int32)
    rank_out = jnp.zeros((tm, LANES), jnp.int32)
    gate_out = jnp.zeros((tm, LANES), F32)
    for k in range(TOP_K):
        rk = jnp.sum(jnp.where(lane == idxs[k], before, 0.0), axis=-1, keepdims=True)
        at_k = lane == k
        idx_out = jnp.where(at_k, idxs[k], idx_out)
        rank_out = jnp.where(at_k, rk.astype(jnp.int32), rank_out)
        gate_out = jnp.where(at_k, ex[k] / den, gate_out)
    idx_ref[...] = idx_out
    rank_ref[...] = rank_out
    gate_ref[...] = gate_out
    total = car_ref[...] + jnp.sum(picked, axis=0, keepdims=True)
    car_ref[...] = total
    cnt_ref[...] = total.astype(jnp.int32)


def _mid(x2, ys, yl, kk, vv, wo1, wo2, gx, wq, wo, gm, wr_hi, wr_lo, br, seq):
    t, d = x2.shape
    tm = MID_TILE
    m = kk.shape[1]
    per_b = seq // tm
    row = lambda i: (i, 0)
    const = lambda i: (0, 0)
    kvmap = lambda i: (i // per_b, 0, 0)
    wspec = pl.BlockSpec((d, d), const)
    vspec = pl.BlockSpec((1, d), const)
    return pl.pallas_call(
        _mid_kernel,
        grid=(t // tm,),
        in_specs=[
            pl.BlockSpec((tm, d), row), pl.BlockSpec((tm, d), row), pl.BlockSpec((tm, d), row),
            pl.BlockSpec((1, m, d), kvmap), pl.BlockSpec((1, m, d), kvmap),
            wspec, wspec, vspec, wspec, wspec, vspec,
            pl.BlockSpec((d, LANES), const), pl.BlockSpec((d, LANES), const), pl.BlockSpec((1, LANES), const),
        ],
        out_specs=[
            pl.BlockSpec((tm, d), row), pl.BlockSpec((tm * d // LANES, LANES), row),
            pl.BlockSpec((tm, LANES), row), pl.BlockSpec((tm, LANES), row), pl.BlockSpec((tm, LANES), row),
            pl.BlockSpec((SUBLANES, LANES), const),
        ],
        out_shape=[
            jax.ShapeDtypeStruct((t, d), F32), jax.ShapeDtypeStruct((t * d // LANES, LANES), F32),
            jax.ShapeDtypeStruct((t, LANES), jnp.int32), jax.ShapeDtypeStruct((t, LANES), jnp.int32),
            jax.ShapeDtypeStruct((t, LANES), F32),
            jax.ShapeDtypeStruct((SUBLANES, LANES), jnp.int32),
        ],
        scratch_shapes=[pltpu.VMEM((SUBLANES, LANES), F32)],
        compiler_params=_cparams(("arbitrary",)),
        name="mid",
    )(x2, ys, yl, kk, vv, wo1, wo2, gx, wq, wo, gm, wr_hi, wr_lo, br)


def _dispatch_kernel(pend_ref, dst_ref, hn_ref, x_hbm, zero_ref, sem, *, parts):
    i = pl.program_id(0)
    tc = hn_ref.shape[0] // parts
    bm = zero_ref.shape[0] // parts
    rows = TOP_K * tc

    def tile(ref, n, count=1):
        return ref.at[pl.ds(pl.multiple_of(n * parts, parts), count * parts), :]

    @pl.when(i == 0)
    def _():
        zero_ref[...] = jnp.zeros(zero_ref.shape, F32)

        def fill(start):
            return pltpu.make_async_copy(zero_ref, tile(x_hbm, start, bm), sem.at[1])

        for e in range(N_EXPERTS):
            fill(jnp.maximum(pend_ref[e] - bm, 0)).start()
        for e in range(N_EXPERTS):
            fill(0).wait()

        def tail(j, c):
            fill(j * bm).start()
            fill(0).wait()
            return c

        lax.fori_loop(pend_ref[N_EXPERTS - 1] // bm, x_hbm.shape[0] // (bm * parts), tail, 0)

    def body(r, c):
        pltpu.make_async_copy(tile(hn_ref, lax.rem(r, tc)), tile(x_hbm, dst_ref[0, 0, r]), sem.at[0]).start()
        return c

    lax.fori_loop(0, rows, body, 0, unroll=8)
    pltpu.make_async_copy(tile(x_hbm, 0, rows), tile(x_hbm, 0, rows), sem.at[0]).wait()


def _dispatch(pend, dest3, hn_tm, n_rows, d):
    parts = d // LANES
    t = hn_tm.shape[0] // parts
    tc = COMB_TILE
    grid_spec = pltpu.PrefetchScalarGridSpec(
        num_scalar_prefetch=1,
        grid=(t // tc,),
        in_specs=[
            pl.BlockSpec((1, 1, TOP_K * tc), lambda i, pe: (i, 0, 0), memory_space=pltpu.SMEM),
            pl.BlockSpec((tc * parts, LANES), lambda i, pe: (i, 0)),
        ],
        out_specs=pl.BlockSpec(memory_space=pl.ANY),
        scratch_shapes=[pltpu.VMEM((MOE_BLOCK * parts, LANES), F32), pltpu.SemaphoreType.DMA((2,))],
    )
    return pl.pallas_call(
        functools.partial(_dispatch_kernel, parts=parts),
        grid_spec=grid_spec,
        out_shape=jax.ShapeDtypeStruct((n_rows * parts, LANES), F32),
        compiler_params=_cparams(("arbitrary",)),
        name="dispatch",
    )(pend, dest3, hn_tm)


def _moe_kernel(be_ref, nvb_ref, x_ref, wgu_ref, bg_ref, bu_ref, wd_ref, bd_ref, perm_ref,
                y_ref, wgu_s, wd_s, act_s):
    i = pl.program_id(0)
    nvb = nvb_ref[0]
    f, d = wd_ref.shape[1], wd_ref.shape[2]
    bm = act_s.shape[0]
    pw = perm_ref.shape[0]
    half = pw // 2

    changed = jnp.logical_or(i == 0, be_ref[i] != be_ref[jnp.maximum(i - 1, 0)])

    @pl.when(jnp.logical_and(changed, i < nvb))
    def _():
        for c in range(2 * f // pw):
            wc = wgu_ref[0, :, c * pw:(c + 1) * pw].astype(BF16)
            wgu_s[:, c * pw:(c + 1) * pw] = _dot(wc, perm_ref[...]).astype(BF16)
        wd_s[...] = wd_ref[0].astype(BF16)

    @pl.when(i < nvb)
    def _():
        xb = jnp.concatenate([p.astype(BF16) for p in _load_token_major(x_ref, bm, d)], axis=1)
        for c in range(2 * f // pw):
            gu = _dot(xb, wgu_s[:, c * pw:(c + 1) * pw])
            g = gu[:, :half] + bg_ref[0, :, c * half:(c + 1) * half]
            u = gu[:, half:] + bu_ref[0, :, c * half:(c + 1) * half]
            g = jnp.minimum(g, SWIGLU_LIMIT)
            u = jnp.clip(u, -SWIGLU_LIMIT, SWIGLU_LIMIT)
            act = (u + 1.0) * (g * _sigmoid(SWIGLU_ALPHA * g))
            act_s[:, c * half:(c + 1) * half] = act.astype(BF16)
        _store_token_major(y_ref, _dot(act_s[...], wd_s[...]) + bd_ref[0])

    @pl.when(i >= nvb)
    def _():
        y_ref[...] = jnp.zeros(y_ref.shape, F32)


def _moe(block_e, nvb, xrows_tm, wgu, bg, bu, wd, bd, perm):
    f, d = wd.shape[1], wd.shape[2]
    parts = d // LANES
    bm = MOE_BLOCK
    nb = xrows_tm.shape[0] // (bm * parts)
    emap = lambda i, be, nv: (be[i], 0, 0)
    grid_spec = pltpu.PrefetchScalarGridSpec(
        num_scalar_prefetch=2,
        grid=(nb,),
        in_specs=[
            pl.BlockSpec((bm * parts, LANES), lambda i, be, nv: (jnp.minimum(i, jnp.maximum(nv[0] - 1, 0)), 0)),
            pl.BlockSpec((1, d, 2 * f), emap),
            pl.BlockSpec((1, 1, f), emap), pl.BlockSpec((1, 1, f), emap),
            pl.BlockSpec((1, f, d), emap), pl.BlockSpec((1, 1, d), emap),
            pl.BlockSpec(perm.shape, lambda i, be, nv: (0, 0)),
        ],
        out_specs=pl.BlockSpec((bm * parts, LANES), lambda i, be, nv: (i, 0)),
        scratch_shapes=[pltpu.VMEM((d, 2 * f), BF16), pltpu.VMEM((f, d), BF16), pltpu.VMEM((bm, f), BF16)],
    )
    return pl.pallas_call(
        _moe_kernel,
        grid_spec=grid_spec,
        out_shape=jax.ShapeDtypeStruct(xrows_tm.shape, F32),
        compiler_params=_cparams(("arbitrary",)),
        name="moe",
    )(block_e, nvb, xrows_tm, wgu, bg, bu, wd, bd, perm)


def _comb_kernel(dst_ref, dstn_ref, h_ref, gate_ref, y_hbm, g_ref, o_ref, ybuf, sem, *, parts):
    i = pl.program_id(0)
    n = pl.num_programs(0)
    tc, d = h_ref.shape
    rows = TOP_K * tc
    slot = i % 2

    def tile(ref, n, count=1):
        return ref.at[pl.ds(pl.multiple_of(n * parts, parts), count * parts), :]

    def start_rows(idx_ref, s):
        def body(r, c):
            pltpu.make_async_copy(tile(y_hbm, idx_ref[0, 0, r]), tile(ybuf.at[s], r), sem.at[s]).start()
            return c

        lax.fori_loop(0, rows, body, 0, unroll=8)

    @pl.when(i == 0)
    def _():
        start_rows(dst_ref, 0)

    @pl.when(i + 1 < n)
    def _():
        start_rows(dstn_ref, 1 - slot)

    pltpu.make_async_copy(tile(y_hbm, 0, rows), ybuf.at[slot], sem.at[slot]).wait()
    acc = [h_ref[:, s * LANES:(s + 1) * LANES] for s in range(parts)]
    for k in range(TOP_K):
        yk = _load_token_major(ybuf.at[slot], tc, d, base=k * tc * parts)
        gk = gate_ref[:, k:k + 1]
        acc = [a + p * gk for a, p in zip(acc, yk)]
    o_ref[...] = _rms(jnp.concatenate(acc, axis=1), g_ref[...])


def _combine(dest3, h2, gate, y_tm, g):
    t, d = h2.shape
    parts = d // LANES
    tc = COMB_TILE
    nt = t // tc
    return pl.pallas_call(
        functools.partial(_comb_kernel, parts=parts),
        grid=(nt,),
        in_specs=[
            pl.BlockSpec((1, 1, TOP_K * tc), lambda i: (i, 0, 0), memory_space=pltpu.SMEM),
            pl.BlockSpec((1, 1, TOP_K * tc), lambda i: (jnp.minimum(i + 1, nt - 1), 0, 0),
                         memory_space=pltpu.SMEM),
            pl.BlockSpec((tc, d), lambda i: (i, 0)),
            pl.BlockSpec((tc, LANES), lambda i: (i, 0)),
            pl.BlockSpec(memory_space=pl.ANY),
            pl.BlockSpec((1, d), lambda i: (0, 0)),
        ],
        out_specs=pl.BlockSpec((tc, d), lambda i: (i, 0)),
        out_shape=jax.ShapeDtypeStruct((t, d), F32),
        scratch_shapes=[pltpu.VMEM((2, TOP_K * tc * parts, LANES), F32), pltpu.SemaphoreType.DMA((2,))],
        compiler_params=_cparams(("arbitrary",)),
        name="combine",
    )(dest3, dest3, h2, gate, y_tm, g)


def _block_diag(wb, per):
    nb, bw, _ = wb.shape
    wq = wb.reshape(nb // per, per, bw, bw)
    eye = jnp.eye(per, dtype=wb.dtype)
    out = jnp.einsum('qaij,ab->qaibj', wq, eye)
    return out.reshape(nb // per, per * bw, per * bw)


def _pad_lanes(v, fill=0.0):
    return jnp.pad(v, (0, LANES - v.shape[0]), constant_values=fill).reshape(1, LANES)


def kernel(x, mem, norm_mix, w_in, ssd_conv_w, ssd_conv_b, ssd_dt_bias, ssd_a_log, ssd_d, ssd_norm, lru_conv_w, lru_conv_b, lru_wa, lru_ba, lru_wx, lru_bx, lru_lambda, w_out, norm_xattn, norm_mem, w_q, w_kv, w_o, norm_moe, w_router, b_router, w_gate_up, b_gate_up, w_down, b_down, norm_final):
    b, s, d = x.shape
    t = b * s
    n_mem = mem.shape[1]
    w = d
    cdim = w + 2 * SSD_GROUPS * SSD_STATE
    o1, o2, o3, o4 = w, w + cdim, w + cdim + SSD_HEADS, w + cdim + SSD_HEADS + w

    wi = w_in[0]
    wzx = wi[:, :o2].astype(BF16)
    wdt = jnp.pad(wi[:, o2:o3], ((0, 0), (0, LANES - SSD_HEADS)))
    wdt_hi, wdt_lo = _hilo(wdt)
    wxr = wi[:, o3:o4].astype(BF16)
    wgr = wi[:, o4:].astype(BF16)
    e01 = (jnp.arange(LANES)[:, None] == (jnp.arange(w)[None, :] // SSD_HEAD_DIM)).astype(BF16)
    dskip = jnp.repeat(ssd_d[0], SSD_HEAD_DIM).reshape(1, w)
    per = 256 // (w // LRU_BLOCKS)
    wa_bd = _block_diag(lru_wa[0], per).astype(BF16)
    wx_bd = _block_diag(lru_wx[0], per).astype(BF16)
    wr = jnp.pad(w_router[0], ((0, 0), (0, LANES - N_EXPERTS)))
    wr_hi, wr_lo = _hilo(wr)
    br = _pad_lanes(b_router[0], fill=-1e30)
    bgu = b_gate_up[0]
    bg = bgu[:, None, 0::2]
    bu = bgu[:, None, 1::2]
    bd = b_down[0][:, None, :]
    pw = 2 * LANES
    col = jnp.arange(pw)
    src_col = jnp.where(col < LANES, 2 * col, 2 * (col - LANES) + 1)
    perm = (jnp.arange(pw)[:, None] == src_col[None, :]).astype(BF16)

    x2 = x.reshape(t, d)
    z, xbc, dt, xr, gr = _in_proj(x2, norm_mix[0].reshape(1, d), wzx, wdt_hi, wdt_lo, wxr, wgr)

    y_ssd = _ssd(xbc.reshape(b, s, cdim), dt.reshape(b, s, LANES), z.reshape(b, s, w),
                 ssd_conv_w[0], ssd_conv_b[0].reshape(1, cdim), _pad_lanes(ssd_dt_bias[0]),
                 _pad_lanes(ssd_a_log[0]), dskip, ssd_norm[0].reshape(1, w), e01)
    y_lru = _lru(xr.reshape(b, s, w), gr.reshape(b, s, w), lru_conv_w[0], lru_conv_b[0].reshape(1, w),
                 wa_bd, lru_ba[0].reshape(1, w), wx_bd, lru_bx[0].reshape(1, w), lru_lambda[0].reshape(1, w))

    kk, vv = _kv(mem.reshape(b * n_mem, d), norm_mem[0].reshape(1, d), w_kv[0].astype(BF16))
    wo_mix = w_out[0].astype(BF16)
    h2, hn, idx_m, rank_m, gate_m, cnt = _mid(
        x2, y_ssd.reshape(t, w), y_lru.reshape(t, w), kk.reshape(b, n_mem, d), vv.reshape(b, n_mem, d),
        wo_mix[:w], wo_mix[w:], norm_xattn[0].reshape(1, d), w_q[0].astype(BF16), w_o[0].astype(BF16),
        norm_moe[0].reshape(1, d), wr_hi, wr_lo, br, s)

    bm = MOE_BLOCK
    counts = cnt[0, :N_EXPERTS]
    pcounts = (counts + bm - 1) // bm * bm
    pend = jnp.cumsum(pcounts).astype(jnp.int32)
    pstart = pend - pcounts
    idx = idx_m[:, :TOP_K]
    onehot = idx[:, :, None] == jnp.arange(N_EXPERTS, dtype=jnp.int32)[None, None, :]
    dest = jnp.sum(jnp.where(onehot, pstart[None, None, :], 0), axis=-1) + rank_m[:, :TOP_K]
    n_pairs = t * TOP_K
    nb = (n_pairs + N_EXPERTS * (bm - 1) + bm - 1) // bm
    blk0 = jnp.arange(nb, dtype=jnp.int32) * bm
    block_e = jnp.minimum(jnp.sum(pend[None, :] <= blk0[:, None], axis=1), N_EXPERTS - 1).astype(jnp.int32)
    nvb = (pend[-1] // bm).astype(jnp.int32).reshape(1)
    tc = COMB_TILE
    dest3 = dest.astype(jnp.int32).reshape(t // tc, tc, TOP_K).transpose(0, 2, 1).reshape(t // tc, 1, TOP_K * tc)

    xrows = _dispatch(pend, dest3, hn, nb * bm, d)
    y = _moe(block_e, nvb, xrows, w_gate_up[0], bg, bu, w_down[0], bd, perm)
    out = _combine(dest3, h2, gate_m, y, norm_final.reshape(1, d))
    return out.reshape(b, s, d)
```

```python
import functools

import jax
import jax.numpy as jnp
from jax import lax
from jax.experimental import pallas as pl
from jax.experimental.pallas import tpu as pltpu

F32 = jnp.float32
BF16 = jnp.bfloat16

NORM_EPS = 1e-6
LANES = 128
SUBLANES = 8
SSD_HEAD_DIM = 64
SSD_HEADS = 16
SSD_GROUPS = 4
SSD_STATE = 128
CONV_K = 4
LRU_BLOCKS = 16
RG_C = 8.0
X_HEADS = 4
N_EXPERTS = 32
TOP_K = 4
SWIGLU_LIMIT = 7.0
SWIGLU_ALPHA = 1.702

VMEM_LIMIT = 56 * 1024 * 1024

IN_TILE = 1024
SSD_CHUNK = 256
LRU_TILE = 256
MID_TILE = 512
MOE_BLOCK = 512
COMB_TILE = 256
ISSUE_UNROLL = 8


def _cparams(sem):
    return pltpu.CompilerParams(dimension_semantics=sem, vmem_limit_bytes=VMEM_LIMIT)


def _rms(x, g):
    ms = jnp.mean(x * x, axis=-1, keepdims=True)
    return x * lax.rsqrt(ms + NORM_EPS) * g


def _sigmoid(x):
    return 1.0 / (1.0 + jnp.exp(-x))


def _softplus(x):
    return jnp.maximum(x, 0.0) + jnp.log(1.0 + jnp.exp(-jnp.abs(x)))


def _split3(x):
    a = x.astype(BF16)
    r = x - a.astype(F32)
    b = r.astype(BF16)
    c = (r - b.astype(F32)).astype(BF16)
    return a, b, c


def _dot(a, b):
    return jnp.dot(a, b, preferred_element_type=F32)


def _dot_nt(a, b):
    return lax.dot_general(a, b, (((1,), (1,)), ((), ())), preferred_element_type=F32)


def _dot01_right(x, m01):
    a, b, c = _split3(x)
    return _dot(a, m01) + _dot(b, m01) + _dot(c, m01)


def _dot01_left(m01, x):
    a, b, c = _split3(x)
    return _dot(m01, a) + _dot(m01, b) + _dot(m01, c)


def _dot_hilo(x, w_hi, w_lo):
    xh = x.astype(BF16)
    xl = (x - xh.astype(F32)).astype(BF16)
    return _dot(xh, w_hi) + _dot(xl, w_hi) + _dot(xh, w_lo)


def _hilo(w):
    hi = w.astype(BF16)
    lo = (w - hi.astype(F32)).astype(BF16)
    return hi, lo


def _store_token_major(ref, val, base=0):
    n, d = val.shape
    parts = d // LANES
    for s in range(parts):
        ref[pl.ds(base + s, n, stride=parts), :] = val[:, s * LANES:(s + 1) * LANES]


def _load_token_major(ref, n, d, base=0):
    parts = d // LANES
    return [ref[pl.ds(base + s, n, stride=parts), :] for s in range(parts)]


def _in_proj_kernel(x_ref, g_ref, wzx_ref, wdth_ref, wdtl_ref, wxr_ref, wgr_ref,
                    z_ref, xbc_ref, dt_ref, xr_ref, gr_ref):
    d = x_ref.shape[1]
    hn = _rms(x_ref[...], g_ref[...])
    hb = hn.astype(BF16)
    z_ref[...] = _dot(hb, wzx_ref[:, :d]).astype(BF16)
    xbc_ref[...] = _dot(hb, wzx_ref[:, d:]).astype(BF16)
    dt_ref[...] = _dot_hilo(hn, wdth_ref[...], wdtl_ref[...])
    xr_ref[...] = _dot(hb, wxr_ref[...]).astype(BF16)
    gr_ref[...] = _dot(hb, wgr_ref[...]).astype(BF16)


def _in_proj(x2, g, wzx, wdt_hi, wdt_lo, wxr, wgr):
    t, d = x2.shape
    tm = IN_TILE
    nzx = wzx.shape[1]
    const = lambda i: (0, 0)
    row = lambda i: (i, 0)
    return pl.pallas_call(
        _in_proj_kernel,
        grid=(t // tm,),
        in_specs=[
            pl.BlockSpec((tm, d), row),
            pl.BlockSpec((1, d), const),
            pl.BlockSpec((d, nzx), const),
            pl.BlockSpec((d, LANES), const),
            pl.BlockSpec((d, LANES), const),
            pl.BlockSpec((d, d), const),
            pl.BlockSpec((d, d), const),
        ],
        out_specs=[
            pl.BlockSpec((tm, d), row),
            pl.BlockSpec((tm, nzx - d), row),
            pl.BlockSpec((tm, LANES), row),
            pl.BlockSpec((tm, d), row),
            pl.BlockSpec((tm, d), row),
        ],
        out_shape=[
            jax.ShapeDtypeStruct((t, d), BF16),
            jax.ShapeDtypeStruct((t, nzx - d), BF16),
            jax.ShapeDtypeStruct((t, LANES), F32),
            jax.ShapeDtypeStruct((t, d), BF16),
            jax.ShapeDtypeStruct((t, d), BF16),
        ],
        compiler_params=_cparams(("arbitrary",)),
        name="in_proj",
    )(x2, g, wzx, wdt_hi, wdt_lo, wxr, wgr)


def _causal_conv(ext_ref, x_f32, w_ref, b_ref, first):
    n = x_f32.shape[0]
    pad = SUBLANES

    @pl.when(first)
    def _():
        ext_ref[0:pad, :] = jnp.zeros((pad, ext_ref.shape[1]), F32)

    ext_ref[pad:pad + n, :] = x_f32
    acc = b_ref[...] + w_ref[0:1, :] * ext_ref[pad - 3:pad - 3 + n, :]
    for j in range(1, CONV_K):
        acc = acc + w_ref[j:j + 1, :] * ext_ref[pad - 3 + j:pad - 3 + j + n, :]
    ext_ref[0:pad, :] = ext_ref[n:n + pad, :]
    return acc


def _ssd_kernel(xbc_ref, dt_ref, z_ref, cw_ref, cb_ref, dtb_ref, alog_ref, dskip_ref, gn_ref, e_ref,
                y_ref, ext_ref, st_ref):
    n = xbc_ref.shape[1]
    w = z_ref.shape[2]
    gw = w // SSD_GROUPS
    first = pl.program_id(1) == 0

    @pl.when(first)
    def _():
        st_ref[...] = jnp.zeros(st_ref.shape, F32)

    conv = _causal_conv(ext_ref, xbc_ref[0].astype(F32), cw_ref, cb_ref, first)
    xc = conv * _sigmoid(conv)
    xs = xc[:, :w]

    dt = _softplus(dt_ref[0] + dtb_ref[...])
    a = -jnp.exp(alog_ref[...])
    da = dt * a
    ri = lax.broadcasted_iota(jnp.int32, (n, n), 0)
    ci = lax.broadcasted_iota(jnp.int32, (n, n), 1)
    causal = ri >= ci
    tril = jnp.where(causal, 1.0, 0.0).astype(BF16)
    a_cs = _dot01_left(tril, da)
    a_cs_t = a_cs.T

    e01 = e_ref[...]
    dt_x = _dot01_right(dt, e01)
    acs_x = _dot01_right(a_cs, e01)
    last_x = acs_x[n - 1:n, :]
    xdt = xs * dt_x
    xdt_b = xdt.astype(BF16)
    xdt_end = (xdt * jnp.exp(last_x - acs_x)).astype(BF16)
    exp_acs = jnp.exp(acs_x)
    chunk_decay = jnp.exp(last_x)
    lane = lax.broadcasted_iota(jnp.int32, (n, gw), 1)

    for g in range(SSD_GROUPS):
        lo = g * gw
        bg = xc[:, w + g * SSD_STATE:w + (g + 1) * SSD_STATE].astype(BF16)
        cg = xc[:, w + (SSD_GROUPS + g) * SSD_STATE:w + (SSD_GROUPS + g + 1) * SSD_STATE].astype(BF16)
        cb = _dot_nt(cg, bg)
        prev = st_ref[g]
        acc = _dot(cg, prev.astype(BF16)) * exp_acs[:, lo:lo + gw]
        new = lax.dot_general(bg, xdt_end[:, lo:lo + gw], (((0,), (0,)), ((), ())),
                              preferred_element_type=F32)
        st_ref[g] = chunk_decay[:, lo:lo + gw] * prev + new
        xg = xdt_b[:, lo:lo + gw]
        for k in range(SSD_HEADS // SSD_GROUPS):
            h = g * (SSD_HEADS // SSD_GROUPS) + k
            seg = a_cs[:, h:h + 1] - a_cs_t[h:h + 1, :]
            dec = jnp.exp(jnp.where(causal, seg, -jnp.inf))
            m = (cb * dec).astype(BF16)
            in_head = (lane >= k * SSD_HEAD_DIM) & (lane < (k + 1) * SSD_HEAD_DIM)
            acc = acc + _dot(m, jnp.where(in_head, xg, jnp.zeros_like(xg)))
        yg = acc + xs[:, lo:lo + gw] * dskip_ref[:, lo:lo + gw]
        zg = z_ref[0, :, lo:lo + gw].astype(F32)
        u = yg * (zg * _sigmoid(zg))
        u = u * lax.rsqrt(jnp.mean(u * u, axis=-1, keepdims=True) + NORM_EPS)
        y_ref[0, :, lo:lo + gw] = (u * gn_ref[:, lo:lo + gw]).astype(BF16)


def _ssd(xbc, dt, z, cw, cb, dtb, alog, dskip, gn, e01):
    b, s, cdim = xbc.shape
    w = z.shape[2]
    n = SSD_CHUNK
    tile = lambda i, j: (i, j, 0)
    const = lambda i, j: (0, 0)
    return pl.pallas_call(
        _ssd_kernel,
        grid=(b, s // n),
        in_specs=[
            pl.BlockSpec((1, n, cdim), tile),
            pl.BlockSpec((1, n, LANES), tile),
            pl.BlockSpec((1, n, w), tile),
            pl.BlockSpec((CONV_K, cdim), const),
            pl.BlockSpec((1, cdim), const),
            pl.BlockSpec((1, LANES), const),
            pl.BlockSpec((1, LANES), const),
            pl.BlockSpec((1, w), const),
            pl.BlockSpec((1, w), const),
            pl.BlockSpec((LANES, w), const),
        ],
        out_specs=pl.BlockSpec((1, n, w), tile),
        out_shape=jax.ShapeDtypeStruct((b, s, w), BF16),
        scratch_shapes=[
            pltpu.VMEM((n + SUBLANES, cdim), F32),
            pltpu.VMEM((SSD_GROUPS, SSD_STATE, w // SSD_GROUPS), F32),
        ],
        compiler_params=_cparams(("arbitrary", "arbitrary")),
        name="ssd",
    )(xbc, dt, z, cw, cb, dtb, alog, dskip, gn, e01)


def _gelu_tanh(x):
    c = 0.7978845608028654
    return 0.5 * x * (1.0 + jnp.tanh(c * (x + 0.044715 * (x * x * x))))


def _lru_kernel(xr_ref, gr_ref, cw_ref, cb_ref, wa_ref, ba_ref, wx_ref, bx_ref, lam_ref,
                y_ref, ext_ref, car_ref, h_ref):
    n = xr_ref.shape[1]
    w = xr_ref.shape[2]
    first = pl.program_id(1) == 0

    @pl.when(first)
    def _():
        car_ref[...] = jnp.zeros(car_ref.shape, F32)

    xc = _causal_conv(ext_ref, xr_ref[0].astype(F32), cw_ref, cb_ref, first)
    xb = xc.astype(BF16)
    nq = wa_ref.shape[0]
    qw = w // nq
    r_parts, i_parts = [], []
    for q in range(nq):
        xq = xb[:, q * qw:(q + 1) * qw]
        r_parts.append(_dot(xq, wa_ref[q]))
        i_parts.append(_dot(xq, wx_ref[q]))
    r = _sigmoid(jnp.concatenate(r_parts, axis=1) + ba_ref[...])
    gi = _sigmoid(jnp.concatenate(i_parts, axis=1) + bx_ref[...])
    log_a = (-RG_C) * r * _softplus(-lam_ref[...])
    a = jnp.exp(log_a)
    u = xc * gi * jnp.sqrt(1.0 - a * a)

    row = lax.broadcasted_iota(jnp.int32, (n, w), 0) & (SUBLANES - 1)
    ap, bp = a, u
    for d in (1, 2, 4):
        a_s = pltpu.roll(ap, d, 0)
        b_s = pltpu.roll(bp, d, 0)
        m = row >= d
        bp = jnp.where(m, ap * b_s + bp, bp)
        ap = jnp.where(m, ap * a_s, ap)
    carry = car_ref[...]
    for g in range(n // SUBLANES):
        sl = slice(g * SUBLANES, (g + 1) * SUBLANES)
        hb = bp[sl] + ap[sl] * carry
        h_ref[sl, :] = hb
        carry = jnp.broadcast_to(hb[SUBLANES - 1:SUBLANES, :], (SUBLANES, w))
    car_ref[...] = carry
    y_ref[0] = (h_ref[...] * _gelu_tanh(gr_ref[0].astype(F32))).astype(BF16)


def _lru(xr, gr, cw, cb, wa, ba, wx, bx, lam):
    b, s, w = xr.shape
    n = LRU_TILE
    nq, qw, _ = wa.shape
    tile = lambda i, j: (i, j, 0)
    const = lambda i, j: (0, 0)
    const3 = lambda i, j: (0, 0, 0)
    return pl.pallas_call(
        _lru_kernel,
        grid=(b, s // n),
        in_specs=[
            pl.BlockSpec((1, n, w), tile),
            pl.BlockSpec((1, n, w), tile),
            pl.BlockSpec((CONV_K, w), const),
            pl.BlockSpec((1, w), const),
            pl.BlockSpec((nq, qw, qw), const3),
            pl.BlockSpec((1, w), const),
            pl.BlockSpec((nq, qw, qw), const3),
            pl.BlockSpec((1, w), const),
            pl.BlockSpec((1, w), const),
        ],
        out_specs=pl.BlockSpec((1, n, w), tile),
        out_shape=jax.ShapeDtypeStruct((b, s, w), BF16),
        scratch_shapes=[
            pltpu.VMEM((n + SUBLANES, w), F32),
            pltpu.VMEM((SUBLANES, w), F32),
            pltpu.VMEM((n, w), F32),
        ],
        compiler_params=_cparams(("arbitrary", "arbitrary")),
        name="lru",
    )(xr, gr, cw, cb, wa, ba, wx, bx, lam)


def _kv_kernel(m_ref, g_ref, w_ref, k_ref, v_ref):
    d = m_ref.shape[1]
    mn = _rms(m_ref[...], g_ref[...]).astype(BF16)
    k_ref[...] = _dot(mn, w_ref[:, :d]).astype(BF16)
    v_ref[...] = _dot(mn, w_ref[:, d:]).astype(BF16)


def _kv(mem2, g, wkv):
    t, d = mem2.shape
    tm = min(t, 512)
    row = lambda i: (i, 0)
    const = lambda i: (0, 0)
    return pl.pallas_call(
        _kv_kernel,
        grid=(t // tm,),
        in_specs=[pl.BlockSpec((tm, d), row), pl.BlockSpec((1, d), const), pl.BlockSpec((d, 2 * d), const)],
        out_specs=[pl.BlockSpec((tm, d), row), pl.BlockSpec((tm, d), row)],
        out_shape=[jax.ShapeDtypeStruct((t, d), BF16), jax.ShapeDtypeStruct((t, d), BF16)],
        compiler_params=_cparams(("arbitrary",)),
        name="kv",
    )(mem2, g, wkv)


def _mid_kernel(x_ref, ys_ref, yl_ref, k_ref, v_ref, wo1_ref, wo2_ref, gx_ref, wq_ref, wo_ref,
                gm_ref, wrh_ref, wrl_ref, br_ref,
                h_ref, hn_ref, idx_ref, rank_ref, gate_ref, cnt_ref, car_ref):
    tm, d = x_ref.shape
    hd = d // X_HEADS

    @pl.when(pl.program_id(0) == 0)
    def _():
        car_ref[...] = jnp.zeros(car_ref.shape, F32)

    h1 = x_ref[...] + _dot(ys_ref[...], wo1_ref[...]) + _dot(yl_ref[...], wo2_ref[...])

    q = _dot(_rms(h1, gx_ref[...]).astype(BF16), wq_ref[...]).astype(BF16)
    o_parts = []
    for hh in range(X_HEADS):
        sl = slice(hh * hd, (hh + 1) * hd)
        sc = _dot_nt(q[:, sl], k_ref[0, :, sl]) * (hd ** -0.5)
        sc = sc - jnp.max(sc, axis=-1, keepdims=True)
        p = jnp.exp(sc)
        p = p / jnp.sum(p, axis=-1, keepdims=True)
        o_parts.append(_dot(p.astype(BF16), v_ref[0, :, sl]))
    o = jnp.concatenate(o_parts, axis=1).astype(BF16)
    h2 = h1 + _dot(o, wo_ref[...])
    h_ref[...] = h2

    hn = _rms(h2, gm_ref[...])
    _store_token_major(hn_ref, hn)
    logits = _dot_hilo(hn, wrh_ref[...], wrl_ref[...]) + br_ref[...]

    lane = lax.broadcasted_iota(jnp.int32, (tm, LANES), 1)
    picked = jnp.zeros((tm, LANES), F32)
    vals, idxs = [], []
    l = logits
    for _ in range(TOP_K):
        m = jnp.max(l, axis=-1, keepdims=True)
        idx = jnp.min(jnp.where(l == m, lane, LANES), axis=-1, keepdims=True)
        sel = lane == idx
        vals.append(m)
        idxs.append(idx)
        picked = jnp.where(sel, 1.0, picked)
        l = jnp.where(sel, -jnp.inf, l)
    ex = [jnp.exp(v - vals[0]) for v in vals]
    den = ex[0] + ex[1] + ex[2] + ex[3]

    ri = lax.broadcasted_iota(jnp.int32, (tm, tm), 0)
    ci = lax.broadcasted_iota(jnp.int32, (tm, tm), 1)
    strict = jnp.where(ri > ci, 1.0, 0.0).astype(BF16)
    before = _dot(strict, picked.astype(BF16)) + car_ref[0:1, :]
    idx_out = jnp.zeros((tm, LANES), jnp.int32)
    rank_out = jnp.zeros((tm, LANES), jnp.int32)
    gate_out = jnp.zeros((tm, LANES), F32)
    for k in range(TOP_K):
        rk = jnp.sum(jnp.where(lane == idxs[k], before, 0.0), axis=-1, keepdims=True)
        at_k = lane == k
        idx_out = jnp.where(at_k, idxs[k], idx_out)
        rank_out = jnp.where(at_k, rk.astype(jnp.int32), rank_out)
        gate_out = jnp.where(at_k, ex[k] / den, gate_out)
    idx_ref[...] = idx_out
    rank_ref[...] = rank_out
    gate_ref[...] = gate_out
    total = car_ref[...] + jnp.sum(picked, axis=0, keepdims=True)
    car_ref[...] = total
    cnt_ref[...] = total.astype(jnp.int32)


def _mid(x2, ys, yl, kk, vv, wo1, wo2, gx, wq, wo, gm, wr_hi, wr_lo, br, seq):
    t, d = x2.shape
    tm = MID_TILE
    m = kk.shape[1]
    per_b = seq // tm
    row = lambda i: (i, 0)
    const = lambda i: (0, 0)
    kvmap = lambda i: (i // per_b, 0, 0)
    wspec = pl.BlockSpec((d, d), const)
    vspec = pl.BlockSpec((1, d), const)
    return pl.pallas_call(
        _mid_kernel,
        grid=(t // tm,),
        in_specs=[
            pl.BlockSpec((tm, d), row), pl.BlockSpec((tm, d), row), pl.BlockSpec((tm, d), row),
            pl.BlockSpec((1, m, d), kvmap), pl.BlockSpec((1, m, d), kvmap),
            wspec, wspec, vspec, wspec, wspec, vspec,
            pl.BlockSpec((d, LANES), const), pl.BlockSpec((d, LANES), const), pl.BlockSpec((1, LANES), const),
        ],
        out_specs=[
            pl.BlockSpec((tm, d), row), pl.BlockSpec((tm * d // LANES, LANES), row),
            pl.BlockSpec((tm, LANES), row), pl.BlockSpec((tm, LANES), row), pl.BlockSpec((tm, LANES), row),
            pl.BlockSpec((SUBLANES, LANES), const),
        ],
        out_shape=[
            jax.ShapeDtypeStruct((t, d), F32), jax.ShapeDtypeStruct((t * d // LANES, LANES), F32),
            jax.ShapeDtypeStruct((t, LANES), jnp.int32), jax.ShapeDtypeStruct((t, LANES), jnp.int32),
            jax.ShapeDtypeStruct((t, LANES), F32),
            jax.ShapeDtypeStruct((SUBLANES, LANES), jnp.int32),
        ],
        scratch_shapes=[pltpu.VMEM((SUBLANES, LANES), F32)],
        compiler_params=_cparams(("arbitrary",)),
        name="mid",
    )(x2, ys, yl, kk, vv, wo1, wo2, gx, wq, wo, gm, wr_hi, wr_lo, br)


def _dispatch_kernel(pend_ref, dst_ref, hn_ref, x_hbm, zero_ref, sem, *, parts):
    i = pl.program_id(0)
    tc = hn_ref.shape[0] // parts
    bm = zero_ref.shape[0] // parts
    rows = TOP_K * tc

    def tile(ref, n, count=1):
        return ref.at[pl.ds(pl.multiple_of(n * parts, parts), count * parts), :]

    @pl.when(i == 0)
    def _():
        zero_ref[...] = jnp.zeros(zero_ref.shape, F32)

        def fill(start):
            return pltpu.make_async_copy(zero_ref, tile(x_hbm, start, bm), sem.at[1])

        for e in range(N_EXPERTS):
            fill(jnp.maximum(pend_ref[e] - bm, 0)).start()
        for e in range(N_EXPERTS):
            fill(0).wait()

        def tail(j, c):
            fill(j * bm).start()
            fill(0).wait()
            return c

        lax.fori_loop(pend_ref[N_EXPERTS - 1] // bm, x_hbm.shape[0] // (bm * parts), tail, 0)

    def body(j, c):
        t0 = pl.multiple_of(j * ISSUE_UNROLL, ISSUE_UNROLL)
        for u in range(ISSUE_UNROLL):
            for k in range(TOP_K):
                dst = dst_ref[0, 0, k * tc + t0 + u]
                pltpu.make_async_copy(tile(hn_ref, t0 + u), tile(x_hbm, dst), sem.at[0]).start()
        return c

    lax.fori_loop(0, tc // ISSUE_UNROLL, body, 0)
    pltpu.make_async_copy(tile(x_hbm, 0, rows), tile(x_hbm, 0, rows), sem.at[0]).wait()


def _dispatch(pend, dest3, hn_tm, n_rows, d):
    parts = d // LANES
    t = hn_tm.shape[0] // parts
    tc = COMB_TILE
    grid_spec = pltpu.PrefetchScalarGridSpec(
        num_scalar_prefetch=1,
        grid=(t // tc,),
        in_specs=[
            pl.BlockSpec((1, 1, TOP_K * tc), lambda i, pe: (i, 0, 0), memory_space=pltpu.SMEM),
            pl.BlockSpec((tc * parts, LANES), lambda i, pe: (i, 0)),
        ],
        out_specs=pl.BlockSpec(memory_space=pl.ANY),
        scratch_shapes=[pltpu.VMEM((MOE_BLOCK * parts, LANES), F32), pltpu.SemaphoreType.DMA((2,))],
    )
    return pl.pallas_call(
        functools.partial(_dispatch_kernel, parts=parts),
        grid_spec=grid_spec,
        out_shape=jax.ShapeDtypeStruct((n_rows * parts, LANES), F32),
        compiler_params=_cparams(("arbitrary",)),
        name="dispatch",
    )(pend, dest3, hn_tm)


def _moe_kernel(be_ref, nxt_ref, nvb_ref, x_ref, wgu_hbm, bg_ref, bu_ref, wd_hbm, bd_ref, perm_ref,
                y_ref, wgu_f, wd_f, wgu_s, wd_s, act_s, sem):
    i = pl.program_id(0)
    nvb = nvb_ref[0]
    f, d = wd_f.shape
    bm = act_s.shape[0]
    pw = perm_ref.shape[0]
    half = pw // 2

    def fetch(e):
        return (pltpu.make_async_copy(wgu_hbm.at[e], wgu_f, sem.at[0]),
                pltpu.make_async_copy(wd_hbm.at[e], wd_f, sem.at[1]))

    @pl.when(jnp.logical_and(i == 0, nvb > 0))
    def _():
        for c in fetch(be_ref[0]):
            c.start()

    changed = jnp.logical_or(i == 0, be_ref[i] != be_ref[jnp.maximum(i - 1, 0)])

    @pl.when(jnp.logical_and(changed, i < nvb))
    def _():
        for c in fetch(be_ref[i]):
            c.wait()
        for c in range(2 * f // pw):
            wc = wgu_f[:, c * pw:(c + 1) * pw].astype(BF16)
            wgu_s[:, c * pw:(c + 1) * pw] = _dot(wc, perm_ref[...]).astype(BF16)
        wd_s[...] = wd_f[...].astype(BF16)

        @pl.when(nxt_ref[i] >= 0)
        def _():
            for c in fetch(nxt_ref[i]):
                c.start()

    @pl.when(i < nvb)
    def _():
        xb = jnp.concatenate([p.astype(BF16) for p in _load_token_major(x_ref, bm, d)], axis=1)
        for c in range(2 * f // pw):
            gu = _dot(xb, wgu_s[:, c * pw:(c + 1) * pw])
            g = gu[:, :half] + bg_ref[0, :, c * half:(c + 1) * half]
            u = gu[:, half:] + bu_ref[0, :, c * half:(c + 1) * half]
            g = jnp.minimum(g, SWIGLU_LIMIT)
            u = jnp.clip(u, -SWIGLU_LIMIT, SWIGLU_LIMIT)
            act = (u + 1.0) * (g * _sigmoid(SWIGLU_ALPHA * g))
            act_s[:, c * half:(c + 1) * half] = act.astype(BF16)
        _store_token_major(y_ref, _dot(act_s[...], wd_s[...]) + bd_ref[0])

    @pl.when(i >= nvb)
    def _():
        y_ref[...] = jnp.zeros(y_ref.shape, F32)


def _moe(block_e, next_e, nvb, xrows_tm, wgu, bg, bu, wd, bd, perm):
    f, d = wd.shape[1], wd.shape[2]
    parts = d // LANES
    bm = MOE_BLOCK
    nb = xrows_tm.shape[0] // (bm * parts)
    emap = lambda i, be, nx, nv: (be[i], 0, 0)
    grid_spec = pltpu.PrefetchScalarGridSpec(
        num_scalar_prefetch=3,
        grid=(nb,),
        in_specs=[
            pl.BlockSpec((bm * parts, LANES),
                         lambda i, be, nx, nv: (jnp.minimum(i, jnp.maximum(nv[0] - 1, 0)), 0)),
            pl.BlockSpec(memory_space=pl.ANY),
            pl.BlockSpec((1, 1, f), emap), pl.BlockSpec((1, 1, f), emap),
            pl.BlockSpec(memory_space=pl.ANY),
            pl.BlockSpec((1, 1, d), emap),
            pl.BlockSpec(perm.shape, lambda i, be, nx, nv: (0, 0)),
        ],
        out_specs=pl.BlockSpec((bm * parts, LANES), lambda i, be, nx, nv: (i, 0)),
        scratch_shapes=[
            pltpu.VMEM((d, 2 * f), F32), pltpu.VMEM((f, d), F32),
            pltpu.VMEM((d, 2 * f), BF16), pltpu.VMEM((f, d), BF16),
            pltpu.VMEM((bm, f), BF16),
            pltpu.SemaphoreType.DMA((2,)),
        ],
    )
    return pl.pallas_call(
        _moe_kernel,
        grid_spec=grid_spec,
        out_shape=jax.ShapeDtypeStruct(xrows_tm.shape, F32),
        compiler_params=_cparams(("arbitrary",)),
        name="moe",
    )(block_e, next_e, nvb, xrows_tm, wgu, bg, bu, wd, bd, perm)


def _comb_kernel(dst_ref, dstn_ref, h_ref, gate_ref, y_hbm, g_ref, o_ref, ybuf, sem, *, parts):
    i = pl.program_id(0)
    n = pl.num_programs(0)
    tc, d = h_ref.shape
    rows = TOP_K * tc
    slot = i % 2

    def tile(ref, n, count=1):
        return ref.at[pl.ds(pl.multiple_of(n * parts, parts), count * parts), :]

    def start_rows(idx_ref, s):
        def body(j, c):
            t0 = pl.multiple_of(j * ISSUE_UNROLL, ISSUE_UNROLL)
            for u in range(ISSUE_UNROLL):
                for k in range(TOP_K):
                    r = k * tc + t0 + u
                    pltpu.make_async_copy(tile(y_hbm, idx_ref[0, 0, r]), tile(ybuf.at[s], r), sem.at[s]).start()
            return c

        lax.fori_loop(0, tc // ISSUE_UNROLL, body, 0)

    @pl.when(i == 0)
    def _():
        start_rows(dst_ref, 0)

    @pl.when(i + 1 < n)
    def _():
        start_rows(dstn_ref, 1 - slot)

    pltpu.make_async_copy(tile(y_hbm, 0, rows), ybuf.at[slot], sem.at[slot]).wait()
    acc = [h_ref[:, s * LANES:(s + 1) * LANES] for s in range(parts)]
    for k in range(TOP_K):
        yk = _load_token_major(ybuf.at[slot], tc, d, base=k * tc * parts)
        gk = gate_ref[:, k:k + 1]
        acc = [a + p * gk for a, p in zip(acc, yk)]
    o_ref[...] = _rms(jnp.concatenate(acc, axis=1), g_ref[...])


def _combine(dest3, h2, gate, y_tm, g):
    t, d = h2.shape
    parts = d // LANES
    tc = COMB_TILE
    nt = t // tc
    return pl.pallas_call(
        functools.partial(_comb_kernel, parts=parts),
        grid=(nt,),
        in_specs=[
            pl.BlockSpec((1, 1, TOP_K * tc), lambda i: (i, 0, 0), memory_space=pltpu.SMEM),
            pl.BlockSpec((1, 1, TOP_K * tc), lambda i: (jnp.minimum(i + 1, nt - 1), 0, 0),
                         memory_space=pltpu.SMEM),
            pl.BlockSpec((tc, d), lambda i: (i, 0)),
            pl.BlockSpec((tc, LANES), lambda i: (i, 0)),
            pl.BlockSpec(memory_space=pl.ANY),
            pl.BlockSpec((1, d), lambda i: (0, 0)),
        ],
        out_specs=pl.BlockSpec((tc, d), lambda i: (i, 0)),
        out_shape=jax.ShapeDtypeStruct((t, d), F32),
        scratch_shapes=[pltpu.VMEM((2, TOP_K * tc * parts, LANES), F32), pltpu.SemaphoreType.DMA((2,))],
        compiler_params=_cparams(("arbitrary",)),
        name="combine",
    )(dest3, dest3, h2, gate, y_tm, g)


def _block_diag(wb, per):
    nb, bw, _ = wb.shape
    wq = wb.reshape(nb // per, per, bw, bw)
    eye = jnp.eye(per, dtype=wb.dtype)
    out = jnp.einsum('qaij,ab->qaibj', wq, eye)
    return out.reshape(nb // per, per * bw, per * bw)


def _pad_lanes(v, fill=0.0):
    return jnp.pad(v, (0, LANES - v.shape[0]), constant_values=fill).reshape(1, LANES)


def kernel(x, mem, norm_mix, w_in, ssd_conv_w, ssd_conv_b, ssd_dt_bias, ssd_a_log, ssd_d, ssd_norm, lru_conv_w, lru_conv_b, lru_wa, lru_ba, lru_wx, lru_bx, lru_lambda, w_out, norm_xattn, norm_mem, w_q, w_kv, w_o, norm_moe, w_router, b_router, w_gate_up, b_gate_up, w_down, b_down, norm_final):
    b, s, d = x.shape
    t = b * s
    n_mem = mem.shape[1]
    w = d
    cdim = w + 2 * SSD_GROUPS * SSD_STATE
    o1, o2, o3, o4 = w, w + cdim, w + cdim + SSD_HEADS, w + cdim + SSD_HEADS + w

    wi = w_in[0]
    wzx = wi[:, :o2].astype(BF16)
    wdt = jnp.pad(wi[:, o2:o3], ((0, 0), (0, LANES - SSD_HEADS)))
    wdt_hi, wdt_lo = _hilo(wdt)
    wxr = wi[:, o3:o4].astype(BF16)
    wgr = wi[:, o4:].astype(BF16)
    e01 = (jnp.arange(LANES)[:, None] == (jnp.arange(w)[None, :] // SSD_HEAD_DIM)).astype(BF16)
    dskip = jnp.repeat(ssd_d[0], SSD_HEAD_DIM).reshape(1, w)
    per = 256 // (w // LRU_BLOCKS)
    wa_bd = _block_diag(lru_wa[0], per).astype(BF16)
    wx_bd = _block_diag(lru_wx[0], per).astype(BF16)
    wr = jnp.pad(w_router[0], ((0, 0), (0, LANES - N_EXPERTS)))
    wr_hi, wr_lo = _hilo(wr)
    br = _pad_lanes(b_router[0], fill=-1e30)
    bgu = b_gate_up[0]
    bg = bgu[:, None, 0::2]
    bu = bgu[:, None, 1::2]
    bd = b_down[0][:, None, :]
    pw = 2 * LANES
    col = jnp.arange(pw)
    src_col = jnp.where(col < LANES, 2 * col, 2 * (col - LANES) + 1)
    perm = (jnp.arange(pw)[:, None] == src_col[None, :]).astype(BF16)

    x2 = x.reshape(t, d)
    z, xbc, dt, xr, gr = _in_proj(x2, norm_mix[0].reshape(1, d), wzx, wdt_hi, wdt_lo, wxr, wgr)

    y_ssd = _ssd(xbc.reshape(b, s, cdim), dt.reshape(b, s, LANES), z.reshape(b, s, w),
                 ssd_conv_w[0], ssd_conv_b[0].reshape(1, cdim), _pad_lanes(ssd_dt_bias[0]),
                 _pad_lanes(ssd_a_log[0]), dskip, ssd_norm[0].reshape(1, w), e01)
    y_lru = _lru(xr.reshape(b, s, w), gr.reshape(b, s, w), lru_conv_w[0], lru_conv_b[0].reshape(1, w),
                 wa_bd, lru_ba[0].reshape(1, w), wx_bd, lru_bx[0].reshape(1, w), lru_lambda[0].reshape(1, w))

    kk, vv = _kv(mem.reshape(b * n_mem, d), norm_mem[0].reshape(1, d), w_kv[0].astype(BF16))
    wo_mix = w_out[0].astype(BF16)
    h2, hn, idx_m, rank_m, gate_m, cnt = _mid(
        x2, y_ssd.reshape(t, w), y_lru.reshape(t, w), kk.reshape(b, n_mem, d), vv.reshape(b, n_mem, d),
        wo_mix[:w], wo_mix[w:], norm_xattn[0].reshape(1, d), w_q[0].astype(BF16), w_o[0].astype(BF16),
        norm_moe[0].reshape(1, d), wr_hi, wr_lo, br, s)

    bm = MOE_BLOCK
    counts = cnt[0, :N_EXPERTS]
    pcounts = (counts + bm - 1) // bm * bm
    pend = jnp.cumsum(pcounts).astype(jnp.int32)
    pstart = pend - pcounts
    idx = idx_m[:, :TOP_K]
    onehot = idx[:, :, None] == jnp.arange(N_EXPERTS, dtype=jnp.int32)[None, None, :]
    dest = jnp.sum(jnp.where(onehot, pstart[None, None, :], 0), axis=-1) + rank_m[:, :TOP_K]
    n_pairs = t * TOP_K
    nb = (n_pairs + N_EXPERTS * (bm - 1) + bm - 1) // bm
    blk0 = jnp.arange(nb, dtype=jnp.int32) * bm
    block_e = jnp.minimum(jnp.sum(pend[None, :] <= blk0[:, None], axis=1), N_EXPERTS - 1).astype(jnp.int32)
    nvb = (pend[-1] // bm).astype(jnp.int32).reshape(1)
    after = pend[block_e] // bm
    next_e = jnp.where(after < nvb[0], block_e[jnp.minimum(after, nb - 1)], -1).astype(jnp.int32)
    tc = COMB_TILE
    dest3 = dest.astype(jnp.int32).reshape(t // tc, tc, TOP_K).transpose(0, 2, 1).reshape(t // tc, 1, TOP_K * tc)

    xrows = _dispatch(pend, dest3, hn, nb * bm, d)
    y = _moe(block_e, next_e, nvb, xrows, w_gate_up[0], bg, bu, w_down[0], bd, perm)
    out = _combine(dest3, h2, gate_m, y, norm_final.reshape(1, d))
    return out.reshape(b, s, d)
```

```python
import functools

import jax
import jax.numpy as jnp
from jax import lax
from jax.experimental import pallas as pl
from jax.experimental.pallas import tpu as pltpu

F32 = jnp.float32
BF16 = jnp.bfloat16

NORM_EPS = 1e-6
LANES = 128
SUBLANES = 8
SSD_HEAD_DIM = 64
SSD_HEADS = 16
SSD_GROUPS = 4
SSD_STATE = 128
CONV_K = 4
LRU_BLOCKS = 16
RG_C = 8.0
X_HEADS = 4
N_EXPERTS = 32
TOP_K = 4
SWIGLU_LIMIT = 7.0
SWIGLU_ALPHA = 1.702

VMEM_LIMIT = 56 * 1024 * 1024

IN_TILE = 1024
SSD_CHUNK = 256
LRU_TILE = 256
MID_TILE = 512
MOE_BLOCK = 512
COMB_TILE = 256
ISSUE_UNROLL = 8


def _cparams(sem):
    return pltpu.CompilerParams(dimension_semantics=sem, vmem_limit_bytes=VMEM_LIMIT)


def _rms(x, g):
    ms = jnp.mean(x * x, axis=-1, keepdims=True)
    return x * lax.rsqrt(ms + NORM_EPS) * g


def _sigmoid(x):
    return 0.5 * jnp.tanh(0.5 * x) + 0.5


def _softplus(x):
    return jnp.maximum(x, 0.0) + jnp.log(1.0 + jnp.exp(-jnp.abs(x)))


def _split3(x):
    a = x.astype(BF16)
    r = x - a.astype(F32)
    b = r.astype(BF16)
    c = (r - b.astype(F32)).astype(BF16)
    return a, b, c


def _dot(a, b):
    return jnp.dot(a, b, preferred_element_type=F32)


def _dot_nt(a, b):
    return lax.dot_general(a, b, (((1,), (1,)), ((), ())), preferred_element_type=F32)


def _dot01_right(x, m01):
    a, b, c = _split3(x)
    return _dot(a, m01) + _dot(b, m01) + _dot(c, m01)


def _dot01_left(m01, x):
    a, b, c = _split3(x)
    return _dot(m01, a) + _dot(m01, b) + _dot(m01, c)


def _dot_hilo(x, w_hi, w_lo):
    xh = x.astype(BF16)
    xl = (x - xh.astype(F32)).astype(BF16)
    return _dot(xh, w_hi) + _dot(xl, w_hi) + _dot(xh, w_lo)


def _hilo(w):
    hi = w.astype(BF16)
    lo = (w - hi.astype(F32)).astype(BF16)
    return hi, lo


def _store_token_major(ref, val, base=0):
    n, d = val.shape
    parts = d // LANES
    for s in range(parts):
        ref[pl.ds(base + s, n, stride=parts), :] = val[:, s * LANES:(s + 1) * LANES]


def _load_token_major(ref, n, d, base=0):
    parts = d // LANES
    return [ref[pl.ds(base + s, n, stride=parts), :] for s in range(parts)]


def _in_proj_kernel(x_ref, g_ref, wzx_ref, wdth_ref, wdtl_ref, wxr_ref, wgr_ref,
                    z_ref, xbc_ref, dt_ref, xr_ref, gr_ref):
    d = x_ref.shape[1]
    hn = _rms(x_ref[...], g_ref[...])
    hb = hn.astype(BF16)
    z_ref[...] = _dot(hb, wzx_ref[:, :d]).astype(BF16)
    xbc_ref[...] = _dot(hb, wzx_ref[:, d:]).astype(BF16)
    dt_ref[...] = _dot_hilo(hn, wdth_ref[...], wdtl_ref[...])
    xr_ref[...] = _dot(hb, wxr_ref[...]).astype(BF16)
    gr_ref[...] = _dot(hb, wgr_ref[...]).astype(BF16)


def _in_proj(x2, g, wzx, wdt_hi, wdt_lo, wxr, wgr):
    t, d = x2.shape
    tm = IN_TILE
    nzx = wzx.shape[1]
    const = lambda i: (0, 0)
    row = lambda i: (i, 0)
    return pl.pallas_call(
        _in_proj_kernel,
        grid=(t // tm,),
        in_specs=[
            pl.BlockSpec((tm, d), row),
            pl.BlockSpec((1, d), const),
            pl.BlockSpec((d, nzx), const),
            pl.BlockSpec((d, LANES), const),
            pl.BlockSpec((d, LANES), const),
            pl.BlockSpec((d, d), const),
            pl.BlockSpec((d, d), const),
        ],
        out_specs=[
            pl.BlockSpec((tm, d), row),
            pl.BlockSpec((tm, nzx - d), row),
            pl.BlockSpec((tm, LANES), row),
            pl.BlockSpec((tm, d), row),
            pl.BlockSpec((tm, d), row),
        ],
        out_shape=[
            jax.ShapeDtypeStruct((t, d), BF16),
            jax.ShapeDtypeStruct((t, nzx - d), BF16),
            jax.ShapeDtypeStruct((t, LANES), F32),
            jax.ShapeDtypeStruct((t, d), BF16),
            jax.ShapeDtypeStruct((t, d), BF16),
        ],
        compiler_params=_cparams(("arbitrary",)),
        name="in_proj",
    )(x2, g, wzx, wdt_hi, wdt_lo, wxr, wgr)


def _causal_conv(ext_ref, x_f32, w_ref, b_ref, first):
    n = x_f32.shape[0]
    pad = SUBLANES

    @pl.when(first)
    def _():
        ext_ref[0:pad, :] = jnp.zeros((pad, ext_ref.shape[1]), F32)

    ext_ref[pad:pad + n, :] = x_f32
    acc = b_ref[...] + w_ref[0:1, :] * ext_ref[pad - 3:pad - 3 + n, :]
    for j in range(1, CONV_K):
        acc = acc + w_ref[j:j + 1, :] * ext_ref[pad - 3 + j:pad - 3 + j + n, :]
    ext_ref[0:pad, :] = ext_ref[n:n + pad, :]
    return acc


def _ssd_kernel(xbc_ref, dt_ref, z_ref, cw_ref, cb_ref, dtb_ref, alog_ref, dskip_ref, gn_ref, e_ref,
                y_ref, ext_ref, st_ref):
    n = xbc_ref.shape[1]
    w = z_ref.shape[2]
    gw = w // SSD_GROUPS
    first = pl.program_id(1) == 0

    @pl.when(first)
    def _():
        st_ref[...] = jnp.zeros(st_ref.shape, F32)

    conv = _causal_conv(ext_ref, xbc_ref[0].astype(F32), cw_ref, cb_ref, first)
    xc = conv * _sigmoid(conv)
    xs = xc[:, :w]

    dt = _softplus(dt_ref[0] + dtb_ref[...])
    a = -jnp.exp(alog_ref[...])
    da = dt * a
    ri = lax.broadcasted_iota(jnp.int32, (n, n), 0)
    ci = lax.broadcasted_iota(jnp.int32, (n, n), 1)
    causal = ri >= ci
    tril = jnp.where(causal, 1.0, 0.0).astype(BF16)
    a_cs = _dot01_left(tril, da)
    a_cs_t = a_cs.T

    e01 = e_ref[...]
    dt_x = _dot01_right(dt, e01)
    acs_x = _dot01_right(a_cs, e01)
    last_x = acs_x[n - 1:n, :]
    xdt = xs * dt_x
    xdt_b = xdt.astype(BF16)
    xdt_end = (xdt * jnp.exp(last_x - acs_x)).astype(BF16)
    exp_acs = jnp.exp(acs_x)
    chunk_decay = jnp.exp(last_x)
    lane = lax.broadcasted_iota(jnp.int32, (n, gw), 1)

    for g in range(SSD_GROUPS):
        lo = g * gw
        bg = xc[:, w + g * SSD_STATE:w + (g + 1) * SSD_STATE].astype(BF16)
        cg = xc[:, w + (SSD_GROUPS + g) * SSD_STATE:w + (SSD_GROUPS + g + 1) * SSD_STATE].astype(BF16)
        cb = _dot_nt(cg, bg)
        prev = st_ref[g]
        acc = _dot(cg, prev.astype(BF16)) * exp_acs[:, lo:lo + gw]
        new = lax.dot_general(bg, xdt_end[:, lo:lo + gw], (((0,), (0,)), ((), ())),
                              preferred_element_type=F32)
        st_ref[g] = chunk_decay[:, lo:lo + gw] * prev + new
        xg = xdt_b[:, lo:lo + gw]
        for k in range(SSD_HEADS // SSD_GROUPS):
            h = g * (SSD_HEADS // SSD_GROUPS) + k
            seg = a_cs[:, h:h + 1] - a_cs_t[h:h + 1, :]
            dec = jnp.exp(jnp.where(causal, seg, -jnp.inf))
            m = (cb * dec).astype(BF16)
            in_head = (lane >= k * SSD_HEAD_DIM) & (lane < (k + 1) * SSD_HEAD_DIM)
            acc = acc + _dot(m, jnp.where(in_head, xg, jnp.zeros_like(xg)))
        yg = acc + xs[:, lo:lo + gw] * dskip_ref[:, lo:lo + gw]
        zg = z_ref[0, :, lo:lo + gw].astype(F32)
        u = yg * (zg * _sigmoid(zg))
        u = u * lax.rsqrt(jnp.mean(u * u, axis=-1, keepdims=True) + NORM_EPS)
        y_ref[0, :, lo:lo + gw] = (u * gn_ref[:, lo:lo + gw]).astype(BF16)


def _ssd(xbc, dt, z, cw, cb, dtb, alog, dskip, gn, e01):
    b, s, cdim = xbc.shape
    w = z.shape[2]
    n = SSD_CHUNK
    tile = lambda i, j: (i, j, 0)
    const = lambda i, j: (0, 0)
    return pl.pallas_call(
        _ssd_kernel,
        grid=(b, s // n),
        in_specs=[
            pl.BlockSpec((1, n, cdim), tile),
            pl.BlockSpec((1, n, LANES), tile),
            pl.BlockSpec((1, n, w), tile),
            pl.BlockSpec((CONV_K, cdim), const),
            pl.BlockSpec((1, cdim), const),
            pl.BlockSpec((1, LANES), const),
            pl.BlockSpec((1, LANES), const),
            pl.BlockSpec((1, w), const),
            pl.BlockSpec((1, w), const),
            pl.BlockSpec((LANES, w), const),
        ],
        out_specs=pl.BlockSpec((1, n, w), tile),
        out_shape=jax.ShapeDtypeStruct((b, s, w), BF16),
        scratch_shapes=[
            pltpu.VMEM((n + SUBLANES, cdim), F32),
            pltpu.VMEM((SSD_GROUPS, SSD_STATE, w // SSD_GROUPS), F32),
        ],
        compiler_params=_cparams(("arbitrary", "arbitrary")),
        name="ssd",
    )(xbc, dt, z, cw, cb, dtb, alog, dskip, gn, e01)


def _gelu_tanh(x):
    c = 0.7978845608028654
    return 0.5 * x * (1.0 + jnp.tanh(c * (x + 0.044715 * (x * x * x))))


def _lru_kernel(xr_ref, gr_ref, cw_ref, cb_ref, wa_ref, ba_ref, wx_ref, bx_ref, lam_ref,
                y_ref, ext_ref, car_ref, h_ref):
    n = xr_ref.shape[1]
    w = xr_ref.shape[2]
    first = pl.program_id(1) == 0

    @pl.when(first)
    def _():
        car_ref[...] = jnp.zeros(car_ref.shape, F32)

    xc = _causal_conv(ext_ref, xr_ref[0].astype(F32), cw_ref, cb_ref, first)
    xb = xc.astype(BF16)
    nq = wa_ref.shape[0]
    qw = w // nq
    r_parts, i_parts = [], []
    for q in range(nq):
        xq = xb[:, q * qw:(q + 1) * qw]
        r_parts.append(_dot(xq, wa_ref[q]))
        i_parts.append(_dot(xq, wx_ref[q]))
    r = _sigmoid(jnp.concatenate(r_parts, axis=1) + ba_ref[...])
    gi = _sigmoid(jnp.concatenate(i_parts, axis=1) + bx_ref[...])
    log_a = (-RG_C) * r * _softplus(-lam_ref[...])
    a = jnp.exp(log_a)
    u = xc * gi * jnp.sqrt(1.0 - a * a)

    groups = n // SUBLANES
    sub = lax.broadcasted_iota(jnp.int32, (groups, SUBLANES, w), 1)
    ap = a.reshape(groups, SUBLANES, w)
    bp = u.reshape(groups, SUBLANES, w)
    for d in (1, 2, 4):
        m = sub >= d
        bp = jnp.where(m, ap * pltpu.roll(bp, d, 1) + bp, bp)
        ap = jnp.where(m, ap * pltpu.roll(ap, d, 1), ap)
    carry = car_ref[...]
    for g in range(groups):
        hb = bp[g] + ap[g] * carry
        h_ref[g * SUBLANES:(g + 1) * SUBLANES, :] = hb
        carry = jnp.broadcast_to(hb[SUBLANES - 1:SUBLANES, :], (SUBLANES, w))
    car_ref[...] = carry
    y_ref[0] = (h_ref[...] * _gelu_tanh(gr_ref[0].astype(F32))).astype(BF16)


def _lru(xr, gr, cw, cb, wa, ba, wx, bx, lam):
    b, s, w = xr.shape
    n = LRU_TILE
    nq, qw, _ = wa.shape
    tile = lambda i, j: (i, j, 0)
    const = lambda i, j: (0, 0)
    const3 = lambda i, j: (0, 0, 0)
    return pl.pallas_call(
        _lru_kernel,
        grid=(b, s // n),
        in_specs=[
            pl.BlockSpec((1, n, w), tile),
            pl.BlockSpec((1, n, w), tile),
            pl.BlockSpec((CONV_K, w), const),
            pl.BlockSpec((1, w), const),
            pl.BlockSpec((nq, qw, qw), const3),
            pl.BlockSpec((1, w), const),
            pl.BlockSpec((nq, qw, qw), const3),
            pl.BlockSpec((1, w), const),
            pl.BlockSpec((1, w), const),
        ],
        out_specs=pl.BlockSpec((1, n, w), tile),
        out_shape=jax.ShapeDtypeStruct((b, s, w), BF16),
        scratch_shapes=[
            pltpu.VMEM((n + SUBLANES, w), F32),
            pltpu.VMEM((SUBLANES, w), F32),
            pltpu.VMEM((n, w), F32),
        ],
        compiler_params=_cparams(("arbitrary", "arbitrary")),
        name="lru",
    )(xr, gr, cw, cb, wa, ba, wx, bx, lam)


def _kv_kernel(m_ref, g_ref, w_ref, k_ref, v_ref):
    d = m_ref.shape[1]
    mn = _rms(m_ref[...], g_ref[...]).astype(BF16)
    k_ref[...] = _dot(mn, w_ref[:, :d]).astype(BF16)
    v_ref[...] = _dot(mn, w_ref[:, d:]).astype(BF16)


def _kv(mem2, g, wkv):
    t, d = mem2.shape
    tm = min(t, 512)
    row = lambda i: (i, 0)
    const = lambda i: (0, 0)
    return pl.pallas_call(
        _kv_kernel,
        grid=(t // tm,),
        in_specs=[pl.BlockSpec((tm, d), row), pl.BlockSpec((1, d), const), pl.BlockSpec((d, 2 * d), const)],
        out_specs=[pl.BlockSpec((tm, d), row), pl.BlockSpec((tm, d), row)],
        out_shape=[jax.ShapeDtypeStruct((t, d), BF16), jax.ShapeDtypeStruct((t, d), BF16)],
        compiler_params=_cparams(("arbitrary",)),
        name="kv",
    )(mem2, g, wkv)


def _mid_kernel(x_ref, ys_ref, yl_ref, k_ref, v_ref, wo1_ref, wo2_ref, gx_ref, wq_ref, wo_ref,
                gm_ref, wrh_ref, wrl_ref, br_ref,
                h_ref, hn_ref, idx_ref, rank_ref, gate_ref, cnt_ref, car_ref):
    tm, d = x_ref.shape
    hd = d // X_HEADS

    @pl.when(pl.program_id(0) == 0)
    def _():
        car_ref[...] = jnp.zeros(car_ref.shape, F32)

    h1 = x_ref[...] + _dot(ys_ref[...], wo1_ref[...]) + _dot(yl_ref[...], wo2_ref[...])

    q = _dot(_rms(h1, gx_ref[...]).astype(BF16), wq_ref[...]).astype(BF16)
    o_parts = []
    for hh in range(X_HEADS):
        sl = slice(hh * hd, (hh + 1) * hd)
        sc = _dot_nt(q[:, sl], k_ref[0, :, sl]) * (hd ** -0.5)
        sc = sc - jnp.max(sc, axis=-1, keepdims=True)
        p = jnp.exp(sc)
        p = p / jnp.sum(p, axis=-1, keepdims=True)
        o_parts.append(_dot(p.astype(BF16), v_ref[0, :, sl]))
    o = jnp.concatenate(o_parts, axis=1).astype(BF16)
    h2 = h1 + _dot(o, wo_ref[...])
    h_ref[...] = h2

    hn = _rms(h2, gm_ref[...])
    _store_token_major(hn_ref, hn)
    logits = _dot_hilo(hn, wrh_ref[...], wrl_ref[...]) + br_ref[...]

    lane = lax.broadcasted_iota(jnp.int32, (tm, LANES), 1)
    picked = jnp.zeros((tm, LANES), F32)
    vals, idxs = [], []
    l = logits
    for _ in range(TOP_K):
        m = jnp.max(l, axis=-1, keepdims=True)
        idx = jnp.min(jnp.where(l == m, lane, LANES), axis=-1, keepdims=True)
        sel = lane == idx
        vals.append(m)
        idxs.append(idx)
        picked = jnp.where(sel, 1.0, picked)
        l = jnp.where(sel, -jnp.inf, l)
    ex = [jnp.exp(v - vals[0]) for v in vals]
    den = ex[0] + ex[1] + ex[2] + ex[3]

    ri = lax.broadcasted_iota(jnp.int32, (tm, tm), 0)
    ci = lax.broadcasted_iota(jnp.int32, (tm, tm), 1)
    strict = jnp.where(ri > ci, 1.0, 0.0).astype(BF16)
    before = _dot(strict, picked.astype(BF16)) + car_ref[0:1, :]
    idx_out = jnp.zeros((tm, LANES), jnp.int32)
    rank_out = jnp.zeros((tm, LANES), jnp.int32)
    gate_out = jnp.zeros((tm, LANES), F32)
    for k in range(TOP_K):
        rk = jnp.sum(jnp.where(lane == idxs[k], before, 0.0), axis=-1, keepdims=True)
        at_k = lane == k
        idx_out = jnp.where(at_k, idxs[k], idx_out)
        rank_out = jnp.where(at_k, rk.astype(jnp.int32), rank_out)
        gate_out = jnp.where(at_k, ex[k] / den, gate_out)
    idx_ref[...] = idx_out
    rank_ref[...] = rank_out
    gate_ref[...] = gate_out
    total = car_ref[...] + jnp.sum(picked, axis=0, keepdims=True)
    car_ref[...] = total
    cnt_ref[...] = total.astype(jnp.int32)


def _mid(x2, ys, yl, kk, vv, wo1, wo2, gx, wq, wo, gm, wr_hi, wr_lo, br, seq):
    t, d = x2.shape
    tm = MID_TILE
    m = kk.shape[1]
    per_b = seq // tm
    row = lambda i: (i, 0)
    const = lambda i: (0, 0)
    kvmap = lambda i: (i // per_b, 0, 0)
    wspec = pl.BlockSpec((d, d), const)
    vspec = pl.BlockSpec((1, d), const)
    return pl.pallas_call(
        _mid_kernel,
        grid=(t // tm,),
        in_specs=[
            pl.BlockSpec((tm, d), row), pl.BlockSpec((tm, d), row), pl.BlockSpec((tm, d), row),
            pl.BlockSpec((1, m, d), kvmap), pl.BlockSpec((1, m, d), kvmap),
            wspec, wspec, vspec, wspec, wspec, vspec,
            pl.BlockSpec((d, LANES), const), pl.BlockSpec((d, LANES), const), pl.BlockSpec((1, LANES), const),
        ],
        out_specs=[
            pl.BlockSpec((tm, d), row), pl.BlockSpec((tm * d // LANES, LANES), row),
            pl.BlockSpec((tm, LANES), row), pl.BlockSpec((tm, LANES), row), pl.BlockSpec((tm, LANES), row),
            pl.BlockSpec((SUBLANES, LANES), const),
        ],
        out_shape=[
            jax.ShapeDtypeStruct((t, d), F32), jax.ShapeDtypeStruct((t * d // LANES, LANES), F32),
            jax.ShapeDtypeStruct((t, LANES), jnp.int32), jax.ShapeDtypeStruct((t, LANES), jnp.int32),
            jax.ShapeDtypeStruct((t, LANES), F32),
            jax.ShapeDtypeStruct((SUBLANES, LANES), jnp.int32),
        ],
        scratch_shapes=[pltpu.VMEM((SUBLANES, LANES), F32)],
        compiler_params=_cparams(("arbitrary",)),
        name="mid",
    )(x2, ys, yl, kk, vv, wo1, wo2, gx, wq, wo, gm, wr_hi, wr_lo, br)


def _dispatch_kernel(pend_ref, dst_ref, hn_ref, x_hbm, zero_ref, sem, *, parts):
    i = pl.program_id(0)
    tc = hn_ref.shape[0] // parts
    bm = zero_ref.shape[0] // parts
    rows = TOP_K * tc

    def tile(ref, n, count=1):
        return ref.at[pl.ds(pl.multiple_of(n * parts, parts), count * parts), :]

    @pl.when(i == 0)
    def _():
        zero_ref[...] = jnp.zeros(zero_ref.shape, F32)

        def fill(start):
            return pltpu.make_async_copy(zero_ref, tile(x_hbm, start, bm), sem.at[1])

        for e in range(N_EXPERTS):
            fill(jnp.maximum(pend_ref[e] - bm, 0)).start()
        for e in range(N_EXPERTS):
            fill(0).wait()

        def tail(j, c):
            fill(j * bm).start()
            fill(0).wait()
            return c

        lax.fori_loop(pend_ref[N_EXPERTS - 1] // bm, x_hbm.shape[0] // (bm * parts), tail, 0)

    def body(j, c):
        t0 = pl.multiple_of(j * ISSUE_UNROLL, ISSUE_UNROLL)
        for u in range(ISSUE_UNROLL):
            for k in range(TOP_K):
                dst = dst_ref[0, 0, k * tc + t0 + u]
                pltpu.make_async_copy(tile(hn_ref, t0 + u), tile(x_hbm, dst), sem.at[0]).start(priority=k % 2)
        return c

    lax.fori_loop(0, tc // ISSUE_UNROLL, body, 0)
    pltpu.make_async_copy(tile(x_hbm, 0, rows), tile(x_hbm, 0, rows), sem.at[0]).wait()


def _dispatch(pend, dest3, hn_tm, n_rows, d):
    parts = d // LANES
    t = hn_tm.shape[0] // parts
    tc = COMB_TILE
    grid_spec = pltpu.PrefetchScalarGridSpec(
        num_scalar_prefetch=1,
        grid=(t // tc,),
        in_specs=[
            pl.BlockSpec((1, 1, TOP_K * tc), lambda i, pe: (i, 0, 0), memory_space=pltpu.SMEM),
            pl.BlockSpec((tc * parts, LANES), lambda i, pe: (i, 0)),
        ],
        out_specs=pl.BlockSpec(memory_space=pl.ANY),
        scratch_shapes=[pltpu.VMEM((MOE_BLOCK * parts, LANES), F32), pltpu.SemaphoreType.DMA((2,))],
    )
    return pl.pallas_call(
        functools.partial(_dispatch_kernel, parts=parts),
        grid_spec=grid_spec,
        out_shape=jax.ShapeDtypeStruct((n_rows * parts, LANES), F32),
        compiler_params=_cparams(("arbitrary",)),
        name="dispatch",
    )(pend, dest3, hn_tm)


def _moe_kernel(be_ref, nxt_ref, nvb_ref, x_ref, wgu_hbm, bg_ref, bu_ref, wd_hbm, bd_ref, perm_ref,
                y_ref, wgu_f, wd_f, wgu_s, wd_s, act_s, sem):
    i = pl.program_id(0)
    nvb = nvb_ref[0]
    f, d = wd_f.shape
    bm = act_s.shape[0]
    pw = perm_ref.shape[0]
    half = pw // 2

    def fetch(e):
        return (pltpu.make_async_copy(wgu_hbm.at[e], wgu_f, sem.at[0]),
                pltpu.make_async_copy(wd_hbm.at[e], wd_f, sem.at[1]))

    @pl.when(jnp.logical_and(i == 0, nvb > 0))
    def _():
        for c in fetch(be_ref[0]):
            c.start()

    changed = jnp.logical_or(i == 0, be_ref[i] != be_ref[jnp.maximum(i - 1, 0)])

    @pl.when(jnp.logical_and(changed, i < nvb))
    def _():
        for c in fetch(be_ref[i]):
            c.wait()
        for c in range(2 * f // pw):
            wc = wgu_f[:, c * pw:(c + 1) * pw].astype(BF16)
            wgu_s[:, c * pw:(c + 1) * pw] = _dot(wc, perm_ref[...]).astype(BF16)
        wd_s[...] = wd_f[...].astype(BF16)

        @pl.when(nxt_ref[i] >= 0)
        def _():
            for c in fetch(nxt_ref[i]):
                c.start()

    @pl.when(i < nvb)
    def _():
        xb = jnp.concatenate([p.astype(BF16) for p in _load_token_major(x_ref, bm, d)], axis=1)
        for c in range(2 * f // pw):
            gu = _dot(xb, wgu_s[:, c * pw:(c + 1) * pw])
            g = gu[:, :half] + bg_ref[0, :, c * half:(c + 1) * half]
            u = gu[:, half:] + bu_ref[0, :, c * half:(c + 1) * half]
            g = jnp.minimum(g, SWIGLU_LIMIT)
            u = jnp.clip(u, -SWIGLU_LIMIT, SWIGLU_LIMIT)
            act = (u + 1.0) * (g * _sigmoid(SWIGLU_ALPHA * g))
            act_s[:, c * half:(c + 1) * half] = act.astype(BF16)
        _store_token_major(y_ref, _dot(act_s[...], wd_s[...]) + bd_ref[0])

    @pl.when(i >= nvb)
    def _():
        y_ref[...] = jnp.zeros(y_ref.shape, F32)


def _moe(block_e, next_e, nvb, xrows_tm, wgu, bg, bu, wd, bd, perm):
    f, d = wd.shape[1], wd.shape[2]
    parts = d // LANES
    bm = MOE_BLOCK
    nb = xrows_tm.shape[0] // (bm * parts)
    emap = lambda i, be, nx, nv: (be[i], 0, 0)
    grid_spec = pltpu.PrefetchScalarGridSpec(
        num_scalar_prefetch=3,
        grid=(nb,),
        in_specs=[
            pl.BlockSpec((bm * parts, LANES),
                         lambda i, be, nx, nv: (jnp.minimum(i, jnp.maximum(nv[0] - 1, 0)), 0)),
            pl.BlockSpec(memory_space=pl.ANY),
            pl.BlockSpec((1, 1, f), emap), pl.BlockSpec((1, 1, f), emap),
            pl.BlockSpec(memory_space=pl.ANY),
            pl.BlockSpec((1, 1, d), emap),
            pl.BlockSpec(perm.shape, lambda i, be, nx, nv: (0, 0)),
        ],
        out_specs=pl.BlockSpec((bm * parts, LANES), lambda i, be, nx, nv: (i, 0)),
        scratch_shapes=[
            pltpu.VMEM((d, 2 * f), F32), pltpu.VMEM((f, d), F32),
            pltpu.VMEM((d, 2 * f), BF16), pltpu.VMEM((f, d), BF16),
            pltpu.VMEM((bm, f), BF16),
            pltpu.SemaphoreType.DMA((2,)),
        ],
    )
    return pl.pallas_call(
        _moe_kernel,
        grid_spec=grid_spec,
        out_shape=jax.ShapeDtypeStruct(xrows_tm.shape, F32),
        compiler_params=_cparams(("arbitrary",)),
        name="moe",
    )(block_e, next_e, nvb, xrows_tm, wgu, bg, bu, wd, bd, perm)


def _comb_kernel(dst_ref, dstn_ref, h_ref, gate_ref, y_hbm, g_ref, o_ref, ybuf, sem, *, parts):
    i = pl.program_id(0)
    n = pl.num_programs(0)
    tc, d = h_ref.shape
    rows = TOP_K * tc
    slot = i % 2

    def tile(ref, n, count=1):
        return ref.at[pl.ds(pl.multiple_of(n * parts, parts), count * parts), :]

    def start_rows(idx_ref, s):
        def body(j, c):
            t0 = pl.multiple_of(j * ISSUE_UNROLL, ISSUE_UNROLL)
            for u in range(ISSUE_UNROLL):
                for k in range(TOP_K):
                    r = k * tc + t0 + u
                    pltpu.make_async_copy(tile(y_hbm, idx_ref[0, 0, r]), tile(ybuf.at[s], r),
                                          sem.at[s]).start(priority=k % 2)
            return c

        lax.fori_loop(0, tc // ISSUE_UNROLL, body, 0)

    @pl.when(i == 0)
    def _():
        start_rows(dst_ref, 0)

    @pl.when(i + 1 < n)
    def _():
        start_rows(dstn_ref, 1 - slot)

    pltpu.make_async_copy(tile(y_hbm, 0, rows), ybuf.at[slot], sem.at[slot]).wait()
    acc = [h_ref[:, s * LANES:(s + 1) * LANES] for s in range(parts)]
    for k in range(TOP_K):
        yk = _load_token_major(ybuf.at[slot], tc, d, base=k * tc * parts)
        gk = gate_ref[:, k:k + 1]
        acc = [a + p * gk for a, p in zip(acc, yk)]
    o_ref[...] = _rms(jnp.concatenate(acc, axis=1), g_ref[...])


def _combine(dest3, h2, gate, y_tm, g):
    t, d = h2.shape
    parts = d // LANES
    tc = COMB_TILE
    nt = t // tc
    return pl.pallas_call(
        functools.partial(_comb_kernel, parts=parts),
        grid=(nt,),
        in_specs=[
            pl.BlockSpec((1, 1, TOP_K * tc), lambda i: (i, 0, 0), memory_space=pltpu.SMEM),
            pl.BlockSpec((1, 1, TOP_K * tc), lambda i: (jnp.minimum(i + 1, nt - 1), 0, 0),
                         memory_space=pltpu.SMEM),
            pl.BlockSpec((tc, d), lambda i: (i, 0)),
            pl.BlockSpec((tc, LANES), lambda i: (i, 0)),
            pl.BlockSpec(memory_space=pl.ANY),
            pl.BlockSpec((1, d), lambda i: (0, 0)),
        ],
        out_specs=pl.BlockSpec((tc, d), lambda i: (i, 0)),
        out_shape=jax.ShapeDtypeStruct((t, d), F32),
        scratch_shapes=[pltpu.VMEM((2, TOP_K * tc * parts, LANES), F32), pltpu.SemaphoreType.DMA((2,))],
        compiler_params=_cparams(("arbitrary",)),
        name="combine",
    )(dest3, dest3, h2, gate, y_tm, g)


def _block_diag(wb, per):
    nb, bw, _ = wb.shape
    wq = wb.reshape(nb // per, per, bw, bw)
    eye = jnp.eye(per, dtype=wb.dtype)
    out = jnp.einsum('qaij,ab->qaibj', wq, eye)
    return out.reshape(nb // per, per * bw, per * bw)


def _pad_lanes(v, fill=0.0):
    return jnp.pad(v, (0, LANES - v.shape[0]), constant_values=fill).reshape(1, LANES)


def kernel(x, mem, norm_mix, w_in, ssd_conv_w, ssd_conv_b, ssd_dt_bias, ssd_a_log, ssd_d, ssd_norm, lru_conv_w, lru_conv_b, lru_wa, lru_ba, lru_wx, lru_bx, lru_lambda, w_out, norm_xattn, norm_mem, w_q, w_kv, w_o, norm_moe, w_router, b_router, w_gate_up, b_gate_up, w_down, b_down, norm_final):
    b, s, d = x.shape
    t = b * s
    n_mem = mem.shape[1]
    w = d
    cdim = w + 2 * SSD_GROUPS * SSD_STATE
    o1, o2, o3, o4 = w, w + cdim, w + cdim + SSD_HEADS, w + cdim + SSD_HEADS + w

    wi = w_in[0]
    wzx = wi[:, :o2].astype(BF16)
    wdt = jnp.pad(wi[:, o2:o3], ((0, 0), (0, LANES - SSD_HEADS)))
    wdt_hi, wdt_lo = _hilo(wdt)
    wxr = wi[:, o3:o4].astype(BF16)
    wgr = wi[:, o4:].astype(BF16)
    e01 = (jnp.arange(LANES)[:, None] == (jnp.arange(w)[None, :] // SSD_HEAD_DIM)).astype(BF16)
    dskip = jnp.repeat(ssd_d[0], SSD_HEAD_DIM).reshape(1, w)
    per = 256 // (w // LRU_BLOCKS)
    wa_bd = _block_diag(lru_wa[0], per).astype(BF16)
    wx_bd = _block_diag(lru_wx[0], per).astype(BF16)
    wr = jnp.pad(w_router[0], ((0, 0), (0, LANES - N_EXPERTS)))
    wr_hi, wr_lo = _hilo(wr)
    br = _pad_lanes(b_router[0], fill=-1e30)
    bgu = b_gate_up[0]
    bg = bgu[:, None, 0::2]
    bu = bgu[:, None, 1::2]
    bd = b_down[0][:, None, :]
    pw = 2 * LANES
    col = jnp.arange(pw)
    src_col = jnp.where(col < LANES, 2 * col, 2 * (col - LANES) + 1)
    perm = (jnp.arange(pw)[:, None] == src_col[None, :]).astype(BF16)

    x2 = x.reshape(t, d)
    z, xbc, dt, xr, gr = _in_proj(x2, norm_mix[0].reshape(1, d), wzx, wdt_hi, wdt_lo, wxr, wgr)

    y_ssd = _ssd(xbc.reshape(b, s, cdim), dt.reshape(b, s, LANES), z.reshape(b, s, w),
                 ssd_conv_w[0], ssd_conv_b[0].reshape(1, cdim), _pad_lanes(ssd_dt_bias[0]),
                 _pad_lanes(ssd_a_log[0]), dskip, ssd_norm[0].reshape(1, w), e01)
    y_lru = _lru(xr.reshape(b, s, w), gr.reshape(b, s, w), lru_conv_w[0], lru_conv_b[0].reshape(1, w),
                 wa_bd, lru_ba[0].reshape(1, w), wx_bd, lru_bx[0].reshape(1, w), lru_lambda[0].reshape(1, w))

    kk, vv = _kv(mem.reshape(b * n_mem, d), norm_mem[0].reshape(1, d), w_kv[0].astype(BF16))
    wo_mix = w_out[0].astype(BF16)
    h2, hn, idx_m, rank_m, gate_m, cnt = _mid(
        x2, y_ssd.reshape(t, w), y_lru.reshape(t, w), kk.reshape(b, n_mem, d), vv.reshape(b, n_mem, d),
        wo_mix[:w], wo_mix[w:], norm_xattn[0].reshape(1, d), w_q[0].astype(BF16), w_o[0].astype(BF16),
        norm_moe[0].reshape(1, d), wr_hi, wr_lo, br, s)

    bm = MOE_BLOCK
    counts = cnt[0, :N_EXPERTS]
    pcounts = (counts + bm - 1) // bm * bm
    pend = jnp.cumsum(pcounts).astype(jnp.int32)
    pstart = pend - pcounts
    idx = idx_m[:, :TOP_K]
    onehot = idx[:, :, None] == jnp.arange(N_EXPERTS, dtype=jnp.int32)[None, None, :]
    dest = jnp.sum(jnp.where(onehot, pstart[None, None, :], 0), axis=-1) + rank_m[:, :TOP_K]
    n_pairs = t * TOP_K
    nb = (n_pairs + N_EXPERTS * (bm - 1) + bm - 1) // bm
    blk0 = jnp.arange(nb, dtype=jnp.int32) * bm
    block_e = jnp.minimum(jnp.sum(pend[None, :] <= blk0[:, None], axis=1), N_EXPERTS - 1).astype(jnp.int32)
    nvb = (pend[-1] // bm).astype(jnp.int32).reshape(1)
    after = pend[block_e] // bm
    next_e = jnp.where(after < nvb[0], block_e[jnp.minimum(after, nb - 1)], -1).astype(jnp.int32)
    tc = COMB_TILE
    dest3 = dest.astype(jnp.int32).reshape(t // tc, tc, TOP_K).transpose(0, 2, 1).reshape(t // tc, 1, TOP_K * tc)

    xrows = _dispatch(pend, dest3, hn, nb * bm, d)
    y = _moe(block_e, next_e, nvb, xrows, w_gate_up[0], bg, bu, w_down[0], bd, perm)
    out = _combine(dest3, h2, gate_m, y, norm_final.reshape(1, d))
    return out.reshape(b, s, d)
```

```python
import functools

import jax
import jax.numpy as jnp
from jax import lax
from jax.experimental import pallas as pl
from jax.experimental.pallas import tpu as pltpu

F32 = jnp.float32
BF16 = jnp.bfloat16

NORM_EPS = 1e-6
LANES = 128
SUBLANES = 8
SSD_HEAD_DIM = 64
SSD_HEADS = 16
SSD_GROUPS = 4
SSD_STATE = 128
CONV_K = 4
LRU_BLOCKS = 16
RG_C = 8.0
X_HEADS = 4
N_EXPERTS = 32
TOP_K = 4
SWIGLU_LIMIT = 7.0
SWIGLU_ALPHA = 1.702

VMEM_LIMIT = 56 * 1024 * 1024

IN_TILE = 1024
SSD_CHUNK = 256
LRU_TILE = 256
MID_TILE = 512
MOE_BLOCK = 512
COMB_TILE = 256
ISSUE_UNROLL = 8


def _cparams(sem):
    return pltpu.CompilerParams(dimension_semantics=sem, vmem_limit_bytes=VMEM_LIMIT)


def _rms(x, g):
    ms = jnp.mean(x * x, axis=-1, keepdims=True)
    return x * lax.rsqrt(ms + NORM_EPS) * g


def _sigmoid(x):
    return 0.5 * jnp.tanh(0.5 * x) + 0.5


def _softplus(x):
    return jnp.maximum(x, 0.0) + jnp.log(1.0 + jnp.exp(-jnp.abs(x)))


def _split3(x):
    a = x.astype(BF16)
    r = x - a.astype(F32)
    b = r.astype(BF16)
    c = (r - b.astype(F32)).astype(BF16)
    return a, b, c


def _dot(a, b):
    return jnp.dot(a, b, preferred_element_type=F32)


def _dot_nt(a, b):
    return lax.dot_general(a, b, (((1,), (1,)), ((), ())), preferred_element_type=F32)


def _dot01_right(x, m01):
    a, b, c = _split3(x)
    return _dot(a, m01) + _dot(b, m01) + _dot(c, m01)


def _dot01_left(m01, x):
    a, b, c = _split3(x)
    return _dot(m01, a) + _dot(m01, b) + _dot(m01, c)


def _dot_hilo(x, w_hi, w_lo):
    xh = x.astype(BF16)
    xl = (x - xh.astype(F32)).astype(BF16)
    return _dot(xh, w_hi) + _dot(xl, w_hi) + _dot(xh, w_lo)


def _hilo(w):
    hi = w.astype(BF16)
    lo = (w - hi.astype(F32)).astype(BF16)
    return hi, lo


def _store_token_major(ref, val, base=0):
    n, d = val.shape
    parts = d // LANES
    for s in range(parts):
        ref[pl.ds(base + s, n, stride=parts), :] = val[:, s * LANES:(s + 1) * LANES]


def _load_token_major(ref, n, d, base=0):
    parts = d // LANES
    return [ref[pl.ds(base + s, n, stride=parts), :] for s in range(parts)]


def _in_proj_kernel(x_ref, g_ref, wzx_ref, wdth_ref, wdtl_ref, wxr_ref, wgr_ref,
                    z_ref, xbc_ref, dt_ref, xr_ref, gr_ref):
    d = x_ref.shape[1]
    hn = _rms(x_ref[...], g_ref[...])
    hb = hn.astype(BF16)
    z_ref[...] = _dot(hb, wzx_ref[:, :d]).astype(BF16)
    xbc_ref[...] = _dot(hb, wzx_ref[:, d:]).astype(BF16)
    dt_ref[...] = _dot_hilo(hn, wdth_ref[...], wdtl_ref[...])
    xr_ref[...] = _dot(hb, wxr_ref[...]).astype(BF16)
    gr_ref[...] = _dot(hb, wgr_ref[...]).astype(BF16)


def _in_proj(x2, g, wzx, wdt_hi, wdt_lo, wxr, wgr):
    t, d = x2.shape
    tm = IN_TILE
    nzx = wzx.shape[1]
    const = lambda i: (0, 0)
    row = lambda i: (i, 0)
    return pl.pallas_call(
        _in_proj_kernel,
        grid=(t // tm,),
        in_specs=[
            pl.BlockSpec((tm, d), row),
            pl.BlockSpec((1, d), const),
            pl.BlockSpec((d, nzx), const),
            pl.BlockSpec((d, LANES), const),
            pl.BlockSpec((d, LANES), const),
            pl.BlockSpec((d, d), const),
            pl.BlockSpec((d, d), const),
        ],
        out_specs=[
            pl.BlockSpec((tm, d), row),
            pl.BlockSpec((tm, nzx - d), row),
            pl.BlockSpec((tm, LANES), row),
            pl.BlockSpec((tm, d), row),
            pl.BlockSpec((tm, d), row),
        ],
        out_shape=[
            jax.ShapeDtypeStruct((t, d), BF16),
            jax.ShapeDtypeStruct((t, nzx - d), BF16),
            jax.ShapeDtypeStruct((t, LANES), F32),
            jax.ShapeDtypeStruct((t, d), BF16),
            jax.ShapeDtypeStruct((t, d), BF16),
        ],
        compiler_params=_cparams(("arbitrary",)),
        name="in_proj",
    )(x2, g, wzx, wdt_hi, wdt_lo, wxr, wgr)


def _shift_matrix(n):
    return jnp.concatenate([jnp.eye(n, k=-(CONV_K - 1 - j), dtype=BF16) for j in range(CONV_K - 1)], axis=0)


def _causal_conv(halo_ref, x, shift_ref, w_ref, b_ref, first):
    n = x.shape[0]
    pad = SUBLANES
    k1 = CONV_K - 1

    @pl.when(first)
    def _():
        halo_ref[...] = jnp.zeros(halo_ref.shape, F32)

    xf = x.astype(F32)
    taps = _dot(shift_ref[...], x)
    acc = b_ref[...] + w_ref[k1:k1 + 1, :] * xf
    for j in range(k1):
        acc = acc + w_ref[j:j + 1, :] * taps[j * n:(j + 1) * n]
    head = acc[:pad]
    for j in range(k1):
        head = head + w_ref[j:j + 1, :] * halo_ref[pad - k1 + j:2 * pad - k1 + j, :]
    halo_ref[0:pad, :] = xf[n - pad:n]
    return jnp.concatenate([head, acc[pad:]], axis=0)


def _ssd_kernel(xbc_ref, dt_ref, z_ref, sh_ref, cw_ref, cb_ref, dtb_ref, alog_ref, dskip_ref, gn_ref, e_ref,
                y_ref, halo_ref, st_ref):
    n = xbc_ref.shape[1]
    w = z_ref.shape[2]
    gw = w // SSD_GROUPS
    first = pl.program_id(1) == 0

    @pl.when(first)
    def _():
        st_ref[...] = jnp.zeros(st_ref.shape, F32)

    conv = _causal_conv(halo_ref, xbc_ref[0], sh_ref, cw_ref, cb_ref, first)
    xc = conv * _sigmoid(conv)
    xs = xc[:, :w]

    dt = _softplus(dt_ref[0] + dtb_ref[...])
    a = -jnp.exp(alog_ref[...])
    da = dt * a
    ri = lax.broadcasted_iota(jnp.int32, (n, n), 0)
    ci = lax.broadcasted_iota(jnp.int32, (n, n), 1)
    causal = ri >= ci
    tril = jnp.where(causal, 1.0, 0.0).astype(BF16)
    a_cs = _dot01_left(tril, da)
    a_cs_t = a_cs.T

    e01 = e_ref[...]
    dt_x = _dot01_right(dt, e01)
    acs_x = _dot01_right(a_cs, e01)
    last_x = acs_x[n - 1:n, :]
    xdt = xs * dt_x
    xdt_b = xdt.astype(BF16)
    xdt_end = (xdt * jnp.exp(last_x - acs_x)).astype(BF16)
    exp_acs = jnp.exp(acs_x)
    chunk_decay = jnp.exp(last_x)
    lane = lax.broadcasted_iota(jnp.int32, (n, gw), 1)

    for g in range(SSD_GROUPS):
        lo = g * gw
        bg = xc[:, w + g * SSD_STATE:w + (g + 1) * SSD_STATE].astype(BF16)
        cg = xc[:, w + (SSD_GROUPS + g) * SSD_STATE:w + (SSD_GROUPS + g + 1) * SSD_STATE].astype(BF16)
        cb = _dot_nt(cg, bg)
        prev = st_ref[g]
        acc = _dot(cg, prev.astype(BF16)) * exp_acs[:, lo:lo + gw]
        new = lax.dot_general(bg, xdt_end[:, lo:lo + gw], (((0,), (0,)), ((), ())),
                              preferred_element_type=F32)
        st_ref[g] = chunk_decay[:, lo:lo + gw] * prev + new
        xg = xdt_b[:, lo:lo + gw]
        for k in range(SSD_HEADS // SSD_GROUPS):
            h = g * (SSD_HEADS // SSD_GROUPS) + k
            seg = a_cs[:, h:h + 1] - a_cs_t[h:h + 1, :]
            dec = jnp.exp(jnp.where(causal, seg, -jnp.inf))
            m = (cb * dec).astype(BF16)
            in_head = (lane >= k * SSD_HEAD_DIM) & (lane < (k + 1) * SSD_HEAD_DIM)
            acc = acc + _dot(m, jnp.where(in_head, xg, jnp.zeros_like(xg)))
        yg = acc + xs[:, lo:lo + gw] * dskip_ref[:, lo:lo + gw]
        zg = z_ref[0, :, lo:lo + gw].astype(F32)
        u = yg * (zg * _sigmoid(zg))
        u = u * lax.rsqrt(jnp.mean(u * u, axis=-1, keepdims=True) + NORM_EPS)
        y_ref[0, :, lo:lo + gw] = (u * gn_ref[:, lo:lo + gw]).astype(BF16)


def _ssd(xbc, dt, z, shift, cw, cb, dtb, alog, dskip, gn, e01):
    b, s, cdim = xbc.shape
    w = z.shape[2]
    n = SSD_CHUNK
    tile = lambda i, j: (i, j, 0)
    const = lambda i, j: (0, 0)
    return pl.pallas_call(
        _ssd_kernel,
        grid=(b, s // n),
        in_specs=[
            pl.BlockSpec((1, n, cdim), tile),
            pl.BlockSpec((1, n, LANES), tile),
            pl.BlockSpec((1, n, w), tile),
            pl.BlockSpec(shift.shape, const),
            pl.BlockSpec((CONV_K, cdim), const),
            pl.BlockSpec((1, cdim), const),
            pl.BlockSpec((1, LANES), const),
            pl.BlockSpec((1, LANES), const),
            pl.BlockSpec((1, w), const),
            pl.BlockSpec((1, w), const),
            pl.BlockSpec((LANES, w), const),
        ],
        out_specs=pl.BlockSpec((1, n, w), tile),
        out_shape=jax.ShapeDtypeStruct((b, s, w), BF16),
        scratch_shapes=[
            pltpu.VMEM((2 * SUBLANES, cdim), F32),
            pltpu.VMEM((SSD_GROUPS, SSD_STATE, w // SSD_GROUPS), F32),
        ],
        compiler_params=_cparams(("arbitrary", "arbitrary")),
        name="ssd",
    )(xbc, dt, z, shift, cw, cb, dtb, alog, dskip, gn, e01)


def _gelu_tanh(x):
    c = 0.7978845608028654
    return 0.5 * x * (1.0 + jnp.tanh(c * (x + 0.044715 * (x * x * x))))


def _lru_kernel(xr_ref, gr_ref, sh_ref, cw_ref, cb_ref, wa_ref, ba_ref, wx_ref, bx_ref, lam_ref,
                y_ref, halo_ref, car_ref, h_ref):
    n = xr_ref.shape[1]
    w = xr_ref.shape[2]
    first = pl.program_id(1) == 0

    @pl.when(first)
    def _():
        car_ref[...] = jnp.zeros(car_ref.shape, F32)

    xc = _causal_conv(halo_ref, xr_ref[0], sh_ref, cw_ref, cb_ref, first)
    xb = xc.astype(BF16)
    nq = wa_ref.shape[0]
    qw = w // nq
    r_parts, i_parts = [], []
    for q in range(nq):
        xq = xb[:, q * qw:(q + 1) * qw]
        r_parts.append(_dot(xq, wa_ref[q]))
        i_parts.append(_dot(xq, wx_ref[q]))
    r = _sigmoid(jnp.concatenate(r_parts, axis=1) + ba_ref[...])
    gi = _sigmoid(jnp.concatenate(i_parts, axis=1) + bx_ref[...])
    log_a = (-RG_C) * r * _softplus(-lam_ref[...])
    a = jnp.exp(log_a)
    u = xc * gi * jnp.sqrt(1.0 - a * a)

    groups = n // SUBLANES
    sub = lax.broadcasted_iota(jnp.int32, (groups, SUBLANES, w), 1)
    ap = a.reshape(groups, SUBLANES, w)
    bp = u.reshape(groups, SUBLANES, w)
    for d in (1, 2, 4):
        m = sub >= d
        bp = jnp.where(m, ap * pltpu.roll(bp, d, 1) + bp, bp)
        ap = jnp.where(m, ap * pltpu.roll(ap, d, 1), ap)
    carry = car_ref[...]
    for g in range(groups):
        hb = bp[g] + ap[g] * carry
        h_ref[g * SUBLANES:(g + 1) * SUBLANES, :] = hb
        carry = jnp.broadcast_to(hb[SUBLANES - 1:SUBLANES, :], (SUBLANES, w))
    car_ref[...] = carry
    y_ref[0] = (h_ref[...] * _gelu_tanh(gr_ref[0].astype(F32))).astype(BF16)


def _lru(xr, gr, shift, cw, cb, wa, ba, wx, bx, lam):
    b, s, w = xr.shape
    n = LRU_TILE
    nq, qw, _ = wa.shape
    tile = lambda i, j: (i, j, 0)
    const = lambda i, j: (0, 0)
    const3 = lambda i, j: (0, 0, 0)
    return pl.pallas_call(
        _lru_kernel,
        grid=(b, s // n),
        in_specs=[
            pl.BlockSpec((1, n, w), tile),
            pl.BlockSpec((1, n, w), tile),
            pl.BlockSpec(shift.shape, const),
            pl.BlockSpec((CONV_K, w), const),
            pl.BlockSpec((1, w), const),
            pl.BlockSpec((nq, qw, qw), const3),
            pl.BlockSpec((1, w), const),
            pl.BlockSpec((nq, qw, qw), const3),
            pl.BlockSpec((1, w), const),
            pl.BlockSpec((1, w), const),
        ],
        out_specs=pl.BlockSpec((1, n, w), tile),
        out_shape=jax.ShapeDtypeStruct((b, s, w), BF16),
        scratch_shapes=[
            pltpu.VMEM((2 * SUBLANES, w), F32),
            pltpu.VMEM((SUBLANES, w), F32),
            pltpu.VMEM((n, w), F32),
        ],
        compiler_params=_cparams(("arbitrary", "arbitrary")),
        name="lru",
    )(xr, gr, shift, cw, cb, wa, ba, wx, bx, lam)


def _kv_kernel(m_ref, g_ref, w_ref, k_ref, v_ref):
    d = m_ref.shape[1]
    mn = _rms(m_ref[...], g_ref[...]).astype(BF16)
    k_ref[...] = _dot(mn, w_ref[:, :d]).astype(BF16)
    v_ref[...] = _dot(mn, w_ref[:, d:]).astype(BF16)


def _kv(mem2, g, wkv):
    t, d = mem2.shape
    tm = min(t, 512)
    row = lambda i: (i, 0)
    const = lambda i: (0, 0)
    return pl.pallas_call(
        _kv_kernel,
        grid=(t // tm,),
        in_specs=[pl.BlockSpec((tm, d), row), pl.BlockSpec((1, d), const), pl.BlockSpec((d, 2 * d), const)],
        out_specs=[pl.BlockSpec((tm, d), row), pl.BlockSpec((tm, d), row)],
        out_shape=[jax.ShapeDtypeStruct((t, d), BF16), jax.ShapeDtypeStruct((t, d), BF16)],
        compiler_params=_cparams(("arbitrary",)),
        name="kv",
    )(mem2, g, wkv)


def _mid_kernel(x_ref, ys_ref, yl_ref, k_ref, v_ref, wo1_ref, wo2_ref, gx_ref, wq_ref, wo_ref,
                gm_ref, wrh_ref, wrl_ref, br_ref,
                h_ref, hn_ref, idx_ref, rank_ref, gate_ref, cnt_ref, car_ref):
    tm, d = x_ref.shape
    hd = d // X_HEADS

    @pl.when(pl.program_id(0) == 0)
    def _():
        car_ref[...] = jnp.zeros(car_ref.shape, F32)

    h1 = x_ref[...] + _dot(ys_ref[...], wo1_ref[...]) + _dot(yl_ref[...], wo2_ref[...])

    q = _dot(_rms(h1, gx_ref[...]).astype(BF16), wq_ref[...]).astype(BF16)
    o_parts = []
    for hh in range(X_HEADS):
        sl = slice(hh * hd, (hh + 1) * hd)
        sc = _dot_nt(q[:, sl], k_ref[0, :, sl]) * (hd ** -0.5)
        sc = sc - jnp.max(sc, axis=-1, keepdims=True)
        p = jnp.exp(sc)
        p = p / jnp.sum(p, axis=-1, keepdims=True)
        o_parts.append(_dot(p.astype(BF16), v_ref[0, :, sl]))
    o = jnp.concatenate(o_parts, axis=1).astype(BF16)
    h2 = h1 + _dot(o, wo_ref[...])
    h_ref[...] = h2

    hn = _rms(h2, gm_ref[...])
    _store_token_major(hn_ref, hn)
    logits = _dot_hilo(hn, wrh_ref[...], wrl_ref[...]) + br_ref[...]

    l = logits.T[:N_EXPERTS, :]
    row = lax.broadcasted_iota(jnp.int32, (N_EXPERTS, tm), 0)
    picked = jnp.zeros((N_EXPERTS, tm), F32)
    vals, idxs = [], []
    for _ in range(TOP_K):
        m = jnp.max(l, axis=0, keepdims=True)
        idx = jnp.min(jnp.where(l == m, row, N_EXPERTS), axis=0, keepdims=True)
        sel = row == idx
        vals.append(m)
        idxs.append(idx)
        picked = jnp.where(sel, 1.0, picked)
        l = jnp.where(sel, -jnp.inf, l)
    ex = [jnp.exp(v - vals[0]) for v in vals]
    den = ex[0] + ex[1] + ex[2] + ex[3]

    ri = lax.broadcasted_iota(jnp.int32, (tm, tm), 0)
    ci = lax.broadcasted_iota(jnp.int32, (tm, tm), 1)
    earlier = jnp.where(ri < ci, 1.0, 0.0).astype(BF16)
    before = _dot(picked.astype(BF16), earlier) + car_ref[:, 0:1]
    out_row = lax.broadcasted_iota(jnp.int32, (SUBLANES, tm), 0)
    idx_out = jnp.zeros((SUBLANES, tm), jnp.int32)
    rank_out = jnp.zeros((SUBLANES, tm), jnp.int32)
    gate_out = jnp.zeros((SUBLANES, tm), F32)
    for k in range(TOP_K):
        rk = jnp.sum(jnp.where(row == idxs[k], before, 0.0), axis=0, keepdims=True)
        at_k = out_row == k
        idx_out = jnp.where(at_k, idxs[k], idx_out)
        rank_out = jnp.where(at_k, rk.astype(jnp.int32), rank_out)
        gate_out = jnp.where(at_k, ex[k] / den, gate_out)
    idx_ref[...] = idx_out
    rank_ref[...] = rank_out
    gate_ref[...] = gate_out
    total = car_ref[...] + jnp.sum(picked, axis=1, keepdims=True)
    car_ref[...] = total
    cnt_ref[...] = total.astype(jnp.int32)


def _mid(x2, ys, yl, kk, vv, wo1, wo2, gx, wq, wo, gm, wr_hi, wr_lo, br, seq):
    t, d = x2.shape
    tm = MID_TILE
    m = kk.shape[1]
    per_b = seq // tm
    row = lambda i: (i, 0)
    col = lambda i: (0, i)
    const = lambda i: (0, 0)
    kvmap = lambda i: (i // per_b, 0, 0)
    wspec = pl.BlockSpec((d, d), const)
    vspec = pl.BlockSpec((1, d), const)
    return pl.pallas_call(
        _mid_kernel,
        grid=(t // tm,),
        in_specs=[
            pl.BlockSpec((tm, d), row), pl.BlockSpec((tm, d), row), pl.BlockSpec((tm, d), row),
            pl.BlockSpec((1, m, d), kvmap), pl.BlockSpec((1, m, d), kvmap),
            wspec, wspec, vspec, wspec, wspec, vspec,
            pl.BlockSpec((d, LANES), const), pl.BlockSpec((d, LANES), const), pl.BlockSpec((1, LANES), const),
        ],
        out_specs=[
            pl.BlockSpec((tm, d), row), pl.BlockSpec((tm * d // LANES, LANES), row),
            pl.BlockSpec((SUBLANES, tm), col), pl.BlockSpec((SUBLANES, tm), col), pl.BlockSpec((SUBLANES, tm), col),
            pl.BlockSpec((N_EXPERTS, LANES), const),
        ],
        out_shape=[
            jax.ShapeDtypeStruct((t, d), F32), jax.ShapeDtypeStruct((t * d // LANES, LANES), F32),
            jax.ShapeDtypeStruct((SUBLANES, t), jnp.int32), jax.ShapeDtypeStruct((SUBLANES, t), jnp.int32),
            jax.ShapeDtypeStruct((SUBLANES, t), F32),
            jax.ShapeDtypeStruct((N_EXPERTS, LANES), jnp.int32),
        ],
        scratch_shapes=[pltpu.VMEM((N_EXPERTS, LANES), F32)],
        compiler_params=_cparams(("arbitrary",)),
        name="mid",
    )(x2, ys, yl, kk, vv, wo1, wo2, gx, wq, wo, gm, wr_hi, wr_lo, br)


def _dispatch_kernel(pend_ref, dst_ref, hn_ref, x_hbm, zero_ref, sem, *, parts):
    i = pl.program_id(0)
    tc = hn_ref.shape[0] // parts
    bm = zero_ref.shape[0] // parts
    rows = TOP_K * tc

    def tile(ref, n, count=1):
        return ref.at[pl.ds(pl.multiple_of(n * parts, parts), count * parts), :]

    @pl.when(i == 0)
    def _():
        zero_ref[...] = jnp.zeros(zero_ref.shape, F32)

        def fill(start):
            return pltpu.make_async_copy(zero_ref, tile(x_hbm, start, bm), sem.at[1])

        for e in range(N_EXPERTS):
            fill(jnp.maximum(pend_ref[e] - bm, 0)).start()
        for e in range(N_EXPERTS):
            fill(0).wait()

        def tail(j, c):
            fill(j * bm).start()
            fill(0).wait()
            return c

        lax.fori_loop(pend_ref[N_EXPERTS - 1] // bm, x_hbm.shape[0] // (bm * parts), tail, 0)

    def body(j, c):
        t0 = pl.multiple_of(j * ISSUE_UNROLL, ISSUE_UNROLL)
        for u in range(ISSUE_UNROLL):
            for k in range(TOP_K):
                dst = dst_ref[0, 0, k * tc + t0 + u]
                pltpu.make_async_copy(tile(hn_ref, t0 + u), tile(x_hbm, dst), sem.at[0]).start(priority=k % 2)
        return c

    lax.fori_loop(0, tc // ISSUE_UNROLL, body, 0)
    pltpu.make_async_copy(tile(x_hbm, 0, rows), tile(x_hbm, 0, rows), sem.at[0]).wait()


def _dispatch(pend, dest3, hn_tm, n_rows, d):
    parts = d // LANES
    t = hn_tm.shape[0] // parts
    tc = COMB_TILE
    grid_spec = pltpu.PrefetchScalarGridSpec(
        num_scalar_prefetch=1,
        grid=(t // tc,),
        in_specs=[
            pl.BlockSpec((1, 1, TOP_K * tc), lambda i, pe: (i, 0, 0), memory_space=pltpu.SMEM),
            pl.BlockSpec((tc * parts, LANES), lambda i, pe: (i, 0)),
        ],
        out_specs=pl.BlockSpec(memory_space=pl.ANY),
        scratch_shapes=[pltpu.VMEM((MOE_BLOCK * parts, LANES), F32), pltpu.SemaphoreType.DMA((2,))],
    )
    return pl.pallas_call(
        functools.partial(_dispatch_kernel, parts=parts),
        grid_spec=grid_spec,
        out_shape=jax.ShapeDtypeStruct((n_rows * parts, LANES), F32),
        compiler_params=_cparams(("arbitrary",)),
        name="dispatch",
    )(pend, dest3, hn_tm)


def _moe_kernel(be_ref, nxt_ref, nvb_ref, x_ref, wgu_hbm, bg_ref, bu_ref, wd_hbm, bd_ref, perm_ref,
                y_ref, wgu_f, wd_f, wgu_s, wd_s, act_s, sem):
    i = pl.program_id(0)
    nvb = nvb_ref[0]
    f, d = wd_f.shape
    bm = act_s.shape[0]
    pw = perm_ref.shape[0]
    half = pw // 2

    def fetch(e):
        return (pltpu.make_async_copy(wgu_hbm.at[e], wgu_f, sem.at[0]),
                pltpu.make_async_copy(wd_hbm.at[e], wd_f, sem.at[1]))

    @pl.when(jnp.logical_and(i == 0, nvb > 0))
    def _():
        for c in fetch(be_ref[0]):
            c.start()

    changed = jnp.logical_or(i == 0, be_ref[i] != be_ref[jnp.maximum(i - 1, 0)])

    @pl.when(jnp.logical_and(changed, i < nvb))
    def _():
        for c in fetch(be_ref[i]):
            c.wait()
        for c in range(2 * f // pw):
            wc = wgu_f[:, c * pw:(c + 1) * pw].astype(BF16)
            wgu_s[:, c * pw:(c + 1) * pw] = _dot(wc, perm_ref[...]).astype(BF16)
        wd_s[...] = wd_f[...].astype(BF16)

        @pl.when(nxt_ref[i] >= 0)
        def _():
            for c in fetch(nxt_ref[i]):
                c.start()

    @pl.when(i < nvb)
    def _():
        xb = jnp.concatenate([p.astype(BF16) for p in _load_token_major(x_ref, bm, d)], axis=1)
        for c in range(2 * f // pw):
            gu = _dot(xb, wgu_s[:, c * pw:(c + 1) * pw])
            g = gu[:, :half] + bg_ref[0, :, c * half:(c + 1) * half]
            u = gu[:, half:] + bu_ref[0, :, c * half:(c + 1) * half]
            g = jnp.minimum(g, SWIGLU_LIMIT)
            u = jnp.clip(u, -SWIGLU_LIMIT, SWIGLU_LIMIT)
            act = (u + 1.0) * (g * _sigmoid(SWIGLU_ALPHA * g))
            act_s[:, c * half:(c + 1) * half] = act.astype(BF16)
        _store_token_major(y_ref, _dot(act_s[...], wd_s[...]) + bd_ref[0])

    @pl.when(i >= nvb)
    def _():
        y_ref[...] = jnp.zeros(y_ref.shape, F32)


def _moe(block_e, next_e, nvb, xrows_tm, wgu, bg, bu, wd, bd, perm):
    f, d = wd.shape[1], wd.shape[2]
    parts = d // LANES
    bm = MOE_BLOCK
    nb = xrows_tm.shape[0] // (bm * parts)
    emap = lambda i, be, nx, nv: (be[i], 0, 0)
    grid_spec = pltpu.PrefetchScalarGridSpec(
        num_scalar_prefetch=3,
        grid=(nb,),
        in_specs=[
            pl.BlockSpec((bm * parts, LANES),
                         lambda i, be, nx, nv: (jnp.minimum(i, jnp.maximum(nv[0] - 1, 0)), 0)),
            pl.BlockSpec(memory_space=pl.ANY),
            pl.BlockSpec((1, 1, f), emap), pl.BlockSpec((1, 1, f), emap),
            pl.BlockSpec(memory_space=pl.ANY),
            pl.BlockSpec((1, 1, d), emap),
            pl.BlockSpec(perm.shape, lambda i, be, nx, nv: (0, 0)),
        ],
        out_specs=pl.BlockSpec((bm * parts, LANES), lambda i, be, nx, nv: (i, 0)),
        scratch_shapes=[
            pltpu.VMEM((d, 2 * f), F32), pltpu.VMEM((f, d), F32),
            pltpu.VMEM((d, 2 * f), BF16), pltpu.VMEM((f, d), BF16),
            pltpu.VMEM((bm, f), BF16),
            pltpu.SemaphoreType.DMA((2,)),
        ],
    )
    return pl.pallas_call(
        _moe_kernel,
        grid_spec=grid_spec,
        out_shape=jax.ShapeDtypeStruct(xrows_tm.shape, F32),
        compiler_params=_cparams(("arbitrary",)),
        name="moe",
    )(block_e, next_e, nvb, xrows_tm, wgu, bg, bu, wd, bd, perm)


def _comb_kernel(dst_ref, dstn_ref, h_ref, gate_ref, y_hbm, g_ref, o_ref, ybuf, sem, *, parts):
    i = pl.program_id(0)
    n = pl.num_programs(0)
    tc, d = h_ref.shape
    rows = TOP_K * tc
    slot = i % 2

    def tile(ref, n, count=1):
        return ref.at[pl.ds(pl.multiple_of(n * parts, parts), count * parts), :]

    def start_rows(idx_ref, s):
        def body(j, c):
            t0 = pl.multiple_of(j * ISSUE_UNROLL, ISSUE_UNROLL)
            for u in range(ISSUE_UNROLL):
                for k in range(TOP_K):
                    r = k * tc + t0 + u
                    pltpu.make_async_copy(tile(y_hbm, idx_ref[0, 0, r]), tile(ybuf.at[s], r),
                                          sem.at[s]).start(priority=k % 2)
            return c

        lax.fori_loop(0, tc // ISSUE_UNROLL, body, 0)

    @pl.when(i == 0)
    def _():
        start_rows(dst_ref, 0)

    @pl.when(i + 1 < n)
    def _():
        start_rows(dstn_ref, 1 - slot)

    pltpu.make_async_copy(tile(y_hbm, 0, rows), ybuf.at[slot], sem.at[slot]).wait()
    acc = [h_ref[:, s * LANES:(s + 1) * LANES] for s in range(parts)]
    for k in range(TOP_K):
        yk = _load_token_major(ybuf.at[slot], tc, d, base=k * tc * parts)
        gk = gate_ref[:, k:k + 1]
        acc = [a + p * gk for a, p in zip(acc, yk)]
    o_ref[...] = _rms(jnp.concatenate(acc, axis=1), g_ref[...])


def _combine(dest3, h2, gate, y_tm, g):
    t, d = h2.shape
    parts = d // LANES
    tc = COMB_TILE
    nt = t // tc
    return pl.pallas_call(
        functools.partial(_comb_kernel, parts=parts),
        grid=(nt,),
        in_specs=[
            pl.BlockSpec((1, 1, TOP_K * tc), lambda i: (i, 0, 0), memory_space=pltpu.SMEM),
            pl.BlockSpec((1, 1, TOP_K * tc), lambda i: (jnp.minimum(i + 1, nt - 1), 0, 0),
                         memory_space=pltpu.SMEM),
            pl.BlockSpec((tc, d), lambda i: (i, 0)),
            pl.BlockSpec((tc, SUBLANES), lambda i: (i, 0)),
            pl.BlockSpec(memory_space=pl.ANY),
            pl.BlockSpec((1, d), lambda i: (0, 0)),
        ],
        out_specs=pl.BlockSpec((tc, d), lambda i: (i, 0)),
        out_shape=jax.ShapeDtypeStruct((t, d), F32),
        scratch_shapes=[pltpu.VMEM((2, TOP_K * tc * parts, LANES), F32), pltpu.SemaphoreType.DMA((2,))],
        compiler_params=_cparams(("arbitrary",)),
        name="combine",
    )(dest3, dest3, h2, gate, y_tm, g)


def _block_diag(wb, per):
    nb, bw, _ = wb.shape
    wq = wb.reshape(nb // per, per, bw, bw)
    eye = jnp.eye(per, dtype=wb.dtype)
    out = jnp.einsum('qaij,ab->qaibj', wq, eye)
    return out.reshape(nb // per, per * bw, per * bw)


def _pad_lanes(v, fill=0.0):
    return jnp.pad(v, (0, LANES - v.shape[0]), constant_values=fill).reshape(1, LANES)


def kernel(x, mem, norm_mix, w_in, ssd_conv_w, ssd_conv_b, ssd_dt_bias, ssd_a_log, ssd_d, ssd_norm, lru_conv_w, lru_conv_b, lru_wa, lru_ba, lru_wx, lru_bx, lru_lambda, w_out, norm_xattn, norm_mem, w_q, w_kv, w_o, norm_moe, w_router, b_router, w_gate_up, b_gate_up, w_down, b_down, norm_final):
    b, s, d = x.shape
    t = b * s
    n_mem = mem.shape[1]
    w = d
    cdim = w + 2 * SSD_GROUPS * SSD_STATE
    o1, o2, o3, o4 = w, w + cdim, w + cdim + SSD_HEADS, w + cdim + SSD_HEADS + w

    wi = w_in[0]
    wzx = wi[:, :o2].astype(BF16)
    wdt = jnp.pad(wi[:, o2:o3], ((0, 0), (0, LANES - SSD_HEADS)))
    wdt_hi, wdt_lo = _hilo(wdt)
    wxr = wi[:, o3:o4].astype(BF16)
    wgr = wi[:, o4:].astype(BF16)
    e01 = (jnp.arange(LANES)[:, None] == (jnp.arange(w)[None, :] // SSD_HEAD_DIM)).astype(BF16)
    dskip = jnp.repeat(ssd_d[0], SSD_HEAD_DIM).reshape(1, w)
    per = 256 // (w // LRU_BLOCKS)
    wa_bd = _block_diag(lru_wa[0], per).astype(BF16)
    wx_bd = _block_diag(lru_wx[0], per).astype(BF16)
    wr = jnp.pad(w_router[0], ((0, 0), (0, LANES - N_EXPERTS)))
    wr_hi, wr_lo = _hilo(wr)
    br = _pad_lanes(b_router[0], fill=-1e30)
    bgu = b_gate_up[0]
    bg = bgu[:, None, 0::2]
    bu = bgu[:, None, 1::2]
    bd = b_down[0][:, None, :]
    pw = 2 * LANES
    col = jnp.arange(pw)
    src_col = jnp.where(col < LANES, 2 * col, 2 * (col - LANES) + 1)
    perm = (jnp.arange(pw)[:, None] == src_col[None, :]).astype(BF16)

    x2 = x.reshape(t, d)
    z, xbc, dt, xr, gr = _in_proj(x2, norm_mix[0].reshape(1, d), wzx, wdt_hi, wdt_lo, wxr, wgr)

    y_ssd = _ssd(xbc.reshape(b, s, cdim), dt.reshape(b, s, LANES), z.reshape(b, s, w),
                 _shift_matrix(SSD_CHUNK), ssd_conv_w[0], ssd_conv_b[0].reshape(1, cdim), _pad_lanes(ssd_dt_bias[0]),
                 _pad_lanes(ssd_a_log[0]), dskip, ssd_norm[0].reshape(1, w), e01)
    y_lru = _lru(xr.reshape(b, s, w), gr.reshape(b, s, w), _shift_matrix(LRU_TILE), lru_conv_w[0], lru_conv_b[0].reshape(1, w),
                 wa_bd, lru_ba[0].reshape(1, w), wx_bd, lru_bx[0].reshape(1, w), lru_lambda[0].reshape(1, w))

    kk, vv = _kv(mem.reshape(b * n_mem, d), norm_mem[0].reshape(1, d), w_kv[0].astype(BF16))
    wo_mix = w_out[0].astype(BF16)
    h2, hn, idx_m, rank_m, gate_m, cnt = _mid(
        x2, y_ssd.reshape(t, w), y_lru.reshape(t, w), kk.reshape(b, n_mem, d), vv.reshape(b, n_mem, d),
        wo_mix[:w], wo_mix[w:], norm_xattn[0].reshape(1, d), w_q[0].astype(BF16), w_o[0].astype(BF16),
        norm_moe[0].reshape(1, d), wr_hi, wr_lo, br, s)

    bm = MOE_BLOCK
    counts = cnt[:, 0]
    pcounts = (counts + bm - 1) // bm * bm
    pend = jnp.cumsum(pcounts).astype(jnp.int32)
    pstart = pend - pcounts
    idx = idx_m[:TOP_K, :]
    onehot = idx[:, :, None] == jnp.arange(N_EXPERTS, dtype=jnp.int32)[None, None, :]
    dest = jnp.sum(jnp.where(onehot, pstart[None, None, :], 0), axis=-1) + rank_m[:TOP_K, :]
    n_pairs = t * TOP_K
    nb = (n_pairs + N_EXPERTS * (bm - 1) + bm - 1) // bm
    blk0 = jnp.arange(nb, dtype=jnp.int32) * bm
    block_e = jnp.minimum(jnp.sum(pend[None, :] <= blk0[:, None], axis=1), N_EXPERTS - 1).astype(jnp.int32)
    nvb = (pend[-1] // bm).astype(jnp.int32).reshape(1)
    after = pend[block_e] // bm
    next_e = jnp.where(after < nvb[0], block_e[jnp.minimum(after, nb - 1)], -1).astype(jnp.int32)
    tc = COMB_TILE
    dest3 = dest.astype(jnp.int32).reshape(TOP_K, t // tc, tc).transpose(1, 0, 2).reshape(t // tc, 1, TOP_K * tc)

    xrows = _dispatch(pend, dest3, hn, nb * bm, d)
    y = _moe(block_e, next_e, nvb, xrows, w_gate_up[0], bg, bu, w_down[0], bd, perm)
    out = _combine(dest3, h2, gate_m.T, y, norm_final.reshape(1, d))
    return out.reshape(b, s, d)
```

```python
import functools

import jax
import jax.numpy as jnp
from jax import lax
from jax.experimental import pallas as pl
from jax.experimental.pallas import tpu as pltpu

F32 = jnp.float32
BF16 = jnp.bfloat16

NORM_EPS = 1e-6
LANES = 128
SUBLANES = 8
SSD_HEAD_DIM = 64
SSD_HEADS = 16
SSD_GROUPS = 4
SSD_STATE = 128
CONV_K = 4
LRU_BLOCKS = 16
RG_C = 8.0
X_HEADS = 4
N_EXPERTS = 32
TOP_K = 4
SWIGLU_LIMIT = 7.0
SWIGLU_ALPHA = 1.702

VMEM_LIMIT = 56 * 1024 * 1024

IN_TILE = 1024
SSD_CHUNK = 256
LRU_TILE = 256
MID_TILE = 512
MOE_BLOCK = 512
COMB_TILE = 256
ISSUE_UNROLL = 8


def _cparams(sem):
    return pltpu.CompilerParams(dimension_semantics=sem, vmem_limit_bytes=VMEM_LIMIT)


def _rms(x, g):
    ms = jnp.mean(x * x, axis=-1, keepdims=True)
    return x * lax.rsqrt(ms + NORM_EPS) * g


def _sigmoid(x):
    return 0.5 * jnp.tanh(0.5 * x) + 0.5


def _softplus(x):
    return jnp.maximum(x, 0.0) + jnp.log(1.0 + jnp.exp(-jnp.abs(x)))


def _split3(x):
    a = x.astype(BF16)
    r = x - a.astype(F32)
    b = r.astype(BF16)
    c = (r - b.astype(F32)).astype(BF16)
    return a, b, c


def _dot(a, b):
    return jnp.dot(a, b, preferred_element_type=F32)


def _dot_nt(a, b):
    return lax.dot_general(a, b, (((1,), (1,)), ((), ())), preferred_element_type=F32)


def _dot01_right(x, m01):
    a, b, c = _split3(x)
    return _dot(a, m01) + _dot(b, m01) + _dot(c, m01)


def _dot01_left(m01, x):
    a, b, c = _split3(x)
    return _dot(m01, a) + _dot(m01, b) + _dot(m01, c)


def _dot_hilo(x, w_hi, w_lo):
    xh = x.astype(BF16)
    xl = (x - xh.astype(F32)).astype(BF16)
    return _dot(xh, w_hi) + _dot(xl, w_hi) + _dot(xh, w_lo)


def _hilo(w):
    hi = w.astype(BF16)
    lo = (w - hi.astype(F32)).astype(BF16)
    return hi, lo


def _store_token_major(ref, val, base=0):
    n, d = val.shape
    parts = d // LANES
    for s in range(parts):
        ref[pl.ds(base + s, n, stride=parts), :] = val[:, s * LANES:(s + 1) * LANES]


def _load_token_major(ref, n, d, base=0):
    parts = d // LANES
    return [ref[pl.ds(base + s, n, stride=parts), :] for s in range(parts)]


def _in_proj_kernel(x_ref, g_ref, wzx_ref, wdth_ref, wdtl_ref, wxr_ref, wgr_ref,
                    z_ref, xbc_ref, dt_ref, xr_ref, gr_ref):
    d = x_ref.shape[1]
    hn = _rms(x_ref[...], g_ref[...])
    hb = hn.astype(BF16)
    z_ref[...] = _dot(hb, wzx_ref[:, :d]).astype(BF16)
    xbc_ref[...] = _dot(hb, wzx_ref[:, d:]).astype(BF16)
    dt_ref[...] = _dot_hilo(hn, wdth_ref[...], wdtl_ref[...])
    xr_ref[...] = _dot(hb, wxr_ref[...]).astype(BF16)
    gr_ref[...] = _dot(hb, wgr_ref[...]).astype(BF16)


def _in_proj(x2, g, wzx, wdt_hi, wdt_lo, wxr, wgr):
    t, d = x2.shape
    tm = IN_TILE
    nzx = wzx.shape[1]
    const = lambda i: (0, 0)
    row = lambda i: (i, 0)
    return pl.pallas_call(
        _in_proj_kernel,
        grid=(t // tm,),
        in_specs=[
            pl.BlockSpec((tm, d), row),
            pl.BlockSpec((1, d), const),
            pl.BlockSpec((d, nzx), const),
            pl.BlockSpec((d, LANES), const),
            pl.BlockSpec((d, LANES), const),
            pl.BlockSpec((d, d), const),
            pl.BlockSpec((d, d), const),
        ],
        out_specs=[
            pl.BlockSpec((tm, d), row),
            pl.BlockSpec((tm, nzx - d), row),
            pl.BlockSpec((tm, LANES), row),
            pl.BlockSpec((tm, d), row),
            pl.BlockSpec((tm, d), row),
        ],
        out_shape=[
            jax.ShapeDtypeStruct((t, d), BF16),
            jax.ShapeDtypeStruct((t, nzx - d), BF16),
            jax.ShapeDtypeStruct((t, LANES), F32),
            jax.ShapeDtypeStruct((t, d), BF16),
            jax.ShapeDtypeStruct((t, d), BF16),
        ],
        compiler_params=_cparams(("arbitrary",)),
        name="in_proj",
    )(x2, g, wzx, wdt_hi, wdt_lo, wxr, wgr)


def _shift_matrix(n):
    return jnp.concatenate([jnp.eye(n, k=-(CONV_K - 1 - j), dtype=BF16) for j in range(CONV_K - 1)], axis=0)


def _causal_conv(halo_ref, x, shift_ref, w_ref, b_ref, first):
    n = x.shape[0]
    pad = SUBLANES
    k1 = CONV_K - 1

    @pl.when(first)
    def _():
        halo_ref[...] = jnp.zeros(halo_ref.shape, F32)

    xf = x.astype(F32)
    taps = _dot(shift_ref[...], x)
    acc = b_ref[...] + w_ref[k1:k1 + 1, :] * xf
    for j in range(k1):
        acc = acc + w_ref[j:j + 1, :] * taps[j * n:(j + 1) * n]
    head = acc[:pad]
    for j in range(k1):
        head = head + w_ref[j:j + 1, :] * halo_ref[pad - k1 + j:2 * pad - k1 + j, :]
    halo_ref[0:pad, :] = xf[n - pad:n]
    return jnp.concatenate([head, acc[pad:]], axis=0)


def _ssd_kernel(xbc_ref, dt_ref, z_ref, sh_ref, cw_ref, cb_ref, dtb_ref, alog_ref, dskip_ref, gn_ref, e_ref,
                y_ref, halo_ref, st_ref):
    n = xbc_ref.shape[1]
    w = z_ref.shape[2]
    gw = w // SSD_GROUPS
    first = pl.program_id(1) == 0

    @pl.when(first)
    def _():
        st_ref[...] = jnp.zeros(st_ref.shape, F32)

    conv = _causal_conv(halo_ref, xbc_ref[0], sh_ref, cw_ref, cb_ref, first)
    xc = conv * _sigmoid(conv)
    xs = xc[:, :w]

    dt = _softplus(dt_ref[0] + dtb_ref[...])
    a = -jnp.exp(alog_ref[...])
    da = dt * a
    ri = lax.broadcasted_iota(jnp.int32, (n, n), 0)
    ci = lax.broadcasted_iota(jnp.int32, (n, n), 1)
    causal = ri >= ci
    tril = jnp.where(causal, 1.0, 0.0).astype(BF16)
    a_cs = _dot01_left(tril, da)
    a_cs_t = a_cs.T

    e01 = e_ref[...]
    dt_x = _dot01_right(dt, e01)
    acs_x = _dot01_right(a_cs, e01)
    last_x = acs_x[n - 1:n, :]
    xdt = xs * dt_x
    xdt_b = xdt.astype(BF16)
    xdt_end = (xdt * jnp.exp(last_x - acs_x)).astype(BF16)
    exp_acs = jnp.exp(acs_x)
    chunk_decay = jnp.exp(last_x)
    lane = lax.broadcasted_iota(jnp.int32, (n, gw), 1)

    for g in range(SSD_GROUPS):
        lo = g * gw
        bg = xc[:, w + g * SSD_STATE:w + (g + 1) * SSD_STATE].astype(BF16)
        cg = xc[:, w + (SSD_GROUPS + g) * SSD_STATE:w + (SSD_GROUPS + g + 1) * SSD_STATE].astype(BF16)
        cb = _dot_nt(cg, bg)
        prev = st_ref[g]
        acc = _dot(cg, prev.astype(BF16)) * exp_acs[:, lo:lo + gw]
        new = lax.dot_general(bg, xdt_end[:, lo:lo + gw], (((0,), (0,)), ((), ())),
                              preferred_element_type=F32)
        st_ref[g] = chunk_decay[:, lo:lo + gw] * prev + new
        xg = xdt_b[:, lo:lo + gw]
        for k in range(SSD_HEADS // SSD_GROUPS):
            h = g * (SSD_HEADS // SSD_GROUPS) + k
            seg = a_cs[:, h:h + 1] - a_cs_t[h:h + 1, :]
            dec = jnp.exp(jnp.where(causal, seg, -jnp.inf))
            m = (cb * dec).astype(BF16)
            in_head = (lane >= k * SSD_HEAD_DIM) & (lane < (k + 1) * SSD_HEAD_DIM)
            acc = acc + _dot(m, jnp.where(in_head, xg, jnp.zeros_like(xg)))
        yg = acc + xs[:, lo:lo + gw] * dskip_ref[:, lo:lo + gw]
        zg = z_ref[0, :, lo:lo + gw].astype(F32)
        u = yg * (zg * _sigmoid(zg))
        u = u * lax.rsqrt(jnp.mean(u * u, axis=-1, keepdims=True) + NORM_EPS)
        y_ref[0, :, lo:lo + gw] = (u * gn_ref[:, lo:lo + gw]).astype(BF16)


def _ssd(xbc, dt, z, shift, cw, cb, dtb, alog, dskip, gn, e01):
    b, s, cdim = xbc.shape
    w = z.shape[2]
    n = SSD_CHUNK
    tile = lambda i, j: (i, j, 0)
    const = lambda i, j: (0, 0)
    return pl.pallas_call(
        _ssd_kernel,
        grid=(b, s // n),
        in_specs=[
            pl.BlockSpec((1, n, cdim), tile),
            pl.BlockSpec((1, n, LANES), tile),
            pl.BlockSpec((1, n, w), tile),
            pl.BlockSpec(shift.shape, const),
            pl.BlockSpec((CONV_K, cdim), const),
            pl.BlockSpec((1, cdim), const),
            pl.BlockSpec((1, LANES), const),
            pl.BlockSpec((1, LANES), const),
            pl.BlockSpec((1, w), const),
            pl.BlockSpec((1, w), const),
            pl.BlockSpec((LANES, w), const),
        ],
        out_specs=pl.BlockSpec((1, n, w), tile),
        out_shape=jax.ShapeDtypeStruct((b, s, w), BF16),
        scratch_shapes=[
            pltpu.VMEM((2 * SUBLANES, cdim), F32),
            pltpu.VMEM((SSD_GROUPS, SSD_STATE, w // SSD_GROUPS), F32),
        ],
        compiler_params=_cparams(("arbitrary", "arbitrary")),
        name="ssd",
    )(xbc, dt, z, shift, cw, cb, dtb, alog, dskip, gn, e01)


def _gelu_tanh(x):
    c = 0.7978845608028654
    return 0.5 * x * (1.0 + jnp.tanh(c * (x + 0.044715 * (x * x * x))))


def _lru_kernel(xr_ref, gr_ref, sh_ref, cw_ref, cb_ref, wa_ref, ba_ref, wx_ref, bx_ref, lam_ref,
                y_ref, halo_ref, car_ref, h_ref):
    n = xr_ref.shape[1]
    w = xr_ref.shape[2]
    first = pl.program_id(1) == 0

    @pl.when(first)
    def _():
        car_ref[...] = jnp.zeros(car_ref.shape, F32)

    xc = _causal_conv(halo_ref, xr_ref[0], sh_ref, cw_ref, cb_ref, first)
    xb = xc.astype(BF16)
    nq = wa_ref.shape[0]
    qw = w // nq
    r_parts, i_parts = [], []
    for q in range(nq):
        xq = xb[:, q * qw:(q + 1) * qw]
        r_parts.append(_dot(xq, wa_ref[q]))
        i_parts.append(_dot(xq, wx_ref[q]))
    r = _sigmoid(jnp.concatenate(r_parts, axis=1) + ba_ref[...])
    gi = _sigmoid(jnp.concatenate(i_parts, axis=1) + bx_ref[...])
    log_a = (-RG_C) * r * _softplus(-lam_ref[...])
    a = jnp.exp(log_a)
    u = xc * gi * jnp.sqrt(1.0 - a * a)

    groups = n // SUBLANES
    sub = lax.broadcasted_iota(jnp.int32, (groups, SUBLANES, w), 1)
    ap = a.reshape(groups, SUBLANES, w)
    bp = u.reshape(groups, SUBLANES, w)
    for d in (1, 2, 4):
        m = sub >= d
        bp = jnp.where(m, ap * pltpu.roll(bp, d, 1) + bp, bp)
        ap = jnp.where(m, ap * pltpu.roll(ap, d, 1), ap)
    carry = car_ref[...]
    for g in range(groups):
        hb = bp[g] + ap[g] * carry
        h_ref[g * SUBLANES:(g + 1) * SUBLANES, :] = hb
        carry = jnp.broadcast_to(hb[SUBLANES - 1:SUBLANES, :], (SUBLANES, w))
    car_ref[...] = carry
    y_ref[0] = (h_ref[...] * _gelu_tanh(gr_ref[0].astype(F32))).astype(BF16)


def _lru(xr, gr, shift, cw, cb, wa, ba, wx, bx, lam):
    b, s, w = xr.shape
    n = LRU_TILE
    nq, qw, _ = wa.shape
    tile = lambda i, j: (i, j, 0)
    const = lambda i, j: (0, 0)
    const3 = lambda i, j: (0, 0, 0)
    return pl.pallas_call(
        _lru_kernel,
        grid=(b, s // n),
        in_specs=[
            pl.BlockSpec((1, n, w), tile),
            pl.BlockSpec((1, n, w), tile),
            pl.BlockSpec(shift.shape, const),
            pl.BlockSpec((CONV_K, w), const),
            pl.BlockSpec((1, w), const),
            pl.BlockSpec((nq, qw, qw), const3),
            pl.BlockSpec((1, w), const),
            pl.BlockSpec((nq, qw, qw), const3),
            pl.BlockSpec((1, w), const),
            pl.BlockSpec((1, w), const),
        ],
        out_specs=pl.BlockSpec((1, n, w), tile),
        out_shape=jax.ShapeDtypeStruct((b, s, w), BF16),
        scratch_shapes=[
            pltpu.VMEM((2 * SUBLANES, w), F32),
            pltpu.VMEM((SUBLANES, w), F32),
            pltpu.VMEM((n, w), F32),
        ],
        compiler_params=_cparams(("arbitrary", "arbitrary")),
        name="lru",
    )(xr, gr, shift, cw, cb, wa, ba, wx, bx, lam)


def _kv_kernel(m_ref, g_ref, w_ref, k_ref, v_ref):
    d = m_ref.shape[1]
    mn = _rms(m_ref[...], g_ref[...]).astype(BF16)
    k_ref[...] = _dot(mn, w_ref[:, :d]).astype(BF16)
    v_ref[...] = _dot(mn, w_ref[:, d:]).astype(BF16)


def _kv(mem2, g, wkv):
    t, d = mem2.shape
    tm = min(t, 512)
    row = lambda i: (i, 0)
    const = lambda i: (0, 0)
    return pl.pallas_call(
        _kv_kernel,
        grid=(t // tm,),
        in_specs=[pl.BlockSpec((tm, d), row), pl.BlockSpec((1, d), const), pl.BlockSpec((d, 2 * d), const)],
        out_specs=[pl.BlockSpec((tm, d), row), pl.BlockSpec((tm, d), row)],
        out_shape=[jax.ShapeDtypeStruct((t, d), BF16), jax.ShapeDtypeStruct((t, d), BF16)],
        compiler_params=_cparams(("arbitrary",)),
        name="kv",
    )(mem2, g, wkv)


def _mid_kernel(x_ref, ys_ref, yl_ref, k_ref, v_ref, wo1_ref, wo2_ref, gx_ref, wq_ref, wo_ref,
                gm_ref, wrh_ref, wrl_ref, br_ref,
                h_ref, hn_ref, idx_ref, rank_ref, gate_ref, cnt_ref, car_ref):
    tm, d = x_ref.shape
    hd = d // X_HEADS

    @pl.when(pl.program_id(0) == 0)
    def _():
        car_ref[...] = jnp.zeros(car_ref.shape, F32)

    h1 = x_ref[...] + _dot(ys_ref[...], wo1_ref[...]) + _dot(yl_ref[...], wo2_ref[...])

    q = _dot(_rms(h1, gx_ref[...]).astype(BF16), wq_ref[...]).astype(BF16)
    o_parts = []
    for hh in range(X_HEADS):
        sl = slice(hh * hd, (hh + 1) * hd)
        sc = _dot_nt(q[:, sl], k_ref[0, :, sl]) * (hd ** -0.5)
        sc = sc - jnp.max(sc, axis=-1, keepdims=True)
        p = jnp.exp(sc)
        p = p / jnp.sum(p, axis=-1, keepdims=True)
        o_parts.append(_dot(p.astype(BF16), v_ref[0, :, sl]))
    o = jnp.concatenate(o_parts, axis=1).astype(BF16)
    h2 = h1 + _dot(o, wo_ref[...])
    h_ref[...] = h2

    hn = _rms(h2, gm_ref[...])
    _store_token_major(hn_ref, hn)
    logits = _dot_hilo(hn, wrh_ref[...], wrl_ref[...]) + br_ref[...]

    l = logits.T[:N_EXPERTS, :]
    row = lax.broadcasted_iota(jnp.int32, (N_EXPERTS, tm), 0)
    picked = jnp.zeros((N_EXPERTS, tm), F32)
    vals, idxs = [], []
    for _ in range(TOP_K):
        m = jnp.max(l, axis=0, keepdims=True)
        idx = jnp.min(jnp.where(l == m, row, N_EXPERTS), axis=0, keepdims=True)
        sel = row == idx
        vals.append(m)
        idxs.append(idx)
        picked = jnp.where(sel, 1.0, picked)
        l = jnp.where(sel, -jnp.inf, l)
    ex = [jnp.exp(v - vals[0]) for v in vals]
    den = ex[0] + ex[1] + ex[2] + ex[3]

    ri = lax.broadcasted_iota(jnp.int32, (tm, tm), 0)
    ci = lax.broadcasted_iota(jnp.int32, (tm, tm), 1)
    earlier = jnp.where(ri < ci, 1.0, 0.0).astype(BF16)
    before = _dot(picked.astype(BF16), earlier) + car_ref[:, 0:1]
    out_row = lax.broadcasted_iota(jnp.int32, (SUBLANES, tm), 0)
    idx_out = jnp.zeros((SUBLANES, tm), jnp.int32)
    rank_out = jnp.zeros((SUBLANES, tm), jnp.int32)
    gate_out = jnp.zeros((SUBLANES, tm), F32)
    for k in range(TOP_K):
        rk = jnp.sum(jnp.where(row == idxs[k], before, 0.0), axis=0, keepdims=True)
        at_k = out_row == k
        idx_out = jnp.where(at_k, idxs[k], idx_out)
        rank_out = jnp.where(at_k, rk.astype(jnp.int32), rank_out)
        gate_out = jnp.where(at_k, ex[k] / den, gate_out)
    idx_ref[...] = idx_out
    rank_ref[...] = rank_out
    gate_ref[...] = gate_out
    total = car_ref[...] + jnp.sum(picked, axis=1, keepdims=True)
    car_ref[...] = total
    cnt_ref[...] = total.astype(jnp.int32)


def _mid(x2, ys, yl, kk, vv, wo1, wo2, gx, wq, wo, gm, wr_hi, wr_lo, br, seq):
    t, d = x2.shape
    tm = MID_TILE
    m = kk.shape[1]
    per_b = seq // tm
    row = lambda i: (i, 0)
    col = lambda i: (0, i)
    const = lambda i: (0, 0)
    kvmap = lambda i: (i // per_b, 0, 0)
    wspec = pl.BlockSpec((d, d), const)
    vspec = pl.BlockSpec((1, d), const)
    return pl.pallas_call(
        _mid_kernel,
        grid=(t // tm,),
        in_specs=[
            pl.BlockSpec((tm, d), row), pl.BlockSpec((tm, d), row), pl.BlockSpec((tm, d), row),
            pl.BlockSpec((1, m, d), kvmap), pl.BlockSpec((1, m, d), kvmap),
            wspec, wspec, vspec, wspec, wspec, vspec,
            pl.BlockSpec((d, LANES), const), pl.BlockSpec((d, LANES), const), pl.BlockSpec((1, LANES), const),
        ],
        out_specs=[
            pl.BlockSpec((tm, d), row), pl.BlockSpec((tm * d // LANES, LANES), row),
            pl.BlockSpec((SUBLANES, tm), col), pl.BlockSpec((SUBLANES, tm), col), pl.BlockSpec((SUBLANES, tm), col),
            pl.BlockSpec((N_EXPERTS, LANES), const),
        ],
        out_shape=[
            jax.ShapeDtypeStruct((t, d), F32), jax.ShapeDtypeStruct((t * d // LANES, LANES), F32),
            jax.ShapeDtypeStruct((SUBLANES, t), jnp.int32), jax.ShapeDtypeStruct((SUBLANES, t), jnp.int32),
            jax.ShapeDtypeStruct((SUBLANES, t), F32),
            jax.ShapeDtypeStruct((N_EXPERTS, LANES), jnp.int32),
        ],
        scratch_shapes=[pltpu.VMEM((N_EXPERTS, LANES), F32)],
        compiler_params=_cparams(("arbitrary",)),
        name="mid",
    )(x2, ys, yl, kk, vv, wo1, wo2, gx, wq, wo, gm, wr_hi, wr_lo, br)


def _dispatch_kernel(pend_ref, pstart_ref, idx_ref, rank_ref, hn_ref, x_hbm, zero_ref, sem, *, parts):
    i = pl.program_id(0)
    tc = hn_ref.shape[0] // parts
    bm = zero_ref.shape[0] // parts
    rows = TOP_K * tc

    def tile(ref, n, count=1):
        return ref.at[pl.ds(pl.multiple_of(n * parts, parts), count * parts), :]

    @pl.when(i == 0)
    def _():
        zero_ref[...] = jnp.zeros(zero_ref.shape, F32)

        def fill(start):
            return pltpu.make_async_copy(zero_ref, tile(x_hbm, start, bm), sem.at[1])

        for e in range(N_EXPERTS):
            fill(jnp.maximum(pend_ref[e] - bm, 0)).start()
        for e in range(N_EXPERTS):
            fill(0).wait()

        def tail(j, c):
            fill(j * bm).start()
            fill(0).wait()
            return c

        lax.fori_loop(pend_ref[N_EXPERTS - 1] // bm, x_hbm.shape[0] // (bm * parts), tail, 0)

    def body(j, c):
        t0 = pl.multiple_of(j * ISSUE_UNROLL, ISSUE_UNROLL)
        for u in range(ISSUE_UNROLL):
            for k in range(TOP_K):
                dst = pstart_ref[idx_ref[k, t0 + u]] + rank_ref[k, t0 + u]
                pltpu.make_async_copy(tile(hn_ref, t0 + u), tile(x_hbm, dst), sem.at[0]).start(priority=k % 2)
        return c

    lax.fori_loop(0, tc // ISSUE_UNROLL, body, 0)
    pltpu.make_async_copy(tile(x_hbm, 0, rows), tile(x_hbm, 0, rows), sem.at[0]).wait()


def _dispatch(pend, pstart, idx_m, rank_m, hn_tm, n_rows, d):
    parts = d // LANES
    t = hn_tm.shape[0] // parts
    tc = COMB_TILE
    meta = pl.BlockSpec((SUBLANES, tc), lambda i, pe, ps: (0, i), memory_space=pltpu.SMEM)
    grid_spec = pltpu.PrefetchScalarGridSpec(
        num_scalar_prefetch=2,
        grid=(t // tc,),
        in_specs=[meta, meta, pl.BlockSpec((tc * parts, LANES), lambda i, pe, ps: (i, 0))],
        out_specs=pl.BlockSpec(memory_space=pl.ANY),
        scratch_shapes=[pltpu.VMEM((MOE_BLOCK * parts, LANES), F32), pltpu.SemaphoreType.DMA((2,))],
    )
    return pl.pallas_call(
        functools.partial(_dispatch_kernel, parts=parts),
        grid_spec=grid_spec,
        out_shape=jax.ShapeDtypeStruct((n_rows * parts, LANES), F32),
        compiler_params=_cparams(("arbitrary",)),
        name="dispatch",
    )(pend, pstart, idx_m, rank_m, hn_tm)


def _moe_kernel(be_ref, nxt_ref, nvb_ref, x_ref, wgu_hbm, bg_ref, bu_ref, wd_hbm, bd_ref, perm_ref,
                y_ref, wgu_f, wd_f, wgu_s, wd_s, act_s, sem):
    i = pl.program_id(0)
    nvb = nvb_ref[0]
    f, d = wd_f.shape
    bm = act_s.shape[0]
    pw = perm_ref.shape[0]
    half = pw // 2

    def fetch(e):
        return (pltpu.make_async_copy(wgu_hbm.at[e], wgu_f, sem.at[0]),
                pltpu.make_async_copy(wd_hbm.at[e], wd_f, sem.at[1]))

    @pl.when(jnp.logical_and(i == 0, nvb > 0))
    def _():
        for c in fetch(be_ref[0]):
            c.start()

    changed = jnp.logical_or(i == 0, be_ref[i] != be_ref[jnp.maximum(i - 1, 0)])

    @pl.when(jnp.logical_and(changed, i < nvb))
    def _():
        for c in fetch(be_ref[i]):
            c.wait()
        for c in range(2 * f // pw):
            wc = wgu_f[:, c * pw:(c + 1) * pw].astype(BF16)
            wgu_s[:, c * pw:(c + 1) * pw] = _dot(wc, perm_ref[...]).astype(BF16)
        wd_s[...] = wd_f[...].astype(BF16)

        @pl.when(nxt_ref[i] >= 0)
        def _():
            for c in fetch(nxt_ref[i]):
                c.start(priority=1)

    @pl.when(i < nvb)
    def _():
        xb = jnp.concatenate([p.astype(BF16) for p in _load_token_major(x_ref, bm, d)], axis=1)
        for c in range(2 * f // pw):
            gu = _dot(xb, wgu_s[:, c * pw:(c + 1) * pw])
            g = gu[:, :half] + bg_ref[0, :, c * half:(c + 1) * half]
            u = gu[:, half:] + bu_ref[0, :, c * half:(c + 1) * half]
            g = jnp.minimum(g, SWIGLU_LIMIT)
            u = jnp.clip(u, -SWIGLU_LIMIT, SWIGLU_LIMIT)
            act = (u + 1.0) * (g * _sigmoid(SWIGLU_ALPHA * g))
            act_s[:, c * half:(c + 1) * half] = act.astype(BF16)
        _store_token_major(y_ref, _dot(act_s[...], wd_s[...]) + bd_ref[0])

    @pl.when(i >= nvb)
    def _():
        y_ref[...] = jnp.zeros(y_ref.shape, F32)


def _moe(block_e, next_e, nvb, xrows_tm, wgu, bg, bu, wd, bd, perm):
    f, d = wd.shape[1], wd.shape[2]
    parts = d // LANES
    bm = MOE_BLOCK
    nb = xrows_tm.shape[0] // (bm * parts)
    emap = lambda i, be, nx, nv: (be[i], 0, 0)
    grid_spec = pltpu.PrefetchScalarGridSpec(
        num_scalar_prefetch=3,
        grid=(nb,),
        in_specs=[
            pl.BlockSpec((bm * parts, LANES),
                         lambda i, be, nx, nv: (jnp.minimum(i, jnp.maximum(nv[0] - 1, 0)), 0)),
            pl.BlockSpec(memory_space=pl.ANY),
            pl.BlockSpec((1, 1, f), emap), pl.BlockSpec((1, 1, f), emap),
            pl.BlockSpec(memory_space=pl.ANY),
            pl.BlockSpec((1, 1, d), emap),
            pl.BlockSpec(perm.shape, lambda i, be, nx, nv: (0, 0)),
        ],
        out_specs=pl.BlockSpec((bm * parts, LANES), lambda i, be, nx, nv: (i, 0)),
        scratch_shapes=[
            pltpu.VMEM((d, 2 * f), F32), pltpu.VMEM((f, d), F32),
            pltpu.VMEM((d, 2 * f), BF16), pltpu.VMEM((f, d), BF16),
            pltpu.VMEM((bm, f), BF16),
            pltpu.SemaphoreType.DMA((2,)),
        ],
    )
    return pl.pallas_call(
        _moe_kernel,
        grid_spec=grid_spec,
        out_shape=jax.ShapeDtypeStruct(xrows_tm.shape, F32),
        compiler_params=_cparams(("arbitrary",)),
        name="moe",
    )(block_e, next_e, nvb, xrows_tm, wgu, bg, bu, wd, bd, perm)


def _comb_kernel(pstart_ref, idx_ref, rank_ref, idxn_ref, rankn_ref, h_ref, gate_ref, y_hbm, g_ref, o_ref,
                 ybuf, sem, *, parts):
    i = pl.program_id(0)
    n = pl.num_programs(0)
    tc, d = h_ref.shape
    rows = TOP_K * tc
    slot = i % 2

    def tile(ref, n, count=1):
        return ref.at[pl.ds(pl.multiple_of(n * parts, parts), count * parts), :]

    def start_rows(e_ref, r_ref, s):
        def body(j, c):
            t0 = pl.multiple_of(j * ISSUE_UNROLL, ISSUE_UNROLL)
            for u in range(ISSUE_UNROLL):
                for k in range(TOP_K):
                    src = pstart_ref[e_ref[k, t0 + u]] + r_ref[k, t0 + u]
                    pltpu.make_async_copy(tile(y_hbm, src), tile(ybuf.at[s], k * tc + t0 + u),
                                          sem.at[s]).start(priority=k % 2)
            return c

        lax.fori_loop(0, tc // ISSUE_UNROLL, body, 0)

    @pl.when(i == 0)
    def _():
        start_rows(idx_ref, rank_ref, 0)

    @pl.when(i + 1 < n)
    def _():
        start_rows(idxn_ref, rankn_ref, 1 - slot)

    pltpu.make_async_copy(tile(y_hbm, 0, rows), ybuf.at[slot], sem.at[slot]).wait()
    acc = [h_ref[:, s * LANES:(s + 1) * LANES] for s in range(parts)]
    for k in range(TOP_K):
        yk = _load_token_major(ybuf.at[slot], tc, d, base=k * tc * parts)
        gk = gate_ref[:, k:k + 1]
        acc = [a + p * gk for a, p in zip(acc, yk)]
    o_ref[...] = _rms(jnp.concatenate(acc, axis=1), g_ref[...])


def _combine(pstart, idx_m, rank_m, h2, gate, y_tm, g):
    t, d = h2.shape
    parts = d // LANES
    tc = COMB_TILE
    nt = t // tc
    cur = pl.BlockSpec((SUBLANES, tc), lambda i, ps: (0, i), memory_space=pltpu.SMEM)
    nxt = pl.BlockSpec((SUBLANES, tc), lambda i, ps: (0, jnp.minimum(i + 1, nt - 1)), memory_space=pltpu.SMEM)
    grid_spec = pltpu.PrefetchScalarGridSpec(
        num_scalar_prefetch=1,
        grid=(nt,),
        in_specs=[
            cur, cur, nxt, nxt,
            pl.BlockSpec((tc, d), lambda i, ps: (i, 0)),
            pl.BlockSpec((tc, SUBLANES), lambda i, ps: (i, 0)),
            pl.BlockSpec(memory_space=pl.ANY),
            pl.BlockSpec((1, d), lambda i, ps: (0, 0)),
        ],
        out_specs=pl.BlockSpec((tc, d), lambda i, ps: (i, 0)),
        scratch_shapes=[pltpu.VMEM((2, TOP_K * tc * parts, LANES), F32), pltpu.SemaphoreType.DMA((2,))],
    )
    return pl.pallas_call(
        functools.partial(_comb_kernel, parts=parts),
        grid_spec=grid_spec,
        out_shape=jax.ShapeDtypeStruct((t, d), F32),
        compiler_params=_cparams(("arbitrary",)),
        name="combine",
    )(pstart, idx_m, rank_m, idx_m, rank_m, h2, gate, y_tm, g)


def _block_diag(wb, per):
    nb, bw, _ = wb.shape
    wq = wb.reshape(nb // per, per, bw, bw)
    eye = jnp.eye(per, dtype=wb.dtype)
    out = jnp.einsum('qaij,ab->qaibj', wq, eye)
    return out.reshape(nb // per, per * bw, per * bw)


def _pad_lanes(v, fill=0.0):
    return jnp.pad(v, (0, LANES - v.shape[0]), constant_values=fill).reshape(1, LANES)


def kernel(x, mem, norm_mix, w_in, ssd_conv_w, ssd_conv_b, ssd_dt_bias, ssd_a_log, ssd_d, ssd_norm, lru_conv_w, lru_conv_b, lru_wa, lru_ba, lru_wx, lru_bx, lru_lambda, w_out, norm_xattn, norm_mem, w_q, w_kv, w_o, norm_moe, w_router, b_router, w_gate_up, b_gate_up, w_down, b_down, norm_final):
    b, s, d = x.shape
    t = b * s
    n_mem = mem.shape[1]
    w = d
    cdim = w + 2 * SSD_GROUPS * SSD_STATE
    o1, o2, o3, o4 = w, w + cdim, w + cdim + SSD_HEADS, w + cdim + SSD_HEADS + w

    wi = w_in[0]
    wzx = wi[:, :o2].astype(BF16)
    wdt = jnp.pad(wi[:, o2:o3], ((0, 0), (0, LANES - SSD_HEADS)))
    wdt_hi, wdt_lo = _hilo(wdt)
    wxr = wi[:, o3:o4].astype(BF16)
    wgr = wi[:, o4:].astype(BF16)
    e01 = (jnp.arange(LANES)[:, None] == (jnp.arange(w)[None, :] // SSD_HEAD_DIM)).astype(BF16)
    dskip = jnp.repeat(ssd_d[0], SSD_HEAD_DIM).reshape(1, w)
    per = 256 // (w // LRU_BLOCKS)
    wa_bd = _block_diag(lru_wa[0], per).astype(BF16)
    wx_bd = _block_diag(lru_wx[0], per).astype(BF16)
    wr = jnp.pad(w_router[0], ((0, 0), (0, LANES - N_EXPERTS)))
    wr_hi, wr_lo = _hilo(wr)
    br = _pad_lanes(b_router[0], fill=-1e30)
    bgu = b_gate_up[0]
    bg = bgu[:, None, 0::2]
    bu = bgu[:, None, 1::2]
    bd = b_down[0][:, None, :]
    pw = 2 * LANES
    col = jnp.arange(pw)
    src_col = jnp.where(col < LANES, 2 * col, 2 * (col - LANES) + 1)
    perm = (jnp.arange(pw)[:, None] == src_col[None, :]).astype(BF16)

    x2 = x.reshape(t, d)
    z, xbc, dt, xr, gr = _in_proj(x2, norm_mix[0].reshape(1, d), wzx, wdt_hi, wdt_lo, wxr, wgr)

    y_ssd = _ssd(xbc.reshape(b, s, cdim), dt.reshape(b, s, LANES), z.reshape(b, s, w),
                 _shift_matrix(SSD_CHUNK), ssd_conv_w[0], ssd_conv_b[0].reshape(1, cdim), _pad_lanes(ssd_dt_bias[0]),
                 _pad_lanes(ssd_a_log[0]), dskip, ssd_norm[0].reshape(1, w), e01)
    y_lru = _lru(xr.reshape(b, s, w), gr.reshape(b, s, w), _shift_matrix(LRU_TILE), lru_conv_w[0], lru_conv_b[0].reshape(1, w),
                 wa_bd, lru_ba[0].reshape(1, w), wx_bd, lru_bx[0].reshape(1, w), lru_lambda[0].reshape(1, w))

    kk, vv = _kv(mem.reshape(b * n_mem, d), norm_mem[0].reshape(1, d), w_kv[0].astype(BF16))
    wo_mix = w_out[0].astype(BF16)
    h2, hn, idx_m, rank_m, gate_m, cnt = _mid(
        x2, y_ssd.reshape(t, w), y_lru.reshape(t, w), kk.reshape(b, n_mem, d), vv.reshape(b, n_mem, d),
        wo_mix[:w], wo_mix[w:], norm_xattn[0].reshape(1, d), w_q[0].astype(BF16), w_o[0].astype(BF16),
        norm_moe[0].reshape(1, d), wr_hi, wr_lo, br, s)

    bm = MOE_BLOCK
    counts = cnt[:, 0]
    pcounts = (counts + bm - 1) // bm * bm
    pend = jnp.cumsum(pcounts).astype(jnp.int32)
    pstart = pend - pcounts
    n_pairs = t * TOP_K
    nb = (n_pairs + N_EXPERTS * (bm - 1) + bm - 1) // bm
    blk0 = jnp.arange(nb, dtype=jnp.int32) * bm
    block_e = jnp.minimum(jnp.sum(pend[None, :] <= blk0[:, None], axis=1), N_EXPERTS - 1).astype(jnp.int32)
    nvb = (pend[-1] // bm).astype(jnp.int32).reshape(1)
    after = pend[block_e] // bm
    next_e = jnp.where(after < nvb[0], block_e[jnp.minimum(after, nb - 1)], -1).astype(jnp.int32)

    xrows = _dispatch(pend, pstart, idx_m, rank_m, hn, nb * bm, d)
    y = _moe(block_e, next_e, nvb, xrows, w_gate_up[0], bg, bu, w_down[0], bd, perm)
    out = _combine(pstart, idx_m, rank_m, h2, gate_m.T, y, norm_final.reshape(1, d))
    return out.reshape(b, s, d)
```

```python
import functools

import jax
import jax.numpy as jnp
from jax import lax
from jax.experimental import pallas as pl
from jax.experimental.pallas import tpu as pltpu

F32 = jnp.float32
BF16 = jnp.bfloat16

NORM_EPS = 1e-6
LANES = 128
SUBLANES = 8
SSD_HEAD_DIM = 64
SSD_HEADS = 16
SSD_GROUPS = 4
SSD_STATE = 128
CONV_K = 4
LRU_BLOCKS = 16
RG_C = 8.0
X_HEADS = 4
N_EXPERTS = 32
TOP_K = 4
SWIGLU_LIMIT = 7.0
SWIGLU_ALPHA = 1.702

VMEM_LIMIT = 56 * 1024 * 1024

MIX_TILE = 256
PROJ_COLS = 512
MID_TILE = 512
MOE_BLOCK = 512
COMB_TILE = 256
ISSUE_UNROLL = 8


def _cparams(sem):
    return pltpu.CompilerParams(dimension_semantics=sem, vmem_limit_bytes=VMEM_LIMIT)


def _rms(x, g):
    ms = jnp.mean(x * x, axis=-1, keepdims=True)
    return x * lax.rsqrt(ms + NORM_EPS) * g


def _sigmoid(x):
    return 0.5 * jnp.tanh(0.5 * x) + 0.5


def _softplus(x):
    return jnp.maximum(x, 0.0) + jnp.log(1.0 + jnp.exp(-jnp.abs(x)))


def _split3(x):
    a = x.astype(BF16)
    r = x - a.astype(F32)
    b = r.astype(BF16)
    c = (r - b.astype(F32)).astype(BF16)
    return a, b, c


def _dot(a, b):
    return jnp.dot(a, b, preferred_element_type=F32)


def _dot_nt(a, b):
    return lax.dot_general(a, b, (((1,), (1,)), ((), ())), preferred_element_type=F32)


def _dot01_right(x, m01):
    a, b, c = _split3(x)
    return _dot(a, m01) + _dot(b, m01) + _dot(c, m01)


def _dot01_left(m01, x):
    a, b, c = _split3(x)
    return _dot(m01, a) + _dot(m01, b) + _dot(m01, c)


def _dot_hilo(x, w_hi, w_lo):
    xh = x.astype(BF16)
    xl = (x - xh.astype(F32)).astype(BF16)
    return _dot(xh, w_hi) + _dot(xl, w_hi) + _dot(xh, w_lo)


def _hilo(w):
    hi = w.astype(BF16)
    lo = (w - hi.astype(F32)).astype(BF16)
    return hi, lo


def _store_token_major(ref, val, base=0):
    n, d = val.shape
    parts = d // LANES
    for s in range(parts):
        ref[pl.ds(base + s, n, stride=parts), :] = val[:, s * LANES:(s + 1) * LANES]


def _load_token_major(ref, n, d, base=0):
    parts = d // LANES
    return [ref[pl.ds(base + s, n, stride=parts), :] for s in range(parts)]


def _shift_matrix(n):
    return jnp.concatenate([jnp.eye(n, k=-(CONV_K - 1 - j), dtype=BF16) for j in range(CONV_K - 1)], axis=0)


def _causal_conv(halo_ref, x, shift_ref, w_ref, b_ref, first):
    n = x.shape[0]
    pad = SUBLANES
    k1 = CONV_K - 1

    @pl.when(first)
    def _():
        halo_ref[...] = jnp.zeros(halo_ref.shape, F32)

    xf = x.astype(F32)
    taps = _dot(shift_ref[...], x)
    acc = b_ref[...] + w_ref[k1:k1 + 1, :] * xf
    for j in range(k1):
        acc = acc + w_ref[j:j + 1, :] * taps[j * n:(j + 1) * n]
    head = acc[:pad]
    for j in range(k1):
        head = head + w_ref[j:j + 1, :] * halo_ref[pad - k1 + j:2 * pad - k1 + j, :]
    halo_ref[0:pad, :] = xf[n - pad:n]
    return jnp.concatenate([head, acc[pad:]], axis=0)


def _ssd_steps(xbc_ref, dt_ref, z_ref, first, sh_ref, cw_ref, cb_ref, dtb_ref, alog_ref, dskip_ref, gn_ref, e_ref,
               y_ref, halo_ref, st_ref):
    n, w = z_ref.shape
    gw = w // SSD_GROUPS

    @pl.when(first)
    def _():
        st_ref[...] = jnp.zeros(st_ref.shape, F32)

    conv = _causal_conv(halo_ref, xbc_ref[...], sh_ref, cw_ref, cb_ref, first)
    yield
    xc = conv * _sigmoid(conv)
    xs = xc[:, :w]
    yield

    dt = _softplus(dt_ref[...] + dtb_ref[...])
    a = -jnp.exp(alog_ref[...])
    da = dt * a
    ri = lax.broadcasted_iota(jnp.int32, (n, n), 0)
    ci = lax.broadcasted_iota(jnp.int32, (n, n), 1)
    causal = ri >= ci
    tril = jnp.where(causal, 1.0, 0.0).astype(BF16)
    a_cs = _dot01_left(tril, da)
    a_cs_t = a_cs.T
    yield

    e01 = e_ref[...]
    dt_x = _dot01_right(dt, e01)
    acs_x = _dot01_right(a_cs, e01)
    last_x = acs_x[n - 1:n, :]
    xdt = xs * dt_x
    xdt_b = xdt.astype(BF16)
    xdt_end = (xdt * jnp.exp(last_x - acs_x)).astype(BF16)
    exp_acs = jnp.exp(acs_x)
    chunk_decay = jnp.exp(last_x)
    lane = lax.broadcasted_iota(jnp.int32, (n, gw), 1)
    yield

    for g in range(SSD_GROUPS):
        lo = g * gw
        bg = xc[:, w + g * SSD_STATE:w + (g + 1) * SSD_STATE].astype(BF16)
        cg = xc[:, w + (SSD_GROUPS + g) * SSD_STATE:w + (SSD_GROUPS + g + 1) * SSD_STATE].astype(BF16)
        cb = _dot_nt(cg, bg)
        prev = st_ref[g]
        acc = _dot(cg, prev.astype(BF16)) * exp_acs[:, lo:lo + gw]
        new = lax.dot_general(bg, xdt_end[:, lo:lo + gw], (((0,), (0,)), ((), ())),
                              preferred_element_type=F32)
        st_ref[g] = chunk_decay[:, lo:lo + gw] * prev + new
        xg = xdt_b[:, lo:lo + gw]
        for k in range(SSD_HEADS // SSD_GROUPS):
            h = g * (SSD_HEADS // SSD_GROUPS) + k
            seg = a_cs[:, h:h + 1] - a_cs_t[h:h + 1, :]
            dec = jnp.exp(jnp.where(causal, seg, -jnp.inf))
            m = (cb * dec).astype(BF16)
            in_head = (lane >= k * SSD_HEAD_DIM) & (lane < (k + 1) * SSD_HEAD_DIM)
            acc = acc + _dot(m, jnp.where(in_head, xg, jnp.zeros_like(xg)))
            yield
        yg = acc + xs[:, lo:lo + gw] * dskip_ref[:, lo:lo + gw]
        zg = z_ref[:, lo:lo + gw]
        u = yg * (zg * _sigmoid(zg))
        u = u * lax.rsqrt(jnp.mean(u * u, axis=-1, keepdims=True) + NORM_EPS)
        y_ref[0, :, lo:lo + gw] = (u * gn_ref[:, lo:lo + gw]).astype(BF16)
        yield


def _gelu_tanh(x):
    c = 0.7978845608028654
    return 0.5 * x * (1.0 + jnp.tanh(c * (x + 0.044715 * (x * x * x))))


def _lru_steps(xr_ref, gr_ref, first, sh_ref, cw_ref, cb_ref, wa_ref, ba_ref, wx_ref, bx_ref, lam_ref,
               y_ref, halo_ref, car_ref, h_ref):
    n, w = xr_ref.shape

    @pl.when(first)
    def _():
        car_ref[...] = jnp.zeros(car_ref.shape, F32)

    xc = _causal_conv(halo_ref, xr_ref[...], sh_ref, cw_ref, cb_ref, first)
    yield
    xb = xc.astype(BF16)
    nq = wa_ref.shape[0]
    qw = w // nq
    r_parts, i_parts = [], []
    for q in range(nq):
        xq = xb[:, q * qw:(q + 1) * qw]
        r_parts.append(_dot(xq, wa_ref[q]))
        i_parts.append(_dot(xq, wx_ref[q]))
    yield
    r = _sigmoid(jnp.concatenate(r_parts, axis=1) + ba_ref[...])
    yield
    gi = _sigmoid(jnp.concatenate(i_parts, axis=1) + bx_ref[...])
    yield
    log_a = (-RG_C) * r * _softplus(-lam_ref[...])
    a = jnp.exp(log_a)
    u = xc * gi * jnp.sqrt(1.0 - a * a)
    yield

    groups = n // SUBLANES
    sub = lax.broadcasted_iota(jnp.int32, (groups, SUBLANES, w), 1)
    ap = a.reshape(groups, SUBLANES, w)
    bp = u.reshape(groups, SUBLANES, w)
    for d in (1, 2, 4):
        m = sub >= d
        bp = jnp.where(m, ap * pltpu.roll(bp, d, 1) + bp, bp)
        ap = jnp.where(m, ap * pltpu.roll(ap, d, 1), ap)
        yield
    carry = car_ref[...]
    for g in range(groups):
        hb = bp[g] + ap[g] * carry
        h_ref[g * SUBLANES:(g + 1) * SUBLANES, :] = hb
        carry = jnp.broadcast_to(hb[SUBLANES - 1:SUBLANES, :], (SUBLANES, w))
        if g % SUBLANES == SUBLANES - 1:
            yield
    car_ref[...] = carry
    y_ref[0] = (h_ref[...] * _gelu_tanh(gr_ref[...])).astype(BF16)
    yield


def _interleave(*gens):
    live = list(gens)
    while live:
        for g in list(live):
            try:
                next(g)
            except StopIteration:
                live.remove(g)


def _proj_steps(x, g_ref, wzx_ref, wdth_ref, wdtl_ref, wxr_ref, wgr_ref, z_ref, xbc_ref, dt_ref, xr_ref, gr_ref):
    d = x.shape[1]
    cdim = xbc_ref.shape[1]
    hn = _rms(x, g_ref[...])
    hb = hn.astype(BF16)
    yield
    for c in range(0, cdim, PROJ_COLS):
        xbc_ref[:, c:c + PROJ_COLS] = _dot(hb, wzx_ref[:, d + c:d + c + PROJ_COLS]).astype(BF16)
        yield
    dt_ref[...] = _dot_hilo(hn, wdth_ref[...], wdtl_ref[...])
    yield
    for c in range(0, d, PROJ_COLS):
        xr_ref[:, c:c + PROJ_COLS] = _dot(hb, wxr_ref[:, c:c + PROJ_COLS]).astype(BF16)
        yield
    for c in range(0, d, PROJ_COLS):
        z_ref[:, c:c + PROJ_COLS] = _dot(hb, wzx_ref[:, c:c + PROJ_COLS])
        yield
    for c in range(0, d, PROJ_COLS):
        gr_ref[:, c:c + PROJ_COLS] = _dot(hb, wgr_ref[:, c:c + PROJ_COLS])
        yield


def _mix_kernel(x0_ref, xn_ref, g_ref, wzx_ref, wdth_ref, wdtl_ref, wxr_ref, wgr_ref, sh_ref,
                scw_ref, scb_ref, dtb_ref, alog_ref, dskip_ref, gn_ref, e_ref,
                lcw_ref, lcb_ref, wa_ref, ba_ref, wx_ref, bx_ref, lam_ref,
                ys_ref, yl_ref, z_s, xbc_s, dt_s, xr_s, gr_s, shalo_ref, st_ref, lhalo_ref, car_ref, h_ref,
                *, tiles_per_seq):
    t = pl.program_id(0)
    cur = lax.rem(t, 2)
    nxt = 1 - cur
    first = lax.rem(t, tiles_per_seq) == 0
    weights = (g_ref, wzx_ref, wdth_ref, wdtl_ref, wxr_ref, wgr_ref)

    def slot(s):
        return (z_s.at[s], xbc_s.at[s], dt_s.at[s], xr_s.at[s], gr_s.at[s])

    @pl.when(t == 0)
    def _():
        _interleave(_proj_steps(x0_ref[0], *weights, *slot(0)))

    z_c, xbc_c, dt_c, xr_c, gr_c = slot(cur)
    _interleave(
        _proj_steps(xn_ref[0], *weights, *slot(nxt)),
        _ssd_steps(xbc_c, dt_c, z_c, first, sh_ref, scw_ref, scb_ref, dtb_ref, alog_ref, dskip_ref, gn_ref, e_ref,
                   ys_ref, shalo_ref, st_ref),
        _lru_steps(xr_c, gr_c, first, sh_ref, lcw_ref, lcb_ref, wa_ref, ba_ref, wx_ref, bx_ref, lam_ref,
                   yl_ref, lhalo_ref, car_ref, h_ref),
    )


def _mix(x, g, wzx, wdt_hi, wdt_lo, wxr, wgr, shift, scw, scb, dtb, alog, dskip, gn, e01,
         lcw, lcb, wa, ba, wx, bx, lam):
    b, s, d = x.shape
    n = MIX_TILE
    nt = b * s // n
    w = wxr.shape[1]
    cdim = wzx.shape[1] - w
    x3 = x.reshape(nt, n, d)

    def const(a):
        return pl.BlockSpec(a.shape, lambda i, nd=a.ndim: (0,) * nd)

    consts = (g, wzx, wdt_hi, wdt_lo, wxr, wgr, shift, scw, scb, dtb, alog, dskip, gn, e01,
              lcw, lcb, wa, ba, wx, bx, lam)
    ys, yl = pl.pallas_call(
        functools.partial(_mix_kernel, tiles_per_seq=s // n),
        grid=(nt,),
        in_specs=[pl.BlockSpec((1, n, d), lambda i: (0, 0, 0)),
                  pl.BlockSpec((1, n, d), lambda i: (jnp.minimum(i + 1, nt - 1), 0, 0))]
                 + [const(a) for a in consts],
        out_specs=[pl.BlockSpec((1, n, w), lambda i: (i, 0, 0)), pl.BlockSpec((1, n, w), lambda i: (i, 0, 0))],
        out_shape=[jax.ShapeDtypeStruct((nt, n, w), BF16), jax.ShapeDtypeStruct((nt, n, w), BF16)],
        scratch_shapes=[
            pltpu.VMEM((2, n, w), F32), pltpu.VMEM((2, n, cdim), BF16), pltpu.VMEM((2, n, LANES), F32),
            pltpu.VMEM((2, n, w), BF16), pltpu.VMEM((2, n, w), F32),
            pltpu.VMEM((2 * SUBLANES, cdim), F32),
            pltpu.VMEM((SSD_GROUPS, SSD_STATE, w // SSD_GROUPS), F32),
            pltpu.VMEM((2 * SUBLANES, w), F32),
            pltpu.VMEM((SUBLANES, w), F32),
            pltpu.VMEM((n, w), F32),
        ],
        compiler_params=_cparams(("arbitrary",)),
        name="mix",
    )(x3, x3, *consts)
    return ys, yl


def _kv_kernel(m_ref, g_ref, w_ref, k_ref, v_ref):
    d = m_ref.shape[1]
    mn = _rms(m_ref[...], g_ref[...]).astype(BF16)
    k_ref[...] = _dot(mn, w_ref[:, :d]).astype(BF16)
    v_ref[...] = _dot(mn, w_ref[:, d:]).astype(BF16)


def _kv(mem2, g, wkv):
    t, d = mem2.shape
    tm = min(t, 512)
    row = lambda i: (i, 0)
    const = lambda i: (0, 0)
    return pl.pallas_call(
        _kv_kernel,
        grid=(t // tm,),
        in_specs=[pl.BlockSpec((tm, d), row), pl.BlockSpec((1, d), const), pl.BlockSpec((d, 2 * d), const)],
        out_specs=[pl.BlockSpec((tm, d), row), pl.BlockSpec((tm, d), row)],
        out_shape=[jax.ShapeDtypeStruct((t, d), BF16), jax.ShapeDtypeStruct((t, d), BF16)],
        compiler_params=_cparams(("arbitrary",)),
        name="kv",
    )(mem2, g, wkv)


def _mid_kernel(x_ref, ys_ref, yl_ref, k_ref, v_ref, wo1_ref, wo2_ref, gx_ref, wq_ref, wo_ref,
                gm_ref, wrh_ref, wrl_ref, br_ref,
                h_ref, hn_ref, idx_ref, rank_ref, gate_ref, cnt_ref, car_ref):
    tm, d = x_ref.shape
    hd = d // X_HEADS

    @pl.when(pl.program_id(0) == 0)
    def _():
        car_ref[...] = jnp.zeros(car_ref.shape, F32)

    h1 = x_ref[...] + _dot(ys_ref[...], wo1_ref[...]) + _dot(yl_ref[...], wo2_ref[...])

    q = _dot(_rms(h1, gx_ref[...]).astype(BF16), wq_ref[...]).astype(BF16)
    o_parts = []
    for hh in range(X_HEADS):
        sl = slice(hh * hd, (hh + 1) * hd)
        sc = _dot_nt(q[:, sl], k_ref[0, :, sl]) * (hd ** -0.5)
        sc = sc - jnp.max(sc, axis=-1, keepdims=True)
        p = jnp.exp(sc)
        p = p / jnp.sum(p, axis=-1, keepdims=True)
        o_parts.append(_dot(p.astype(BF16), v_ref[0, :, sl]))
    o = jnp.concatenate(o_parts, axis=1).astype(BF16)
    h2 = h1 + _dot(o, wo_ref[...])
    h_ref[...] = h2

    hn = _rms(h2, gm_ref[...])
    _store_token_major(hn_ref, hn)
    logits = _dot_hilo(hn, wrh_ref[...], wrl_ref[...]) + br_ref[...]

    l = logits.T[:N_EXPERTS, :]
    row = lax.broadcasted_iota(jnp.int32, (N_EXPERTS, tm), 0)
    picked = jnp.zeros((N_EXPERTS, tm), F32)
    vals, idxs = [], []
    for _ in range(TOP_K):
        m = jnp.max(l, axis=0, keepdims=True)
        idx = jnp.min(jnp.where(l == m, row, N_EXPERTS), axis=0, keepdims=True)
        sel = row == idx
        vals.append(m)
        idxs.append(idx)
        picked = jnp.where(sel, 1.0, picked)
        l = jnp.where(sel, -jnp.inf, l)
    ex = [jnp.exp(v - vals[0]) for v in vals]
    den = ex[0] + ex[1] + ex[2] + ex[3]

    ri = lax.broadcasted_iota(jnp.int32, (tm, tm), 0)
    ci = lax.broadcasted_iota(jnp.int32, (tm, tm), 1)
    earlier = jnp.where(ri < ci, 1.0, 0.0).astype(BF16)
    before = _dot(picked.astype(BF16), earlier) + car_ref[:, 0:1]
    out_row = lax.broadcasted_iota(jnp.int32, (SUBLANES, tm), 0)
    idx_out = jnp.zeros((SUBLANES, tm), jnp.int32)
    rank_out = jnp.zeros((SUBLANES, tm), jnp.int32)
    gate_out = jnp.zeros((SUBLANES, tm), F32)
    for k in range(TOP_K):
        rk = jnp.sum(jnp.where(row == idxs[k], before, 0.0), axis=0, keepdims=True)
        at_k = out_row == k
        idx_out = jnp.where(at_k, idxs[k], idx_out)
        rank_out = jnp.where(at_k, rk.astype(jnp.int32), rank_out)
        gate_out = jnp.where(at_k, ex[k] / den, gate_out)
    idx_ref[...] = idx_out
    rank_ref[...] = rank_out
    gate_ref[...] = gate_out
    total = car_ref[...] + jnp.sum(picked, axis=1, keepdims=True)
    car_ref[...] = total
    cnt_ref[...] = total.astype(jnp.int32)


def _mid(x2, ys, yl, kk, vv, wo1, wo2, gx, wq, wo, gm, wr_hi, wr_lo, br, seq):
    t, d = x2.shape
    tm = MID_TILE
    m = kk.shape[1]
    per_b = seq // tm
    row = lambda i: (i, 0)
    col = lambda i: (0, i)
    const = lambda i: (0, 0)
    kvmap = lambda i: (i // per_b, 0, 0)
    wspec = pl.BlockSpec((d, d), const)
    vspec = pl.BlockSpec((1, d), const)
    return pl.pallas_call(
        _mid_kernel,
        grid=(t // tm,),
        in_specs=[
            pl.BlockSpec((tm, d), row), pl.BlockSpec((tm, d), row), pl.BlockSpec((tm, d), row),
            pl.BlockSpec((1, m, d), kvmap), pl.BlockSpec((1, m, d), kvmap),
            wspec, wspec, vspec, wspec, wspec, vspec,
            pl.BlockSpec((d, LANES), const), pl.BlockSpec((d, LANES), const), pl.BlockSpec((1, LANES), const),
        ],
        out_specs=[
            pl.BlockSpec((tm, d), row), pl.BlockSpec((tm * d // LANES, LANES), row),
            pl.BlockSpec((SUBLANES, tm), col), pl.BlockSpec((SUBLANES, tm), col), pl.BlockSpec((SUBLANES, tm), col),
            pl.BlockSpec((N_EXPERTS, LANES), const),
        ],
        out_shape=[
            jax.ShapeDtypeStruct((t, d), F32), jax.ShapeDtypeStruct((t * d // LANES, LANES), F32),
            jax.ShapeDtypeStruct((SUBLANES, t), jnp.int32), jax.ShapeDtypeStruct((SUBLANES, t), jnp.int32),
            jax.ShapeDtypeStruct((SUBLANES, t), F32),
            jax.ShapeDtypeStruct((N_EXPERTS, LANES), jnp.int32),
        ],
        scratch_shapes=[pltpu.VMEM((N_EXPERTS, LANES), F32)],
        compiler_params=_cparams(("arbitrary",)),
        name="mid",
    )(x2, ys, yl, kk, vv, wo1, wo2, gx, wq, wo, gm, wr_hi, wr_lo, br)


def _dispatch_kernel(pend_ref, dst_ref, hn_ref, x_hbm, zero_ref, sem, *, parts):
    i = pl.program_id(0)
    tc = hn_ref.shape[0] // parts
    bm = zero_ref.shape[0] // parts
    rows = TOP_K * tc

    def tile(ref, n, count=1):
        return ref.at[pl.ds(pl.multiple_of(n * parts, parts), count * parts), :]

    @pl.when(i == 0)
    def _():
        zero_ref[...] = jnp.zeros(zero_ref.shape, F32)

        def fill(start):
            return pltpu.make_async_copy(zero_ref, tile(x_hbm, start, bm), sem.at[1])

        for e in range(N_EXPERTS):
            fill(jnp.maximum(pend_ref[e] - bm, 0)).start()
        for e in range(N_EXPERTS):
            fill(0).wait()

        def tail(j, c):
            fill(j * bm).start()
            fill(0).wait()
            return c

        lax.fori_loop(pend_ref[N_EXPERTS - 1] // bm, x_hbm.shape[0] // (bm * parts), tail, 0)

    def body(j, c):
        t0 = pl.multiple_of(j * ISSUE_UNROLL, ISSUE_UNROLL)
        for u in range(ISSUE_UNROLL):
            for k in range(TOP_K):
                dst = dst_ref[0, 0, k * tc + t0 + u]
                pltpu.make_async_copy(tile(hn_ref, t0 + u), tile(x_hbm, dst), sem.at[0]).start(priority=k % 2)
        return c

    lax.fori_loop(0, tc // ISSUE_UNROLL, body, 0)
    pltpu.make_async_copy(tile(x_hbm, 0, rows), tile(x_hbm, 0, rows), sem.at[0]).wait()


def _dispatch(pend, dest3, hn_tm, n_rows, d):
    parts = d // LANES
    t = hn_tm.shape[0] // parts
    tc = COMB_TILE
    grid_spec = pltpu.PrefetchScalarGridSpec(
        num_scalar_prefetch=1,
        grid=(t // tc,),
        in_specs=[
            pl.BlockSpec((1, 1, TOP_K * tc), lambda i, pe: (i, 0, 0), memory_space=pltpu.SMEM),
            pl.BlockSpec((tc * parts, LANES), lambda i, pe: (i, 0)),
        ],
        out_specs=pl.BlockSpec(memory_space=pl.ANY),
        scratch_shapes=[pltpu.VMEM((MOE_BLOCK * parts, LANES), F32), pltpu.SemaphoreType.DMA((2,))],
    )
    return pl.pallas_call(
        functools.partial(_dispatch_kernel, parts=parts),
        grid_spec=grid_spec,
        out_shape=jax.ShapeDtypeStruct((n_rows * parts, LANES), F32),
        compiler_params=_cparams(("arbitrary",)),
        name="dispatch",
    )(pend, dest3, hn_tm)


def _moe_kernel(be_ref, nxt_ref, nvb_ref, x_ref, wgu_hbm, bg_ref, bu_ref, wd_hbm, bd_ref, perm_ref,
                y_ref, wgu_f, wd_f, wgu_s, wd_s, act_s, sem):
    i = pl.program_id(0)
    nvb = nvb_ref[0]
    f, d = wd_f.shape
    bm = act_s.shape[0]
    pw = perm_ref.shape[0]
    half = pw // 2

    def fetch(e):
        return (pltpu.make_async_copy(wgu_hbm.at[e], wgu_f, sem.at[0]),
                pltpu.make_async_copy(wd_hbm.at[e], wd_f, sem.at[1]))

    @pl.when(jnp.logical_and(i == 0, nvb > 0))
    def _():
        for c in fetch(be_ref[0]):
            c.start()

    changed = jnp.logical_or(i == 0, be_ref[i] != be_ref[jnp.maximum(i - 1, 0)])

    @pl.when(jnp.logical_and(changed, i < nvb))
    def _():
        for c in fetch(be_ref[i]):
            c.wait()
        for c in range(2 * f // pw):
            wc = wgu_f[:, c * pw:(c + 1) * pw].astype(BF16)
            wgu_s[:, c * pw:(c + 1) * pw] = _dot(wc, perm_ref[...]).astype(BF16)
        wd_s[...] = wd_f[...].astype(BF16)

        @pl.when(nxt_ref[i] >= 0)
        def _():
            for c in fetch(nxt_ref[i]):
                c.start()

    @pl.when(i < nvb)
    def _():
        xb = jnp.concatenate([p.astype(BF16) for p in _load_token_major(x_ref, bm, d)], axis=1)
        for c in range(2 * f // pw):
            gu = _dot(xb, wgu_s[:, c * pw:(c + 1) * pw])
            g = gu[:, :half] + bg_ref[0, :, c * half:(c + 1) * half]
            u = gu[:, half:] + bu_ref[0, :, c * half:(c + 1) * half]
            g = jnp.minimum(g, SWIGLU_LIMIT)
            u = jnp.clip(u, -SWIGLU_LIMIT, SWIGLU_LIMIT)
            act = (u + 1.0) * (g * _sigmoid(SWIGLU_ALPHA * g))
            act_s[:, c * half:(c + 1) * half] = act.astype(BF16)
        _store_token_major(y_ref, _dot(act_s[...], wd_s[...]) + bd_ref[0])

    @pl.when(i >= nvb)
    def _():
        y_ref[...] = jnp.zeros(y_ref.shape, F32)


def _moe(block_e, next_e, nvb, xrows_tm, wgu, bg, bu, wd, bd, perm):
    f, d = wd.shape[1], wd.shape[2]
    parts = d // LANES
    bm = MOE_BLOCK
    nb = xrows_tm.shape[0] // (bm * parts)
    emap = lambda i, be, nx, nv: (be[i], 0, 0)
    grid_spec = pltpu.PrefetchScalarGridSpec(
        num_scalar_prefetch=3,
        grid=(nb,),
        in_specs=[
            pl.BlockSpec((bm * parts, LANES),
                         lambda i, be, nx, nv: (jnp.minimum(i, jnp.maximum(nv[0] - 1, 0)), 0)),
            pl.BlockSpec(memory_space=pl.ANY),
            pl.BlockSpec((1, 1, f), emap), pl.BlockSpec((1, 1, f), emap),
            pl.BlockSpec(memory_space=pl.ANY),
            pl.BlockSpec((1, 1, d), emap),
            pl.BlockSpec(perm.shape, lambda i, be, nx, nv: (0, 0)),
        ],
        out_specs=pl.BlockSpec((bm * parts, LANES), lambda i, be, nx, nv: (i, 0)),
        scratch_shapes=[
            pltpu.VMEM((d, 2 * f), F32), pltpu.VMEM((f, d), F32),
            pltpu.VMEM((d, 2 * f), BF16), pltpu.VMEM((f, d), BF16),
            pltpu.VMEM((bm, f), BF16),
            pltpu.SemaphoreType.DMA((2,)),
        ],
    )
    return pl.pallas_call(
        _moe_kernel,
        grid_spec=grid_spec,
        out_shape=jax.ShapeDtypeStruct(xrows_tm.shape, F32),
        compiler_params=_cparams(("arbitrary",)),
        name="moe",
    )(block_e, next_e, nvb, xrows_tm, wgu, bg, bu, wd, bd, perm)


def _comb_kernel(dst_ref, dstn_ref, h_ref, gate_ref, y_hbm, g_ref, o_ref, ybuf, sem, *, parts):
    i = pl.program_id(0)
    n = pl.num_programs(0)
    tc, d = h_ref.shape
    rows = TOP_K * tc
    slot = i % 2

    def tile(ref, n, count=1):
        return ref.at[pl.ds(pl.multiple_of(n * parts, parts), count * parts), :]

    def start_rows(idx_ref, s):
        def body(j, c):
            t0 = pl.multiple_of(j * ISSUE_UNROLL, ISSUE_UNROLL)
            for u in range(ISSUE_UNROLL):
                for k in range(TOP_K):
                    r = k * tc + t0 + u
                    pltpu.make_async_copy(tile(y_hbm, idx_ref[0, 0, r]), tile(ybuf.at[s], r),
                                          sem.at[s]).start(priority=k % 2)
            return c

        lax.fori_loop(0, tc // ISSUE_UNROLL, body, 0)

    @pl.when(i == 0)
    def _():
        start_rows(dst_ref, 0)

    @pl.when(i + 1 < n)
    def _():
        start_rows(dstn_ref, 1 - slot)

    pltpu.make_async_copy(tile(y_hbm, 0, rows), ybuf.at[slot], sem.at[slot]).wait()
    acc = [h_ref[:, s * LANES:(s + 1) * LANES] for s in range(parts)]
    for k in range(TOP_K):
        yk = _load_token_major(ybuf.at[slot], tc, d, base=k * tc * parts)
        gk = gate_ref[:, k:k + 1]
        acc = [a + p * gk for a, p in zip(acc, yk)]
    o_ref[...] = _rms(jnp.concatenate(acc, axis=1), g_ref[...])


def _combine(dest3, h2, gate, y_tm, g):
    t, d = h2.shape
    parts = d // LANES
    tc = COMB_TILE
    nt = t // tc
    return pl.pallas_call(
        functools.partial(_comb_kernel, parts=parts),
        grid=(nt,),
        in_specs=[
            pl.BlockSpec((1, 1, TOP_K * tc), lambda i: (i, 0, 0), memory_space=pltpu.SMEM),
            pl.BlockSpec((1, 1, TOP_K * tc), lambda i: (jnp.minimum(i + 1, nt - 1), 0, 0),
                         memory_space=pltpu.SMEM),
            pl.BlockSpec((tc, d), lambda i: (i, 0)),
            pl.BlockSpec((tc, SUBLANES), lambda i: (i, 0)),
            pl.BlockSpec(memory_space=pl.ANY),
            pl.BlockSpec((1, d), lambda i: (0, 0)),
        ],
        out_specs=pl.BlockSpec((tc, d), lambda i: (i, 0)),
        out_shape=jax.ShapeDtypeStruct((t, d), F32),
        scratch_shapes=[pltpu.VMEM((2, TOP_K * tc * parts, LANES), F32), pltpu.SemaphoreType.DMA((2,))],
        compiler_params=_cparams(("arbitrary",)),
        name="combine",
    )(dest3, dest3, h2, gate, y_tm, g)


def _block_diag(wb, per):
    nb, bw, _ = wb.shape
    wq = wb.reshape(nb // per, per, bw, bw)
    eye = jnp.eye(per, dtype=wb.dtype)
    out = jnp.einsum('qaij,ab->qaibj', wq, eye)
    return out.reshape(nb // per, per * bw, per * bw)


def _pad_lanes(v, fill=0.0):
    return jnp.pad(v, (0, LANES - v.shape[0]), constant_values=fill).reshape(1, LANES)


def kernel(x, mem, norm_mix, w_in, ssd_conv_w, ssd_conv_b, ssd_dt_bias, ssd_a_log, ssd_d, ssd_norm, lru_conv_w, lru_conv_b, lru_wa, lru_ba, lru_wx, lru_bx, lru_lambda, w_out, norm_xattn, norm_mem, w_q, w_kv, w_o, norm_moe, w_router, b_router, w_gate_up, b_gate_up, w_down, b_down, norm_final):
    b, s, d = x.shape
    t = b * s
    n_mem = mem.shape[1]
    w = d
    cdim = w + 2 * SSD_GROUPS * SSD_STATE
    o1, o2, o3, o4 = w, w + cdim, w + cdim + SSD_HEADS, w + cdim + SSD_HEADS + w

    wi = w_in[0]
    wzx = wi[:, :o2].astype(BF16)
    wdt = jnp.pad(wi[:, o2:o3], ((0, 0), (0, LANES - SSD_HEADS)))
    wdt_hi, wdt_lo = _hilo(wdt)
    wxr = wi[:, o3:o4].astype(BF16)
    wgr = wi[:, o4:].astype(BF16)
    e01 = (jnp.arange(LANES)[:, None] == (jnp.arange(w)[None, :] // SSD_HEAD_DIM)).astype(BF16)
    dskip = jnp.repeat(ssd_d[0], SSD_HEAD_DIM).reshape(1, w)
    per = 256 // (w // LRU_BLOCKS)
    wa_bd = _block_diag(lru_wa[0], per).astype(BF16)
    wx_bd = _block_diag(lru_wx[0], per).astype(BF16)
    wr = jnp.pad(w_router[0], ((0, 0), (0, LANES - N_EXPERTS)))
    wr_hi, wr_lo = _hilo(wr)
    br = _pad_lanes(b_router[0], fill=-1e30)
    bgu = b_gate_up[0]
    bg = bgu[:, None, 0::2]
    bu = bgu[:, None, 1::2]
    bd = b_down[0][:, None, :]
    pw = 2 * LANES
    col = jnp.arange(pw)
    src_col = jnp.where(col < LANES, 2 * col, 2 * (col - LANES) + 1)
    perm = (jnp.arange(pw)[:, None] == src_col[None, :]).astype(BF16)

    x2 = x.reshape(t, d)
    y_ssd, y_lru = _mix(
        x, norm_mix[0].reshape(1, d), wzx, wdt_hi, wdt_lo, wxr, wgr, _shift_matrix(MIX_TILE),
        ssd_conv_w[0], ssd_conv_b[0].reshape(1, cdim), _pad_lanes(ssd_dt_bias[0]), _pad_lanes(ssd_a_log[0]),
        dskip, ssd_norm[0].reshape(1, w), e01,
        lru_conv_w[0], lru_conv_b[0].reshape(1, w), wa_bd, lru_ba[0].reshape(1, w), wx_bd, lru_bx[0].reshape(1, w),
        lru_lambda[0].reshape(1, w))

    kk, vv = _kv(mem.reshape(b * n_mem, d), norm_mem[0].reshape(1, d), w_kv[0].astype(BF16))
    wo_mix = w_out[0].astype(BF16)
    h2, hn, idx_m, rank_m, gate_m, cnt = _mid(
        x2, y_ssd.reshape(t, w), y_lru.reshape(t, w), kk.reshape(b, n_mem, d), vv.reshape(b, n_mem, d),
        wo_mix[:w], wo_mix[w:], norm_xattn[0].reshape(1, d), w_q[0].astype(BF16), w_o[0].astype(BF16),
        norm_moe[0].reshape(1, d), wr_hi, wr_lo, br, s)

    bm = MOE_BLOCK
    counts = cnt[:, 0]
    pcounts = (counts + bm - 1) // bm * bm
    pend = jnp.cumsum(pcounts).astype(jnp.int32)
    pstart = pend - pcounts
    idx = idx_m[:TOP_K, :]
    onehot = idx[:, :, None] == jnp.arange(N_EXPERTS, dtype=jnp.int32)[None, None, :]
    dest = jnp.sum(jnp.where(onehot, pstart[None, None, :], 0), axis=-1) + rank_m[:TOP_K, :]
    n_pairs = t * TOP_K
    nb = (n_pairs + N_EXPERTS * (bm - 1) + bm - 1) // bm
    blk0 = jnp.arange(nb, dtype=jnp.int32) * bm
    block_e = jnp.minimum(jnp.sum(pend[None, :] <= blk0[:, None], axis=1), N_EXPERTS - 1).astype(jnp.int32)
    nvb = (pend[-1] // bm).astype(jnp.int32).reshape(1)
    after = pend[block_e] // bm
    next_e = jnp.where(after < nvb[0], block_e[jnp.minimum(after, nb - 1)], -1).astype(jnp.int32)
    tc = COMB_TILE
    dest3 = dest.astype(jnp.int32).reshape(TOP_K, t // tc, tc).transpose(1, 0, 2).reshape(t // tc, 1, TOP_K * tc)

    xrows = _dispatch(pend, dest3, hn, nb * bm, d)
    y = _moe(block_e, next_e, nvb, xrows, w_gate_up[0], bg, bu, w_down[0], bd, perm)
    out = _combine(dest3, h2, gate_m.T, y, norm_final.reshape(1, d))
    return out.reshape(b, s, d)
```

```python
import functools

import jax
import jax.numpy as jnp
from jax import lax
from jax.experimental import pallas as pl
from jax.experimental.pallas import tpu as pltpu

F32 = jnp.float32
BF16 = jnp.bfloat16

NORM_EPS = 1e-6
LANES = 128
SUBLANES = 8
SSD_HEAD_DIM = 64
SSD_HEADS = 16
SSD_GROUPS = 4
SSD_STATE = 128
CONV_K = 4
LRU_BLOCKS = 16
RG_C = 8.0
X_HEADS = 4
N_EXPERTS = 32
TOP_K = 4
SWIGLU_LIMIT = 7.0
SWIGLU_ALPHA = 1.702

VMEM_LIMIT = 56 * 1024 * 1024

IN_TILE = 1024
SSD_CHUNK = 256
LRU_TILE = 256
MID_TILE = 512
MOE_BLOCK = 512
COMB_TILE = 256
ISSUE_UNROLL = 8


def _cparams(sem):
    return pltpu.CompilerParams(dimension_semantics=sem, vmem_limit_bytes=VMEM_LIMIT)


def _rms(x, g):
    ms = jnp.mean(x * x, axis=-1, keepdims=True)
    return x * lax.rsqrt(ms + NORM_EPS) * g


def _sigmoid(x):
    return 0.5 * jnp.tanh(0.5 * x) + 0.5


def _softplus(x):
    return jnp.maximum(x, 0.0) + jnp.log(1.0 + jnp.exp(-jnp.abs(x)))


def _split3(x):
    a = x.astype(BF16)
    r = x - a.astype(F32)
    b = r.astype(BF16)
    c = (r - b.astype(F32)).astype(BF16)
    return a, b, c


def _dot(a, b):
    return jnp.dot(a, b, preferred_element_type=F32)


def _dot_nt(a, b):
    return lax.dot_general(a, b, (((1,), (1,)), ((), ())), preferred_element_type=F32)


def _dot01_right(x, m01):
    a, b, c = _split3(x)
    return _dot(a, m01) + _dot(b, m01) + _dot(c, m01)


def _dot01_left(m01, x):
    a, b, c = _split3(x)
    return _dot(m01, a) + _dot(m01, b) + _dot(m01, c)


def _dot_hilo(x, w_hi, w_lo):
    xh = x.astype(BF16)
    xl = (x - xh.astype(F32)).astype(BF16)
    return _dot(xh, w_hi) + _dot(xl, w_hi) + _dot(xh, w_lo)


def _hilo(w):
    hi = w.astype(BF16)
    lo = (w - hi.astype(F32)).astype(BF16)
    return hi, lo


def _store_token_major(ref, val, base=0):
    n, d = val.shape
    parts = d // LANES
    for s in range(parts):
        ref[pl.ds(base + s, n, stride=parts), :] = val[:, s * LANES:(s + 1) * LANES]


def _load_token_major(ref, n, d, base=0):
    parts = d // LANES
    return [ref[pl.ds(base + s, n, stride=parts), :] for s in range(parts)]


def _in_proj_kernel(x_ref, g_ref, wzx_ref, wdth_ref, wdtl_ref, wxr_ref, wgr_ref,
                    z_ref, xbc_ref, dt_ref, xr_ref, gr_ref):
    d = x_ref.shape[1]
    hn = _rms(x_ref[...], g_ref[...])
    hb = hn.astype(BF16)
    z_ref[...] = _dot(hb, wzx_ref[:, :d]).astype(BF16)
    xbc_ref[...] = _dot(hb, wzx_ref[:, d:]).astype(BF16)
    dt_ref[...] = _dot_hilo(hn, wdth_ref[...], wdtl_ref[...])
    xr_ref[...] = _dot(hb, wxr_ref[...]).astype(BF16)
    gr_ref[...] = _dot(hb, wgr_ref[...]).astype(BF16)


def _in_proj(x2, g, wzx, wdt_hi, wdt_lo, wxr, wgr):
    t, d = x2.shape
    tm = IN_TILE
    nzx = wzx.shape[1]
    const = lambda i: (0, 0)
    row = lambda i: (i, 0)
    return pl.pallas_call(
        _in_proj_kernel,
        grid=(t // tm,),
        in_specs=[
            pl.BlockSpec((tm, d), row),
            pl.BlockSpec((1, d), const),
            pl.BlockSpec((d, nzx), const),
            pl.BlockSpec((d, LANES), const),
            pl.BlockSpec((d, LANES), const),
            pl.BlockSpec((d, d), const),
            pl.BlockSpec((d, d), const),
        ],
        out_specs=[
            pl.BlockSpec((tm, d), row),
            pl.BlockSpec((tm, nzx - d), row),
            pl.BlockSpec((tm, LANES), row),
            pl.BlockSpec((tm, d), row),
            pl.BlockSpec((tm, d), row),
        ],
        out_shape=[
            jax.ShapeDtypeStruct((t, d), BF16),
            jax.ShapeDtypeStruct((t, nzx - d), BF16),
            jax.ShapeDtypeStruct((t, LANES), F32),
            jax.ShapeDtypeStruct((t, d), BF16),
            jax.ShapeDtypeStruct((t, d), BF16),
        ],
        compiler_params=_cparams(("arbitrary",)),
        name="in_proj",
    )(x2, g, wzx, wdt_hi, wdt_lo, wxr, wgr)


def _shift_matrix(n):
    return jnp.concatenate([jnp.eye(n, k=-(CONV_K - 1 - j), dtype=BF16) for j in range(CONV_K - 1)], axis=0)


def _causal_conv(halo_ref, x, shift_ref, w_ref, b_ref, first):
    n = x.shape[0]
    pad = SUBLANES
    k1 = CONV_K - 1

    @pl.when(first)
    def _():
        halo_ref[...] = jnp.zeros(halo_ref.shape, F32)

    xf = x.astype(F32)
    taps = _dot(shift_ref[...], x)
    acc = b_ref[...] + w_ref[k1:k1 + 1, :] * xf
    for j in range(k1):
        acc = acc + w_ref[j:j + 1, :] * taps[j * n:(j + 1) * n]
    head = acc[:pad]
    for j in range(k1):
        head = head + w_ref[j:j + 1, :] * halo_ref[pad - k1 + j:2 * pad - k1 + j, :]
    halo_ref[0:pad, :] = xf[n - pad:n]
    return jnp.concatenate([head, acc[pad:]], axis=0)


def _ssd_kernel(xbc_ref, dt_ref, z_ref, sh_ref, cw_ref, cb_ref, dtb_ref, alog_ref, dskip_ref, gn_ref, e_ref,
                y_ref, halo_ref, st_ref):
    n = xbc_ref.shape[1]
    w = z_ref.shape[2]
    gw = w // SSD_GROUPS
    first = pl.program_id(1) == 0

    @pl.when(first)
    def _():
        st_ref[...] = jnp.zeros(st_ref.shape, F32)

    conv = _causal_conv(halo_ref, xbc_ref[0], sh_ref, cw_ref, cb_ref, first)
    xc = conv * _sigmoid(conv)
    xs = xc[:, :w]

    dt = _softplus(dt_ref[0] + dtb_ref[...])
    a = -jnp.exp(alog_ref[...])
    da = dt * a
    ri = lax.broadcasted_iota(jnp.int32, (n, n), 0)
    ci = lax.broadcasted_iota(jnp.int32, (n, n), 1)
    causal = ri >= ci
    tril = jnp.where(causal, 1.0, 0.0).astype(BF16)
    a_cs = _dot01_left(tril, da)
    a_cs_t = a_cs.T

    e01 = e_ref[...]
    dt_x = _dot01_right(dt, e01)
    acs_x = _dot01_right(a_cs, e01)
    last_x = acs_x[n - 1:n, :]
    xdt = xs * dt_x
    xdt_b = xdt.astype(BF16)
    xdt_end = (xdt * jnp.exp(last_x - acs_x)).astype(BF16)
    exp_acs = jnp.exp(acs_x)
    chunk_decay = jnp.exp(last_x)
    lane = lax.broadcasted_iota(jnp.int32, (n, gw), 1)

    for g in range(SSD_GROUPS):
        lo = g * gw
        bg = xc[:, w + g * SSD_STATE:w + (g + 1) * SSD_STATE].astype(BF16)
        cg = xc[:, w + (SSD_GROUPS + g) * SSD_STATE:w + (SSD_GROUPS + g + 1) * SSD_STATE].astype(BF16)
        cb = _dot_nt(cg, bg)
        prev = st_ref[g]
        acc = _dot(cg, prev.astype(BF16)) * exp_acs[:, lo:lo + gw]
        new = lax.dot_general(bg, xdt_end[:, lo:lo + gw], (((0,), (0,)), ((), ())),
                              preferred_element_type=F32)
        st_ref[g] = chunk_decay[:, lo:lo + gw] * prev + new
        xg = xdt_b[:, lo:lo + gw]
        for k in range(SSD_HEADS // SSD_GROUPS):
            h = g * (SSD_HEADS // SSD_GROUPS) + k
            seg = a_cs[:, h:h + 1] - a_cs_t[h:h + 1, :]
            dec = jnp.exp(jnp.where(causal, seg, -jnp.inf))
            m = (cb * dec).astype(BF16)
            in_head = (lane >= k * SSD_HEAD_DIM) & (lane < (k + 1) * SSD_HEAD_DIM)
            acc = acc + _dot(m, jnp.where(in_head, xg, jnp.zeros_like(xg)))
        yg = acc + xs[:, lo:lo + gw] * dskip_ref[:, lo:lo + gw]
        zg = z_ref[0, :, lo:lo + gw].astype(F32)
        u = yg * (zg * _sigmoid(zg))
        u = u * lax.rsqrt(jnp.mean(u * u, axis=-1, keepdims=True) + NORM_EPS)
        y_ref[0, :, lo:lo + gw] = (u * gn_ref[:, lo:lo + gw]).astype(BF16)


def _ssd(xbc, dt, z, shift, cw, cb, dtb, alog, dskip, gn, e01):
    b, s, cdim = xbc.shape
    w = z.shape[2]
    n = SSD_CHUNK
    tile = lambda i, j: (i, j, 0)
    const = lambda i, j: (0, 0)
    return pl.pallas_call(
        _ssd_kernel,
        grid=(b, s // n),
        in_specs=[
            pl.BlockSpec((1, n, cdim), tile),
            pl.BlockSpec((1, n, LANES), tile),
            pl.BlockSpec((1, n, w), tile),
            pl.BlockSpec(shift.shape, const),
            pl.BlockSpec((CONV_K, cdim), const),
            pl.BlockSpec((1, cdim), const),
            pl.BlockSpec((1, LANES), const),
            pl.BlockSpec((1, LANES), const),
            pl.BlockSpec((1, w), const),
            pl.BlockSpec((1, w), const),
            pl.BlockSpec((LANES, w), const),
        ],
        out_specs=pl.BlockSpec((1, n, w), tile),
        out_shape=jax.ShapeDtypeStruct((b, s, w), BF16),
        scratch_shapes=[
            pltpu.VMEM((2 * SUBLANES, cdim), F32),
            pltpu.VMEM((SSD_GROUPS, SSD_STATE, w // SSD_GROUPS), F32),
        ],
        compiler_params=_cparams(("arbitrary", "arbitrary")),
        name="ssd",
    )(xbc, dt, z, shift, cw, cb, dtb, alog, dskip, gn, e01)


def _gelu_tanh(x):
    c = 0.7978845608028654
    return 0.5 * x * (1.0 + jnp.tanh(c * (x + 0.044715 * (x * x * x))))


def _lru_kernel(xr_ref, gr_ref, sh_ref, cw_ref, cb_ref, wa_ref, ba_ref, wx_ref, bx_ref, lam_ref,
                y_ref, halo_ref, car_ref, h_ref):
    n = xr_ref.shape[1]
    w = xr_ref.shape[2]
    first = pl.program_id(1) == 0

    @pl.when(first)
    def _():
        car_ref[...] = jnp.zeros(car_ref.shape, F32)

    xc = _causal_conv(halo_ref, xr_ref[0], sh_ref, cw_ref, cb_ref, first)
    xb = xc.astype(BF16)
    nq = wa_ref.shape[0]
    qw = w // nq
    r_parts, i_parts = [], []
    for q in range(nq):
        xq = xb[:, q * qw:(q + 1) * qw]
        r_parts.append(_dot(xq, wa_ref[q]))
        i_parts.append(_dot(xq, wx_ref[q]))
    r = _sigmoid(jnp.concatenate(r_parts, axis=1) + ba_ref[...])
    gi = _sigmoid(jnp.concatenate(i_parts, axis=1) + bx_ref[...])
    log_a = (-RG_C) * r * _softplus(-lam_ref[...])
    a = jnp.exp(log_a)
    u = xc * gi * jnp.sqrt(1.0 - a * a)

    groups = n // SUBLANES
    sub = lax.broadcasted_iota(jnp.int32, (groups, SUBLANES, w), 1)
    ap = a.reshape(groups, SUBLANES, w)
    bp = u.reshape(groups, SUBLANES, w)
    for d in (1, 2, 4):
        m = sub >= d
        bp = jnp.where(m, ap * pltpu.roll(bp, d, 1) + bp, bp)
        ap = jnp.where(m, ap * pltpu.roll(ap, d, 1), ap)
    carry = car_ref[...]
    for g in range(groups):
        hb = bp[g] + ap[g] * carry
        h_ref[g * SUBLANES:(g + 1) * SUBLANES, :] = hb
        carry = jnp.broadcast_to(hb[SUBLANES - 1:SUBLANES, :], (SUBLANES, w))
    car_ref[...] = carry
    y_ref[0] = (h_ref[...] * _gelu_tanh(gr_ref[0].astype(F32))).astype(BF16)


def _lru(xr, gr, shift, cw, cb, wa, ba, wx, bx, lam):
    b, s, w = xr.shape
    n = LRU_TILE
    nq, qw, _ = wa.shape
    tile = lambda i, j: (i, j, 0)
    const = lambda i, j: (0, 0)
    const3 = lambda i, j: (0, 0, 0)
    return pl.pallas_call(
        _lru_kernel,
        grid=(b, s // n),
        in_specs=[
            pl.BlockSpec((1, n, w), tile),
            pl.BlockSpec((1, n, w), tile),
            pl.BlockSpec(shift.shape, const),
            pl.BlockSpec((CONV_K, w), const),
            pl.BlockSpec((1, w), const),
            pl.BlockSpec((nq, qw, qw), const3),
            pl.BlockSpec((1, w), const),
            pl.BlockSpec((nq, qw, qw), const3),
            pl.BlockSpec((1, w), const),
            pl.BlockSpec((1, w), const),
        ],
        out_specs=pl.BlockSpec((1, n, w), tile),
        out_shape=jax.ShapeDtypeStruct((b, s, w), BF16),
        scratch_shapes=[
            pltpu.VMEM((2 * SUBLANES, w), F32),
            pltpu.VMEM((SUBLANES, w), F32),
            pltpu.VMEM((n, w), F32),
        ],
        compiler_params=_cparams(("arbitrary", "arbitrary")),
        name="lru",
    )(xr, gr, shift, cw, cb, wa, ba, wx, bx, lam)


def _kv_kernel(m_ref, g_ref, w_ref, k_ref, v_ref):
    d = m_ref.shape[1]
    mn = _rms(m_ref[...], g_ref[...]).astype(BF16)
    k_ref[...] = _dot(mn, w_ref[:, :d]).astype(BF16)
    v_ref[...] = _dot(mn, w_ref[:, d:]).astype(BF16)


def _kv(mem2, g, wkv):
    t, d = mem2.shape
    tm = min(t, 512)
    row = lambda i: (i, 0)
    const = lambda i: (0, 0)
    return pl.pallas_call(
        _kv_kernel,
        grid=(t // tm,),
        in_specs=[pl.BlockSpec((tm, d), row), pl.BlockSpec((1, d), const), pl.BlockSpec((d, 2 * d), const)],
        out_specs=[pl.BlockSpec((tm, d), row), pl.BlockSpec((tm, d), row)],
        out_shape=[jax.ShapeDtypeStruct((t, d), BF16), jax.ShapeDtypeStruct((t, d), BF16)],
        compiler_params=_cparams(("arbitrary",)),
        name="kv",
    )(mem2, g, wkv)


def _mid_kernel(x_ref, ys_ref, yl_ref, k_ref, v_ref, wo1_ref, wo2_ref, gx_ref, wq_ref, wo_ref,
                gm_ref, wrh_ref, wrl_ref, br_ref,
                h_ref, hn_ref, idx_ref, rank_ref, gate_ref, cnt_ref, car_ref):
    tm, d = x_ref.shape
    hd = d // X_HEADS

    @pl.when(pl.program_id(0) == 0)
    def _():
        car_ref[...] = jnp.zeros(car_ref.shape, F32)

    h1 = x_ref[...] + _dot(ys_ref[...], wo1_ref[...]) + _dot(yl_ref[...], wo2_ref[...])

    q = _dot(_rms(h1, gx_ref[...]).astype(BF16), wq_ref[...]).astype(BF16)
    o_parts = []
    for hh in range(X_HEADS):
        sl = slice(hh * hd, (hh + 1) * hd)
        sc = _dot_nt(q[:, sl], k_ref[0, :, sl]) * (hd ** -0.5)
        sc = sc - jnp.max(sc, axis=-1, keepdims=True)
        p = jnp.exp(sc)
        p = p / jnp.sum(p, axis=-1, keepdims=True)
        o_parts.append(_dot(p.astype(BF16), v_ref[0, :, sl]))
    o = jnp.concatenate(o_parts, axis=1).astype(BF16)
    h2 = h1 + _dot(o, wo_ref[...])
    h_ref[...] = h2

    hn = _rms(h2, gm_ref[...])
    _store_token_major(hn_ref, hn)
    logits = _dot_hilo(hn, wrh_ref[...], wrl_ref[...]) + br_ref[...]

    l = logits.T[:N_EXPERTS, :]
    row = lax.broadcasted_iota(jnp.int32, (N_EXPERTS, tm), 0)
    picked = jnp.zeros((N_EXPERTS, tm), F32)
    vals, idxs = [], []
    for _ in range(TOP_K):
        m = jnp.max(l, axis=0, keepdims=True)
        idx = jnp.min(jnp.where(l == m, row, N_EXPERTS), axis=0, keepdims=True)
        sel = row == idx
        vals.append(m)
        idxs.append(idx)
        picked = jnp.where(sel, 1.0, picked)
        l = jnp.where(sel, -jnp.inf, l)
    ex = [jnp.exp(v - vals[0]) for v in vals]
    den = ex[0] + ex[1] + ex[2] + ex[3]

    ri = lax.broadcasted_iota(jnp.int32, (tm, tm), 0)
    ci = lax.broadcasted_iota(jnp.int32, (tm, tm), 1)
    earlier = jnp.where(ri < ci, 1.0, 0.0).astype(BF16)
    before = _dot(picked.astype(BF16), earlier) + car_ref[:, 0:1]
    out_row = lax.broadcasted_iota(jnp.int32, (SUBLANES, tm), 0)
    idx_out = jnp.zeros((SUBLANES, tm), jnp.int32)
    rank_out = jnp.zeros((SUBLANES, tm), jnp.int32)
    gate_out = jnp.zeros((SUBLANES, tm), F32)
    for k in range(TOP_K):
        rk = jnp.sum(jnp.where(row == idxs[k], before, 0.0), axis=0, keepdims=True)
        at_k = out_row == k
        idx_out = jnp.where(at_k, idxs[k], idx_out)
        rank_out = jnp.where(at_k, rk.astype(jnp.int32), rank_out)
        gate_out = jnp.where(at_k, ex[k] / den, gate_out)
    idx_ref[...] = idx_out
    rank_ref[...] = rank_out
    gate_ref[...] = gate_out
    total = car_ref[...] + jnp.sum(picked, axis=1, keepdims=True)
    car_ref[...] = total
    cnt_ref[...] = total.astype(jnp.int32)


def _mid(x2, ys, yl, kk, vv, wo1, wo2, gx, wq, wo, gm, wr_hi, wr_lo, br, seq):
    t, d = x2.shape
    tm = MID_TILE
    m = kk.shape[1]
    per_b = seq // tm
    row = lambda i: (i, 0)
    col = lambda i: (0, i)
    const = lambda i: (0, 0)
    kvmap = lambda i: (i // per_b, 0, 0)
    wspec = pl.BlockSpec((d, d), const)
    vspec = pl.BlockSpec((1, d), const)
    return pl.pallas_call(
        _mid_kernel,
        grid=(t // tm,),
        in_specs=[
            pl.BlockSpec((tm, d), row), pl.BlockSpec((tm, d), row), pl.BlockSpec((tm, d), row),
            pl.BlockSpec((1, m, d), kvmap), pl.BlockSpec((1, m, d), kvmap),
            wspec, wspec, vspec, wspec, wspec, vspec,
            pl.BlockSpec((d, LANES), const), pl.BlockSpec((d, LANES), const), pl.BlockSpec((1, LANES), const),
        ],
        out_specs=[
            pl.BlockSpec((tm, d), row), pl.BlockSpec((tm * d // LANES, LANES), row),
            pl.BlockSpec((SUBLANES, tm), col), pl.BlockSpec((SUBLANES, tm), col), pl.BlockSpec((SUBLANES, tm), col),
            pl.BlockSpec((N_EXPERTS, LANES), const),
        ],
        out_shape=[
            jax.ShapeDtypeStruct((t, d), F32), jax.ShapeDtypeStruct((t * d // LANES, LANES), F32),
            jax.ShapeDtypeStruct((SUBLANES, t), jnp.int32), jax.ShapeDtypeStruct((SUBLANES, t), jnp.int32),
            jax.ShapeDtypeStruct((SUBLANES, t), F32),
            jax.ShapeDtypeStruct((N_EXPERTS, LANES), jnp.int32),
        ],
        scratch_shapes=[pltpu.VMEM((N_EXPERTS, LANES), F32)],
        compiler_params=_cparams(("arbitrary",)),
        name="mid",
    )(x2, ys, yl, kk, vv, wo1, wo2, gx, wq, wo, gm, wr_hi, wr_lo, br)


def _dispatch_kernel(pend_ref, dst_ref, hn_ref, x_hbm, zero_ref, sem, *, parts):
    i = pl.program_id(0)
    tc = hn_ref.shape[0] // parts
    bm = zero_ref.shape[0] // parts
    rows = TOP_K * tc

    def tile(ref, n, count=1):
        return ref.at[pl.ds(pl.multiple_of(n * parts, parts), count * parts), :]

    @pl.when(i == 0)
    def _():
        zero_ref[...] = jnp.zeros(zero_ref.shape, F32)

        def fill(start):
            return pltpu.make_async_copy(zero_ref, tile(x_hbm, start, bm), sem.at[1])

        for e in range(N_EXPERTS):
            fill(jnp.maximum(pend_ref[e] - bm, 0)).start()
        for e in range(N_EXPERTS):
            fill(0).wait()

        def tail(j, c):
            fill(j * bm).start()
            fill(0).wait()
            return c

        lax.fori_loop(pend_ref[N_EXPERTS - 1] // bm, x_hbm.shape[0] // (bm * parts), tail, 0)

    def body(j, c):
        t0 = pl.multiple_of(j * ISSUE_UNROLL, ISSUE_UNROLL)
        for u in range(ISSUE_UNROLL):
            for k in range(TOP_K):
                dst = dst_ref[0, 0, k * tc + t0 + u]
                pltpu.make_async_copy(tile(hn_ref, t0 + u), tile(x_hbm, dst), sem.at[0]).start(priority=k % 2)
        return c

    lax.fori_loop(0, tc // ISSUE_UNROLL, body, 0)
    pltpu.make_async_copy(tile(x_hbm, 0, rows), tile(x_hbm, 0, rows), sem.at[0]).wait()


def _dispatch(pend, dest3, hn_tm, n_rows, d):
    parts = d // LANES
    t = hn_tm.shape[0] // parts
    tc = COMB_TILE
    grid_spec = pltpu.PrefetchScalarGridSpec(
        num_scalar_prefetch=1,
        grid=(t // tc,),
        in_specs=[
            pl.BlockSpec((1, 1, TOP_K * tc), lambda i, pe: (i, 0, 0), memory_space=pltpu.SMEM),
            pl.BlockSpec((tc * parts, LANES), lambda i, pe: (i, 0)),
        ],
        out_specs=pl.BlockSpec(memory_space=pl.ANY),
        scratch_shapes=[pltpu.VMEM((MOE_BLOCK * parts, LANES), F32), pltpu.SemaphoreType.DMA((2,))],
    )
    return pl.pallas_call(
        functools.partial(_dispatch_kernel, parts=parts),
        grid_spec=grid_spec,
        out_shape=jax.ShapeDtypeStruct((n_rows * parts, LANES), F32),
        compiler_params=_cparams(("arbitrary",)),
        name="dispatch",
    )(pend, dest3, hn_tm)


def _moe_kernel(be_ref, nxt_ref, nvb_ref, x_ref, wgu_hbm, bg_ref, bu_ref, wd_hbm, bd_ref, perm_ref,
                y_ref, wgu_f, wd_f, wgu_s, wd_s, act_s, sem):
    i = pl.program_id(0)
    nvb = nvb_ref[0]
    f, d = wd_f.shape
    bm = act_s.shape[0]
    pw = perm_ref.shape[0]
    half = pw // 2

    def fetch(e):
        return (pltpu.make_async_copy(wgu_hbm.at[e], wgu_f, sem.at[0]),
                pltpu.make_async_copy(wd_hbm.at[e], wd_f, sem.at[1]))

    @pl.when(jnp.logical_and(i == 0, nvb > 0))
    def _():
        for c in fetch(be_ref[0]):
            c.start()

    changed = jnp.logical_or(i == 0, be_ref[i] != be_ref[jnp.maximum(i - 1, 0)])

    @pl.when(jnp.logical_and(changed, i < nvb))
    def _():
        for c in fetch(be_ref[i]):
            c.wait()
        for c in range(2 * f // pw):
            wc = wgu_f[:, c * pw:(c + 1) * pw].astype(BF16)
            wgu_s[:, c * pw:(c + 1) * pw] = _dot(wc, perm_ref[...]).astype(BF16)
        wd_s[...] = wd_f[...].astype(BF16)

        @pl.when(nxt_ref[i] >= 0)
        def _():
            for c in fetch(nxt_ref[i]):
                c.start()

    @pl.when(i < nvb)
    def _():
        e = be_ref[i]
        xb = jnp.concatenate([p.astype(BF16) for p in _load_token_major(x_ref, bm, d)], axis=1)
        for c in range(2 * f // pw):
            gu = _dot(xb, wgu_s[:, c * pw:(c + 1) * pw])
            g = gu[:, :half] + bg_ref[e, :, c * half:(c + 1) * half]
            u = gu[:, half:] + bu_ref[e, :, c * half:(c + 1) * half]
            g = jnp.minimum(g, SWIGLU_LIMIT)
            u = jnp.clip(u, -SWIGLU_LIMIT, SWIGLU_LIMIT)
            act = (u + 1.0) * (g * _sigmoid(SWIGLU_ALPHA * g))
            act_s[:, c * half:(c + 1) * half] = act.astype(BF16)
        _store_token_major(y_ref, _dot(act_s[...], wd_s[...]) + bd_ref[e])

    @pl.when(i >= nvb)
    def _():
        y_ref[...] = jnp.zeros(y_ref.shape, F32)


def _moe(block_e, next_e, nvb, xrows_tm, wgu, bg, bu, wd, bd, perm):
    f, d = wd.shape[1], wd.shape[2]
    parts = d // LANES
    bm = MOE_BLOCK
    nb = xrows_tm.shape[0] // (bm * parts)

    def whole(a):
        return pl.BlockSpec(a.shape, lambda i, be, nx, nv, nd=a.ndim: (0,) * nd)

    grid_spec = pltpu.PrefetchScalarGridSpec(
        num_scalar_prefetch=3,
        grid=(nb,),
        in_specs=[
            pl.BlockSpec((bm * parts, LANES),
                         lambda i, be, nx, nv: (jnp.minimum(i, jnp.maximum(nv[0] - 1, 0)), 0)),
            pl.BlockSpec(memory_space=pl.ANY),
            whole(bg), whole(bu),
            pl.BlockSpec(memory_space=pl.ANY),
            whole(bd), whole(perm),
        ],
        out_specs=pl.BlockSpec((bm * parts, LANES), lambda i, be, nx, nv: (i, 0)),
        scratch_shapes=[
            pltpu.VMEM((d, 2 * f), F32), pltpu.VMEM((f, d), F32),
            pltpu.VMEM((d, 2 * f), BF16), pltpu.VMEM((f, d), BF16),
            pltpu.VMEM((bm, f), BF16),
            pltpu.SemaphoreType.DMA((2,)),
        ],
    )
    return pl.pallas_call(
        _moe_kernel,
        grid_spec=grid_spec,
        out_shape=jax.ShapeDtypeStruct(xrows_tm.shape, F32),
        compiler_params=_cparams(("arbitrary",)),
        name="moe",
    )(block_e, next_e, nvb, xrows_tm, wgu, bg, bu, wd, bd, perm)


def _comb_kernel(dst_ref, dstn_ref, h_ref, gate_ref, y_hbm, g_ref, o_ref, ybuf, sem, *, parts):
    i = pl.program_id(0)
    n = pl.num_programs(0)
    tc, d = h_ref.shape
    rows = TOP_K * tc
    slot = i % 2

    def tile(ref, n, count=1):
        return ref.at[pl.ds(pl.multiple_of(n * parts, parts), count * parts), :]

    def start_rows(idx_ref, s):
        def body(j, c):
            t0 = pl.multiple_of(j * ISSUE_UNROLL, ISSUE_UNROLL)
            for u in range(ISSUE_UNROLL):
                for k in range(TOP_K):
                    r = k * tc + t0 + u
                    pltpu.make_async_copy(tile(y_hbm, idx_ref[0, 0, r]), tile(ybuf.at[s], r),
                                          sem.at[s]).start(priority=k % 2)
            return c

        lax.fori_loop(0, tc // ISSUE_UNROLL, body, 0)

    @pl.when(i == 0)
    def _():
        start_rows(dst_ref, 0)

    @pl.when(i + 1 < n)
    def _():
        start_rows(dstn_ref, 1 - slot)

    pltpu.make_async_copy(tile(y_hbm, 0, rows), ybuf.at[slot], sem.at[slot]).wait()
    acc = [h_ref[:, s * LANES:(s + 1) * LANES] for s in range(parts)]
    for k in range(TOP_K):
        yk = _load_token_major(ybuf.at[slot], tc, d, base=k * tc * parts)
        gk = gate_ref[:, k:k + 1]
        acc = [a + p * gk for a, p in zip(acc, yk)]
    o_ref[...] = _rms(jnp.concatenate(acc, axis=1), g_ref[...])


def _combine(dest3, h2, gate, y_tm, g):
    t, d = h2.shape
    parts = d // LANES
    tc = COMB_TILE
    nt = t // tc
    return pl.pallas_call(
        functools.partial(_comb_kernel, parts=parts),
        grid=(nt,),
        in_specs=[
            pl.BlockSpec((1, 1, TOP_K * tc), lambda i: (i, 0, 0), memory_space=pltpu.SMEM),
            pl.BlockSpec((1, 1, TOP_K * tc), lambda i: (jnp.minimum(i + 1, nt - 1), 0, 0),
                         memory_space=pltpu.SMEM),
            pl.BlockSpec((tc, d), lambda i: (i, 0)),
            pl.BlockSpec((tc, SUBLANES), lambda i: (i, 0)),
            pl.BlockSpec(memory_space=pl.ANY),
            pl.BlockSpec((1, d), lambda i: (0, 0)),
        ],
        out_specs=pl.BlockSpec((tc, d), lambda i: (i, 0)),
        out_shape=jax.ShapeDtypeStruct((t, d), F32),
        scratch_shapes=[pltpu.VMEM((2, TOP_K * tc * parts, LANES), F32), pltpu.SemaphoreType.DMA((2,))],
        compiler_params=_cparams(("arbitrary",)),
        name="combine",
    )(dest3, dest3, h2, gate, y_tm, g)


def _block_diag(wb, per):
    nb, bw, _ = wb.shape
    wq = wb.reshape(nb // per, per, bw, bw)
    eye = jnp.eye(per, dtype=wb.dtype)
    out = jnp.einsum('qaij,ab->qaibj', wq, eye)
    return out.reshape(nb // per, per * bw, per * bw)


def _pad_lanes(v, fill=0.0):
    return jnp.pad(v, (0, LANES - v.shape[0]), constant_values=fill).reshape(1, LANES)


def kernel(x, mem, norm_mix, w_in, ssd_conv_w, ssd_conv_b, ssd_dt_bias, ssd_a_log, ssd_d, ssd_norm, lru_conv_w, lru_conv_b, lru_wa, lru_ba, lru_wx, lru_bx, lru_lambda, w_out, norm_xattn, norm_mem, w_q, w_kv, w_o, norm_moe, w_router, b_router, w_gate_up, b_gate_up, w_down, b_down, norm_final):
    b, s, d = x.shape
    t = b * s
    n_mem = mem.shape[1]
    w = d
    cdim = w + 2 * SSD_GROUPS * SSD_STATE
    o1, o2, o3, o4 = w, w + cdim, w + cdim + SSD_HEADS, w + cdim + SSD_HEADS + w

    wi = w_in[0]
    wzx = wi[:, :o2].astype(BF16)
    wdt = jnp.pad(wi[:, o2:o3], ((0, 0), (0, LANES - SSD_HEADS)))
    wdt_hi, wdt_lo = _hilo(wdt)
    wxr = wi[:, o3:o4].astype(BF16)
    wgr = wi[:, o4:].astype(BF16)
    e01 = (jnp.arange(LANES)[:, None] == (jnp.arange(w)[None, :] // SSD_HEAD_DIM)).astype(BF16)
    dskip = jnp.repeat(ssd_d[0], SSD_HEAD_DIM).reshape(1, w)
    per = 256 // (w // LRU_BLOCKS)
    wa_bd = _block_diag(lru_wa[0], per).astype(BF16)
    wx_bd = _block_diag(lru_wx[0], per).astype(BF16)
    wr = jnp.pad(w_router[0], ((0, 0), (0, LANES - N_EXPERTS)))
    wr_hi, wr_lo = _hilo(wr)
    br = _pad_lanes(b_router[0], fill=-1e30)
    bgu = b_gate_up[0]
    bg = bgu[:, None, 0::2]
    bu = bgu[:, None, 1::2]
    bd = b_down[0][:, None, :]
    pw = 2 * LANES
    col = jnp.arange(pw)
    src_col = jnp.where(col < LANES, 2 * col, 2 * (col - LANES) + 1)
    perm = (jnp.arange(pw)[:, None] == src_col[None, :]).astype(BF16)

    x2 = x.reshape(t, d)
    z, xbc, dt, xr, gr = _in_proj(x2, norm_mix[0].reshape(1, d), wzx, wdt_hi, wdt_lo, wxr, wgr)

    y_ssd = _ssd(xbc.reshape(b, s, cdim), dt.reshape(b, s, LANES), z.reshape(b, s, w),
                 _shift_matrix(SSD_CHUNK), ssd_conv_w[0], ssd_conv_b[0].reshape(1, cdim), _pad_lanes(ssd_dt_bias[0]),
                 _pad_lanes(ssd_a_log[0]), dskip, ssd_norm[0].reshape(1, w), e01)
    y_lru = _lru(xr.reshape(b, s, w), gr.reshape(b, s, w), _shift_matrix(LRU_TILE), lru_conv_w[0], lru_conv_b[0].reshape(1, w),
                 wa_bd, lru_ba[0].reshape(1, w), wx_bd, lru_bx[0].reshape(1, w), lru_lambda[0].reshape(1, w))

    kk, vv = _kv(mem.reshape(b * n_mem, d), norm_mem[0].reshape(1, d), w_kv[0].astype(BF16))
    wo_mix = w_out[0].astype(BF16)
    h2, hn, idx_m, rank_m, gate_m, cnt = _mid(
        x2, y_ssd.reshape(t, w), y_lru.reshape(t, w), kk.reshape(b, n_mem, d), vv.reshape(b, n_mem, d),
        wo_mix[:w], wo_mix[w:], norm_xattn[0].reshape(1, d), w_q[0].astype(BF16), w_o[0].astype(BF16),
        norm_moe[0].reshape(1, d), wr_hi, wr_lo, br, s)

    bm = MOE_BLOCK
    counts = cnt[:, 0]
    pcounts = (counts + bm - 1) // bm * bm
    pend = jnp.cumsum(pcounts).astype(jnp.int32)
    pstart = pend - pcounts
    idx = idx_m[:TOP_K, :]
    onehot = idx[:, :, None] == jnp.arange(N_EXPERTS, dtype=jnp.int32)[None, None, :]
    dest = jnp.sum(jnp.where(onehot, pstart[None, None, :], 0), axis=-1) + rank_m[:TOP_K, :]
    n_pairs = t * TOP_K
    nb = (n_pairs + N_EXPERTS * (bm - 1) + bm - 1) // bm
    blk0 = jnp.arange(nb, dtype=jnp.int32) * bm
    block_e = jnp.minimum(jnp.sum(pend[None, :] <= blk0[:, None], axis=1), N_EXPERTS - 1).astype(jnp.int32)
    nvb = (pend[-1] // bm).astype(jnp.int32).reshape(1)
    after = pend[block_e] // bm
    next_e = jnp.where(after < nvb[0], block_e[jnp.minimum(after, nb - 1)], -1).astype(jnp.int32)
    tc = COMB_TILE
    dest3 = dest.astype(jnp.int32).reshape(TOP_K, t // tc, tc).transpose(1, 0, 2).reshape(t // tc, 1, TOP_K * tc)

    xrows = _dispatch(pend, dest3, hn, nb * bm, d)
    y = _moe(block_e, next_e, nvb, xrows, w_gate_up[0], bg, bu, w_down[0], bd, perm)
    out = _combine(dest3, h2, gate_m.T, y, norm_final.reshape(1, d))
    return out.reshape(b, s, d)
```

```python
import functools

import jax
import jax.numpy as jnp
from jax import lax
from jax.experimental import pallas as pl
from jax.experimental.pallas import tpu as pltpu

F32 = jnp.float32
BF16 = jnp.bfloat16

NORM_EPS = 1e-6
LANES = 128
SUBLANES = 8
MXU_DIM = 256
SSD_HEAD_DIM = 64
SSD_HEADS = 16
SSD_GROUPS = 4
SSD_STATE = 128
CONV_K = 4
LRU_BLOCKS = 16
RG_C = 8.0
X_HEADS = 4
N_EXPERTS = 32
TOP_K = 4
SWIGLU_LIMIT = 7.0
SWIGLU_ALPHA = 1.702

VMEM_LIMIT = 56 * 1024 * 1024

IN_TILE = 1024
SSD_CHUNK = 256
LRU_TILE = 256
MID_TILE = 512
MOE_BLOCK = 1024
COMB_TILE = 256
ISSUE_UNROLL = 8


def _cparams(sem):
    return pltpu.CompilerParams(dimension_semantics=sem, vmem_limit_bytes=VMEM_LIMIT)


def _rms(x, g):
    ms = jnp.mean(x * x, axis=-1, keepdims=True)
    return x * lax.rsqrt(ms + NORM_EPS) * g


def _sigmoid(x):
    return 0.5 * jnp.tanh(0.5 * x) + 0.5


def _softplus(x):
    return jnp.maximum(x, 0.0) + jnp.log(1.0 + jnp.exp(-jnp.abs(x)))


def _split3(x):
    a = x.astype(BF16)
    r = x - a.astype(F32)
    b = r.astype(BF16)
    c = (r - b.astype(F32)).astype(BF16)
    return a, b, c


def _dot(a, b):
    return jnp.dot(a, b, preferred_element_type=F32)


def _dot_nt(a, b):
    return lax.dot_general(a, b, (((1,), (1,)), ((), ())), preferred_element_type=F32)


def _dot01_right(x, m01):
    a, b, c = _split3(x)
    return _dot(a, m01) + _dot(b, m01) + _dot(c, m01)


def _dot01_left(m01, x):
    a, b, c = _split3(x)
    return _dot(m01, a) + _dot(m01, b) + _dot(m01, c)


def _dot_hilo(x, w_hi, w_lo):
    xh = x.astype(BF16)
    xl = (x - xh.astype(F32)).astype(BF16)
    return _dot(xh, w_hi) + _dot(xl, w_hi) + _dot(xh, w_lo)


def _hilo(w):
    hi = w.astype(BF16)
    lo = (w - hi.astype(F32)).astype(BF16)
    return hi, lo


def _store_token_major(ref, val, base=0):
    n, d = val.shape
    parts = d // LANES
    for s in range(parts):
        ref[pl.ds(base + s, n, stride=parts), :] = val[:, s * LANES:(s + 1) * LANES]


def _load_token_major(ref, n, d, base=0):
    parts = d // LANES
    return [ref[pl.ds(base + s, n, stride=parts), :] for s in range(parts)]


def _in_proj_kernel(x_ref, g_ref, wzx_ref, wdth_ref, wdtl_ref, wxr_ref, wgr_ref,
                    z_ref, xbc_ref, dt_ref, xr_ref, gr_ref):
    d = x_ref.shape[1]
    hn = _rms(x_ref[...], g_ref[...])
    hb = hn.astype(BF16)
    z_ref[...] = _dot(hb, wzx_ref[:, :d]).astype(BF16)
    xbc_ref[...] = _dot(hb, wzx_ref[:, d:]).astype(BF16)
    dt_ref[...] = _dot_hilo(hn, wdth_ref[...], wdtl_ref[...])
    xr_ref[...] = _dot(hb, wxr_ref[...]).astype(BF16)
    gr_ref[...] = _dot(hb, wgr_ref[...]).astype(BF16)


def _in_proj(x2, g, wzx, wdt_hi, wdt_lo, wxr, wgr):
    t, d = x2.shape
    tm = IN_TILE
    nzx = wzx.shape[1]
    const = lambda i: (0, 0)
    row = lambda i: (i, 0)
    return pl.pallas_call(
        _in_proj_kernel,
        grid=(t // tm,),
        in_specs=[
            pl.BlockSpec((tm, d), row),
            pl.BlockSpec((1, d), const),
            pl.BlockSpec((d, nzx), const),
            pl.BlockSpec((d, LANES), const),
            pl.BlockSpec((d, LANES), const),
            pl.BlockSpec((d, d), const),
            pl.BlockSpec((d, d), const),
        ],
        out_specs=[
            pl.BlockSpec((tm, d), row),
            pl.BlockSpec((tm, nzx - d), row),
            pl.BlockSpec((tm, LANES), row),
            pl.BlockSpec((tm, d), row),
            pl.BlockSpec((tm, d), row),
        ],
        out_shape=[
            jax.ShapeDtypeStruct((t, d), BF16),
            jax.ShapeDtypeStruct((t, nzx - d), BF16),
            jax.ShapeDtypeStruct((t, LANES), F32),
            jax.ShapeDtypeStruct((t, d), BF16),
            jax.ShapeDtypeStruct((t, d), BF16),
        ],
        compiler_params=_cparams(("arbitrary",)),
        name="in_proj",
    )(x2, g, wzx, wdt_hi, wdt_lo, wxr, wgr)


def _shift_matrix(n):
    return jnp.concatenate([jnp.eye(n, k=-(CONV_K - 1 - j), dtype=BF16) for j in range(CONV_K - 1)], axis=0)


def _causal_conv(halo_ref, x, shift_ref, w_ref, b_ref, first):
    n = x.shape[0]
    pad = SUBLANES
    k1 = CONV_K - 1

    @pl.when(first)
    def _():
        halo_ref[...] = jnp.zeros(halo_ref.shape, F32)

    xf = x.astype(F32)
    taps = _dot(shift_ref[...], x)
    acc = b_ref[...] + w_ref[k1:k1 + 1, :] * xf
    for j in range(k1):
        acc = acc + w_ref[j:j + 1, :] * taps[j * n:(j + 1) * n]
    head = acc[:pad]
    for j in range(k1):
        head = head + w_ref[j:j + 1, :] * halo_ref[pad - k1 + j:2 * pad - k1 + j, :]
    halo_ref[0:pad, :] = xf[n - pad:n]
    return jnp.concatenate([head, acc[pad:]], axis=0)


def _ssd_kernel(xbc_ref, dt_ref, z_ref, sh_ref, cw_ref, cb_ref, dtb_ref, alog_ref, dskip_ref, gn_ref, e_ref,
                y_ref, halo_ref, st_ref):
    n = xbc_ref.shape[1]
    w = z_ref.shape[2]
    gw = w // SSD_GROUPS
    first = pl.program_id(1) == 0

    @pl.when(first)
    def _():
        st_ref[...] = jnp.zeros(st_ref.shape, F32)

    conv = _causal_conv(halo_ref, xbc_ref[0], sh_ref, cw_ref, cb_ref, first)
    xc = conv * _sigmoid(conv)
    xs = xc[:, :w]

    dt = _softplus(dt_ref[0] + dtb_ref[...])
    a = -jnp.exp(alog_ref[...])
    da = dt * a
    ri = lax.broadcasted_iota(jnp.int32, (n, n), 0)
    ci = lax.broadcasted_iota(jnp.int32, (n, n), 1)
    causal = ri >= ci
    tril = jnp.where(causal, 1.0, 0.0).astype(BF16)
    a_cs = _dot01_left(tril, da)
    a_cs_t = a_cs.T

    e01 = e_ref[...]
    dt_x = _dot01_right(dt, e01)
    acs_x = _dot01_right(a_cs, e01)
    last_x = acs_x[n - 1:n, :]
    xdt = xs * dt_x
    xdt_b = xdt.astype(BF16)
    xdt_end = (xdt * jnp.exp(last_x - acs_x)).astype(BF16)
    exp_acs = jnp.exp(acs_x)
    chunk_decay = jnp.exp(last_x)
    lane = lax.broadcasted_iota(jnp.int32, (n, gw), 1)

    for g in range(SSD_GROUPS):
        lo = g * gw
        bg = xc[:, w + g * SSD_STATE:w + (g + 1) * SSD_STATE].astype(BF16)
        cg = xc[:, w + (SSD_GROUPS + g) * SSD_STATE:w + (SSD_GROUPS + g + 1) * SSD_STATE].astype(BF16)
        cb = _dot_nt(cg, bg)
        prev = st_ref[g]
        acc = _dot(cg, prev.astype(BF16)) * exp_acs[:, lo:lo + gw]
        new = lax.dot_general(bg, xdt_end[:, lo:lo + gw], (((0,), (0,)), ((), ())),
                              preferred_element_type=F32)
        st_ref[g] = chunk_decay[:, lo:lo + gw] * prev + new
        xg = xdt_b[:, lo:lo + gw]
        for k in range(SSD_HEADS // SSD_GROUPS):
            h = g * (SSD_HEADS // SSD_GROUPS) + k
            seg = a_cs[:, h:h + 1] - a_cs_t[h:h + 1, :]
            dec = jnp.exp(jnp.where(causal, seg, -jnp.inf))
            m = (cb * dec).astype(BF16)
            in_head = (lane >= k * SSD_HEAD_DIM) & (lane < (k + 1) * SSD_HEAD_DIM)
            acc = acc + _dot(m, jnp.where(in_head, xg, jnp.zeros_like(xg)))
        yg = acc + xs[:, lo:lo + gw] * dskip_ref[:, lo:lo + gw]
        zg = z_ref[0, :, lo:lo + gw].astype(F32)
        u = yg * (zg * _sigmoid(zg))
        u = u * lax.rsqrt(jnp.mean(u * u, axis=-1, keepdims=True) + NORM_EPS)
        y_ref[0, :, lo:lo + gw] = (u * gn_ref[:, lo:lo + gw]).astype(BF16)


def _ssd(xbc, dt, z, shift, cw, cb, dtb, alog, dskip, gn, e01):
    b, s, cdim = xbc.shape
    w = z.shape[2]
    n = SSD_CHUNK
    tile = lambda i, j: (i, j, 0)
    const = lambda i, j: (0, 0)
    return pl.pallas_call(
        _ssd_kernel,
        grid=(b, s // n),
        in_specs=[
            pl.BlockSpec((1, n, cdim), tile),
            pl.BlockSpec((1, n, LANES), tile),
            pl.BlockSpec((1, n, w), tile),
            pl.BlockSpec(shift.shape, const),
            pl.BlockSpec((CONV_K, cdim), const),
            pl.BlockSpec((1, cdim), const),
            pl.BlockSpec((1, LANES), const),
            pl.BlockSpec((1, LANES), const),
            pl.BlockSpec((1, w), const),
            pl.BlockSpec((1, w), const),
            pl.BlockSpec((LANES, w), const),
        ],
        out_specs=pl.BlockSpec((1, n, w), tile),
        out_shape=jax.ShapeDtypeStruct((b, s, w), BF16),
        scratch_shapes=[
            pltpu.VMEM((2 * SUBLANES, cdim), F32),
            pltpu.VMEM((SSD_GROUPS, SSD_STATE, w // SSD_GROUPS), F32),
        ],
        compiler_params=_cparams(("arbitrary", "arbitrary")),
        name="ssd",
    )(xbc, dt, z, shift, cw, cb, dtb, alog, dskip, gn, e01)


def _gelu_tanh(x):
    c = 0.7978845608028654
    return 0.5 * x * (1.0 + jnp.tanh(c * (x + 0.044715 * (x * x * x))))


def _lru_kernel(xr_ref, gr_ref, sh_ref, cw_ref, cb_ref, wa_ref, ba_ref, wx_ref, bx_ref, lam_ref,
                y_ref, halo_ref, car_ref, h_ref):
    n = xr_ref.shape[1]
    w = xr_ref.shape[2]
    first = pl.program_id(1) == 0

    @pl.when(first)
    def _():
        car_ref[...] = jnp.zeros(car_ref.shape, F32)

    xc = _causal_conv(halo_ref, xr_ref[0], sh_ref, cw_ref, cb_ref, first)
    xb = xc.astype(BF16)
    nq = wa_ref.shape[0]
    qw = w // nq
    r_parts, i_parts = [], []
    for q in range(nq):
        xq = xb[:, q * qw:(q + 1) * qw]
        r_parts.append(_dot(xq, wa_ref[q]))
        i_parts.append(_dot(xq, wx_ref[q]))
    r = _sigmoid(jnp.concatenate(r_parts, axis=1) + ba_ref[...])
    gi = _sigmoid(jnp.concatenate(i_parts, axis=1) + bx_ref[...])
    log_a = (-RG_C) * r * _softplus(-lam_ref[...])
    a = jnp.exp(log_a)
    u = xc * gi * jnp.sqrt(1.0 - a * a)

    groups = n // SUBLANES
    sub = lax.broadcasted_iota(jnp.int32, (groups, SUBLANES, w), 1)
    ap = a.reshape(groups, SUBLANES, w)
    bp = u.reshape(groups, SUBLANES, w)
    for d in (1, 2, 4):
        m = sub >= d
        bp = jnp.where(m, ap * pltpu.roll(bp, d, 1) + bp, bp)
        ap = jnp.where(m, ap * pltpu.roll(ap, d, 1), ap)
    carry = car_ref[...]
    for g in range(groups):
        hb = bp[g] + ap[g] * carry
        h_ref[g * SUBLANES:(g + 1) * SUBLANES, :] = hb
        carry = jnp.broadcast_to(hb[SUBLANES - 1:SUBLANES, :], (SUBLANES, w))
    car_ref[...] = carry
    y_ref[0] = (h_ref[...] * _gelu_tanh(gr_ref[0].astype(F32))).astype(BF16)


def _lru(xr, gr, shift, cw, cb, wa, ba, wx, bx, lam):
    b, s, w = xr.shape
    n = LRU_TILE
    nq, qw, _ = wa.shape
    tile = lambda i, j: (i, j, 0)
    const = lambda i, j: (0, 0)
    const3 = lambda i, j: (0, 0, 0)
    return pl.pallas_call(
        _lru_kernel,
        grid=(b, s // n),
        in_specs=[
            pl.BlockSpec((1, n, w), tile),
            pl.BlockSpec((1, n, w), tile),
            pl.BlockSpec(shift.shape, const),
            pl.BlockSpec((CONV_K, w), const),
            pl.BlockSpec((1, w), const),
            pl.BlockSpec((nq, qw, qw), const3),
            pl.BlockSpec((1, w), const),
            pl.BlockSpec((nq, qw, qw), const3),
            pl.BlockSpec((1, w), const),
            pl.BlockSpec((1, w), const),
        ],
        out_specs=pl.BlockSpec((1, n, w), tile),
        out_shape=jax.ShapeDtypeStruct((b, s, w), BF16),
        scratch_shapes=[
            pltpu.VMEM((2 * SUBLANES, w), F32),
            pltpu.VMEM((SUBLANES, w), F32),
            pltpu.VMEM((n, w), F32),
        ],
        compiler_params=_cparams(("arbitrary", "arbitrary")),
        name="lru",
    )(xr, gr, shift, cw, cb, wa, ba, wx, bx, lam)


def _kv_kernel(m_ref, g_ref, w_ref, k_ref, v_ref):
    d = m_ref.shape[1]
    mn = _rms(m_ref[...], g_ref[...]).astype(BF16)
    k_ref[...] = _dot(mn, w_ref[:, :d]).astype(BF16)
    v_ref[...] = _dot(mn, w_ref[:, d:]).astype(BF16)


def _kv(mem2, g, wkv):
    t, d = mem2.shape
    tm = min(t, 512)
    row = lambda i: (i, 0)
    const = lambda i: (0, 0)
    return pl.pallas_call(
        _kv_kernel,
        grid=(t // tm,),
        in_specs=[pl.BlockSpec((tm, d), row), pl.BlockSpec((1, d), const), pl.BlockSpec((d, 2 * d), const)],
        out_specs=[pl.BlockSpec((tm, d), row), pl.BlockSpec((tm, d), row)],
        out_shape=[jax.ShapeDtypeStruct((t, d), BF16), jax.ShapeDtypeStruct((t, d), BF16)],
        compiler_params=_cparams(("arbitrary",)),
        name="kv",
    )(mem2, g, wkv)


def _mid_kernel(x_ref, ys_ref, yl_ref, k_ref, v_ref, wo1_ref, wo2_ref, gx_ref, wq_ref, wo_ref,
                gm_ref, wrh_ref, wrl_ref, br_ref,
                h_ref, hn_ref, idx_ref, rank_ref, gate_ref, cnt_ref, car_ref):
    tm, d = x_ref.shape
    hd = d // X_HEADS

    @pl.when(pl.program_id(0) == 0)
    def _():
        car_ref[...] = jnp.zeros(car_ref.shape, F32)

    h1 = x_ref[...] + _dot(ys_ref[...], wo1_ref[...]) + _dot(yl_ref[...], wo2_ref[...])

    q = _dot(_rms(h1, gx_ref[...]).astype(BF16), wq_ref[...]).astype(BF16)
    o_parts = []
    for hh in range(X_HEADS):
        sl = slice(hh * hd, (hh + 1) * hd)
        sc = _dot_nt(q[:, sl], k_ref[0, :, sl]) * (hd ** -0.5)
        sc = sc - jnp.max(sc, axis=-1, keepdims=True)
        p = jnp.exp(sc)
        p = p / jnp.sum(p, axis=-1, keepdims=True)
        o_parts.append(_dot(p.astype(BF16), v_ref[0, :, sl]))
    o = jnp.concatenate(o_parts, axis=1).astype(BF16)
    h2 = h1 + _dot(o, wo_ref[...])
    h_ref[...] = h2

    hn = _rms(h2, gm_ref[...])
    _store_token_major(hn_ref, hn)
    logits = _dot_hilo(hn, wrh_ref[...], wrl_ref[...]) + br_ref[...]

    l = logits.T[:N_EXPERTS, :]
    row = lax.broadcasted_iota(jnp.int32, (N_EXPERTS, tm), 0)
    picked = jnp.zeros((N_EXPERTS, tm), F32)
    vals, idxs = [], []
    for _ in range(TOP_K):
        m = jnp.max(l, axis=0, keepdims=True)
        idx = jnp.min(jnp.where(l == m, row, N_EXPERTS), axis=0, keepdims=True)
        sel = row == idx
        vals.append(m)
        idxs.append(idx)
        picked = jnp.where(sel, 1.0, picked)
        l = jnp.where(sel, -jnp.inf, l)
    ex = [jnp.exp(v - vals[0]) for v in vals]
    den = ex[0] + ex[1] + ex[2] + ex[3]

    ri = lax.broadcasted_iota(jnp.int32, (tm, tm), 0)
    ci = lax.broadcasted_iota(jnp.int32, (tm, tm), 1)
    earlier = jnp.where(ri < ci, 1.0, 0.0).astype(BF16)
    before = _dot(picked.astype(BF16), earlier) + car_ref[:, 0:1]
    out_row = lax.broadcasted_iota(jnp.int32, (SUBLANES, tm), 0)
    idx_out = jnp.zeros((SUBLANES, tm), jnp.int32)
    rank_out = jnp.zeros((SUBLANES, tm), jnp.int32)
    gate_out = jnp.zeros((SUBLANES, tm), F32)
    for k in range(TOP_K):
        rk = jnp.sum(jnp.where(row == idxs[k], before, 0.0), axis=0, keepdims=True)
        at_k = out_row == k
        idx_out = jnp.where(at_k, idxs[k], idx_out)
        rank_out = jnp.where(at_k, rk.astype(jnp.int32), rank_out)
        gate_out = jnp.where(at_k, ex[k] / den, gate_out)
    idx_ref[...] = idx_out
    rank_ref[...] = rank_out
    gate_ref[...] = gate_out
    total = car_ref[...] + jnp.sum(picked, axis=1, keepdims=True)
    car_ref[...] = total
    cnt_ref[...] = total.astype(jnp.int32)


def _mid(x2, ys, yl, kk, vv, wo1, wo2, gx, wq, wo, gm, wr_hi, wr_lo, br, seq):
    t, d = x2.shape
    tm = MID_TILE
    m = kk.shape[1]
    per_b = seq // tm
    row = lambda i: (i, 0)
    col = lambda i: (0, i)
    const = lambda i: (0, 0)
    kvmap = lambda i: (i // per_b, 0, 0)
    wspec = pl.BlockSpec((d, d), const)
    vspec = pl.BlockSpec((1, d), const)
    return pl.pallas_call(
        _mid_kernel,
        grid=(t // tm,),
        in_specs=[
            pl.BlockSpec((tm, d), row), pl.BlockSpec((tm, d), row), pl.BlockSpec((tm, d), row),
            pl.BlockSpec((1, m, d), kvmap), pl.BlockSpec((1, m, d), kvmap),
            wspec, wspec, vspec, wspec, wspec, vspec,
            pl.BlockSpec((d, LANES), const), pl.BlockSpec((d, LANES), const), pl.BlockSpec((1, LANES), const),
        ],
        out_specs=[
            pl.BlockSpec((tm, d), row), pl.BlockSpec((tm * d // LANES, LANES), row),
            pl.BlockSpec((SUBLANES, tm), col), pl.BlockSpec((SUBLANES, tm), col), pl.BlockSpec((SUBLANES, tm), col),
            pl.BlockSpec((N_EXPERTS, LANES), const),
        ],
        out_shape=[
            jax.ShapeDtypeStruct((t, d), F32), jax.ShapeDtypeStruct((t * d // LANES, LANES), F32),
            jax.ShapeDtypeStruct((SUBLANES, t), jnp.int32), jax.ShapeDtypeStruct((SUBLANES, t), jnp.int32),
            jax.ShapeDtypeStruct((SUBLANES, t), F32),
            jax.ShapeDtypeStruct((N_EXPERTS, LANES), jnp.int32),
        ],
        scratch_shapes=[pltpu.VMEM((N_EXPERTS, LANES), F32)],
        compiler_params=_cparams(("arbitrary",)),
        name="mid",
    )(x2, ys, yl, kk, vv, wo1, wo2, gx, wq, wo, gm, wr_hi, wr_lo, br)


def _dispatch_kernel(pend_ref, dst_ref, hn_ref, x_hbm, zero_ref, sem, *, parts):
    i = pl.program_id(0)
    tc = hn_ref.shape[0] // parts
    bm = zero_ref.shape[0] // parts
    rows = TOP_K * tc

    def tile(ref, n, count=1):
        return ref.at[pl.ds(pl.multiple_of(n * parts, parts), count * parts), :]

    @pl.when(i == 0)
    def _():
        zero_ref[...] = jnp.zeros(zero_ref.shape, F32)

        def fill(start):
            return pltpu.make_async_copy(zero_ref, tile(x_hbm, start, bm), sem.at[1])

        for e in range(N_EXPERTS):
            fill(jnp.maximum(pend_ref[e] - bm, 0)).start()
        for e in range(N_EXPERTS):
            fill(0).wait()

        def tail(j, c):
            fill(j * bm).start()
            fill(0).wait()
            return c

        lax.fori_loop(pend_ref[N_EXPERTS - 1] // bm, x_hbm.shape[0] // (bm * parts), tail, 0)

    def body(j, c):
        t0 = pl.multiple_of(j * ISSUE_UNROLL, ISSUE_UNROLL)
        for u in range(ISSUE_UNROLL):
            for k in range(TOP_K):
                dst = dst_ref[0, 0, k * tc + t0 + u]
                pltpu.make_async_copy(tile(hn_ref, t0 + u), tile(x_hbm, dst), sem.at[0]).start(priority=k % 2)
        return c

    lax.fori_loop(0, tc // ISSUE_UNROLL, body, 0)
    pltpu.make_async_copy(tile(x_hbm, 0, rows), tile(x_hbm, 0, rows), sem.at[0]).wait()


def _dispatch(pend, dest3, hn_tm, n_rows, d):
    parts = d // LANES
    t = hn_tm.shape[0] // parts
    tc = COMB_TILE
    grid_spec = pltpu.PrefetchScalarGridSpec(
        num_scalar_prefetch=1,
        grid=(t // tc,),
        in_specs=[
            pl.BlockSpec((1, 1, TOP_K * tc), lambda i, pe: (i, 0, 0), memory_space=pltpu.SMEM),
            pl.BlockSpec((tc * parts, LANES), lambda i, pe: (i, 0)),
        ],
        out_specs=pl.BlockSpec(memory_space=pl.ANY),
        scratch_shapes=[pltpu.VMEM((MOE_BLOCK * parts, LANES), F32), pltpu.SemaphoreType.DMA((2,))],
    )
    return pl.pallas_call(
        functools.partial(_dispatch_kernel, parts=parts),
        grid_spec=grid_spec,
        out_shape=jax.ShapeDtypeStruct((n_rows * parts, LANES), F32),
        compiler_params=_cparams(("arbitrary",)),
        name="dispatch",
    )(pend, dest3, hn_tm)


def _moe_kernel(be_ref, nxt_ref, nvb_ref, x_ref, wgu_hbm, bg_ref, bu_ref, wd_hbm, bd_ref, perm_ref,
                y_ref, wgu_f, wd_f, wgu_s, wd_s, act_s, sem):
    i = pl.program_id(0)
    nvb = nvb_ref[0]
    f, d = wd_f.shape
    bm = act_s.shape[0]
    pw = perm_ref.shape[0]
    half = pw // 2

    def fetch(e):
        return (pltpu.make_async_copy(wgu_hbm.at[e], wgu_f, sem.at[0]),
                pltpu.make_async_copy(wd_hbm.at[e], wd_f, sem.at[1]))

    @pl.when(jnp.logical_and(i == 0, nvb > 0))
    def _():
        for c in fetch(be_ref[0]):
            c.start()

    changed = jnp.logical_or(i == 0, be_ref[i] != be_ref[jnp.maximum(i - 1, 0)])

    @pl.when(jnp.logical_and(changed, i < nvb))
    def _():
        for c in fetch(be_ref[i]):
            c.wait()
        for c in range(2 * f // pw):
            wc = wgu_f[:, c * pw:(c + 1) * pw].astype(BF16)
            wgu_s[:, c * pw:(c + 1) * pw] = _dot(wc, perm_ref[...]).astype(BF16)
        wd_s[...] = wd_f[...].astype(BF16)

        @pl.when(nxt_ref[i] >= 0)
        def _():
            for c in fetch(nxt_ref[i]):
                c.start()

    @pl.when(i < nvb)
    def _():
        e = be_ref[i]
        xb = jnp.concatenate([p.astype(BF16) for p in _load_token_major(x_ref, bm, d)], axis=1)
        for c in range(2 * f // pw):
            gu = _dot(xb, wgu_s[:, c * pw:(c + 1) * pw])
            g = gu[:, :half] + bg_ref[e, :, c * half:(c + 1) * half]
            u = gu[:, half:] + bu_ref[e, :, c * half:(c + 1) * half]
            g = jnp.minimum(g, SWIGLU_LIMIT)
            u = jnp.clip(u, -SWIGLU_LIMIT, SWIGLU_LIMIT)
            act = (u + 1.0) * (g * _sigmoid(SWIGLU_ALPHA * g))
            act_s[:, c * half:(c + 1) * half] = act.astype(BF16)
        _store_token_major(y_ref, _dot(act_s[...], wd_s[...]) + bd_ref[e])

    @pl.when(i >= nvb)
    def _():
        y_ref[...] = jnp.zeros(y_ref.shape, F32)


def _moe(block_e, next_e, nvb, xrows_tm, wgu, bg, bu, wd, bd, perm):
    f, d = wd.shape[1], wd.shape[2]
    parts = d // LANES
    bm = MOE_BLOCK
    nb = xrows_tm.shape[0] // (bm * parts)

    def whole(a):
        return pl.BlockSpec(a.shape, lambda i, be, nx, nv, nd=a.ndim: (0,) * nd)

    grid_spec = pltpu.PrefetchScalarGridSpec(
        num_scalar_prefetch=3,
        grid=(nb,),
        in_specs=[
            pl.BlockSpec((bm * parts, LANES),
                         lambda i, be, nx, nv: (jnp.minimum(i, jnp.maximum(nv[0] - 1, 0)), 0)),
            pl.BlockSpec(memory_space=pl.ANY),
            whole(bg), whole(bu),
            pl.BlockSpec(memory_space=pl.ANY),
            whole(bd), whole(perm),
        ],
        out_specs=pl.BlockSpec((bm * parts, LANES), lambda i, be, nx, nv: (i, 0)),
        scratch_shapes=[
            pltpu.VMEM((d, 2 * f), F32), pltpu.VMEM((f, d), F32),
            pltpu.VMEM((d, 2 * f), BF16), pltpu.VMEM((f, d), BF16),
            pltpu.VMEM((bm, f), BF16),
            pltpu.SemaphoreType.DMA((2,)),
        ],
    )
    return pl.pallas_call(
        _moe_kernel,
        grid_spec=grid_spec,
        out_shape=jax.ShapeDtypeStruct(xrows_tm.shape, F32),
        compiler_params=_cparams(("arbitrary",)),
        name="moe",
    )(block_e, next_e, nvb, xrows_tm, wgu, bg, bu, wd, bd, perm)


def _comb_kernel(dst_ref, dstn_ref, h_ref, gate_ref, y_hbm, g_ref, o_ref, ybuf, sem, *, parts):
    i = pl.program_id(0)
    n = pl.num_programs(0)
    tc, d = h_ref.shape
    rows = TOP_K * tc
    slot = i % 2

    def tile(ref, n, count=1):
        return ref.at[pl.ds(pl.multiple_of(n * parts, parts), count * parts), :]

    def start_rows(idx_ref, s):
        def body(j, c):
            t0 = pl.multiple_of(j * ISSUE_UNROLL, ISSUE_UNROLL)
            for u in range(ISSUE_UNROLL):
                for k in range(TOP_K):
                    r = k * tc + t0 + u
                    pltpu.make_async_copy(tile(y_hbm, idx_ref[0, 0, r]), tile(ybuf.at[s], r),
                                          sem.at[s]).start(priority=k % 2)
            return c

        lax.fori_loop(0, tc // ISSUE_UNROLL, body, 0)

    @pl.when(i == 0)
    def _():
        start_rows(dst_ref, 0)

    @pl.when(i + 1 < n)
    def _():
        start_rows(dstn_ref, 1 - slot)

    pltpu.make_async_copy(tile(y_hbm, 0, rows), ybuf.at[slot], sem.at[slot]).wait()
    acc = [h_ref[:, s * LANES:(s + 1) * LANES] for s in range(parts)]
    for k in range(TOP_K):
        yk = _load_token_major(ybuf.at[slot], tc, d, base=k * tc * parts)
        gk = gate_ref[:, k:k + 1]
        acc = [a + p * gk for a, p in zip(acc, yk)]
    o_ref[...] = _rms(jnp.concatenate(acc, axis=1), g_ref[...])


def _combine(dest3, h2, gate, y_tm, g):
    t, d = h2.shape
    parts = d // LANES
    tc = COMB_TILE
    nt = t // tc
    return pl.pallas_call(
        functools.partial(_comb_kernel, parts=parts),
        grid=(nt,),
        in_specs=[
            pl.BlockSpec((1, 1, TOP_K * tc), lambda i: (i, 0, 0), memory_space=pltpu.SMEM),
            pl.BlockSpec((1, 1, TOP_K * tc), lambda i: (jnp.minimum(i + 1, nt - 1), 0, 0),
                         memory_space=pltpu.SMEM),
            pl.BlockSpec((tc, d), lambda i: (i, 0)),
            pl.BlockSpec((tc, SUBLANES), lambda i: (i, 0)),
            pl.BlockSpec(memory_space=pl.ANY),
            pl.BlockSpec((1, d), lambda i: (0, 0)),
        ],
        out_specs=pl.BlockSpec((tc, d), lambda i: (i, 0)),
        out_shape=jax.ShapeDtypeStruct((t, d), F32),
        scratch_shapes=[pltpu.VMEM((2, TOP_K * tc * parts, LANES), F32), pltpu.SemaphoreType.DMA((2,))],
        compiler_params=_cparams(("arbitrary",)),
        name="combine",
    )(dest3, dest3, h2, gate, y_tm, g)


def _block_diag(wb, per):
    nb, bw, _ = wb.shape
    wq = wb.reshape(nb // per, per, bw, bw)
    eye = jnp.eye(per, dtype=wb.dtype)
    out = jnp.einsum('qaij,ab->qaibj', wq, eye)
    return out.reshape(nb // per, per * bw, per * bw)


def _pad_lanes(v, fill=0.0):
    return jnp.pad(v, (0, LANES - v.shape[0]), constant_values=fill).reshape(1, LANES)


def kernel(x, mem, norm_mix, w_in, ssd_conv_w, ssd_conv_b, ssd_dt_bias, ssd_a_log, ssd_d, ssd_norm, lru_conv_w, lru_conv_b, lru_wa, lru_ba, lru_wx, lru_bx, lru_lambda, w_out, norm_xattn, norm_mem, w_q, w_kv, w_o, norm_moe, w_router, b_router, w_gate_up, b_gate_up, w_down, b_down, norm_final):
    b, s, d = x.shape
    t = b * s
    n_mem = mem.shape[1]
    w = d
    cdim = w + 2 * SSD_GROUPS * SSD_STATE
    o1, o2, o3, o4 = w, w + cdim, w + cdim + SSD_HEADS, w + cdim + SSD_HEADS + w

    wi = w_in[0]
    wzx = wi[:, :o2].astype(BF16)
    wdt = jnp.pad(wi[:, o2:o3], ((0, 0), (0, LANES - SSD_HEADS)))
    wdt_hi, wdt_lo = _hilo(wdt)
    wxr = wi[:, o3:o4].astype(BF16)
    wgr = wi[:, o4:].astype(BF16)
    e01 = (jnp.arange(LANES)[:, None] == (jnp.arange(w)[None, :] // SSD_HEAD_DIM)).astype(BF16)
    dskip = jnp.repeat(ssd_d[0], SSD_HEAD_DIM).reshape(1, w)
    per = MXU_DIM // (w // LRU_BLOCKS)
    wa_bd = _block_diag(lru_wa[0], per).astype(BF16)
    wx_bd = _block_diag(lru_wx[0], per).astype(BF16)
    wr = jnp.pad(w_router[0], ((0, 0), (0, LANES - N_EXPERTS)))
    wr_hi, wr_lo = _hilo(wr)
    br = _pad_lanes(b_router[0], fill=-1e30)
    bgu = b_gate_up[0]
    bg = bgu[:, None, 0::2]
    bu = bgu[:, None, 1::2]
    bd = b_down[0][:, None, :]
    pw = 2 * LANES
    col = jnp.arange(pw)
    src_col = jnp.where(col < LANES, 2 * col, 2 * (col - LANES) + 1)
    perm = (jnp.arange(pw)[:, None] == src_col[None, :]).astype(BF16)

    x2 = x.reshape(t, d)
    z, xbc, dt, xr, gr = _in_proj(x2, norm_mix[0].reshape(1, d), wzx, wdt_hi, wdt_lo, wxr, wgr)

    y_ssd = _ssd(xbc.reshape(b, s, cdim), dt.reshape(b, s, LANES), z.reshape(b, s, w),
                 _shift_matrix(SSD_CHUNK), ssd_conv_w[0], ssd_conv_b[0].reshape(1, cdim), _pad_lanes(ssd_dt_bias[0]),
                 _pad_lanes(ssd_a_log[0]), dskip, ssd_norm[0].reshape(1, w), e01)
    y_lru = _lru(xr.reshape(b, s, w), gr.reshape(b, s, w), _shift_matrix(LRU_TILE), lru_conv_w[0], lru_conv_b[0].reshape(1, w),
                 wa_bd, lru_ba[0].reshape(1, w), wx_bd, lru_bx[0].reshape(1, w), lru_lambda[0].reshape(1, w))

    kk, vv = _kv(mem.reshape(b * n_mem, d), norm_mem[0].reshape(1, d), w_kv[0].astype(BF16))
    wo_mix = w_out[0].astype(BF16)
    h2, hn, idx_m, rank_m, gate_m, cnt = _mid(
        x2, y_ssd.reshape(t, w), y_lru.reshape(t, w), kk.reshape(b, n_mem, d), vv.reshape(b, n_mem, d),
        wo_mix[:w], wo_mix[w:], norm_xattn[0].reshape(1, d), w_q[0].astype(BF16), w_o[0].astype(BF16),
        norm_moe[0].reshape(1, d), wr_hi, wr_lo, br, s)

    bm = MOE_BLOCK
    counts = cnt[:, 0]
    pcounts = (counts + bm - 1) // bm * bm
    pend = jnp.cumsum(pcounts).astype(jnp.int32)
    pstart = pend - pcounts
    idx = idx_m[:TOP_K, :]
    onehot = idx[:, :, None] == jnp.arange(N_EXPERTS, dtype=jnp.int32)[None, None, :]
    dest = jnp.sum(jnp.where(onehot, pstart[None, None, :], 0), axis=-1) + rank_m[:TOP_K, :]
    n_pairs = t * TOP_K
    nb = (n_pairs + N_EXPERTS * (bm - 1) + bm - 1) // bm
    blk0 = jnp.arange(nb, dtype=jnp.int32) * bm
    block_e = jnp.minimum(jnp.sum(pend[None, :] <= blk0[:, None], axis=1), N_EXPERTS - 1).astype(jnp.int32)
    nvb = (pend[-1] // bm).astype(jnp.int32).reshape(1)
    after = pend[block_e] // bm
    next_e = jnp.where(after < nvb[0], block_e[jnp.minimum(after, nb - 1)], -1).astype(jnp.int32)
    tc = COMB_TILE
    dest3 = dest.astype(jnp.int32).reshape(TOP_K, t // tc, tc).transpose(1, 0, 2).reshape(t // tc, 1, TOP_K * tc)

    xrows = _dispatch(pend, dest3, hn, nb * bm, d)
    y = _moe(block_e, next_e, nvb, xrows, w_gate_up[0], bg, bu, w_down[0], bd, perm)
    out = _combine(dest3, h2, gate_m.T, y, norm_final.reshape(1, d))
    return out.reshape(b, s, d)
```

```python
import functools

import jax
import jax.numpy as jnp
from jax import lax
from jax.experimental import pallas as pl
from jax.experimental.pallas import tpu as pltpu

F32 = jnp.float32
BF16 = jnp.bfloat16

NORM_EPS = 1e-6
LANES = 128
SUBLANES = 8
MXU_DIM = 256
SSD_HEAD_DIM = 64
SSD_HEADS = 16
SSD_GROUPS = 4
SSD_STATE = 128
CONV_K = 4
LRU_BLOCKS = 16
RG_C = 8.0
X_HEADS = 4
N_EXPERTS = 32
TOP_K = 4
SWIGLU_LIMIT = 7.0
SWIGLU_ALPHA = 1.702

VMEM_LIMIT = 56 * 1024 * 1024

IN_TILE = 1024
SSD_CHUNK = 256
LRU_TILE = 256
MID_TILE = 512
MOE_BLOCK = 512
COMB_TILE = 256
ISSUE_UNROLL = 8


def _cparams(sem):
    return pltpu.CompilerParams(dimension_semantics=sem, vmem_limit_bytes=VMEM_LIMIT)


def _rms(x, g):
    ms = jnp.mean(x * x, axis=-1, keepdims=True)
    return x * lax.rsqrt(ms + NORM_EPS) * g


def _sigmoid(x):
    return 0.5 * jnp.tanh(0.5 * x) + 0.5


def _softplus(x):
    return jnp.maximum(x, 0.0) + jnp.log(1.0 + jnp.exp(-jnp.abs(x)))


def _split3(x):
    a = x.astype(BF16)
    r = x - a.astype(F32)
    b = r.astype(BF16)
    c = (r - b.astype(F32)).astype(BF16)
    return a, b, c


def _dot(a, b):
    return jnp.dot(a, b, preferred_element_type=F32)


def _dot_nt(a, b):
    return lax.dot_general(a, b, (((1,), (1,)), ((), ())), preferred_element_type=F32)


def _dot01_right(x, m01):
    a, b, c = _split3(x)
    return _dot(a, m01) + _dot(b, m01) + _dot(c, m01)


def _dot01_left(m01, x):
    a, b, c = _split3(x)
    return _dot(m01, a) + _dot(m01, b) + _dot(m01, c)


def _dot_hilo(x, w_hi, w_lo):
    xh = x.astype(BF16)
    xl = (x - xh.astype(F32)).astype(BF16)
    return _dot(xh, w_hi) + _dot(xl, w_hi) + _dot(xh, w_lo)


def _hilo(w):
    hi = w.astype(BF16)
    lo = (w - hi.astype(F32)).astype(BF16)
    return hi, lo


def _store_token_major(ref, val, base=0):
    n, d = val.shape
    parts = d // LANES
    for s in range(parts):
        ref[pl.ds(base + s, n, stride=parts), :] = val[:, s * LANES:(s + 1) * LANES]


def _load_token_major(ref, n, d, base=0):
    parts = d // LANES
    return [ref[pl.ds(base + s, n, stride=parts), :] for s in range(parts)]


def _in_proj_kernel(x_ref, g_ref, wzx_ref, wdth_ref, wdtl_ref, wxr_ref, wgr_ref,
                    z_ref, xbc_ref, dt_ref, xr_ref, gr_ref):
    d = x_ref.shape[1]
    hn = _rms(x_ref[...], g_ref[...])
    hb = hn.astype(BF16)
    z_ref[...] = _dot(hb, wzx_ref[:, :d]).astype(BF16)
    xbc_ref[...] = _dot(hb, wzx_ref[:, d:]).astype(BF16)
    dt_ref[...] = _dot_hilo(hn, wdth_ref[...], wdtl_ref[...])
    xr_ref[...] = _dot(hb, wxr_ref[...]).astype(BF16)
    gr_ref[...] = _dot(hb, wgr_ref[...]).astype(BF16)


def _in_proj(x2, g, wzx, wdt_hi, wdt_lo, wxr, wgr):
    t, d = x2.shape
    tm = IN_TILE
    nzx = wzx.shape[1]
    const = lambda i: (0, 0)
    row = lambda i: (i, 0)
    return pl.pallas_call(
        _in_proj_kernel,
        grid=(t // tm,),
        in_specs=[
            pl.BlockSpec((tm, d), row),
            pl.BlockSpec((1, d), const),
            pl.BlockSpec((d, nzx), const),
            pl.BlockSpec((d, LANES), const),
            pl.BlockSpec((d, LANES), const),
            pl.BlockSpec((d, d), const),
            pl.BlockSpec((d, d), const),
        ],
        out_specs=[
            pl.BlockSpec((tm, d), row),
            pl.BlockSpec((tm, nzx - d), row),
            pl.BlockSpec((tm, LANES), row),
            pl.BlockSpec((tm, d), row),
            pl.BlockSpec((tm, d), row),
        ],
        out_shape=[
            jax.ShapeDtypeStruct((t, d), BF16),
            jax.ShapeDtypeStruct((t, nzx - d), BF16),
            jax.ShapeDtypeStruct((t, LANES), F32),
            jax.ShapeDtypeStruct((t, d), BF16),
            jax.ShapeDtypeStruct((t, d), BF16),
        ],
        compiler_params=_cparams(("arbitrary",)),
        name="in_proj",
    )(x2, g, wzx, wdt_hi, wdt_lo, wxr, wgr)


def _shift_matrix(n):
    return jnp.concatenate([jnp.eye(n, k=-(CONV_K - 1 - j), dtype=BF16) for j in range(CONV_K - 1)], axis=0)


def _causal_conv(halo_ref, x, shift_ref, w_ref, b_ref, first):
    n = x.shape[0]
    pad = SUBLANES
    k1 = CONV_K - 1

    @pl.when(first)
    def _():
        halo_ref[...] = jnp.zeros(halo_ref.shape, F32)

    xf = x.astype(F32)
    taps = _dot(shift_ref[...], x)
    acc = b_ref[...] + w_ref[k1:k1 + 1, :] * xf
    for j in range(k1):
        acc = acc + w_ref[j:j + 1, :] * taps[j * n:(j + 1) * n]
    head = acc[:pad]
    for j in range(k1):
        head = head + w_ref[j:j + 1, :] * halo_ref[pad - k1 + j:2 * pad - k1 + j, :]
    halo_ref[0:pad, :] = xf[n - pad:n]
    return jnp.concatenate([head, acc[pad:]], axis=0)


def _ssd_kernel(xbc_ref, dt_ref, z_ref, sh_ref, cw_ref, cb_ref, dtb_ref, alog_ref, dskip_ref, gn_ref, e_ref,
                y_ref, halo_ref, st_ref):
    n = xbc_ref.shape[1]
    w = z_ref.shape[2]
    gw = w // SSD_GROUPS
    first = pl.program_id(1) == 0

    @pl.when(first)
    def _():
        st_ref[...] = jnp.zeros(st_ref.shape, F32)

    conv = _causal_conv(halo_ref, xbc_ref[0], sh_ref, cw_ref, cb_ref, first)
    xc = conv * _sigmoid(conv)
    xs = xc[:, :w]

    dt = _softplus(dt_ref[0] + dtb_ref[...])
    a = -jnp.exp(alog_ref[...])
    da = dt * a
    ri = lax.broadcasted_iota(jnp.int32, (n, n), 0)
    ci = lax.broadcasted_iota(jnp.int32, (n, n), 1)
    causal = ri >= ci
    tril = jnp.where(causal, 1.0, 0.0).astype(BF16)
    a_cs = _dot01_left(tril, da)
    a_cs_t = a_cs.T

    e01 = e_ref[...]
    dt_x = _dot01_right(dt, e01)
    acs_x = _dot01_right(a_cs, e01)
    last_x = acs_x[n - 1:n, :]
    xdt = xs * dt_x
    xdt_b = xdt.astype(BF16)
    xdt_end = (xdt * jnp.exp(last_x - acs_x)).astype(BF16)
    exp_acs = jnp.exp(acs_x)
    chunk_decay = jnp.exp(last_x)
    lane = lax.broadcasted_iota(jnp.int32, (n, gw), 1)

    for g in range(SSD_GROUPS):
        lo = g * gw
        bg = xc[:, w + g * SSD_STATE:w + (g + 1) * SSD_STATE].astype(BF16)
        cg = xc[:, w + (SSD_GROUPS + g) * SSD_STATE:w + (SSD_GROUPS + g + 1) * SSD_STATE].astype(BF16)
        cb = _dot_nt(cg, bg)
        prev = st_ref[g]
        acc = _dot(cg, prev.astype(BF16)) * exp_acs[:, lo:lo + gw]
        new = lax.dot_general(bg, xdt_end[:, lo:lo + gw], (((0,), (0,)), ((), ())),
                              preferred_element_type=F32)
        st_ref[g] = chunk_decay[:, lo:lo + gw] * prev + new
        xg = xdt_b[:, lo:lo + gw]
        for k in range(SSD_HEADS // SSD_GROUPS):
            h = g * (SSD_HEADS // SSD_GROUPS) + k
            seg = a_cs[:, h:h + 1] - a_cs_t[h:h + 1, :]
            dec = jnp.exp(jnp.where(causal, seg, -jnp.inf))
            m = (cb * dec).astype(BF16)
            in_head = (lane >= k * SSD_HEAD_DIM) & (lane < (k + 1) * SSD_HEAD_DIM)
            acc = acc + _dot(m, jnp.where(in_head, xg, jnp.zeros_like(xg)))
        yg = acc + xs[:, lo:lo + gw] * dskip_ref[:, lo:lo + gw]
        zg = z_ref[0, :, lo:lo + gw].astype(F32)
        u = yg * (zg * _sigmoid(zg))
        u = u * lax.rsqrt(jnp.mean(u * u, axis=-1, keepdims=True) + NORM_EPS)
        y_ref[0, :, lo:lo + gw] = (u * gn_ref[:, lo:lo + gw]).astype(BF16)


def _ssd(xbc, dt, z, shift, cw, cb, dtb, alog, dskip, gn, e01):
    b, s, cdim = xbc.shape
    w = z.shape[2]
    n = SSD_CHUNK
    tile = lambda i, j: (i, j, 0)
    const = lambda i, j: (0, 0)
    return pl.pallas_call(
        _ssd_kernel,
        grid=(b, s // n),
        in_specs=[
            pl.BlockSpec((1, n, cdim), tile),
            pl.BlockSpec((1, n, LANES), tile),
            pl.BlockSpec((1, n, w), tile),
            pl.BlockSpec(shift.shape, const),
            pl.BlockSpec((CONV_K, cdim), const),
            pl.BlockSpec((1, cdim), const),
            pl.BlockSpec((1, LANES), const),
            pl.BlockSpec((1, LANES), const),
            pl.BlockSpec((1, w), const),
            pl.BlockSpec((1, w), const),
            pl.BlockSpec((LANES, w), const),
        ],
        out_specs=pl.BlockSpec((1, n, w), tile),
        out_shape=jax.ShapeDtypeStruct((b, s, w), BF16),
        scratch_shapes=[
            pltpu.VMEM((2 * SUBLANES, cdim), F32),
            pltpu.VMEM((SSD_GROUPS, SSD_STATE, w // SSD_GROUPS), F32),
        ],
        compiler_params=_cparams(("arbitrary", "arbitrary")),
        name="ssd",
    )(xbc, dt, z, shift, cw, cb, dtb, alog, dskip, gn, e01)


def _gelu_tanh(x):
    c = 0.7978845608028654
    return 0.5 * x * (1.0 + jnp.tanh(c * (x + 0.044715 * (x * x * x))))


def _lru_kernel(xr_ref, gr_ref, sh_ref, cw_ref, cb_ref, wa_ref, ba_ref, wx_ref, bx_ref, lam_ref,
                y_ref, halo_ref, car_ref, h_ref):
    n = xr_ref.shape[1]
    w = xr_ref.shape[2]
    first = pl.program_id(1) == 0

    @pl.when(first)
    def _():
        car_ref[...] = jnp.zeros(car_ref.shape, F32)

    xc = _causal_conv(halo_ref, xr_ref[0], sh_ref, cw_ref, cb_ref, first)
    xb = xc.astype(BF16)
    nq = wa_ref.shape[0]
    qw = w // nq
    r_parts, i_parts = [], []
    for q in range(nq):
        xq = xb[:, q * qw:(q + 1) * qw]
        r_parts.append(_dot(xq, wa_ref[q]))
        i_parts.append(_dot(xq, wx_ref[q]))
    r = _sigmoid(jnp.concatenate(r_parts, axis=1) + ba_ref[...])
    gi = _sigmoid(jnp.concatenate(i_parts, axis=1) + bx_ref[...])
    log_a = (-RG_C) * r * _softplus(-lam_ref[...])
    a = jnp.exp(log_a)
    u = xc * gi * jnp.sqrt(1.0 - a * a)

    groups = n // SUBLANES
    sub = lax.broadcasted_iota(jnp.int32, (groups, SUBLANES, w), 1)
    ap = a.reshape(groups, SUBLANES, w)
    bp = u.reshape(groups, SUBLANES, w)
    for d in (1, 2, 4):
        m = sub >= d
        bp = jnp.where(m, ap * pltpu.roll(bp, d, 1) + bp, bp)
        ap = jnp.where(m, ap * pltpu.roll(ap, d, 1), ap)
    carry = car_ref[...]
    for g in range(groups):
        hb = bp[g] + ap[g] * carry
        h_ref[g * SUBLANES:(g + 1) * SUBLANES, :] = hb
        carry = jnp.broadcast_to(hb[SUBLANES - 1:SUBLANES, :], (SUBLANES, w))
    car_ref[...] = carry
    y_ref[0] = (h_ref[...] * _gelu_tanh(gr_ref[0].astype(F32))).astype(BF16)


def _lru(xr, gr, shift, cw, cb, wa, ba, wx, bx, lam):
    b, s, w = xr.shape
    n = LRU_TILE
    nq, qw, _ = wa.shape
    tile = lambda i, j: (i, j, 0)
    const = lambda i, j: (0, 0)
    const3 = lambda i, j: (0, 0, 0)
    return pl.pallas_call(
        _lru_kernel,
        grid=(b, s // n),
        in_specs=[
            pl.BlockSpec((1, n, w), tile),
            pl.BlockSpec((1, n, w), tile),
            pl.BlockSpec(shift.shape, const),
            pl.BlockSpec((CONV_K, w), const),
            pl.BlockSpec((1, w), const),
            pl.BlockSpec((nq, qw, qw), const3),
            pl.BlockSpec((1, w), const),
            pl.BlockSpec((nq, qw, qw), const3),
            pl.BlockSpec((1, w), const),
            pl.BlockSpec((1, w), const),
        ],
        out_specs=pl.BlockSpec((1, n, w), tile),
        out_shape=jax.ShapeDtypeStruct((b, s, w), BF16),
        scratch_shapes=[
            pltpu.VMEM((2 * SUBLANES, w), F32),
            pltpu.VMEM((SUBLANES, w), F32),
            pltpu.VMEM((n, w), F32),
        ],
        compiler_params=_cparams(("arbitrary", "arbitrary")),
        name="lru",
    )(xr, gr, shift, cw, cb, wa, ba, wx, bx, lam)


def _kv_kernel(m_ref, g_ref, w_ref, k_ref, v_ref):
    d = m_ref.shape[1]
    mn = _rms(m_ref[...], g_ref[...]).astype(BF16)
    k_ref[...] = _dot(mn, w_ref[:, :d]).astype(BF16)
    v_ref[...] = _dot(mn, w_ref[:, d:]).astype(BF16)


def _kv(mem2, g, wkv):
    t, d = mem2.shape
    tm = min(t, 512)
    row = lambda i: (i, 0)
    const = lambda i: (0, 0)
    return pl.pallas_call(
        _kv_kernel,
        grid=(t // tm,),
        in_specs=[pl.BlockSpec((tm, d), row), pl.BlockSpec((1, d), const), pl.BlockSpec((d, 2 * d), const)],
        out_specs=[pl.BlockSpec((tm, d), row), pl.BlockSpec((tm, d), row)],
        out_shape=[jax.ShapeDtypeStruct((t, d), BF16), jax.ShapeDtypeStruct((t, d), BF16)],
        compiler_params=_cparams(("arbitrary",)),
        name="kv",
    )(mem2, g, wkv)


def _mid_kernel(x_ref, ys_ref, yl_ref, k_ref, v_ref, wo1_ref, wo2_ref, gx_ref, wq_ref, wo_ref,
                gm_ref, wrh_ref, wrl_ref, br_ref,
                h_ref, hn_ref, idx_ref, rank_ref, gate_ref, cnt_ref, car_ref):
    tm, d = x_ref.shape
    hd = d // X_HEADS

    @pl.when(pl.program_id(0) == 0)
    def _():
        car_ref[...] = jnp.zeros(car_ref.shape, F32)

    h1 = x_ref[...] + _dot(ys_ref[...], wo1_ref[...]) + _dot(yl_ref[...], wo2_ref[...])

    q = _dot(_rms(h1, gx_ref[...]).astype(BF16), wq_ref[...]).astype(BF16)
    o_parts = []
    for hh in range(X_HEADS):
        sl = slice(hh * hd, (hh + 1) * hd)
        sc = _dot_nt(q[:, sl], k_ref[0, :, sl]) * (hd ** -0.5)
        sc = sc - jnp.max(sc, axis=-1, keepdims=True)
        p = jnp.exp(sc)
        p = p / jnp.sum(p, axis=-1, keepdims=True)
        o_parts.append(_dot(p.astype(BF16), v_ref[0, :, sl]))
    o = jnp.concatenate(o_parts, axis=1).astype(BF16)
    h2 = h1 + _dot(o, wo_ref[...])
    h_ref[...] = h2

    hn = _rms(h2, gm_ref[...])
    _store_token_major(hn_ref, hn)
    logits = _dot_hilo(hn, wrh_ref[...], wrl_ref[...]) + br_ref[...]

    l = logits.T[:N_EXPERTS, :]
    row = lax.broadcasted_iota(jnp.int32, (N_EXPERTS, tm), 0)
    picked = jnp.zeros((N_EXPERTS, tm), F32)
    vals, idxs = [], []
    for _ in range(TOP_K):
        m = jnp.max(l, axis=0, keepdims=True)
        idx = jnp.min(jnp.where(l == m, row, N_EXPERTS), axis=0, keepdims=True)
        sel = row == idx
        vals.append(m)
        idxs.append(idx)
        picked = jnp.where(sel, 1.0, picked)
        l = jnp.where(sel, -jnp.inf, l)
    ex = [jnp.exp(v - vals[0]) for v in vals]
    den = ex[0] + ex[1] + ex[2] + ex[3]

    ri = lax.broadcasted_iota(jnp.int32, (tm, tm), 0)
    ci = lax.broadcasted_iota(jnp.int32, (tm, tm), 1)
    earlier = jnp.where(ri < ci, 1.0, 0.0).astype(BF16)
    before = _dot(picked.astype(BF16), earlier) + car_ref[:, 0:1]
    out_row = lax.broadcasted_iota(jnp.int32, (SUBLANES, tm), 0)
    idx_out = jnp.zeros((SUBLANES, tm), jnp.int32)
    rank_out = jnp.zeros((SUBLANES, tm), jnp.int32)
    gate_out = jnp.zeros((SUBLANES, tm), F32)
    for k in range(TOP_K):
        rk = jnp.sum(jnp.where(row == idxs[k], before, 0.0), axis=0, keepdims=True)
        at_k = out_row == k
        idx_out = jnp.where(at_k, idxs[k], idx_out)
        rank_out = jnp.where(at_k, rk.astype(jnp.int32), rank_out)
        gate_out = jnp.where(at_k, ex[k] / den, gate_out)
    idx_ref[...] = idx_out
    rank_ref[...] = rank_out
    gate_ref[...] = gate_out
    total = car_ref[...] + jnp.sum(picked, axis=1, keepdims=True)
    car_ref[...] = total
    cnt_ref[...] = total.astype(jnp.int32)


def _mid(x2, ys, yl, kk, vv, wo1, wo2, gx, wq, wo, gm, wr_hi, wr_lo, br, seq):
    t, d = x2.shape
    tm = MID_TILE
    m = kk.shape[1]
    per_b = seq // tm
    row = lambda i: (i, 0)
    col = lambda i: (0, i)
    const = lambda i: (0, 0)
    kvmap = lambda i: (i // per_b, 0, 0)
    wspec = pl.BlockSpec((d, d), const)
    vspec = pl.BlockSpec((1, d), const)
    return pl.pallas_call(
        _mid_kernel,
        grid=(t // tm,),
        in_specs=[
            pl.BlockSpec((tm, d), row), pl.BlockSpec((tm, d), row), pl.BlockSpec((tm, d), row),
            pl.BlockSpec((1, m, d), kvmap), pl.BlockSpec((1, m, d), kvmap),
            wspec, wspec, vspec, wspec, wspec, vspec,
            pl.BlockSpec((d, LANES), const), pl.BlockSpec((d, LANES), const), pl.BlockSpec((1, LANES), const),
        ],
        out_specs=[
            pl.BlockSpec((tm, d), row), pl.BlockSpec((tm * d // LANES, LANES), row),
            pl.BlockSpec((SUBLANES, tm), col), pl.BlockSpec((SUBLANES, tm), col), pl.BlockSpec((SUBLANES, tm), col),
            pl.BlockSpec((N_EXPERTS, LANES), const),
        ],
        out_shape=[
            jax.ShapeDtypeStruct((t, d), F32), jax.ShapeDtypeStruct((t * d // LANES, LANES), F32),
            jax.ShapeDtypeStruct((SUBLANES, t), jnp.int32), jax.ShapeDtypeStruct((SUBLANES, t), jnp.int32),
            jax.ShapeDtypeStruct((SUBLANES, t), F32),
            jax.ShapeDtypeStruct((N_EXPERTS, LANES), jnp.int32),
        ],
        scratch_shapes=[pltpu.VMEM((N_EXPERTS, LANES), F32)],
        compiler_params=_cparams(("arbitrary",)),
        name="mid",
    )(x2, ys, yl, kk, vv, wo1, wo2, gx, wq, wo, gm, wr_hi, wr_lo, br)


def _dispatch_kernel(pend_ref, dst_ref, hn_ref, hn_hbm, x_hbm, zero_ref, sem, *, parts):
    i = pl.program_id(0)
    tc = hn_ref.shape[0] // parts
    bm = zero_ref.shape[0] // parts
    rows = TOP_K * tc

    def tile(ref, n, count=1):
        return ref.at[pl.ds(pl.multiple_of(n * parts, parts), count * parts), :]

    @pl.when(i == 0)
    def _():
        zero_ref[...] = jnp.zeros(zero_ref.shape, F32)

        def fill(start):
            return pltpu.make_async_copy(zero_ref, tile(x_hbm, start, bm), sem.at[1])

        for e in range(N_EXPERTS):
            fill(jnp.maximum(pend_ref[e] - bm, 0)).start()
        for e in range(N_EXPERTS):
            fill(0).wait()

        def tail(j, c):
            fill(j * bm).start()
            fill(0).wait()
            return c

        lax.fori_loop(pend_ref[N_EXPERTS - 1] // bm, x_hbm.shape[0] // (bm * parts), tail, 0)

    def body(j, c):
        t0 = pl.multiple_of(j * ISSUE_UNROLL, ISSUE_UNROLL)
        for u in range(ISSUE_UNROLL):
            for k in range(TOP_K):
                dst = dst_ref[0, 0, k * tc + t0 + u]
                src = tile(hn_ref, t0 + u) if k < TOP_K // 2 else tile(hn_hbm, i * tc + t0 + u)
                pltpu.make_async_copy(src, tile(x_hbm, dst), sem.at[0]).start(priority=k % 2)
        return c

    lax.fori_loop(0, tc // ISSUE_UNROLL, body, 0)
    pltpu.make_async_copy(tile(x_hbm, 0, rows), tile(x_hbm, 0, rows), sem.at[0]).wait()


def _dispatch(pend, dest3, hn_tm, n_rows, d):
    parts = d // LANES
    t = hn_tm.shape[0] // parts
    tc = COMB_TILE
    grid_spec = pltpu.PrefetchScalarGridSpec(
        num_scalar_prefetch=1,
        grid=(t // tc,),
        in_specs=[
            pl.BlockSpec((1, 1, TOP_K * tc), lambda i, pe: (i, 0, 0), memory_space=pltpu.SMEM),
            pl.BlockSpec((tc * parts, LANES), lambda i, pe: (i, 0)),
            pl.BlockSpec(memory_space=pl.ANY),
        ],
        out_specs=pl.BlockSpec(memory_space=pl.ANY),
        scratch_shapes=[pltpu.VMEM((MOE_BLOCK * parts, LANES), F32), pltpu.SemaphoreType.DMA((2,))],
    )
    return pl.pallas_call(
        functools.partial(_dispatch_kernel, parts=parts),
        grid_spec=grid_spec,
        out_shape=jax.ShapeDtypeStruct((n_rows * parts, LANES), F32),
        compiler_params=_cparams(("arbitrary",)),
        name="dispatch",
    )(pend, dest3, hn_tm, hn_tm)


def _moe_kernel(be_ref, nxt_ref, nvb_ref, x_ref, wgu_hbm, bg_ref, bu_ref, wd_hbm, bd_ref, perm_ref,
                y_ref, wgu_f, wd_f, wgu_s, wd_s, act_s, sem):
    i = pl.program_id(0)
    nvb = nvb_ref[0]
    f, d = wd_f.shape
    bm = act_s.shape[0]
    pw = perm_ref.shape[0]
    half = pw // 2

    def fetch(e):
        return (pltpu.make_async_copy(wgu_hbm.at[e], wgu_f, sem.at[0]),
                pltpu.make_async_copy(wd_hbm.at[e], wd_f, sem.at[1]))

    @pl.when(jnp.logical_and(i == 0, nvb > 0))
    def _():
        for c in fetch(be_ref[0]):
            c.start()

    changed = jnp.logical_or(i == 0, be_ref[i] != be_ref[jnp.maximum(i - 1, 0)])

    @pl.when(jnp.logical_and(changed, i < nvb))
    def _():
        for c in fetch(be_ref[i]):
            c.wait()
        for c in range(2 * f // pw):
            wc = wgu_f[:, c * pw:(c + 1) * pw].astype(BF16)
            wgu_s[:, c * pw:(c + 1) * pw] = _dot(wc, perm_ref[...]).astype(BF16)
        wd_s[...] = wd_f[...].astype(BF16)

        @pl.when(nxt_ref[i] >= 0)
        def _():
            for c in fetch(nxt_ref[i]):
                c.start()

    @pl.when(i < nvb)
    def _():
        e = be_ref[i]
        xb = jnp.concatenate([p.astype(BF16) for p in _load_token_major(x_ref, bm, d)], axis=1)
        for c in range(2 * f // pw):
            gu = _dot(xb, wgu_s[:, c * pw:(c + 1) * pw])
            g = gu[:, :half] + bg_ref[e, :, c * half:(c + 1) * half]
            u = gu[:, half:] + bu_ref[e, :, c * half:(c + 1) * half]
            g = jnp.minimum(g, SWIGLU_LIMIT)
            u = jnp.clip(u, -SWIGLU_LIMIT, SWIGLU_LIMIT)
            act = (u + 1.0) * (g * _sigmoid(SWIGLU_ALPHA * g))
            act_s[:, c * half:(c + 1) * half] = act.astype(BF16)
        _store_token_major(y_ref, _dot(act_s[...], wd_s[...]) + bd_ref[e])

    @pl.when(i >= nvb)
    def _():
        y_ref[...] = jnp.zeros(y_ref.shape, F32)


def _moe(block_e, next_e, nvb, xrows_tm, wgu, bg, bu, wd, bd, perm):
    f, d = wd.shape[1], wd.shape[2]
    parts = d // LANES
    bm = MOE_BLOCK
    nb = xrows_tm.shape[0] // (bm * parts)

    def whole(a):
        return pl.BlockSpec(a.shape, lambda i, be, nx, nv, nd=a.ndim: (0,) * nd)

    grid_spec = pltpu.PrefetchScalarGridSpec(
        num_scalar_prefetch=3,
        grid=(nb,),
        in_specs=[
            pl.BlockSpec((bm * parts, LANES),
                         lambda i, be, nx, nv: (jnp.minimum(i, jnp.maximum(nv[0] - 1, 0)), 0)),
            pl.BlockSpec(memory_space=pl.ANY),
            whole(bg), whole(bu),
            pl.BlockSpec(memory_space=pl.ANY),
            whole(bd), whole(perm),
        ],
        out_specs=pl.BlockSpec((bm * parts, LANES), lambda i, be, nx, nv: (i, 0)),
        scratch_shapes=[
            pltpu.VMEM((d, 2 * f), F32), pltpu.VMEM((f, d), F32),
            pltpu.VMEM((d, 2 * f), BF16), pltpu.VMEM((f, d), BF16),
            pltpu.VMEM((bm, f), BF16),
            pltpu.SemaphoreType.DMA((2,)),
        ],
    )
    return pl.pallas_call(
        _moe_kernel,
        grid_spec=grid_spec,
        out_shape=jax.ShapeDtypeStruct(xrows_tm.shape, F32),
        compiler_params=_cparams(("arbitrary",)),
        name="moe",
    )(block_e, next_e, nvb, xrows_tm, wgu, bg, bu, wd, bd, perm)


def _comb_kernel(dst_ref, dstn_ref, h_ref, gate_ref, y_hbm, g_ref, o_ref, ybuf, sem, *, parts):
    i = pl.program_id(0)
    n = pl.num_programs(0)
    tc, d = h_ref.shape
    rows = TOP_K * tc
    slot = i % 2

    def tile(ref, n, count=1):
        return ref.at[pl.ds(pl.multiple_of(n * parts, parts), count * parts), :]

    def start_rows(idx_ref, s):
        def body(j, c):
            t0 = pl.multiple_of(j * ISSUE_UNROLL, ISSUE_UNROLL)
            for u in range(ISSUE_UNROLL):
                for k in range(TOP_K):
                    r = k * tc + t0 + u
                    pltpu.make_async_copy(tile(y_hbm, idx_ref[0, 0, r]), tile(ybuf.at[s], r),
                                          sem.at[s]).start(priority=k % 2)
            return c

        lax.fori_loop(0, tc // ISSUE_UNROLL, body, 0)

    @pl.when(i == 0)
    def _():
        start_rows(dst_ref, 0)

    @pl.when(i + 1 < n)
    def _():
        start_rows(dstn_ref, 1 - slot)

    pltpu.make_async_copy(tile(y_hbm, 0, rows), ybuf.at[slot], sem.at[slot]).wait()
    acc = [h_ref[:, s * LANES:(s + 1) * LANES] for s in range(parts)]
    for k in range(TOP_K):
        yk = _load_token_major(ybuf.at[slot], tc, d, base=k * tc * parts)
        gk = gate_ref[:, k:k + 1]
        acc = [a + p * gk for a, p in zip(acc, yk)]
    o_ref[...] = _rms(jnp.concatenate(acc, axis=1), g_ref[...])


def _combine(dest3, h2, gate, y_tm, g):
    t, d = h2.shape
    parts = d // LANES
    tc = COMB_TILE
    nt = t // tc
    return pl.pallas_call(
        functools.partial(_comb_kernel, parts=parts),
        grid=(nt,),
        in_specs=[
            pl.BlockSpec((1, 1, TOP_K * tc), lambda i: (i, 0, 0), memory_space=pltpu.SMEM),
            pl.BlockSpec((1, 1, TOP_K * tc), lambda i: (jnp.minimum(i + 1, nt - 1), 0, 0),
                         memory_space=pltpu.SMEM),
            pl.BlockSpec((tc, d), lambda i: (i, 0)),
            pl.BlockSpec((tc, SUBLANES), lambda i: (i, 0)),
            pl.BlockSpec(memory_space=pl.ANY),
            pl.BlockSpec((1, d), lambda i: (0, 0)),
        ],
        out_specs=pl.BlockSpec((tc, d), lambda i: (i, 0)),
        out_shape=jax.ShapeDtypeStruct((t, d), F32),
        scratch_shapes=[pltpu.VMEM((2, TOP_K * tc * parts, LANES), F32), pltpu.SemaphoreType.DMA((2,))],
        compiler_params=_cparams(("arbitrary",)),
        name="combine",
    )(dest3, dest3, h2, gate, y_tm, g)


def _block_diag(wb, per):
    nb, bw, _ = wb.shape
    wq = wb.reshape(nb // per, per, bw, bw)
    eye = jnp.eye(per, dtype=wb.dtype)
    out = jnp.einsum('qaij,ab->qaibj', wq, eye)
    return out.reshape(nb // per, per * bw, per * bw)


def _pad_lanes(v, fill=0.0):
    return jnp.pad(v, (0, LANES - v.shape[0]), constant_values=fill).reshape(1, LANES)


def kernel(x, mem, norm_mix, w_in, ssd_conv_w, ssd_conv_b, ssd_dt_bias, ssd_a_log, ssd_d, ssd_norm, lru_conv_w, lru_conv_b, lru_wa, lru_ba, lru_wx, lru_bx, lru_lambda, w_out, norm_xattn, norm_mem, w_q, w_kv, w_o, norm_moe, w_router, b_router, w_gate_up, b_gate_up, w_down, b_down, norm_final):
    b, s, d = x.shape
    t = b * s
    n_mem = mem.shape[1]
    w = d
    cdim = w + 2 * SSD_GROUPS * SSD_STATE
    o1, o2, o3, o4 = w, w + cdim, w + cdim + SSD_HEADS, w + cdim + SSD_HEADS + w

    wi = w_in[0]
    wzx = wi[:, :o2].astype(BF16)
    wdt = jnp.pad(wi[:, o2:o3], ((0, 0), (0, LANES - SSD_HEADS)))
    wdt_hi, wdt_lo = _hilo(wdt)
    wxr = wi[:, o3:o4].astype(BF16)
    wgr = wi[:, o4:].astype(BF16)
    e01 = (jnp.arange(LANES)[:, None] == (jnp.arange(w)[None, :] // SSD_HEAD_DIM)).astype(BF16)
    dskip = jnp.repeat(ssd_d[0], SSD_HEAD_DIM).reshape(1, w)
    per = MXU_DIM // (w // LRU_BLOCKS)
    wa_bd = _block_diag(lru_wa[0], per).astype(BF16)
    wx_bd = _block_diag(lru_wx[0], per).astype(BF16)
    wr = jnp.pad(w_router[0], ((0, 0), (0, LANES - N_EXPERTS)))
    wr_hi, wr_lo = _hilo(wr)
    br = _pad_lanes(b_router[0], fill=-1e30)
    bgu = b_gate_up[0]
    bg = bgu[:, None, 0::2]
    bu = bgu[:, None, 1::2]
    bd = b_down[0][:, None, :]
    pw = 2 * LANES
    col = jnp.arange(pw)
    src_col = jnp.where(col < LANES, 2 * col, 2 * (col - LANES) + 1)
    perm = (jnp.arange(pw)[:, None] == src_col[None, :]).astype(BF16)

    x2 = x.reshape(t, d)
    z, xbc, dt, xr, gr = _in_proj(x2, norm_mix[0].reshape(1, d), wzx, wdt_hi, wdt_lo, wxr, wgr)

    y_ssd = _ssd(xbc.reshape(b, s, cdim), dt.reshape(b, s, LANES), z.reshape(b, s, w),
                 _shift_matrix(SSD_CHUNK), ssd_conv_w[0], ssd_conv_b[0].reshape(1, cdim), _pad_lanes(ssd_dt_bias[0]),
                 _pad_lanes(ssd_a_log[0]), dskip, ssd_norm[0].reshape(1, w), e01)
    y_lru = _lru(xr.reshape(b, s, w), gr.reshape(b, s, w), _shift_matrix(LRU_TILE), lru_conv_w[0], lru_conv_b[0].reshape(1, w),
                 wa_bd, lru_ba[0].reshape(1, w), wx_bd, lru_bx[0].reshape(1, w), lru_lambda[0].reshape(1, w))

    kk, vv = _kv(mem.reshape(b * n_mem, d), norm_mem[0].reshape(1, d), w_kv[0].astype(BF16))
    wo_mix = w_out[0].astype(BF16)
    h2, hn, idx_m, rank_m, gate_m, cnt = _mid(
        x2, y_ssd.reshape(t, w), y_lru.reshape(t, w), kk.reshape(b, n_mem, d), vv.reshape(b, n_mem, d),
        wo_mix[:w], wo_mix[w:], norm_xattn[0].reshape(1, d), w_q[0].astype(BF16), w_o[0].astype(BF16),
        norm_moe[0].reshape(1, d), wr_hi, wr_lo, br, s)

    bm = MOE_BLOCK
    counts = cnt[:, 0]
    pcounts = (counts + bm - 1) // bm * bm
    pend = jnp.cumsum(pcounts).astype(jnp.int32)
    pstart = pend - pcounts
    idx = idx_m[:TOP_K, :]
    onehot = idx[:, :, None] == jnp.arange(N_EXPERTS, dtype=jnp.int32)[None, None, :]
    dest = jnp.sum(jnp.where(onehot, pstart[None, None, :], 0), axis=-1) + rank_m[:TOP_K, :]
    n_pairs = t * TOP_K
    nb = (n_pairs + N_EXPERTS * (bm - 1) + bm - 1) // bm
    blk0 = jnp.arange(nb, dtype=jnp.int32) * bm
    block_e = jnp.minimum(jnp.sum(pend[None, :] <= blk0[:, None], axis=1), N_EXPERTS - 1).astype(jnp.int32)
    nvb = (pend[-1] // bm).astype(jnp.int32).reshape(1)
    after = pend[block_e] // bm
    next_e = jnp.where(after < nvb[0], block_e[jnp.minimum(after, nb - 1)], -1).astype(jnp.int32)
    tc = COMB_TILE
    dest3 = dest.astype(jnp.int32).reshape(TOP_K, t // tc, tc).transpose(1, 0, 2).reshape(t // tc, 1, TOP_K * tc)

    xrows = _dispatch(pend, dest3, hn, nb * bm, d)
    y = _moe(block_e, next_e, nvb, xrows, w_gate_up[0], bg, bu, w_down[0], bd, perm)
    out = _combine(dest3, h2, gate_m.T, y, norm_final.reshape(1, d))
    return out.reshape(b, s, d)
```

```python
import functools

import jax
import jax.numpy as jnp
from jax import lax
from jax.experimental import pallas as pl
from jax.experimental.pallas import tpu as pltpu

F32 = jnp.float32
BF16 = jnp.bfloat16

NORM_EPS = 1e-6
LANES = 128
SUBLANES = 8
MXU_DIM = 256
SSD_HEAD_DIM = 64
SSD_HEADS = 16
SSD_GROUPS = 4
SSD_STATE = 128
CONV_K = 4
LRU_BLOCKS = 16
RG_C = 8.0
X_HEADS = 4
N_EXPERTS = 32
TOP_K = 4
SWIGLU_LIMIT = 7.0
SWIGLU_ALPHA = 1.702

VMEM_LIMIT = 56 * 1024 * 1024

IN_TILE = 1024
SSD_CHUNK = 256
LRU_TILE = 256
MID_TILE = 512
MOE_BLOCK = 512
DISP_TILE = 1024
COMB_TILE = 256
ISSUE_UNROLL = 8


def _cparams(sem):
    return pltpu.CompilerParams(dimension_semantics=sem, vmem_limit_bytes=VMEM_LIMIT)


def _rms(x, g):
    ms = jnp.mean(x * x, axis=-1, keepdims=True)
    return x * lax.rsqrt(ms + NORM_EPS) * g


def _sigmoid(x):
    return 0.5 * jnp.tanh(0.5 * x) + 0.5


def _softplus(x):
    return jnp.maximum(x, 0.0) + jnp.log(1.0 + jnp.exp(-jnp.abs(x)))


def _split3(x):
    a = x.astype(BF16)
    r = x - a.astype(F32)
    b = r.astype(BF16)
    c = (r - b.astype(F32)).astype(BF16)
    return a, b, c


def _dot(a, b):
    return jnp.dot(a, b, preferred_element_type=F32)


def _dot_nt(a, b):
    return lax.dot_general(a, b, (((1,), (1,)), ((), ())), preferred_element_type=F32)


def _dot01_right(x, m01):
    a, b, c = _split3(x)
    return _dot(a, m01) + _dot(b, m01) + _dot(c, m01)


def _dot01_left(m01, x):
    a, b, c = _split3(x)
    return _dot(m01, a) + _dot(m01, b) + _dot(m01, c)


def _dot_hilo(x, w_hi, w_lo):
    xh = x.astype(BF16)
    xl = (x - xh.astype(F32)).astype(BF16)
    return _dot(xh, w_hi) + _dot(xl, w_hi) + _dot(xh, w_lo)


def _hilo(w):
    hi = w.astype(BF16)
    lo = (w - hi.astype(F32)).astype(BF16)
    return hi, lo


def _store_token_major(ref, val, base=0):
    n, d = val.shape
    parts = d // LANES
    for s in range(parts):
        ref[pl.ds(base + s, n, stride=parts), :] = val[:, s * LANES:(s + 1) * LANES]


def _load_token_major(ref, n, d, base=0):
    parts = d // LANES
    return [ref[pl.ds(base + s, n, stride=parts), :] for s in range(parts)]


def _in_proj_kernel(x_ref, g_ref, wzx_ref, wdth_ref, wdtl_ref, wxr_ref, wgr_ref,
                    z_ref, xbc_ref, dt_ref, xr_ref, gr_ref):
    d = x_ref.shape[1]
    hn = _rms(x_ref[...], g_ref[...])
    hb = hn.astype(BF16)
    z_ref[...] = _dot(hb, wzx_ref[:, :d]).astype(BF16)
    xbc_ref[...] = _dot(hb, wzx_ref[:, d:]).astype(BF16)
    dt_ref[...] = _dot_hilo(hn, wdth_ref[...], wdtl_ref[...])
    xr_ref[...] = _dot(hb, wxr_ref[...]).astype(BF16)
    gr_ref[...] = _dot(hb, wgr_ref[...]).astype(BF16)


def _in_proj(x2, g, wzx, wdt_hi, wdt_lo, wxr, wgr):
    t, d = x2.shape
    tm = IN_TILE
    nzx = wzx.shape[1]
    const = lambda i: (0, 0)
    row = lambda i: (i, 0)
    return pl.pallas_call(
        _in_proj_kernel,
        grid=(t // tm,),
        in_specs=[
            pl.BlockSpec((tm, d), row),
            pl.BlockSpec((1, d), const),
            pl.BlockSpec((d, nzx), const),
            pl.BlockSpec((d, LANES), const),
            pl.BlockSpec((d, LANES), const),
            pl.BlockSpec((d, d), const),
            pl.BlockSpec((d, d), const),
        ],
        out_specs=[
            pl.BlockSpec((tm, d), row),
            pl.BlockSpec((tm, nzx - d), row),
            pl.BlockSpec((tm, LANES), row),
            pl.BlockSpec((tm, d), row),
            pl.BlockSpec((tm, d), row),
        ],
        out_shape=[
            jax.ShapeDtypeStruct((t, d), BF16),
            jax.ShapeDtypeStruct((t, nzx - d), BF16),
            jax.ShapeDtypeStruct((t, LANES), F32),
            jax.ShapeDtypeStruct((t, d), BF16),
            jax.ShapeDtypeStruct((t, d), BF16),
        ],
        compiler_params=_cparams(("arbitrary",)),
        name="in_proj",
    )(x2, g, wzx, wdt_hi, wdt_lo, wxr, wgr)


def _shift_matrix(n):
    return jnp.concatenate([jnp.eye(n, k=-(CONV_K - 1 - j), dtype=BF16) for j in range(CONV_K - 1)], axis=0)


def _causal_conv(halo_ref, x, shift_ref, w_ref, b_ref, first):
    n = x.shape[0]
    pad = SUBLANES
    k1 = CONV_K - 1

    @pl.when(first)
    def _():
        halo_ref[...] = jnp.zeros(halo_ref.shape, F32)

    xf = x.astype(F32)
    taps = _dot(shift_ref[...], x)
    acc = b_ref[...] + w_ref[k1:k1 + 1, :] * xf
    for j in range(k1):
        acc = acc + w_ref[j:j + 1, :] * taps[j * n:(j + 1) * n]
    head = acc[:pad]
    for j in range(k1):
        head = head + w_ref[j:j + 1, :] * halo_ref[pad - k1 + j:2 * pad - k1 + j, :]
    halo_ref[0:pad, :] = xf[n - pad:n]
    return jnp.concatenate([head, acc[pad:]], axis=0)


def _ssd_kernel(xbc_ref, dt_ref, z_ref, sh_ref, cw_ref, cb_ref, dtb_ref, alog_ref, dskip_ref, gn_ref, e_ref,
                y_ref, halo_ref, st_ref):
    n = xbc_ref.shape[1]
    w = z_ref.shape[2]
    gw = w // SSD_GROUPS
    first = pl.program_id(1) == 0

    @pl.when(first)
    def _():
        st_ref[...] = jnp.zeros(st_ref.shape, F32)

    conv = _causal_conv(halo_ref, xbc_ref[0], sh_ref, cw_ref, cb_ref, first)
    xc = conv * _sigmoid(conv)
    xs = xc[:, :w]

    dt = _softplus(dt_ref[0] + dtb_ref[...])
    a = -jnp.exp(alog_ref[...])
    da = dt * a
    ri = lax.broadcasted_iota(jnp.int32, (n, n), 0)
    ci = lax.broadcasted_iota(jnp.int32, (n, n), 1)
    causal = ri >= ci
    tril = jnp.where(causal, 1.0, 0.0).astype(BF16)
    a_cs = _dot01_left(tril, da)
    a_cs_t = a_cs.T

    e01 = e_ref[...]
    dt_x = _dot01_right(dt, e01)
    acs_x = _dot01_right(a_cs, e01)
    last_x = acs_x[n - 1:n, :]
    xdt = xs * dt_x
    xdt_b = xdt.astype(BF16)
    xdt_end = (xdt * jnp.exp(last_x - acs_x)).astype(BF16)
    exp_acs = jnp.exp(acs_x)
    chunk_decay = jnp.exp(last_x)
    lane = lax.broadcasted_iota(jnp.int32, (n, gw), 1)

    for g in range(SSD_GROUPS):
        lo = g * gw
        bg = xc[:, w + g * SSD_STATE:w + (g + 1) * SSD_STATE].astype(BF16)
        cg = xc[:, w + (SSD_GROUPS + g) * SSD_STATE:w + (SSD_GROUPS + g + 1) * SSD_STATE].astype(BF16)
        cb = _dot_nt(cg, bg)
        prev = st_ref[g]
        acc = _dot(cg, prev.astype(BF16)) * exp_acs[:, lo:lo + gw]
        new = lax.dot_general(bg, xdt_end[:, lo:lo + gw], (((0,), (0,)), ((), ())),
                              preferred_element_type=F32)
        st_ref[g] = chunk_decay[:, lo:lo + gw] * prev + new
        xg = xdt_b[:, lo:lo + gw]
        for k in range(SSD_HEADS // SSD_GROUPS):
            h = g * (SSD_HEADS // SSD_GROUPS) + k
            seg = a_cs[:, h:h + 1] - a_cs_t[h:h + 1, :]
            dec = jnp.exp(jnp.where(causal, seg, -jnp.inf))
            m = (cb * dec).astype(BF16)
            in_head = (lane >= k * SSD_HEAD_DIM) & (lane < (k + 1) * SSD_HEAD_DIM)
            acc = acc + _dot(m, jnp.where(in_head, xg, jnp.zeros_like(xg)))
        yg = acc + xs[:, lo:lo + gw] * dskip_ref[:, lo:lo + gw]
        zg = z_ref[0, :, lo:lo + gw].astype(F32)
        u = yg * (zg * _sigmoid(zg))
        u = u * lax.rsqrt(jnp.mean(u * u, axis=-1, keepdims=True) + NORM_EPS)
        y_ref[0, :, lo:lo + gw] = (u * gn_ref[:, lo:lo + gw]).astype(BF16)


def _ssd(xbc, dt, z, shift, cw, cb, dtb, alog, dskip, gn, e01):
    b, s, cdim = xbc.shape
    w = z.shape[2]
    n = SSD_CHUNK
    tile = lambda i, j: (i, j, 0)
    const = lambda i, j: (0, 0)
    return pl.pallas_call(
        _ssd_kernel,
        grid=(b, s // n),
        in_specs=[
            pl.BlockSpec((1, n, cdim), tile),
            pl.BlockSpec((1, n, LANES), tile),
            pl.BlockSpec((1, n, w), tile),
            pl.BlockSpec(shift.shape, const),
            pl.BlockSpec((CONV_K, cdim), const),
            pl.BlockSpec((1, cdim), const),
            pl.BlockSpec((1, LANES), const),
            pl.BlockSpec((1, LANES), const),
            pl.BlockSpec((1, w), const),
            pl.BlockSpec((1, w), const),
            pl.BlockSpec((LANES, w), const),
        ],
        out_specs=pl.BlockSpec((1, n, w), tile),
        out_shape=jax.ShapeDtypeStruct((b, s, w), BF16),
        scratch_shapes=[
            pltpu.VMEM((2 * SUBLANES, cdim), F32),
            pltpu.VMEM((SSD_GROUPS, SSD_STATE, w // SSD_GROUPS), F32),
        ],
        compiler_params=_cparams(("arbitrary", "arbitrary")),
        name="ssd",
    )(xbc, dt, z, shift, cw, cb, dtb, alog, dskip, gn, e01)


def _gelu_tanh(x):
    c = 0.7978845608028654
    return 0.5 * x * (1.0 + jnp.tanh(c * (x + 0.044715 * (x * x * x))))


def _lru_kernel(xr_ref, gr_ref, sh_ref, cw_ref, cb_ref, wa_ref, ba_ref, wx_ref, bx_ref, lam_ref,
                y_ref, halo_ref, car_ref, h_ref):
    n = xr_ref.shape[1]
    w = xr_ref.shape[2]
    first = pl.program_id(1) == 0

    @pl.when(first)
    def _():
        car_ref[...] = jnp.zeros(car_ref.shape, F32)

    xc = _causal_conv(halo_ref, xr_ref[0], sh_ref, cw_ref, cb_ref, first)
    xb = xc.astype(BF16)
    nq = wa_ref.shape[0]
    qw = w // nq
    r_parts, i_parts = [], []
    for q in range(nq):
        xq = xb[:, q * qw:(q + 1) * qw]
        r_parts.append(_dot(xq, wa_ref[q]))
        i_parts.append(_dot(xq, wx_ref[q]))
    r = _sigmoid(jnp.concatenate(r_parts, axis=1) + ba_ref[...])
    gi = _sigmoid(jnp.concatenate(i_parts, axis=1) + bx_ref[...])
    log_a = (-RG_C) * r * _softplus(-lam_ref[...])
    a = jnp.exp(log_a)
    u = xc * gi * jnp.sqrt(1.0 - a * a)

    groups = n // SUBLANES
    sub = lax.broadcasted_iota(jnp.int32, (groups, SUBLANES, w), 1)
    ap = a.reshape(groups, SUBLANES, w)
    bp = u.reshape(groups, SUBLANES, w)
    for d in (1, 2, 4):
        m = sub >= d
        bp = jnp.where(m, ap * pltpu.roll(bp, d, 1) + bp, bp)
        ap = jnp.where(m, ap * pltpu.roll(ap, d, 1), ap)
    carry = car_ref[...]
    for g in range(groups):
        hb = bp[g] + ap[g] * carry
        h_ref[g * SUBLANES:(g + 1) * SUBLANES, :] = hb
        carry = jnp.broadcast_to(hb[SUBLANES - 1:SUBLANES, :], (SUBLANES, w))
    car_ref[...] = carry
    y_ref[0] = (h_ref[...] * _gelu_tanh(gr_ref[0].astype(F32))).astype(BF16)


def _lru(xr, gr, shift, cw, cb, wa, ba, wx, bx, lam):
    b, s, w = xr.shape
    n = LRU_TILE
    nq, qw, _ = wa.shape
    tile = lambda i, j: (i, j, 0)
    const = lambda i, j: (0, 0)
    const3 = lambda i, j: (0, 0, 0)
    return pl.pallas_call(
        _lru_kernel,
        grid=(b, s // n),
        in_specs=[
            pl.BlockSpec((1, n, w), tile),
            pl.BlockSpec((1, n, w), tile),
            pl.BlockSpec(shift.shape, const),
            pl.BlockSpec((CONV_K, w), const),
            pl.BlockSpec((1, w), const),
            pl.BlockSpec((nq, qw, qw), const3),
            pl.BlockSpec((1, w), const),
            pl.BlockSpec((nq, qw, qw), const3),
            pl.BlockSpec((1, w), const),
            pl.BlockSpec((1, w), const),
        ],
        out_specs=pl.BlockSpec((1, n, w), tile),
        out_shape=jax.ShapeDtypeStruct((b, s, w), BF16),
        scratch_shapes=[
            pltpu.VMEM((2 * SUBLANES, w), F32),
            pltpu.VMEM((SUBLANES, w), F32),
            pltpu.VMEM((n, w), F32),
        ],
        compiler_params=_cparams(("arbitrary", "arbitrary")),
        name="lru",
    )(xr, gr, shift, cw, cb, wa, ba, wx, bx, lam)


def _kv_kernel(m_ref, g_ref, w_ref, k_ref, v_ref):
    d = m_ref.shape[1]
    mn = _rms(m_ref[...], g_ref[...]).astype(BF16)
    k_ref[...] = _dot(mn, w_ref[:, :d]).astype(BF16)
    v_ref[...] = _dot(mn, w_ref[:, d:]).astype(BF16)


def _kv(mem2, g, wkv):
    t, d = mem2.shape
    tm = min(t, 512)
    row = lambda i: (i, 0)
    const = lambda i: (0, 0)
    return pl.pallas_call(
        _kv_kernel,
        grid=(t // tm,),
        in_specs=[pl.BlockSpec((tm, d), row), pl.BlockSpec((1, d), const), pl.BlockSpec((d, 2 * d), const)],
        out_specs=[pl.BlockSpec((tm, d), row), pl.BlockSpec((tm, d), row)],
        out_shape=[jax.ShapeDtypeStruct((t, d), BF16), jax.ShapeDtypeStruct((t, d), BF16)],
        compiler_params=_cparams(("arbitrary",)),
        name="kv",
    )(mem2, g, wkv)


def _mid_kernel(x_ref, ys_ref, yl_ref, k_ref, v_ref, wo1_ref, wo2_ref, gx_ref, wq_ref, wo_ref,
                gm_ref, wrh_ref, wrl_ref, br_ref,
                h_ref, hn_ref, idx_ref, rank_ref, gate_ref, cnt_ref, car_ref):
    tm, d = x_ref.shape
    hd = d // X_HEADS

    @pl.when(pl.program_id(0) == 0)
    def _():
        car_ref[...] = jnp.zeros(car_ref.shape, F32)

    h1 = x_ref[...] + _dot(ys_ref[...], wo1_ref[...]) + _dot(yl_ref[...], wo2_ref[...])

    q = _dot(_rms(h1, gx_ref[...]).astype(BF16), wq_ref[...]).astype(BF16)
    o_parts = []
    for hh in range(X_HEADS):
        sl = slice(hh * hd, (hh + 1) * hd)
        sc = _dot_nt(q[:, sl], k_ref[0, :, sl]) * (hd ** -0.5)
        sc = sc - jnp.max(sc, axis=-1, keepdims=True)
        p = jnp.exp(sc)
        p = p / jnp.sum(p, axis=-1, keepdims=True)
        o_parts.append(_dot(p.astype(BF16), v_ref[0, :, sl]))
    o = jnp.concatenate(o_parts, axis=1).astype(BF16)
    h2 = h1 + _dot(o, wo_ref[...])
    h_ref[...] = h2

    hn = _rms(h2, gm_ref[...])
    _store_token_major(hn_ref, hn)
    logits = _dot_hilo(hn, wrh_ref[...], wrl_ref[...]) + br_ref[...]

    l = logits.T[:N_EXPERTS, :]
    row = lax.broadcasted_iota(jnp.int32, (N_EXPERTS, tm), 0)
    picked = jnp.zeros((N_EXPERTS, tm), F32)
    vals, idxs = [], []
    for _ in range(TOP_K):
        m = jnp.max(l, axis=0, keepdims=True)
        idx = jnp.min(jnp.where(l == m, row, N_EXPERTS), axis=0, keepdims=True)
        sel = row == idx
        vals.append(m)
        idxs.append(idx)
        picked = jnp.where(sel, 1.0, picked)
        l = jnp.where(sel, -jnp.inf, l)
    ex = [jnp.exp(v - vals[0]) for v in vals]
    den = ex[0] + ex[1] + ex[2] + ex[3]

    ri = lax.broadcasted_iota(jnp.int32, (tm, tm), 0)
    ci = lax.broadcasted_iota(jnp.int32, (tm, tm), 1)
    earlier = jnp.where(ri < ci, 1.0, 0.0).astype(BF16)
    before = _dot(picked.astype(BF16), earlier) + car_ref[:, 0:1]
    out_row = lax.broadcasted_iota(jnp.int32, (SUBLANES, tm), 0)
    idx_out = jnp.zeros((SUBLANES, tm), jnp.int32)
    rank_out = jnp.zeros((SUBLANES, tm), jnp.int32)
    gate_out = jnp.zeros((SUBLANES, tm), F32)
    for k in range(TOP_K):
        rk = jnp.sum(jnp.where(row == idxs[k], before, 0.0), axis=0, keepdims=True)
        at_k = out_row == k
        idx_out = jnp.where(at_k, idxs[k], idx_out)
        rank_out = jnp.where(at_k, rk.astype(jnp.int32), rank_out)
        gate_out = jnp.where(at_k, ex[k] / den, gate_out)
    idx_ref[...] = idx_out
    rank_ref[...] = rank_out
    gate_ref[...] = gate_out
    total = car_ref[...] + jnp.sum(picked, axis=1, keepdims=True)
    car_ref[...] = total
    cnt_ref[...] = total.astype(jnp.int32)


def _mid(x2, ys, yl, kk, vv, wo1, wo2, gx, wq, wo, gm, wr_hi, wr_lo, br, seq):
    t, d = x2.shape
    tm = MID_TILE
    m = kk.shape[1]
    per_b = seq // tm
    row = lambda i: (i, 0)
    col = lambda i: (0, i)
    const = lambda i: (0, 0)
    kvmap = lambda i: (i // per_b, 0, 0)
    wspec = pl.BlockSpec((d, d), const)
    vspec = pl.BlockSpec((1, d), const)
    return pl.pallas_call(
        _mid_kernel,
        grid=(t // tm,),
        in_specs=[
            pl.BlockSpec((tm, d), row), pl.BlockSpec((tm, d), row), pl.BlockSpec((tm, d), row),
            pl.BlockSpec((1, m, d), kvmap), pl.BlockSpec((1, m, d), kvmap),
            wspec, wspec, vspec, wspec, wspec, vspec,
            pl.BlockSpec((d, LANES), const), pl.BlockSpec((d, LANES), const), pl.BlockSpec((1, LANES), const),
        ],
        out_specs=[
            pl.BlockSpec((tm, d), row), pl.BlockSpec((tm * d // LANES, LANES), row),
            pl.BlockSpec((SUBLANES, tm), col), pl.BlockSpec((SUBLANES, tm), col), pl.BlockSpec((SUBLANES, tm), col),
            pl.BlockSpec((N_EXPERTS, LANES), const),
        ],
        out_shape=[
            jax.ShapeDtypeStruct((t, d), F32), jax.ShapeDtypeStruct((t * d // LANES, LANES), F32),
            jax.ShapeDtypeStruct((SUBLANES, t), jnp.int32), jax.ShapeDtypeStruct((SUBLANES, t), jnp.int32),
            jax.ShapeDtypeStruct((SUBLANES, t), F32),
            jax.ShapeDtypeStruct((N_EXPERTS, LANES), jnp.int32),
        ],
        scratch_shapes=[pltpu.VMEM((N_EXPERTS, LANES), F32)],
        compiler_params=_cparams(("arbitrary",)),
        name="mid",
    )(x2, ys, yl, kk, vv, wo1, wo2, gx, wq, wo, gm, wr_hi, wr_lo, br)


def _dispatch_kernel(pend_ref, dst_ref, hn_ref, x_hbm, zero_ref, sem, *, parts):
    i = pl.program_id(0)
    tc = hn_ref.shape[0] // parts
    bm = zero_ref.shape[0] // parts
    rows = TOP_K * tc

    def tile(ref, n, count=1):
        return ref.at[pl.ds(pl.multiple_of(n * parts, parts), count * parts), :]

    @pl.when(i == 0)
    def _():
        zero_ref[...] = jnp.zeros(zero_ref.shape, F32)

        def fill(start):
            return pltpu.make_async_copy(zero_ref, tile(x_hbm, start, bm), sem.at[1])

        for e in range(N_EXPERTS):
            fill(jnp.maximum(pend_ref[e] - bm, 0)).start()
        for e in range(N_EXPERTS):
            fill(0).wait()

        def tail(j, c):
            fill(j * bm).start()
            fill(0).wait()
            return c

        lax.fori_loop(pend_ref[N_EXPERTS - 1] // bm, x_hbm.shape[0] // (bm * parts), tail, 0)

    def body(j, c):
        t0 = pl.multiple_of(j * ISSUE_UNROLL, ISSUE_UNROLL)
        for u in range(ISSUE_UNROLL):
            for k in range(TOP_K):
                dst = dst_ref[0, 0, k * tc + t0 + u]
                pltpu.make_async_copy(tile(hn_ref, t0 + u), tile(x_hbm, dst), sem.at[0]).start(priority=k % 2)
        return c

    lax.fori_loop(0, tc // ISSUE_UNROLL, body, 0)
    pltpu.make_async_copy(tile(x_hbm, 0, rows), tile(x_hbm, 0, rows), sem.at[0]).wait()


def _dispatch(pend, dest3, hn_tm, n_rows, d):
    parts = d // LANES
    t = hn_tm.shape[0] // parts
    tc = DISP_TILE
    grid_spec = pltpu.PrefetchScalarGridSpec(
        num_scalar_prefetch=1,
        grid=(t // tc,),
        in_specs=[
            pl.BlockSpec((1, 1, TOP_K * tc), lambda i, pe: (i, 0, 0), memory_space=pltpu.SMEM),
            pl.BlockSpec((tc * parts, LANES), lambda i, pe: (i, 0)),
        ],
        out_specs=pl.BlockSpec(memory_space=pl.ANY),
        scratch_shapes=[pltpu.VMEM((MOE_BLOCK * parts, LANES), F32), pltpu.SemaphoreType.DMA((2,))],
    )
    return pl.pallas_call(
        functools.partial(_dispatch_kernel, parts=parts),
        grid_spec=grid_spec,
        out_shape=jax.ShapeDtypeStruct((n_rows * parts, LANES), F32),
        compiler_params=_cparams(("arbitrary",)),
        name="dispatch",
    )(pend, dest3, hn_tm)


def _moe_kernel(be_ref, nxt_ref, nvb_ref, x_ref, wgu_hbm, bg_ref, bu_ref, wd_hbm, bd_ref, perm_ref,
                y_ref, wgu_f, wd_f, wgu_s, wd_s, act_s, sem):
    i = pl.program_id(0)
    nvb = nvb_ref[0]
    f, d = wd_f.shape
    bm = act_s.shape[0]
    pw = perm_ref.shape[0]
    half = pw // 2

    def fetch(e):
        return (pltpu.make_async_copy(wgu_hbm.at[e], wgu_f, sem.at[0]),
                pltpu.make_async_copy(wd_hbm.at[e], wd_f, sem.at[1]))

    @pl.when(jnp.logical_and(i == 0, nvb > 0))
    def _():
        for c in fetch(be_ref[0]):
            c.start()

    changed = jnp.logical_or(i == 0, be_ref[i] != be_ref[jnp.maximum(i - 1, 0)])

    @pl.when(jnp.logical_and(changed, i < nvb))
    def _():
        for c in fetch(be_ref[i]):
            c.wait()
        for c in range(2 * f // pw):
            wc = wgu_f[:, c * pw:(c + 1) * pw].astype(BF16)
            wgu_s[:, c * pw:(c + 1) * pw] = _dot(wc, perm_ref[...]).astype(BF16)
        wd_s[...] = wd_f[...].astype(BF16)

        @pl.when(nxt_ref[i] >= 0)
        def _():
            for c in fetch(nxt_ref[i]):
                c.start()

    @pl.when(i < nvb)
    def _():
        e = be_ref[i]
        xb = jnp.concatenate([p.astype(BF16) for p in _load_token_major(x_ref, bm, d)], axis=1)
        for c in range(2 * f // pw):
            gu = _dot(xb, wgu_s[:, c * pw:(c + 1) * pw])
            g = gu[:, :half] + bg_ref[e, :, c * half:(c + 1) * half]
            u = gu[:, half:] + bu_ref[e, :, c * half:(c + 1) * half]
            g = jnp.minimum(g, SWIGLU_LIMIT)
            u = jnp.clip(u, -SWIGLU_LIMIT, SWIGLU_LIMIT)
            act = (u + 1.0) * (g * _sigmoid(SWIGLU_ALPHA * g))
            act_s[:, c * half:(c + 1) * half] = act.astype(BF16)
        _store_token_major(y_ref, _dot(act_s[...], wd_s[...]) + bd_ref[e])

    @pl.when(i >= nvb)
    def _():
        y_ref[...] = jnp.zeros(y_ref.shape, F32)


def _moe(block_e, next_e, nvb, xrows_tm, wgu, bg, bu, wd, bd, perm):
    f, d = wd.shape[1], wd.shape[2]
    parts = d // LANES
    bm = MOE_BLOCK
    nb = xrows_tm.shape[0] // (bm * parts)

    def whole(a):
        return pl.BlockSpec(a.shape, lambda i, be, nx, nv, nd=a.ndim: (0,) * nd)

    grid_spec = pltpu.PrefetchScalarGridSpec(
        num_scalar_prefetch=3,
        grid=(nb,),
        in_specs=[
            pl.BlockSpec((bm * parts, LANES),
                         lambda i, be, nx, nv: (jnp.minimum(i, jnp.maximum(nv[0] - 1, 0)), 0)),
            pl.BlockSpec(memory_space=pl.ANY),
            whole(bg), whole(bu),
            pl.BlockSpec(memory_space=pl.ANY),
            whole(bd), whole(perm),
        ],
        out_specs=pl.BlockSpec((bm * parts, LANES), lambda i, be, nx, nv: (i, 0)),
        scratch_shapes=[
            pltpu.VMEM((d, 2 * f), F32), pltpu.VMEM((f, d), F32),
            pltpu.VMEM((d, 2 * f), BF16), pltpu.VMEM((f, d), BF16),
            pltpu.VMEM((bm, f), BF16),
            pltpu.SemaphoreType.DMA((2,)),
        ],
    )
    return pl.pallas_call(
        _moe_kernel,
        grid_spec=grid_spec,
        out_shape=jax.ShapeDtypeStruct(xrows_tm.shape, F32),
        compiler_params=_cparams(("arbitrary",)),
        name="moe",
    )(block_e, next_e, nvb, xrows_tm, wgu, bg, bu, wd, bd, perm)


def _comb_kernel(dst_ref, dstn_ref, h_ref, gate_ref, y_hbm, g_ref, o_ref, ybuf, sem, *, parts):
    i = pl.program_id(0)
    n = pl.num_programs(0)
    tc, d = h_ref.shape
    rows = TOP_K * tc
    slot = i % 2

    def tile(ref, n, count=1):
        return ref.at[pl.ds(pl.multiple_of(n * parts, parts), count * parts), :]

    def start_rows(idx_ref, s):
        def body(j, c):
            t0 = pl.multiple_of(j * ISSUE_UNROLL, ISSUE_UNROLL)
            for u in range(ISSUE_UNROLL):
                for k in range(TOP_K):
                    r = k * tc + t0 + u
                    pltpu.make_async_copy(tile(y_hbm, idx_ref[0, 0, r]), tile(ybuf.at[s], r),
                                          sem.at[s]).start(priority=k % 2)
            return c

        lax.fori_loop(0, tc // ISSUE_UNROLL, body, 0)

    @pl.when(i == 0)
    def _():
        start_rows(dst_ref, 0)

    @pl.when(i + 1 < n)
    def _():
        start_rows(dstn_ref, 1 - slot)

    pltpu.make_async_copy(tile(y_hbm, 0, rows), ybuf.at[slot], sem.at[slot]).wait()
    acc = [h_ref[:, s * LANES:(s + 1) * LANES] for s in range(parts)]
    for k in range(TOP_K):
        yk = _load_token_major(ybuf.at[slot], tc, d, base=k * tc * parts)
        gk = gate_ref[:, k:k + 1]
        acc = [a + p * gk for a, p in zip(acc, yk)]
    o_ref[...] = _rms(jnp.concatenate(acc, axis=1), g_ref[...])


def _combine(dest3, h2, gate, y_tm, g):
    t, d = h2.shape
    parts = d // LANES
    tc = COMB_TILE
    nt = t // tc
    return pl.pallas_call(
        functools.partial(_comb_kernel, parts=parts),
        grid=(nt,),
        in_specs=[
            pl.BlockSpec((1, 1, TOP_K * tc), lambda i: (i, 0, 0), memory_space=pltpu.SMEM),
            pl.BlockSpec((1, 1, TOP_K * tc), lambda i: (jnp.minimum(i + 1, nt - 1), 0, 0),
                         memory_space=pltpu.SMEM),
            pl.BlockSpec((tc, d), lambda i: (i, 0)),
            pl.BlockSpec((tc, SUBLANES), lambda i: (i, 0)),
            pl.BlockSpec(memory_space=pl.ANY),
            pl.BlockSpec((1, d), lambda i: (0, 0)),
        ],
        out_specs=pl.BlockSpec((tc, d), lambda i: (i, 0)),
        out_shape=jax.ShapeDtypeStruct((t, d), F32),
        scratch_shapes=[pltpu.VMEM((2, TOP_K * tc * parts, LANES), F32), pltpu.SemaphoreType.DMA((2,))],
        compiler_params=_cparams(("arbitrary",)),
        name="combine",
    )(dest3, dest3, h2, gate, y_tm, g)


def _block_diag(wb, per):
    nb, bw, _ = wb.shape
    wq = wb.reshape(nb // per, per, bw, bw)
    eye = jnp.eye(per, dtype=wb.dtype)
    out = jnp.einsum('qaij,ab->qaibj', wq, eye)
    return out.reshape(nb // per, per * bw, per * bw)


def _pad_lanes(v, fill=0.0):
    return jnp.pad(v, (0, LANES - v.shape[0]), constant_values=fill).reshape(1, LANES)


def kernel(x, mem, norm_mix, w_in, ssd_conv_w, ssd_conv_b, ssd_dt_bias, ssd_a_log, ssd_d, ssd_norm, lru_conv_w, lru_conv_b, lru_wa, lru_ba, lru_wx, lru_bx, lru_lambda, w_out, norm_xattn, norm_mem, w_q, w_kv, w_o, norm_moe, w_router, b_router, w_gate_up, b_gate_up, w_down, b_down, norm_final):
    b, s, d = x.shape
    t = b * s
    n_mem = mem.shape[1]
    w = d
    cdim = w + 2 * SSD_GROUPS * SSD_STATE
    o1, o2, o3, o4 = w, w + cdim, w + cdim + SSD_HEADS, w + cdim + SSD_HEADS + w

    wi = w_in[0]
    wzx = wi[:, :o2].astype(BF16)
    wdt = jnp.pad(wi[:, o2:o3], ((0, 0), (0, LANES - SSD_HEADS)))
    wdt_hi, wdt_lo = _hilo(wdt)
    wxr = wi[:, o3:o4].astype(BF16)
    wgr = wi[:, o4:].astype(BF16)
    e01 = (jnp.arange(LANES)[:, None] == (jnp.arange(w)[None, :] // SSD_HEAD_DIM)).astype(BF16)
    dskip = jnp.repeat(ssd_d[0], SSD_HEAD_DIM).reshape(1, w)
    per = MXU_DIM // (w // LRU_BLOCKS)
    wa_bd = _block_diag(lru_wa[0], per).astype(BF16)
    wx_bd = _block_diag(lru_wx[0], per).astype(BF16)
    wr = jnp.pad(w_router[0], ((0, 0), (0, LANES - N_EXPERTS)))
    wr_hi, wr_lo = _hilo(wr)
    br = _pad_lanes(b_router[0], fill=-1e30)
    bgu = b_gate_up[0]
    bg = bgu[:, None, 0::2]
    bu = bgu[:, None, 1::2]
    bd = b_down[0][:, None, :]
    pw = 2 * LANES
    col = jnp.arange(pw)
    src_col = jnp.where(col < LANES, 2 * col, 2 * (col - LANES) + 1)
    perm = (jnp.arange(pw)[:, None] == src_col[None, :]).astype(BF16)

    x2 = x.reshape(t, d)
    z, xbc, dt, xr, gr = _in_proj(x2, norm_mix[0].reshape(1, d), wzx, wdt_hi, wdt_lo, wxr, wgr)

    y_ssd = _ssd(xbc.reshape(b, s, cdim), dt.reshape(b, s, LANES), z.reshape(b, s, w),
                 _shift_matrix(SSD_CHUNK), ssd_conv_w[0], ssd_conv_b[0].reshape(1, cdim), _pad_lanes(ssd_dt_bias[0]),
                 _pad_lanes(ssd_a_log[0]), dskip, ssd_norm[0].reshape(1, w), e01)
    y_lru = _lru(xr.reshape(b, s, w), gr.reshape(b, s, w), _shift_matrix(LRU_TILE), lru_conv_w[0], lru_conv_b[0].reshape(1, w),
                 wa_bd, lru_ba[0].reshape(1, w), wx_bd, lru_bx[0].reshape(1, w), lru_lambda[0].reshape(1, w))

    kk, vv = _kv(mem.reshape(b * n_mem, d), norm_mem[0].reshape(1, d), w_kv[0].astype(BF16))
    wo_mix = w_out[0].astype(BF16)
    h2, hn, idx_m, rank_m, gate_m, cnt = _mid(
        x2, y_ssd.reshape(t, w), y_lru.reshape(t, w), kk.reshape(b, n_mem, d), vv.reshape(b, n_mem, d),
        wo_mix[:w], wo_mix[w:], norm_xattn[0].reshape(1, d), w_q[0].astype(BF16), w_o[0].astype(BF16),
        norm_moe[0].reshape(1, d), wr_hi, wr_lo, br, s)

    bm = MOE_BLOCK
    counts = cnt[:, 0]
    pcounts = (counts + bm - 1) // bm * bm
    pend = jnp.cumsum(pcounts).astype(jnp.int32)
    pstart = pend - pcounts
    idx = idx_m[:TOP_K, :]
    onehot = idx[:, :, None] == jnp.arange(N_EXPERTS, dtype=jnp.int32)[None, None, :]
    dest = jnp.sum(jnp.where(onehot, pstart[None, None, :], 0), axis=-1) + rank_m[:TOP_K, :]
    n_pairs = t * TOP_K
    nb = (n_pairs + N_EXPERTS * (bm - 1) + bm - 1) // bm
    blk0 = jnp.arange(nb, dtype=jnp.int32) * bm
    block_e = jnp.minimum(jnp.sum(pend[None, :] <= blk0[:, None], axis=1), N_EXPERTS - 1).astype(jnp.int32)
    nvb = (pend[-1] // bm).astype(jnp.int32).reshape(1)
    after = pend[block_e] // bm
    next_e = jnp.where(after < nvb[0], block_e[jnp.minimum(after, nb - 1)], -1).astype(jnp.int32)
    dest = dest.astype(jnp.int32)

    def by_tile(tc):
        return dest.reshape(TOP_K, t // tc, tc).transpose(1, 0, 2).reshape(t // tc, 1, TOP_K * tc)

    xrows = _dispatch(pend, by_tile(DISP_TILE), hn, nb * bm, d)
    y = _moe(block_e, next_e, nvb, xrows, w_gate_up[0], bg, bu, w_down[0], bd, perm)
    out = _combine(by_tile(COMB_TILE), h2, gate_m.T, y, norm_final.reshape(1, d))
    return out.reshape(b, s, d)
```

```python
import functools

import jax
import jax.numpy as jnp
from jax import lax
from jax.experimental import pallas as pl
from jax.experimental.pallas import tpu as pltpu

F32 = jnp.float32
BF16 = jnp.bfloat16

NORM_EPS = 1e-6
LANES = 128
SUBLANES = 8
MXU_DIM = 256
SSD_HEAD_DIM = 64
SSD_HEADS = 16
SSD_GROUPS = 4
SSD_STATE = 128
CONV_K = 4
LRU_BLOCKS = 16
RG_C = 8.0
X_HEADS = 4
N_EXPERTS = 32
TOP_K = 4
SWIGLU_LIMIT = 7.0
SWIGLU_ALPHA = 1.702

VMEM_LIMIT = 56 * 1024 * 1024

IN_TILE = 1024
SSD_CHUNK = 256
LRU_TILE = 256
MID_TILE = 512
MOE_BLOCK = 512
DISP_TILE = 2048
COMB_TILE = 256
ISSUE_UNROLL = 8


def _cparams(sem):
    return pltpu.CompilerParams(dimension_semantics=sem, vmem_limit_bytes=VMEM_LIMIT)


def _rms(x, g):
    ms = jnp.mean(x * x, axis=-1, keepdims=True)
    return x * lax.rsqrt(ms + NORM_EPS) * g


def _sigmoid(x):
    return 0.5 * jnp.tanh(0.5 * x) + 0.5


def _softplus(x):
    return jnp.maximum(x, 0.0) + jnp.log(1.0 + jnp.exp(-jnp.abs(x)))


def _split3(x):
    a = x.astype(BF16)
    r = x - a.astype(F32)
    b = r.astype(BF16)
    c = (r - b.astype(F32)).astype(BF16)
    return a, b, c


def _dot(a, b):
    return jnp.dot(a, b, preferred_element_type=F32)


def _dot_nt(a, b):
    return lax.dot_general(a, b, (((1,), (1,)), ((), ())), preferred_element_type=F32)


def _dot01_right(x, m01):
    a, b, c = _split3(x)
    return _dot(a, m01) + _dot(b, m01) + _dot(c, m01)


def _dot01_left(m01, x):
    a, b, c = _split3(x)
    return _dot(m01, a) + _dot(m01, b) + _dot(m01, c)


def _dot_hilo(x, w_hi, w_lo):
    xh = x.astype(BF16)
    xl = (x - xh.astype(F32)).astype(BF16)
    return _dot(xh, w_hi) + _dot(xl, w_hi) + _dot(xh, w_lo)


def _hilo(w):
    hi = w.astype(BF16)
    lo = (w - hi.astype(F32)).astype(BF16)
    return hi, lo


def _store_token_major(ref, val, base=0):
    n, d = val.shape
    parts = d // LANES
    for s in range(parts):
        ref[pl.ds(base + s, n, stride=parts), :] = val[:, s * LANES:(s + 1) * LANES]


def _load_token_major(ref, n, d, base=0):
    parts = d // LANES
    return [ref[pl.ds(base + s, n, stride=parts), :] for s in range(parts)]


def _in_proj_kernel(x_ref, g_ref, wzx_ref, wdthl_ref, wdth_ref, wxr_ref, wgr_ref,
                    z_ref, xbc_ref, dt_ref, xr_ref, gr_ref):
    d = x_ref.shape[1]
    hn = _rms(x_ref[...], g_ref[...])
    hb = hn.astype(BF16)
    z_ref[...] = _dot(hb, wzx_ref[:, :d]).astype(BF16)
    xbc_ref[...] = _dot(hb, wzx_ref[:, d:]).astype(BF16)
    hl = _dot(hb, wdthl_ref[...])
    hn_lo = (hn - hb.astype(F32)).astype(BF16)
    dt_ref[...] = hl[:, :LANES] + hl[:, LANES:] + _dot(hn_lo, wdth_ref[...])
    xr_ref[...] = _dot(hb, wxr_ref[...]).astype(BF16)
    gr_ref[...] = _dot(hb, wgr_ref[...]).astype(BF16)


def _in_proj(x2, g, wzx, wdt_hilo, wdt_hi, wxr, wgr):
    t, d = x2.shape
    tm = IN_TILE
    nzx = wzx.shape[1]
    const = lambda i: (0, 0)
    row = lambda i: (i, 0)
    return pl.pallas_call(
        _in_proj_kernel,
        grid=(t // tm,),
        in_specs=[
            pl.BlockSpec((tm, d), row),
            pl.BlockSpec((1, d), const),
            pl.BlockSpec((d, nzx), const),
            pl.BlockSpec((d, 2 * LANES), const),
            pl.BlockSpec((d, LANES), const),
            pl.BlockSpec((d, d), const),
            pl.BlockSpec((d, d), const),
        ],
        out_specs=[
            pl.BlockSpec((tm, d), row),
            pl.BlockSpec((tm, nzx - d), row),
            pl.BlockSpec((tm, LANES), row),
            pl.BlockSpec((tm, d), row),
            pl.BlockSpec((tm, d), row),
        ],
        out_shape=[
            jax.ShapeDtypeStruct((t, d), BF16),
            jax.ShapeDtypeStruct((t, nzx - d), BF16),
            jax.ShapeDtypeStruct((t, LANES), F32),
            jax.ShapeDtypeStruct((t, d), BF16),
            jax.ShapeDtypeStruct((t, d), BF16),
        ],
        compiler_params=_cparams(("arbitrary",)),
        name="in_proj",
    )(x2, g, wzx, wdt_hilo, wdt_hi, wxr, wgr)


def _shift_matrix(n):
    return jnp.concatenate([jnp.eye(n, k=-(CONV_K - 1 - j), dtype=BF16) for j in range(CONV_K - 1)], axis=0)


def _causal_conv(halo_ref, x, shift_ref, w_ref, b_ref, first):
    n = x.shape[0]
    pad = SUBLANES
    k1 = CONV_K - 1

    @pl.when(first)
    def _():
        halo_ref[...] = jnp.zeros(halo_ref.shape, F32)

    xf = x.astype(F32)
    taps = _dot(shift_ref[...], x)
    acc = b_ref[...] + w_ref[k1:k1 + 1, :] * xf
    for j in range(k1):
        acc = acc + w_ref[j:j + 1, :] * taps[j * n:(j + 1) * n]
    head = acc[:pad]
    for j in range(k1):
        head = head + w_ref[j:j + 1, :] * halo_ref[pad - k1 + j:2 * pad - k1 + j, :]
    halo_ref[0:pad, :] = xf[n - pad:n]
    return jnp.concatenate([head, acc[pad:]], axis=0)


def _ssd_kernel(xbc_ref, dt_ref, z_ref, sh_ref, cw_ref, cb_ref, dtb_ref, alog_ref, dskip_ref, gn_ref, e_ref,
                y_ref, halo_ref, st_ref):
    n = xbc_ref.shape[1]
    w = z_ref.shape[2]
    gw = w // SSD_GROUPS
    first = pl.program_id(1) == 0

    @pl.when(first)
    def _():
        st_ref[...] = jnp.zeros(st_ref.shape, F32)

    conv = _causal_conv(halo_ref, xbc_ref[0], sh_ref, cw_ref, cb_ref, first)
    xc = conv * _sigmoid(conv)
    xs = xc[:, :w]

    dt = _softplus(dt_ref[0] + dtb_ref[...])
    a = -jnp.exp(alog_ref[...])
    da = dt * a
    ri = lax.broadcasted_iota(jnp.int32, (n, n), 0)
    ci = lax.broadcasted_iota(jnp.int32, (n, n), 1)
    causal = ri >= ci
    tril = jnp.where(causal, 1.0, 0.0).astype(BF16)
    a_cs = _dot01_left(tril, da)
    a_cs_t = a_cs.T

    e01 = e_ref[...]
    dt_x = _dot01_right(dt, e01)
    acs_x = _dot01_right(a_cs, e01)
    last_x = acs_x[n - 1:n, :]
    xdt = xs * dt_x
    xdt_b = xdt.astype(BF16)
    xdt_end = (xdt * jnp.exp(last_x - acs_x)).astype(BF16)
    exp_acs = jnp.exp(acs_x)
    chunk_decay = jnp.exp(last_x)
    lane = lax.broadcasted_iota(jnp.int32, (n, gw), 1)

    for g in range(SSD_GROUPS):
        lo = g * gw
        bg = xc[:, w + g * SSD_STATE:w + (g + 1) * SSD_STATE].astype(BF16)
        cg = xc[:, w + (SSD_GROUPS + g) * SSD_STATE:w + (SSD_GROUPS + g + 1) * SSD_STATE].astype(BF16)
        cb = _dot_nt(cg, bg)
        prev = st_ref[g]
        acc = _dot(cg, prev.astype(BF16)) * exp_acs[:, lo:lo + gw]
        new = lax.dot_general(bg, xdt_end[:, lo:lo + gw], (((0,), (0,)), ((), ())),
                              preferred_element_type=F32)
        st_ref[g] = chunk_decay[:, lo:lo + gw] * prev + new
        xg = xdt_b[:, lo:lo + gw]
        for k in range(SSD_HEADS // SSD_GROUPS):
            h = g * (SSD_HEADS // SSD_GROUPS) + k
            seg = a_cs[:, h:h + 1] - a_cs_t[h:h + 1, :]
            dec = jnp.exp(jnp.where(causal, seg, -jnp.inf))
            m = (cb * dec).astype(BF16)
            in_head = (lane >= k * SSD_HEAD_DIM) & (lane < (k + 1) * SSD_HEAD_DIM)
            acc = acc + _dot(m, jnp.where(in_head, xg, jnp.zeros_like(xg)))
        yg = acc + xs[:, lo:lo + gw] * dskip_ref[:, lo:lo + gw]
        zg = z_ref[0, :, lo:lo + gw].astype(F32)
        u = yg * (zg * _sigmoid(zg))
        u = u * lax.rsqrt(jnp.mean(u * u, axis=-1, keepdims=True) + NORM_EPS)
        y_ref[0, :, lo:lo + gw] = (u * gn_ref[:, lo:lo + gw]).astype(BF16)


def _ssd(xbc, dt, z, shift, cw, cb, dtb, alog, dskip, gn, e01):
    b, s, cdim = xbc.shape
    w = z.shape[2]
    n = SSD_CHUNK
    tile = lambda i, j: (i, j, 0)
    const = lambda i, j: (0, 0)
    return pl.pallas_call(
        _ssd_kernel,
        grid=(b, s // n),
        in_specs=[
            pl.BlockSpec((1, n, cdim), tile),
            pl.BlockSpec((1, n, LANES), tile),
            pl.BlockSpec((1, n, w), tile),
            pl.BlockSpec(shift.shape, const),
            pl.BlockSpec((CONV_K, cdim), const),
            pl.BlockSpec((1, cdim), const),
            pl.BlockSpec((1, LANES), const),
            pl.BlockSpec((1, LANES), const),
            pl.BlockSpec((1, w), const),
            pl.BlockSpec((1, w), const),
            pl.BlockSpec((LANES, w), const),
        ],
        out_specs=pl.BlockSpec((1, n, w), tile),
        out_shape=jax.ShapeDtypeStruct((b, s, w), BF16),
        scratch_shapes=[
            pltpu.VMEM((2 * SUBLANES, cdim), F32),
            pltpu.VMEM((SSD_GROUPS, SSD_STATE, w // SSD_GROUPS), F32),
        ],
        compiler_params=_cparams(("arbitrary", "arbitrary")),
        name="ssd",
    )(xbc, dt, z, shift, cw, cb, dtb, alog, dskip, gn, e01)


def _gelu_tanh(x):
    c = 0.7978845608028654
    return 0.5 * x * (1.0 + jnp.tanh(c * (x + 0.044715 * (x * x * x))))


def _lru_kernel(xr_ref, gr_ref, sh_ref, cw_ref, cb_ref, wa_ref, ba_ref, wx_ref, bx_ref, lam_ref,
                y_ref, halo_ref, car_ref, h_ref):
    n = xr_ref.shape[1]
    w = xr_ref.shape[2]
    first = pl.program_id(1) == 0

    @pl.when(first)
    def _():
        car_ref[...] = jnp.zeros(car_ref.shape, F32)

    xc = _causal_conv(halo_ref, xr_ref[0], sh_ref, cw_ref, cb_ref, first)
    xb = xc.astype(BF16)
    nq = wa_ref.shape[0]
    qw = w // nq
    r_parts, i_parts = [], []
    for q in range(nq):
        xq = xb[:, q * qw:(q + 1) * qw]
        r_parts.append(_dot(xq, wa_ref[q]))
        i_parts.append(_dot(xq, wx_ref[q]))
    r = _sigmoid(jnp.concatenate(r_parts, axis=1) + ba_ref[...])
    gi = _sigmoid(jnp.concatenate(i_parts, axis=1) + bx_ref[...])
    log_a = (-RG_C) * r * _softplus(-lam_ref[...])
    a = jnp.exp(log_a)
    u = xc * gi * jnp.sqrt(1.0 - a * a)

    groups = n // SUBLANES
    sub = lax.broadcasted_iota(jnp.int32, (groups, SUBLANES, w), 1)
    ap = a.reshape(groups, SUBLANES, w)
    bp = u.reshape(groups, SUBLANES, w)
    for d in (1, 2, 4):
        m = sub >= d
        bp = jnp.where(m, ap * pltpu.roll(bp, d, 1) + bp, bp)
        ap = jnp.where(m, ap * pltpu.roll(ap, d, 1), ap)
    carry = car_ref[...]
    for g in range(groups):
        hb = bp[g] + ap[g] * carry
        h_ref[g * SUBLANES:(g + 1) * SUBLANES, :] = hb
        carry = jnp.broadcast_to(hb[SUBLANES - 1:SUBLANES, :], (SUBLANES, w))
    car_ref[...] = carry
    y_ref[0] = (h_ref[...] * _gelu_tanh(gr_ref[0].astype(F32))).astype(BF16)


def _lru(xr, gr, shift, cw, cb, wa, ba, wx, bx, lam):
    b, s, w = xr.shape
    n = LRU_TILE
    nq, qw, _ = wa.shape
    tile = lambda i, j: (i, j, 0)
    const = lambda i, j: (0, 0)
    const3 = lambda i, j: (0, 0, 0)
    return pl.pallas_call(
        _lru_kernel,
        grid=(b, s // n),
        in_specs=[
            pl.BlockSpec((1, n, w), tile),
            pl.BlockSpec((1, n, w), tile),
            pl.BlockSpec(shift.shape, const),
            pl.BlockSpec((CONV_K, w), const),
            pl.BlockSpec((1, w), const),
            pl.BlockSpec((nq, qw, qw), const3),
            pl.BlockSpec((1, w), const),
            pl.BlockSpec((nq, qw, qw), const3),
            pl.BlockSpec((1, w), const),
            pl.BlockSpec((1, w), const),
        ],
        out_specs=pl.BlockSpec((1, n, w), tile),
        out_shape=jax.ShapeDtypeStruct((b, s, w), BF16),
        scratch_shapes=[
            pltpu.VMEM((2 * SUBLANES, w), F32),
            pltpu.VMEM((SUBLANES, w), F32),
            pltpu.VMEM((n, w), F32),
        ],
        compiler_params=_cparams(("arbitrary", "arbitrary")),
        name="lru",
    )(xr, gr, shift, cw, cb, wa, ba, wx, bx, lam)


def _kv_kernel(m_ref, g_ref, w_ref, k_ref, v_ref):
    d = m_ref.shape[1]
    mn = _rms(m_ref[...], g_ref[...]).astype(BF16)
    k_ref[...] = _dot(mn, w_ref[:, :d]).astype(BF16)
    v_ref[...] = _dot(mn, w_ref[:, d:]).astype(BF16)


def _kv(mem2, g, wkv):
    t, d = mem2.shape
    tm = min(t, 512)
    row = lambda i: (i, 0)
    const = lambda i: (0, 0)
    return pl.pallas_call(
        _kv_kernel,
        grid=(t // tm,),
        in_specs=[pl.BlockSpec((tm, d), row), pl.BlockSpec((1, d), const), pl.BlockSpec((d, 2 * d), const)],
        out_specs=[pl.BlockSpec((tm, d), row), pl.BlockSpec((tm, d), row)],
        out_shape=[jax.ShapeDtypeStruct((t, d), BF16), jax.ShapeDtypeStruct((t, d), BF16)],
        compiler_params=_cparams(("arbitrary",)),
        name="kv",
    )(mem2, g, wkv)


def _mid_kernel(x_ref, ys_ref, yl_ref, k_ref, v_ref, wo1_ref, wo2_ref, gx_ref, wq_ref, wo_ref,
                gm_ref, wrh_ref, wrl_ref, br_ref,
                h_ref, hn_ref, idx_ref, rank_ref, gate_ref, cnt_ref, car_ref):
    tm, d = x_ref.shape
    hd = d // X_HEADS

    @pl.when(pl.program_id(0) == 0)
    def _():
        car_ref[...] = jnp.zeros(car_ref.shape, F32)

    h1 = x_ref[...] + _dot(ys_ref[...], wo1_ref[...]) + _dot(yl_ref[...], wo2_ref[...])

    q = _dot(_rms(h1, gx_ref[...]).astype(BF16), wq_ref[...]).astype(BF16)
    o_parts = []
    for hh in range(X_HEADS):
        sl = slice(hh * hd, (hh + 1) * hd)
        sc = _dot_nt(q[:, sl], k_ref[0, :, sl]) * (hd ** -0.5)
        sc = sc - jnp.max(sc, axis=-1, keepdims=True)
        p = jnp.exp(sc)
        p = p * (1.0 / jnp.sum(p, axis=-1, keepdims=True))
        o_parts.append(_dot(p.astype(BF16), v_ref[0, :, sl]))
    o = jnp.concatenate(o_parts, axis=1).astype(BF16)
    h2 = h1 + _dot(o, wo_ref[...])
    h_ref[...] = h2

    hn = _rms(h2, gm_ref[...])
    _store_token_major(hn_ref, hn)
    logits = _dot_hilo(hn, wrh_ref[...], wrl_ref[...]) + br_ref[...]

    l = logits.T[:N_EXPERTS, :]
    row = lax.broadcasted_iota(jnp.int32, (N_EXPERTS, tm), 0)
    picked = jnp.zeros((N_EXPERTS, tm), F32)
    vals, idxs = [], []
    for _ in range(TOP_K):
        m = jnp.max(l, axis=0, keepdims=True)
        idx = jnp.min(jnp.where(l == m, row, N_EXPERTS), axis=0, keepdims=True)
        sel = row == idx
        vals.append(m)
        idxs.append(idx)
        picked = jnp.where(sel, 1.0, picked)
        l = jnp.where(sel, -jnp.inf, l)
    ex = [jnp.exp(v - vals[0]) for v in vals]
    den = ex[0] + ex[1] + ex[2] + ex[3]

    ri = lax.broadcasted_iota(jnp.int32, (tm, tm), 0)
    ci = lax.broadcasted_iota(jnp.int32, (tm, tm), 1)
    earlier = jnp.where(ri < ci, 1.0, 0.0).astype(BF16)
    before = _dot(picked.astype(BF16), earlier) + car_ref[:, 0:1]
    out_row = lax.broadcasted_iota(jnp.int32, (SUBLANES, tm), 0)
    idx_out = jnp.zeros((SUBLANES, tm), jnp.int32)
    rank_out = jnp.zeros((SUBLANES, tm), jnp.int32)
    gate_out = jnp.zeros((SUBLANES, tm), F32)
    for k in range(TOP_K):
        rk = jnp.sum(jnp.where(row == idxs[k], before, 0.0), axis=0, keepdims=True)
        at_k = out_row == k
        idx_out = jnp.where(at_k, idxs[k], idx_out)
        rank_out = jnp.where(at_k, rk.astype(jnp.int32), rank_out)
        gate_out = jnp.where(at_k, ex[k] / den, gate_out)
    idx_ref[...] = idx_out
    rank_ref[...] = rank_out
    gate_ref[...] = gate_out
    total = car_ref[...] + jnp.sum(picked, axis=1, keepdims=True)
    car_ref[...] = total
    cnt_ref[...] = total.astype(jnp.int32)


def _mid(x2, ys, yl, kk, vv, wo1, wo2, gx, wq, wo, gm, wr_hi, wr_lo, br, seq):
    t, d = x2.shape
    tm = MID_TILE
    m = kk.shape[1]
    per_b = seq // tm
    row = lambda i: (i, 0)
    col = lambda i: (0, i)
    const = lambda i: (0, 0)
    kvmap = lambda i: (i // per_b, 0, 0)
    wspec = pl.BlockSpec((d, d), const)
    vspec = pl.BlockSpec((1, d), const)
    return pl.pallas_call(
        _mid_kernel,
        grid=(t // tm,),
        in_specs=[
            pl.BlockSpec((tm, d), row), pl.BlockSpec((tm, d), row), pl.BlockSpec((tm, d), row),
            pl.BlockSpec((1, m, d), kvmap), pl.BlockSpec((1, m, d), kvmap),
            wspec, wspec, vspec, wspec, wspec, vspec,
            pl.BlockSpec((d, LANES), const), pl.BlockSpec((d, LANES), const), pl.BlockSpec((1, LANES), const),
        ],
        out_specs=[
            pl.BlockSpec((tm, d), row), pl.BlockSpec((tm * d // LANES, LANES), row),
            pl.BlockSpec((SUBLANES, tm), col), pl.BlockSpec((SUBLANES, tm), col), pl.BlockSpec((SUBLANES, tm), col),
            pl.BlockSpec((N_EXPERTS, LANES), const),
        ],
        out_shape=[
            jax.ShapeDtypeStruct((t, d), F32), jax.ShapeDtypeStruct((t * d // LANES, LANES), F32),
            jax.ShapeDtypeStruct((SUBLANES, t), jnp.int32), jax.ShapeDtypeStruct((SUBLANES, t), jnp.int32),
            jax.ShapeDtypeStruct((SUBLANES, t), F32),
            jax.ShapeDtypeStruct((N_EXPERTS, LANES), jnp.int32),
        ],
        scratch_shapes=[pltpu.VMEM((N_EXPERTS, LANES), F32)],
        compiler_params=_cparams(("arbitrary",)),
        name="mid",
    )(x2, ys, yl, kk, vv, wo1, wo2, gx, wq, wo, gm, wr_hi, wr_lo, br)


def _dispatch_kernel(pend_ref, dst_ref, hn_ref, x_hbm, zero_ref, sem, *, parts):
    i = pl.program_id(0)
    tc = hn_ref.shape[0] // parts
    bm = zero_ref.shape[0] // parts
    rows = TOP_K * tc

    def tile(ref, n, count=1):
        return ref.at[pl.ds(pl.multiple_of(n * parts, parts), count * parts), :]

    @pl.when(i == 0)
    def _():
        zero_ref[...] = jnp.zeros(zero_ref.shape, F32)

        def fill(start):
            return pltpu.make_async_copy(zero_ref, tile(x_hbm, start, bm), sem.at[1])

        for e in range(N_EXPERTS):
            fill(jnp.maximum(pend_ref[e] - bm, 0)).start()
        for e in range(N_EXPERTS):
            fill(0).wait()

        def tail(j, c):
            fill(j * bm).start()
            fill(0).wait()
            return c

        lax.fori_loop(pend_ref[N_EXPERTS - 1] // bm, x_hbm.shape[0] // (bm * parts), tail, 0)

    def body(j, c):
        t0 = pl.multiple_of(j * ISSUE_UNROLL, ISSUE_UNROLL)
        for u in range(ISSUE_UNROLL):
            for k in range(TOP_K):
                dst = dst_ref[0, 0, k * tc + t0 + u]
                pltpu.make_async_copy(tile(hn_ref, t0 + u), tile(x_hbm, dst), sem.at[0]).start(priority=k % 2)
        return c

    lax.fori_loop(0, tc // ISSUE_UNROLL, body, 0)
    pltpu.make_async_copy(tile(x_hbm, 0, rows), tile(x_hbm, 0, rows), sem.at[0]).wait()


def _dispatch(pend, dest3, hn_tm, n_rows, d):
    parts = d // LANES
    t = hn_tm.shape[0] // parts
    tc = DISP_TILE
    grid_spec = pltpu.PrefetchScalarGridSpec(
        num_scalar_prefetch=1,
        grid=(t // tc,),
        in_specs=[
            pl.BlockSpec((1, 1, TOP_K * tc), lambda i, pe: (i, 0, 0), memory_space=pltpu.SMEM),
            pl.BlockSpec((tc * parts, LANES), lambda i, pe: (i, 0)),
        ],
        out_specs=pl.BlockSpec(memory_space=pl.ANY),
        scratch_shapes=[pltpu.VMEM((MOE_BLOCK * parts, LANES), F32), pltpu.SemaphoreType.DMA((2,))],
    )
    return pl.pallas_call(
        functools.partial(_dispatch_kernel, parts=parts),
        grid_spec=grid_spec,
        out_shape=jax.ShapeDtypeStruct((n_rows * parts, LANES), F32),
        compiler_params=_cparams(("arbitrary",)),
        name="dispatch",
    )(pend, dest3, hn_tm)


def _moe_kernel(be_ref, nxt_ref, nvb_ref, x_ref, wgu_hbm, bg_ref, bu_ref, wd_hbm, bd_ref, perm_ref,
                y_ref, wgu_f, wd_f, wgu_s, wd_s, act_s, sem):
    i = pl.program_id(0)
    nvb = nvb_ref[0]
    f, d = wd_f.shape
    bm = act_s.shape[0]
    pw = perm_ref.shape[0]
    half = pw // 2

    def fetch(e):
        return (pltpu.make_async_copy(wgu_hbm.at[e], wgu_f, sem.at[0]),
                pltpu.make_async_copy(wd_hbm.at[e], wd_f, sem.at[1]))

    @pl.when(jnp.logical_and(i == 0, nvb > 0))
    def _():
        for c in fetch(be_ref[0]):
            c.start()

    changed = jnp.logical_or(i == 0, be_ref[i] != be_ref[jnp.maximum(i - 1, 0)])

    @pl.when(jnp.logical_and(changed, i < nvb))
    def _():
        for c in fetch(be_ref[i]):
            c.wait()
        for c in range(2 * f // pw):
            wc = wgu_f[:, c * pw:(c + 1) * pw].astype(BF16)
            wgu_s[:, c * pw:(c + 1) * pw] = _dot(wc, perm_ref[...]).astype(BF16)
        wd_s[...] = wd_f[...].astype(BF16)

        @pl.when(nxt_ref[i] >= 0)
        def _():
            for c in fetch(nxt_ref[i]):
                c.start()

    @pl.when(i < nvb)
    def _():
        e = be_ref[i]
        xb = jnp.concatenate([p.astype(BF16) for p in _load_token_major(x_ref, bm, d)], axis=1)
        for c in range(2 * f // pw):
            gu = _dot(xb, wgu_s[:, c * pw:(c + 1) * pw])
            g = gu[:, :half] + bg_ref[e, :, c * half:(c + 1) * half]
            u = gu[:, half:] + bu_ref[e, :, c * half:(c + 1) * half]
            g = jnp.minimum(g, SWIGLU_LIMIT)
            u = jnp.clip(u, -SWIGLU_LIMIT, SWIGLU_LIMIT)
            act = (u + 1.0) * (g * _sigmoid(SWIGLU_ALPHA * g))
            act_s[:, c * half:(c + 1) * half] = act.astype(BF16)
        _store_token_major(y_ref, _dot(act_s[...], wd_s[...]) + bd_ref[e])

    @pl.when(i >= nvb)
    def _():
        y_ref[...] = jnp.zeros(y_ref.shape, F32)


def _moe(block_e, next_e, nvb, xrows_tm, wgu, bg, bu, wd, bd, perm):
    f, d = wd.shape[1], wd.shape[2]
    parts = d // LANES
    bm = MOE_BLOCK
    nb = xrows_tm.shape[0] // (bm * parts)

    def whole(a):
        return pl.BlockSpec(a.shape, lambda i, be, nx, nv, nd=a.ndim: (0,) * nd)

    grid_spec = pltpu.PrefetchScalarGridSpec(
        num_scalar_prefetch=3,
        grid=(nb,),
        in_specs=[
            pl.BlockSpec((bm * parts, LANES),
                         lambda i, be, nx, nv: (jnp.minimum(i, jnp.maximum(nv[0] - 1, 0)), 0)),
            pl.BlockSpec(memory_space=pl.ANY),
            whole(bg), whole(bu),
            pl.BlockSpec(memory_space=pl.ANY),
            whole(bd), whole(perm),
        ],
        out_specs=pl.BlockSpec((bm * parts, LANES), lambda i, be, nx, nv: (i, 0)),
        scratch_shapes=[
            pltpu.VMEM((d, 2 * f), F32), pltpu.VMEM((f, d), F32),
            pltpu.VMEM((d, 2 * f), BF16), pltpu.VMEM((f, d), BF16),
            pltpu.VMEM((bm, f), BF16),
            pltpu.SemaphoreType.DMA((2,)),
        ],
    )
    return pl.pallas_call(
        _moe_kernel,
        grid_spec=grid_spec,
        out_shape=jax.ShapeDtypeStruct(xrows_tm.shape, F32),
        compiler_params=_cparams(("arbitrary",)),
        name="moe",
    )(block_e, next_e, nvb, xrows_tm, wgu, bg, bu, wd, bd, perm)


def _comb_kernel(dst_ref, dstn_ref, h_ref, gate_ref, y_hbm, g_ref, o_ref, ybuf, sem, *, parts):
    i = pl.program_id(0)
    n = pl.num_programs(0)
    tc, d = h_ref.shape
    rows = TOP_K * tc
    slot = i % 2

    def tile(ref, n, count=1):
        return ref.at[pl.ds(pl.multiple_of(n * parts, parts), count * parts), :]

    def start_rows(idx_ref, s):
        def body(j, c):
            t0 = pl.multiple_of(j * ISSUE_UNROLL, ISSUE_UNROLL)
            for u in range(ISSUE_UNROLL):
                for k in range(TOP_K):
                    r = k * tc + t0 + u
                    pltpu.make_async_copy(tile(y_hbm, idx_ref[0, 0, r]), tile(ybuf.at[s], r),
                                          sem.at[s]).start(priority=k % 2)
            return c

        lax.fori_loop(0, tc // ISSUE_UNROLL, body, 0)

    @pl.when(i == 0)
    def _():
        start_rows(dst_ref, 0)

    @pl.when(i + 1 < n)
    def _():
        start_rows(dstn_ref, 1 - slot)

    pltpu.make_async_copy(tile(y_hbm, 0, rows), ybuf.at[slot], sem.at[slot]).wait()
    acc = [h_ref[:, s * LANES:(s + 1) * LANES] for s in range(parts)]
    for k in range(TOP_K):
        yk = _load_token_major(ybuf.at[slot], tc, d, base=k * tc * parts)
        gk = gate_ref[:, k:k + 1]
        acc = [a + p * gk for a, p in zip(acc, yk)]
    o_ref[...] = _rms(jnp.concatenate(acc, axis=1), g_ref[...])


def _combine(dest3, h2, gate, y_tm, g):
    t, d = h2.shape
    parts = d // LANES
    tc = COMB_TILE
    nt = t // tc
    return pl.pallas_call(
        functools.partial(_comb_kernel, parts=parts),
        grid=(nt,),
        in_specs=[
            pl.BlockSpec((1, 1, TOP_K * tc), lambda i: (i, 0, 0), memory_space=pltpu.SMEM),
            pl.BlockSpec((1, 1, TOP_K * tc), lambda i: (jnp.minimum(i + 1, nt - 1), 0, 0),
                         memory_space=pltpu.SMEM),
            pl.BlockSpec((tc, d), lambda i: (i, 0)),
            pl.BlockSpec((tc, SUBLANES), lambda i: (i, 0)),
            pl.BlockSpec(memory_space=pl.ANY),
            pl.BlockSpec((1, d), lambda i: (0, 0)),
        ],
        out_specs=pl.BlockSpec((tc, d), lambda i: (i, 0)),
        out_shape=jax.ShapeDtypeStruct((t, d), F32),
        scratch_shapes=[pltpu.VMEM((2, TOP_K * tc * parts, LANES), F32), pltpu.SemaphoreType.DMA((2,))],
        compiler_params=_cparams(("arbitrary",)),
        name="combine",
    )(dest3, dest3, h2, gate, y_tm, g)


def _block_diag(wb, per):
    nb, bw, _ = wb.shape
    wq = wb.reshape(nb // per, per, bw, bw)
    eye = jnp.eye(per, dtype=wb.dtype)
    out = jnp.einsum('qaij,ab->qaibj', wq, eye)
    return out.reshape(nb // per, per * bw, per * bw)


def _pad_lanes(v, fill=0.0):
    return jnp.pad(v, (0, LANES - v.shape[0]), constant_values=fill).reshape(1, LANES)


def kernel(x, mem, norm_mix, w_in, ssd_conv_w, ssd_conv_b, ssd_dt_bias, ssd_a_log, ssd_d, ssd_norm, lru_conv_w, lru_conv_b, lru_wa, lru_ba, lru_wx, lru_bx, lru_lambda, w_out, norm_xattn, norm_mem, w_q, w_kv, w_o, norm_moe, w_router, b_router, w_gate_up, b_gate_up, w_down, b_down, norm_final):
    b, s, d = x.shape
    t = b * s
    n_mem = mem.shape[1]
    w = d
    cdim = w + 2 * SSD_GROUPS * SSD_STATE
    o1, o2, o3, o4 = w, w + cdim, w + cdim + SSD_HEADS, w + cdim + SSD_HEADS + w

    wi = w_in[0]
    wzx = wi[:, :o2].astype(BF16)
    wdt = jnp.pad(wi[:, o2:o3], ((0, 0), (0, LANES - SSD_HEADS)))
    wdt_hi, wdt_lo = _hilo(wdt)
    wxr = wi[:, o3:o4].astype(BF16)
    wgr = wi[:, o4:].astype(BF16)
    e01 = (jnp.arange(LANES)[:, None] == (jnp.arange(w)[None, :] // SSD_HEAD_DIM)).astype(BF16)
    dskip = jnp.repeat(ssd_d[0], SSD_HEAD_DIM).reshape(1, w)
    per = MXU_DIM // (w // LRU_BLOCKS)
    wa_bd = _block_diag(lru_wa[0], per).astype(BF16)
    wx_bd = _block_diag(lru_wx[0], per).astype(BF16)
    wr = jnp.pad(w_router[0], ((0, 0), (0, LANES - N_EXPERTS)))
    wr_hi, wr_lo = _hilo(wr)
    br = _pad_lanes(b_router[0], fill=-1e30)
    bgu = b_gate_up[0]
    bg = bgu[:, None, 0::2]
    bu = bgu[:, None, 1::2]
    bd = b_down[0][:, None, :]
    pw = 2 * LANES
    col = jnp.arange(pw)
    src_col = jnp.where(col < LANES, 2 * col, 2 * (col - LANES) + 1)
    perm = (jnp.arange(pw)[:, None] == src_col[None, :]).astype(BF16)

    x2 = x.reshape(t, d)
    z, xbc, dt, xr, gr = _in_proj(x2, norm_mix[0].reshape(1, d), wzx, jnp.concatenate([wdt_hi, wdt_lo], axis=1),
                                  wdt_hi, wxr, wgr)

    y_ssd = _ssd(xbc.reshape(b, s, cdim), dt.reshape(b, s, LANES), z.reshape(b, s, w),
                 _shift_matrix(SSD_CHUNK), ssd_conv_w[0], ssd_conv_b[0].reshape(1, cdim), _pad_lanes(ssd_dt_bias[0]),
                 _pad_lanes(ssd_a_log[0]), dskip, ssd_norm[0].reshape(1, w), e01)
    y_lru = _lru(xr.reshape(b, s, w), gr.reshape(b, s, w), _shift_matrix(LRU_TILE), lru_conv_w[0], lru_conv_b[0].reshape(1, w),
                 wa_bd, lru_ba[0].reshape(1, w), wx_bd, lru_bx[0].reshape(1, w), lru_lambda[0].reshape(1, w))

    kk, vv = _kv(mem.reshape(b * n_mem, d), norm_mem[0].reshape(1, d), w_kv[0].astype(BF16))
    wo_mix = w_out[0].astype(BF16)
    h2, hn, idx_m, rank_m, gate_m, cnt = _mid(
        x2, y_ssd.reshape(t, w), y_lru.reshape(t, w), kk.reshape(b, n_mem, d), vv.reshape(b, n_mem, d),
        wo_mix[:w], wo_mix[w:], norm_xattn[0].reshape(1, d), w_q[0].astype(BF16), w_o[0].astype(BF16),
        norm_moe[0].reshape(1, d), wr_hi, wr_lo, br, s)

    bm = MOE_BLOCK
    counts = cnt[:, 0]
    pcounts = (counts + bm - 1) // bm * bm
    pend = jnp.cumsum(pcounts).astype(jnp.int32)
    pstart = pend - pcounts
    idx = idx_m[:TOP_K, :]
    onehot = idx[:, :, None] == jnp.arange(N_EXPERTS, dtype=jnp.int32)[None, None, :]
    dest = jnp.sum(jnp.where(onehot, pstart[None, None, :], 0), axis=-1) + rank_m[:TOP_K, :]
    n_pairs = t * TOP_K
    nb = (n_pairs + N_EXPERTS * (bm - 1) + bm - 1) // bm
    blk0 = jnp.arange(nb, dtype=jnp.int32) * bm
    block_e = jnp.minimum(jnp.sum(pend[None, :] <= blk0[:, None], axis=1), N_EXPERTS - 1).astype(jnp.int32)
    nvb = (pend[-1] // bm).astype(jnp.int32).reshape(1)
    after = pend[block_e] // bm
    next_e = jnp.where(after < nvb[0], block_e[jnp.minimum(after, nb - 1)], -1).astype(jnp.int32)
    dest = dest.astype(jnp.int32)

    def by_tile(tc):
        return dest.reshape(TOP_K, t // tc, tc).transpose(1, 0, 2).reshape(t // tc, 1, TOP_K * tc)

    xrows = _dispatch(pend, by_tile(DISP_TILE), hn, nb * bm, d)
    y = _moe(block_e, next_e, nvb, xrows, w_gate_up[0], bg, bu, w_down[0], bd, perm)
    out = _combine(by_tile(COMB_TILE), h2, gate_m.T, y, norm_final.reshape(1, d))
    return out.reshape(b, s, d)
```

```python
import functools

import jax
import jax.numpy as jnp
from jax import lax
from jax.experimental import pallas as pl
from jax.experimental.pallas import tpu as pltpu

F32 = jnp.float32
BF16 = jnp.bfloat16

NORM_EPS = 1e-6
LANES = 128
SUBLANES = 8
MXU_DIM = 256
SSD_HEAD_DIM = 64
SSD_HEADS = 16
SSD_GROUPS = 4
SSD_STATE = 128
CONV_K = 4
LRU_BLOCKS = 16
RG_C = 8.0
X_HEADS = 4
N_EXPERTS = 32
TOP_K = 4
SWIGLU_LIMIT = 7.0
SWIGLU_ALPHA = 1.702

VMEM_LIMIT = 56 * 1024 * 1024

IN_TILE = 1024
SSD_CHUNK = 256
LRU_TILE = 256
MID_TILE = 512
MOE_BLOCK = 512
DISP_TILE = 2048
COMB_TILE = 256
ISSUE_UNROLL = 8


def _cparams(sem):
    return pltpu.CompilerParams(dimension_semantics=sem, vmem_limit_bytes=VMEM_LIMIT)


def _rms(x, g):
    ms = jnp.mean(x * x, axis=-1, keepdims=True)
    return x * lax.rsqrt(ms + NORM_EPS) * g


def _sigmoid(x):
    return 0.5 * jnp.tanh(0.5 * x) + 0.5


def _softplus(x):
    return jnp.maximum(x, 0.0) + jnp.log(1.0 + jnp.exp(-jnp.abs(x)))


def _split3(x):
    a = x.astype(BF16)
    r = x - a.astype(F32)
    b = r.astype(BF16)
    c = (r - b.astype(F32)).astype(BF16)
    return a, b, c


def _dot(a, b):
    return jnp.dot(a, b, preferred_element_type=F32)


def _dot_nt(a, b):
    return lax.dot_general(a, b, (((1,), (1,)), ((), ())), preferred_element_type=F32)


def _dot01_right(x, m01):
    a, b, c = _split3(x)
    return _dot(a, m01) + _dot(b, m01) + _dot(c, m01)


def _dot01_left(m01, x):
    a, b, c = _split3(x)
    return _dot(m01, a) + _dot(m01, b) + _dot(m01, c)


def _dot_hilo(x, w_hi, w_lo):
    xh = x.astype(BF16)
    xl = (x - xh.astype(F32)).astype(BF16)
    return _dot(xh, w_hi) + _dot(xl, w_hi) + _dot(xh, w_lo)


def _hilo(w):
    hi = w.astype(BF16)
    lo = (w - hi.astype(F32)).astype(BF16)
    return hi, lo


def _store_token_major(ref, val, base=0):
    n, d = val.shape
    parts = d // LANES
    for s in range(parts):
        ref[pl.ds(base + s, n, stride=parts), :] = val[:, s * LANES:(s + 1) * LANES]


def _load_token_major(ref, n, d, base=0):
    parts = d // LANES
    return [ref[pl.ds(base + s, n, stride=parts), :] for s in range(parts)]


def _in_proj_kernel(x_ref, g_ref, wzx_ref, wdthl_ref, wdth_ref, wxr_ref, wgr_ref,
                    z_ref, xbc_ref, dt_ref, xr_ref, gr_ref):
    d = x_ref.shape[1]
    hn = _rms(x_ref[...], g_ref[...])
    hb = hn.astype(BF16)
    z_ref[...] = _dot_nt(hb, wzx_ref[:d, :]).astype(BF16)
    xbc_ref[...] = _dot_nt(hb, wzx_ref[d:, :]).astype(BF16)
    hl = _dot_nt(hb, wdthl_ref[...])
    hn_lo = (hn - hb.astype(F32)).astype(BF16)
    dt_ref[...] = hl[:, :LANES] + hl[:, LANES:] + _dot_nt(hn_lo, wdth_ref[...])
    xr_ref[...] = _dot_nt(hb, wxr_ref[...]).astype(BF16)
    gr_ref[...] = _dot_nt(hb, wgr_ref[...]).astype(BF16)


def _in_proj(x2, g, wzx, wdt_hilo, wdt_hi, wxr, wgr):
    t, d = x2.shape
    tm = IN_TILE
    nzx = wzx.shape[0]
    const = lambda i: (0, 0)
    row = lambda i: (i, 0)
    return pl.pallas_call(
        _in_proj_kernel,
        grid=(t // tm,),
        in_specs=[
            pl.BlockSpec((tm, d), row),
            pl.BlockSpec((1, d), const),
            pl.BlockSpec((nzx, d), const),
            pl.BlockSpec((2 * LANES, d), const),
            pl.BlockSpec((LANES, d), const),
            pl.BlockSpec((d, d), const),
            pl.BlockSpec((d, d), const),
        ],
        out_specs=[
            pl.BlockSpec((tm, d), row),
            pl.BlockSpec((tm, nzx - d), row),
            pl.BlockSpec((tm, LANES), row),
            pl.BlockSpec((tm, d), row),
            pl.BlockSpec((tm, d), row),
        ],
        out_shape=[
            jax.ShapeDtypeStruct((t, d), BF16),
            jax.ShapeDtypeStruct((t, nzx - d), BF16),
            jax.ShapeDtypeStruct((t, LANES), F32),
            jax.ShapeDtypeStruct((t, d), BF16),
            jax.ShapeDtypeStruct((t, d), BF16),
        ],
        compiler_params=_cparams(("arbitrary",)),
        name="in_proj",
    )(x2, g, wzx, wdt_hilo, wdt_hi, wxr, wgr)


def _shift_matrix(n):
    return jnp.concatenate([jnp.eye(n, k=-(CONV_K - 1 - j), dtype=BF16) for j in range(CONV_K - 1)], axis=0)


def _causal_conv(halo_ref, x, shift_ref, w_ref, b_ref, first):
    n = x.shape[0]
    pad = SUBLANES
    k1 = CONV_K - 1

    @pl.when(first)
    def _():
        halo_ref[...] = jnp.zeros(halo_ref.shape, F32)

    xf = x.astype(F32)
    taps = _dot(shift_ref[...], x)
    acc = b_ref[...] + w_ref[k1:k1 + 1, :] * xf
    for j in range(k1):
        acc = acc + w_ref[j:j + 1, :] * taps[j * n:(j + 1) * n]
    head = acc[:pad]
    for j in range(k1):
        head = head + w_ref[j:j + 1, :] * halo_ref[pad - k1 + j:2 * pad - k1 + j, :]
    halo_ref[0:pad, :] = xf[n - pad:n]
    return jnp.concatenate([head, acc[pad:]], axis=0)


def _ssd_kernel(xbc_ref, dt_ref, z_ref, sh_ref, cw_ref, cb_ref, dtb_ref, alog_ref, dskip_ref, gn_ref, e_ref,
                y_ref, halo_ref, st_ref):
    n = xbc_ref.shape[1]
    w = z_ref.shape[2]
    gw = w // SSD_GROUPS
    first = pl.program_id(1) == 0

    @pl.when(first)
    def _():
        st_ref[...] = jnp.zeros(st_ref.shape, F32)

    conv = _causal_conv(halo_ref, xbc_ref[0], sh_ref, cw_ref, cb_ref, first)
    xc = conv * _sigmoid(conv)
    xs = xc[:, :w]

    dt = _softplus(dt_ref[0] + dtb_ref[...])
    a = -jnp.exp(alog_ref[...])
    da = dt * a
    ri = lax.broadcasted_iota(jnp.int32, (n, n), 0)
    ci = lax.broadcasted_iota(jnp.int32, (n, n), 1)
    causal = ri >= ci
    tril = jnp.where(causal, 1.0, 0.0).astype(BF16)
    a_cs = _dot01_left(tril, da)
    a_cs_t = a_cs.T

    e01 = e_ref[...]
    dt_x = _dot01_right(dt, e01)
    acs_x = _dot01_right(a_cs, e01)
    last_x = acs_x[n - 1:n, :]
    xdt = xs * dt_x
    xdt_b = xdt.astype(BF16)
    xdt_end = (xdt * jnp.exp(last_x - acs_x)).astype(BF16)
    exp_acs = jnp.exp(acs_x)
    chunk_decay = jnp.exp(last_x)
    lane = lax.broadcasted_iota(jnp.int32, (n, gw), 1)

    for g in range(SSD_GROUPS):
        lo = g * gw
        bg = xc[:, w + g * SSD_STATE:w + (g + 1) * SSD_STATE].astype(BF16)
        cg = xc[:, w + (SSD_GROUPS + g) * SSD_STATE:w + (SSD_GROUPS + g + 1) * SSD_STATE].astype(BF16)
        cb = _dot_nt(cg, bg)
        prev = st_ref[g]
        acc = _dot(cg, prev.astype(BF16)) * exp_acs[:, lo:lo + gw]
        new = lax.dot_general(bg, xdt_end[:, lo:lo + gw], (((0,), (0,)), ((), ())),
                              preferred_element_type=F32)
        st_ref[g] = chunk_decay[:, lo:lo + gw] * prev + new
        xg = xdt_b[:, lo:lo + gw]
        for k in range(SSD_HEADS // SSD_GROUPS):
            h = g * (SSD_HEADS // SSD_GROUPS) + k
            seg = a_cs[:, h:h + 1] - a_cs_t[h:h + 1, :]
            dec = jnp.exp(jnp.where(causal, seg, -jnp.inf))
            m = (cb * dec).astype(BF16)
            in_head = (lane >= k * SSD_HEAD_DIM) & (lane < (k + 1) * SSD_HEAD_DIM)
            acc = acc + _dot(m, jnp.where(in_head, xg, jnp.zeros_like(xg)))
        yg = acc + xs[:, lo:lo + gw] * dskip_ref[:, lo:lo + gw]
        zg = z_ref[0, :, lo:lo + gw].astype(F32)
        u = yg * (zg * _sigmoid(zg))
        u = u * lax.rsqrt(jnp.mean(u * u, axis=-1, keepdims=True) + NORM_EPS)
        y_ref[0, :, lo:lo + gw] = (u * gn_ref[:, lo:lo + gw]).astype(BF16)


def _ssd(xbc, dt, z, shift, cw, cb, dtb, alog, dskip, gn, e01):
    b, s, cdim = xbc.shape
    w = z.shape[2]
    n = SSD_CHUNK
    tile = lambda i, j: (i, j, 0)
    const = lambda i, j: (0, 0)
    return pl.pallas_call(
        _ssd_kernel,
        grid=(b, s // n),
        in_specs=[
            pl.BlockSpec((1, n, cdim), tile),
            pl.BlockSpec((1, n, LANES), tile),
            pl.BlockSpec((1, n, w), tile),
            pl.BlockSpec(shift.shape, const),
            pl.BlockSpec((CONV_K, cdim), const),
            pl.BlockSpec((1, cdim), const),
            pl.BlockSpec((1, LANES), const),
            pl.BlockSpec((1, LANES), const),
            pl.BlockSpec((1, w), const),
            pl.BlockSpec((1, w), const),
            pl.BlockSpec((LANES, w), const),
        ],
        out_specs=pl.BlockSpec((1, n, w), tile),
        out_shape=jax.ShapeDtypeStruct((b, s, w), BF16),
        scratch_shapes=[
            pltpu.VMEM((2 * SUBLANES, cdim), F32),
            pltpu.VMEM((SSD_GROUPS, SSD_STATE, w // SSD_GROUPS), F32),
        ],
        compiler_params=_cparams(("arbitrary", "arbitrary")),
        name="ssd",
    )(xbc, dt, z, shift, cw, cb, dtb, alog, dskip, gn, e01)


def _gelu_tanh(x):
    c = 0.7978845608028654
    return 0.5 * x * (1.0 + jnp.tanh(c * (x + 0.044715 * (x * x * x))))


def _lru_kernel(xr_ref, gr_ref, sh_ref, cw_ref, cb_ref, wa_ref, ba_ref, wx_ref, bx_ref, lam_ref,
                y_ref, halo_ref, car_ref, h_ref):
    n = xr_ref.shape[1]
    w = xr_ref.shape[2]
    first = pl.program_id(1) == 0

    @pl.when(first)
    def _():
        car_ref[...] = jnp.zeros(car_ref.shape, F32)

    xc = _causal_conv(halo_ref, xr_ref[0], sh_ref, cw_ref, cb_ref, first)
    xb = xc.astype(BF16)
    nq = wa_ref.shape[0]
    qw = w // nq
    r_parts, i_parts = [], []
    for q in range(nq):
        xq = xb[:, q * qw:(q + 1) * qw]
        r_parts.append(_dot(xq, wa_ref[q]))
        i_parts.append(_dot(xq, wx_ref[q]))
    r = _sigmoid(jnp.concatenate(r_parts, axis=1) + ba_ref[...])
    gi = _sigmoid(jnp.concatenate(i_parts, axis=1) + bx_ref[...])
    log_a = (-RG_C) * r * _softplus(-lam_ref[...])
    a = jnp.exp(log_a)
    u = xc * gi * jnp.sqrt(1.0 - a * a)

    groups = n // SUBLANES
    sub = lax.broadcasted_iota(jnp.int32, (groups, SUBLANES, w), 1)
    ap = a.reshape(groups, SUBLANES, w)
    bp = u.reshape(groups, SUBLANES, w)
    for d in (1, 2, 4):
        m = sub >= d
        bp = jnp.where(m, ap * pltpu.roll(bp, d, 1) + bp, bp)
        ap = jnp.where(m, ap * pltpu.roll(ap, d, 1), ap)
    carry = car_ref[...]
    for g in range(groups):
        hb = bp[g] + ap[g] * carry
        h_ref[g * SUBLANES:(g + 1) * SUBLANES, :] = hb
        carry = jnp.broadcast_to(hb[SUBLANES - 1:SUBLANES, :], (SUBLANES, w))
    car_ref[...] = carry
    y_ref[0] = (h_ref[...] * _gelu_tanh(gr_ref[0].astype(F32))).astype(BF16)


def _lru(xr, gr, shift, cw, cb, wa, ba, wx, bx, lam):
    b, s, w = xr.shape
    n = LRU_TILE
    nq, qw, _ = wa.shape
    tile = lambda i, j: (i, j, 0)
    const = lambda i, j: (0, 0)
    const3 = lambda i, j: (0, 0, 0)
    return pl.pallas_call(
        _lru_kernel,
        grid=(b, s // n),
        in_specs=[
            pl.BlockSpec((1, n, w), tile),
            pl.BlockSpec((1, n, w), tile),
            pl.BlockSpec(shift.shape, const),
            pl.BlockSpec((CONV_K, w), const),
            pl.BlockSpec((1, w), const),
            pl.BlockSpec((nq, qw, qw), const3),
            pl.BlockSpec((1, w), const),
            pl.BlockSpec((nq, qw, qw), const3),
            pl.BlockSpec((1, w), const),
            pl.BlockSpec((1, w), const),
        ],
        out_specs=pl.BlockSpec((1, n, w), tile),
        out_shape=jax.ShapeDtypeStruct((b, s, w), BF16),
        scratch_shapes=[
            pltpu.VMEM((2 * SUBLANES, w), F32),
            pltpu.VMEM((SUBLANES, w), F32),
            pltpu.VMEM((n, w), F32),
        ],
        compiler_params=_cparams(("arbitrary", "arbitrary")),
        name="lru",
    )(xr, gr, shift, cw, cb, wa, ba, wx, bx, lam)


def _kv_kernel(m_ref, g_ref, w_ref, k_ref, v_ref):
    d = m_ref.shape[1]
    mn = _rms(m_ref[...], g_ref[...]).astype(BF16)
    k_ref[...] = _dot(mn, w_ref[:, :d]).astype(BF16)
    v_ref[...] = _dot(mn, w_ref[:, d:]).astype(BF16)


def _kv(mem2, g, wkv):
    t, d = mem2.shape
    tm = min(t, 512)
    row = lambda i: (i, 0)
    const = lambda i: (0, 0)
    return pl.pallas_call(
        _kv_kernel,
        grid=(t // tm,),
        in_specs=[pl.BlockSpec((tm, d), row), pl.BlockSpec((1, d), const), pl.BlockSpec((d, 2 * d), const)],
        out_specs=[pl.BlockSpec((tm, d), row), pl.BlockSpec((tm, d), row)],
        out_shape=[jax.ShapeDtypeStruct((t, d), BF16), jax.ShapeDtypeStruct((t, d), BF16)],
        compiler_params=_cparams(("arbitrary",)),
        name="kv",
    )(mem2, g, wkv)


def _mid_kernel(x_ref, ys_ref, yl_ref, k_ref, v_ref, wo1_ref, wo2_ref, gx_ref, wq_ref, wo_ref,
                gm_ref, wrh_ref, wrl_ref, br_ref,
                h_ref, hn_ref, idx_ref, rank_ref, gate_ref, cnt_ref, car_ref):
    tm, d = x_ref.shape
    hd = d // X_HEADS

    @pl.when(pl.program_id(0) == 0)
    def _():
        car_ref[...] = jnp.zeros(car_ref.shape, F32)

    h1 = x_ref[...] + _dot(ys_ref[...], wo1_ref[...]) + _dot(yl_ref[...], wo2_ref[...])

    q = _dot(_rms(h1, gx_ref[...]).astype(BF16), wq_ref[...]).astype(BF16)
    o_parts = []
    for hh in range(X_HEADS):
        sl = slice(hh * hd, (hh + 1) * hd)
        sc = _dot_nt(q[:, sl], k_ref[0, :, sl]) * (hd ** -0.5)
        sc = sc - jnp.max(sc, axis=-1, keepdims=True)
        p = jnp.exp(sc)
        p = p * (1.0 / jnp.sum(p, axis=-1, keepdims=True))
        o_parts.append(_dot(p.astype(BF16), v_ref[0, :, sl]))
    o = jnp.concatenate(o_parts, axis=1).astype(BF16)
    h2 = h1 + _dot(o, wo_ref[...])
    h_ref[...] = h2

    hn = _rms(h2, gm_ref[...])
    _store_token_major(hn_ref, hn)
    logits = _dot_hilo(hn, wrh_ref[...], wrl_ref[...]) + br_ref[...]

    l = logits.T[:N_EXPERTS, :]
    row = lax.broadcasted_iota(jnp.int32, (N_EXPERTS, tm), 0)
    picked = jnp.zeros((N_EXPERTS, tm), F32)
    vals, idxs = [], []
    for _ in range(TOP_K):
        m = jnp.max(l, axis=0, keepdims=True)
        idx = jnp.min(jnp.where(l == m, row, N_EXPERTS), axis=0, keepdims=True)
        sel = row == idx
        vals.append(m)
        idxs.append(idx)
        picked = jnp.where(sel, 1.0, picked)
        l = jnp.where(sel, -jnp.inf, l)
    ex = [jnp.exp(v - vals[0]) for v in vals]
    den = ex[0] + ex[1] + ex[2] + ex[3]

    ri = lax.broadcasted_iota(jnp.int32, (tm, tm), 0)
    ci = lax.broadcasted_iota(jnp.int32, (tm, tm), 1)
    earlier = jnp.where(ri < ci, 1.0, 0.0).astype(BF16)
    before = _dot(picked.astype(BF16), earlier) + car_ref[:, 0:1]
    out_row = lax.broadcasted_iota(jnp.int32, (SUBLANES, tm), 0)
    idx_out = jnp.zeros((SUBLANES, tm), jnp.int32)
    rank_out = jnp.zeros((SUBLANES, tm), jnp.int32)
    gate_out = jnp.zeros((SUBLANES, tm), F32)
    for k in range(TOP_K):
        rk = jnp.sum(jnp.where(row == idxs[k], before, 0.0), axis=0, keepdims=True)
        at_k = out_row == k
        idx_out = jnp.where(at_k, idxs[k], idx_out)
        rank_out = jnp.where(at_k, rk.astype(jnp.int32), rank_out)
        gate_out = jnp.where(at_k, ex[k] / den, gate_out)
    idx_ref[...] = idx_out
    rank_ref[...] = rank_out
    gate_ref[...] = gate_out
    total = car_ref[...] + jnp.sum(picked, axis=1, keepdims=True)
    car_ref[...] = total
    cnt_ref[...] = total.astype(jnp.int32)


def _mid(x2, ys, yl, kk, vv, wo1, wo2, gx, wq, wo, gm, wr_hi, wr_lo, br, seq):
    t, d = x2.shape
    tm = MID_TILE
    m = kk.shape[1]
    per_b = seq // tm
    row = lambda i: (i, 0)
    col = lambda i: (0, i)
    const = lambda i: (0, 0)
    kvmap = lambda i: (i // per_b, 0, 0)
    wspec = pl.BlockSpec((d, d), const)
    vspec = pl.BlockSpec((1, d), const)
    return pl.pallas_call(
        _mid_kernel,
        grid=(t // tm,),
        in_specs=[
            pl.BlockSpec((tm, d), row), pl.BlockSpec((tm, d), row), pl.BlockSpec((tm, d), row),
            pl.BlockSpec((1, m, d), kvmap), pl.BlockSpec((1, m, d), kvmap),
            wspec, wspec, vspec, wspec, wspec, vspec,
            pl.BlockSpec((d, LANES), const), pl.BlockSpec((d, LANES), const), pl.BlockSpec((1, LANES), const),
        ],
        out_specs=[
            pl.BlockSpec((tm, d), row), pl.BlockSpec((tm * d // LANES, LANES), row),
            pl.BlockSpec((SUBLANES, tm), col), pl.BlockSpec((SUBLANES, tm), col), pl.BlockSpec((SUBLANES, tm), col),
            pl.BlockSpec((N_EXPERTS, LANES), const),
        ],
        out_shape=[
            jax.ShapeDtypeStruct((t, d), F32), jax.ShapeDtypeStruct((t * d // LANES, LANES), F32),
            jax.ShapeDtypeStruct((SUBLANES, t), jnp.int32), jax.ShapeDtypeStruct((SUBLANES, t), jnp.int32),
            jax.ShapeDtypeStruct((SUBLANES, t), F32),
            jax.ShapeDtypeStruct((N_EXPERTS, LANES), jnp.int32),
        ],
        scratch_shapes=[pltpu.VMEM((N_EXPERTS, LANES), F32)],
        compiler_params=_cparams(("arbitrary",)),
        name="mid",
    )(x2, ys, yl, kk, vv, wo1, wo2, gx, wq, wo, gm, wr_hi, wr_lo, br)


def _dispatch_kernel(pend_ref, dst_ref, hn_ref, x_hbm, zero_ref, sem, *, parts):
    i = pl.program_id(0)
    tc = hn_ref.shape[0] // parts
    bm = zero_ref.shape[0] // parts
    rows = TOP_K * tc

    def tile(ref, n, count=1):
        return ref.at[pl.ds(pl.multiple_of(n * parts, parts), count * parts), :]

    @pl.when(i == 0)
    def _():
        zero_ref[...] = jnp.zeros(zero_ref.shape, F32)

        def fill(start):
            return pltpu.make_async_copy(zero_ref, tile(x_hbm, start, bm), sem.at[1])

        for e in range(N_EXPERTS):
            fill(jnp.maximum(pend_ref[e] - bm, 0)).start()
        for e in range(N_EXPERTS):
            fill(0).wait()

        def tail(j, c):
            fill(j * bm).start()
            fill(0).wait()
            return c

        lax.fori_loop(pend_ref[N_EXPERTS - 1] // bm, x_hbm.shape[0] // (bm * parts), tail, 0)

    def body(j, c):
        t0 = pl.multiple_of(j * ISSUE_UNROLL, ISSUE_UNROLL)
        for u in range(ISSUE_UNROLL):
            for k in range(TOP_K):
                dst = dst_ref[0, 0, k * tc + t0 + u]
                pltpu.make_async_copy(tile(hn_ref, t0 + u), tile(x_hbm, dst), sem.at[0]).start(priority=k % 2)
        return c

    lax.fori_loop(0, tc // ISSUE_UNROLL, body, 0)
    pltpu.make_async_copy(tile(x_hbm, 0, rows), tile(x_hbm, 0, rows), sem.at[0]).wait()


def _dispatch(pend, dest3, hn_tm, n_rows, d):
    parts = d // LANES
    t = hn_tm.shape[0] // parts
    tc = DISP_TILE
    grid_spec = pltpu.PrefetchScalarGridSpec(
        num_scalar_prefetch=1,
        grid=(t // tc,),
        in_specs=[
            pl.BlockSpec((1, 1, TOP_K * tc), lambda i, pe: (i, 0, 0), memory_space=pltpu.SMEM),
            pl.BlockSpec((tc * parts, LANES), lambda i, pe: (i, 0)),
        ],
        out_specs=pl.BlockSpec(memory_space=pl.ANY),
        scratch_shapes=[pltpu.VMEM((MOE_BLOCK * parts, LANES), F32), pltpu.SemaphoreType.DMA((2,))],
    )
    return pl.pallas_call(
        functools.partial(_dispatch_kernel, parts=parts),
        grid_spec=grid_spec,
        out_shape=jax.ShapeDtypeStruct((n_rows * parts, LANES), F32),
        compiler_params=_cparams(("arbitrary",)),
        name="dispatch",
    )(pend, dest3, hn_tm)


def _moe_kernel(be_ref, nxt_ref, nvb_ref, x_ref, wgu_hbm, bg_ref, bu_ref, wd_hbm, bd_ref, perm_ref,
                y_ref, wgu_f, wd_f, wgu_s, wd_s, act_s, sem):
    i = pl.program_id(0)
    nvb = nvb_ref[0]
    f, d = wd_f.shape
    bm = act_s.shape[0]
    pw = perm_ref.shape[0]
    half = pw // 2

    def fetch(e):
        return (pltpu.make_async_copy(wgu_hbm.at[e], wgu_f, sem.at[0]),
                pltpu.make_async_copy(wd_hbm.at[e], wd_f, sem.at[1]))

    @pl.when(jnp.logical_and(i == 0, nvb > 0))
    def _():
        for c in fetch(be_ref[0]):
            c.start()

    changed = jnp.logical_or(i == 0, be_ref[i] != be_ref[jnp.maximum(i - 1, 0)])

    @pl.when(jnp.logical_and(changed, i < nvb))
    def _():
        for c in fetch(be_ref[i]):
            c.wait()
        for c in range(2 * f // pw):
            wc = wgu_f[:, c * pw:(c + 1) * pw].astype(BF16)
            wgu_s[:, c * pw:(c + 1) * pw] = _dot(wc, perm_ref[...]).astype(BF16)
        wd_s[...] = wd_f[...].astype(BF16)

        @pl.when(nxt_ref[i] >= 0)
        def _():
            for c in fetch(nxt_ref[i]):
                c.start()

    @pl.when(i < nvb)
    def _():
        e = be_ref[i]
        xb = jnp.concatenate([p.astype(BF16) for p in _load_token_major(x_ref, bm, d)], axis=1)
        for c in range(2 * f // pw):
            gu = _dot(xb, wgu_s[:, c * pw:(c + 1) * pw])
            g = gu[:, :half] + bg_ref[e, :, c * half:(c + 1) * half]
            u = gu[:, half:] + bu_ref[e, :, c * half:(c + 1) * half]
            g = jnp.minimum(g, SWIGLU_LIMIT)
            u = jnp.clip(u, -SWIGLU_LIMIT, SWIGLU_LIMIT)
            act = (u + 1.0) * (g * _sigmoid(SWIGLU_ALPHA * g))
            act_s[:, c * half:(c + 1) * half] = act.astype(BF16)
        _store_token_major(y_ref, _dot(act_s[...], wd_s[...]) + bd_ref[e])

    @pl.when(i >= nvb)
    def _():
        y_ref[...] = jnp.zeros(y_ref.shape, F32)


def _moe(block_e, next_e, nvb, xrows_tm, wgu, bg, bu, wd, bd, perm):
    f, d = wd.shape[1], wd.shape[2]
    parts = d // LANES
    bm = MOE_BLOCK
    nb = xrows_tm.shape[0] // (bm * parts)

    def whole(a):
        return pl.BlockSpec(a.shape, lambda i, be, nx, nv, nd=a.ndim: (0,) * nd)

    grid_spec = pltpu.PrefetchScalarGridSpec(
        num_scalar_prefetch=3,
        grid=(nb,),
        in_specs=[
            pl.BlockSpec((bm * parts, LANES),
                         lambda i, be, nx, nv: (jnp.minimum(i, jnp.maximum(nv[0] - 1, 0)), 0)),
            pl.BlockSpec(memory_space=pl.ANY),
            whole(bg), whole(bu),
            pl.BlockSpec(memory_space=pl.ANY),
            whole(bd), whole(perm),
        ],
        out_specs=pl.BlockSpec((bm * parts, LANES), lambda i, be, nx, nv: (i, 0)),
        scratch_shapes=[
            pltpu.VMEM((d, 2 * f), F32), pltpu.VMEM((f, d), F32),
            pltpu.VMEM((d, 2 * f), BF16), pltpu.VMEM((f, d), BF16),
            pltpu.VMEM((bm, f), BF16),
            pltpu.SemaphoreType.DMA((2,)),
        ],
    )
    return pl.pallas_call(
        _moe_kernel,
        grid_spec=grid_spec,
        out_shape=jax.ShapeDtypeStruct(xrows_tm.shape, F32),
        compiler_params=_cparams(("arbitrary",)),
        name="moe",
    )(block_e, next_e, nvb, xrows_tm, wgu, bg, bu, wd, bd, perm)


def _comb_kernel(dst_ref, dstn_ref, h_ref, gate_ref, y_hbm, g_ref, o_ref, ybuf, sem, *, parts):
    i = pl.program_id(0)
    n = pl.num_programs(0)
    tc, d = h_ref.shape
    rows = TOP_K * tc
    slot = i % 2

    def tile(ref, n, count=1):
        return ref.at[pl.ds(pl.multiple_of(n * parts, parts), count * parts), :]

    def start_rows(idx_ref, s):
        def body(j, c):
            t0 = pl.multiple_of(j * ISSUE_UNROLL, ISSUE_UNROLL)
            for u in range(ISSUE_UNROLL):
                for k in range(TOP_K):
                    r = k * tc + t0 + u
                    pltpu.make_async_copy(tile(y_hbm, idx_ref[0, 0, r]), tile(ybuf.at[s], r),
                                          sem.at[s]).start(priority=k % 2)
            return c

        lax.fori_loop(0, tc // ISSUE_UNROLL, body, 0)

    @pl.when(i == 0)
    def _():
        start_rows(dst_ref, 0)

    @pl.when(i + 1 < n)
    def _():
        start_rows(dstn_ref, 1 - slot)

    pltpu.make_async_copy(tile(y_hbm, 0, rows), ybuf.at[slot], sem.at[slot]).wait()
    acc = [h_ref[:, s * LANES:(s + 1) * LANES] for s in range(parts)]
    for k in range(TOP_K):
        yk = _load_token_major(ybuf.at[slot], tc, d, base=k * tc * parts)
        gk = gate_ref[:, k:k + 1]
        acc = [a + p * gk for a, p in zip(acc, yk)]
    o_ref[...] = _rms(jnp.concatenate(acc, axis=1), g_ref[...])


def _combine(dest3, h2, gate, y_tm, g):
    t, d = h2.shape
    parts = d // LANES
    tc = COMB_TILE
    nt = t // tc
    return pl.pallas_call(
        functools.partial(_comb_kernel, parts=parts),
        grid=(nt,),
        in_specs=[
            pl.BlockSpec((1, 1, TOP_K * tc), lambda i: (i, 0, 0), memory_space=pltpu.SMEM),
            pl.BlockSpec((1, 1, TOP_K * tc), lambda i: (jnp.minimum(i + 1, nt - 1), 0, 0),
                         memory_space=pltpu.SMEM),
            pl.BlockSpec((tc, d), lambda i: (i, 0)),
            pl.BlockSpec((tc, SUBLANES), lambda i: (i, 0)),
            pl.BlockSpec(memory_space=pl.ANY),
            pl.BlockSpec((1, d), lambda i: (0, 0)),
        ],
        out_specs=pl.BlockSpec((tc, d), lambda i: (i, 0)),
        out_shape=jax.ShapeDtypeStruct((t, d), F32),
        scratch_shapes=[pltpu.VMEM((2, TOP_K * tc * parts, LANES), F32), pltpu.SemaphoreType.DMA((2,))],
        compiler_params=_cparams(("arbitrary",)),
        name="combine",
    )(dest3, dest3, h2, gate, y_tm, g)


def _block_diag(wb, per):
    nb, bw, _ = wb.shape
    wq = wb.reshape(nb // per, per, bw, bw)
    eye = jnp.eye(per, dtype=wb.dtype)
    out = jnp.einsum('qaij,ab->qaibj', wq, eye)
    return out.reshape(nb // per, per * bw, per * bw)


def _pad_lanes(v, fill=0.0):
    return jnp.pad(v, (0, LANES - v.shape[0]), constant_values=fill).reshape(1, LANES)


def kernel(x, mem, norm_mix, w_in, ssd_conv_w, ssd_conv_b, ssd_dt_bias, ssd_a_log, ssd_d, ssd_norm, lru_conv_w, lru_conv_b, lru_wa, lru_ba, lru_wx, lru_bx, lru_lambda, w_out, norm_xattn, norm_mem, w_q, w_kv, w_o, norm_moe, w_router, b_router, w_gate_up, b_gate_up, w_down, b_down, norm_final):
    b, s, d = x.shape
    t = b * s
    n_mem = mem.shape[1]
    w = d
    cdim = w + 2 * SSD_GROUPS * SSD_STATE
    o1, o2, o3, o4 = w, w + cdim, w + cdim + SSD_HEADS, w + cdim + SSD_HEADS + w

    wi = jnp.swapaxes(w_in[0], 0, 1)
    wzx = wi[:o2].astype(BF16)
    wdt = jnp.pad(wi[o2:o3], ((0, LANES - SSD_HEADS), (0, 0)))
    wdt_hi, wdt_lo = _hilo(wdt)
    wxr = wi[o3:o4].astype(BF16)
    wgr = wi[o4:].astype(BF16)
    e01 = (jnp.arange(LANES)[:, None] == (jnp.arange(w)[None, :] // SSD_HEAD_DIM)).astype(BF16)
    dskip = jnp.repeat(ssd_d[0], SSD_HEAD_DIM).reshape(1, w)
    per = MXU_DIM // (w // LRU_BLOCKS)
    wa_bd = _block_diag(lru_wa[0], per).astype(BF16)
    wx_bd = _block_diag(lru_wx[0], per).astype(BF16)
    wr = jnp.pad(w_router[0], ((0, 0), (0, LANES - N_EXPERTS)))
    wr_hi, wr_lo = _hilo(wr)
    br = _pad_lanes(b_router[0], fill=-1e30)
    bgu = b_gate_up[0]
    bg = bgu[:, None, 0::2]
    bu = bgu[:, None, 1::2]
    bd = b_down[0][:, None, :]
    pw = 2 * LANES
    col = jnp.arange(pw)
    src_col = jnp.where(col < LANES, 2 * col, 2 * (col - LANES) + 1)
    perm = (jnp.arange(pw)[:, None] == src_col[None, :]).astype(BF16)

    x2 = x.reshape(t, d)
    z, xbc, dt, xr, gr = _in_proj(x2, norm_mix[0].reshape(1, d), wzx, jnp.concatenate([wdt_hi, wdt_lo], axis=0),
                                  wdt_hi, wxr, wgr)

    y_ssd = _ssd(xbc.reshape(b, s, cdim), dt.reshape(b, s, LANES), z.reshape(b, s, w),
                 _shift_matrix(SSD_CHUNK), ssd_conv_w[0], ssd_conv_b[0].reshape(1, cdim), _pad_lanes(ssd_dt_bias[0]),
                 _pad_lanes(ssd_a_log[0]), dskip, ssd_norm[0].reshape(1, w), e01)
    y_lru = _lru(xr.reshape(b, s, w), gr.reshape(b, s, w), _shift_matrix(LRU_TILE), lru_conv_w[0], lru_conv_b[0].reshape(1, w),
                 wa_bd, lru_ba[0].reshape(1, w), wx_bd, lru_bx[0].reshape(1, w), lru_lambda[0].reshape(1, w))

    kk, vv = _kv(mem.reshape(b * n_mem, d), norm_mem[0].reshape(1, d), w_kv[0].astype(BF16))
    wo_mix = w_out[0].astype(BF16)
    h2, hn, idx_m, rank_m, gate_m, cnt = _mid(
        x2, y_ssd.reshape(t, w), y_lru.reshape(t, w), kk.reshape(b, n_mem, d), vv.reshape(b, n_mem, d),
        wo_mix[:w], wo_mix[w:], norm_xattn[0].reshape(1, d), w_q[0].astype(BF16), w_o[0].astype(BF16),
        norm_moe[0].reshape(1, d), wr_hi, wr_lo, br, s)

    bm = MOE_BLOCK
    counts = cnt[:, 0]
    pcounts = (counts + bm - 1) // bm * bm
    pend = jnp.cumsum(pcounts).astype(jnp.int32)
    pstart = pend - pcounts
    idx = idx_m[:TOP_K, :]
    onehot = idx[:, :, None] == jnp.arange(N_EXPERTS, dtype=jnp.int32)[None, None, :]
    dest = jnp.sum(jnp.where(onehot, pstart[None, None, :], 0), axis=-1) + rank_m[:TOP_K, :]
    n_pairs = t * TOP_K
    nb = (n_pairs + N_EXPERTS * (bm - 1) + bm - 1) // bm
    blk0 = jnp.arange(nb, dtype=jnp.int32) * bm
    block_e = jnp.minimum(jnp.sum(pend[None, :] <= blk0[:, None], axis=1), N_EXPERTS - 1).astype(jnp.int32)
    nvb = (pend[-1] // bm).astype(jnp.int32).reshape(1)
    experts = jnp.arange(N_EXPERTS, dtype=jnp.int32)
    later = (pcounts > 0)[None, :] & (experts[None, :] > experts[:, None])
    follower = jnp.min(jnp.where(later, experts[None, :], N_EXPERTS), axis=1)
    follower = jnp.where(follower == N_EXPERTS, -1, follower)
    next_e = jnp.sum(jnp.where(block_e[:, None] == experts[None, :], follower[None, :], 0), axis=1).astype(jnp.int32)
    dest = dest.astype(jnp.int32)

    def by_tile(tc):
        return dest.reshape(TOP_K, t // tc, tc).transpose(1, 0, 2).reshape(t // tc, 1, TOP_K * tc)

    xrows = _dispatch(pend, by_tile(DISP_TILE), hn, nb * bm, d)
    y = _moe(block_e, next_e, nvb, xrows, w_gate_up[0], bg, bu, w_down[0], bd, perm)
    out = _combine(by_tile(COMB_TILE), h2, gate_m.T, y, norm_final.reshape(1, d))
    return out.reshape(b, s, d)
```

```python
import functools

import jax
import jax.numpy as jnp
from jax import lax
from jax.experimental import pallas as pl
from jax.experimental.pallas import tpu as pltpu

F32 = jnp.float32
BF16 = jnp.bfloat16

NORM_EPS = 1e-6
LANES = 128
SUBLANES = 8
MXU_DIM = 256
SSD_HEAD_DIM = 64
SSD_HEADS = 16
SSD_GROUPS = 4
SSD_STATE = 128
CONV_K = 4
LRU_BLOCKS = 16
RG_C = 8.0
X_HEADS = 4
N_EXPERTS = 32
TOP_K = 4
SWIGLU_LIMIT = 7.0
SWIGLU_ALPHA = 1.702

VMEM_LIMIT = 56 * 1024 * 1024

IN_TILE = 1024
SSD_CHUNK = 256
LRU_TILE = 256
MID_TILE = 1024
MOE_BLOCK = 512
DISP_TILE = 2048
COMB_TILE = 512
ISSUE_UNROLL = 8


def _cparams(sem):
    return pltpu.CompilerParams(dimension_semantics=sem, vmem_limit_bytes=VMEM_LIMIT)


def _rms(x, g):
    ms = jnp.mean(x * x, axis=-1, keepdims=True)
    return x * lax.rsqrt(ms + NORM_EPS) * g


def _sigmoid(x):
    return 0.5 * jnp.tanh(0.5 * x) + 0.5


def _softplus(x):
    return jnp.maximum(x, 0.0) + jnp.log(1.0 + jnp.exp(-jnp.abs(x)))


def _split3(x):
    a = x.astype(BF16)
    r = x - a.astype(F32)
    b = r.astype(BF16)
    c = (r - b.astype(F32)).astype(BF16)
    return a, b, c


def _dot(a, b):
    return jnp.dot(a, b, preferred_element_type=F32)


def _dot_nt(a, b):
    return lax.dot_general(a, b, (((1,), (1,)), ((), ())), preferred_element_type=F32)


def _dot01_right(x, m01):
    a, b, c = _split3(x)
    return _dot(a, m01) + _dot(b, m01) + _dot(c, m01)


def _dot01_left(m01, x):
    a, b, c = _split3(x)
    return _dot(m01, a) + _dot(m01, b) + _dot(m01, c)


def _dot_hilo(x, w_hi, w_lo):
    xh = x.astype(BF16)
    xl = (x - xh.astype(F32)).astype(BF16)
    return _dot(xh, w_hi) + _dot(xl, w_hi) + _dot(xh, w_lo)


def _hilo(w):
    hi = w.astype(BF16)
    lo = (w - hi.astype(F32)).astype(BF16)
    return hi, lo


def _store_token_major(ref, val, base=0):
    n, d = val.shape
    parts = d // LANES
    for s in range(parts):
        ref[pl.ds(base + s, n, stride=parts), :] = val[:, s * LANES:(s + 1) * LANES]


def _load_token_major(ref, n, d, base=0):
    parts = d // LANES
    return [ref[pl.ds(base + s, n, stride=parts), :] for s in range(parts)]


def _in_proj_kernel(x_ref, g_ref, wzx_ref, wdthl_ref, wdth_ref, wxr_ref, wgr_ref,
                    z_ref, xbc_ref, dt_ref, xr_ref, gr_ref):
    d = x_ref.shape[1]
    hn = _rms(x_ref[...], g_ref[...])
    hb = hn.astype(BF16)
    z_ref[...] = _dot_nt(hb, wzx_ref[:d, :]).astype(BF16)
    xbc_ref[...] = _dot_nt(hb, wzx_ref[d:, :]).astype(BF16)
    hl = _dot_nt(hb, wdthl_ref[...])
    hn_lo = (hn - hb.astype(F32)).astype(BF16)
    dt_ref[...] = hl[:, :LANES] + hl[:, LANES:] + _dot_nt(hn_lo, wdth_ref[...])
    xr_ref[...] = _dot_nt(hb, wxr_ref[...]).astype(BF16)
    gr_ref[...] = _dot_nt(hb, wgr_ref[...]).astype(BF16)


def _in_proj(x2, g, wzx, wdt_hilo, wdt_hi, wxr, wgr):
    t, d = x2.shape
    tm = IN_TILE
    nzx = wzx.shape[0]
    const = lambda i: (0, 0)
    row = lambda i: (i, 0)
    return pl.pallas_call(
        _in_proj_kernel,
        grid=(t // tm,),
        in_specs=[
            pl.BlockSpec((tm, d), row),
            pl.BlockSpec((1, d), const),
            pl.BlockSpec((nzx, d), const),
            pl.BlockSpec((2 * LANES, d), const),
            pl.BlockSpec((LANES, d), const),
            pl.BlockSpec((d, d), const),
            pl.BlockSpec((d, d), const),
        ],
        out_specs=[
            pl.BlockSpec((tm, d), row),
            pl.BlockSpec((tm, nzx - d), row),
            pl.BlockSpec((tm, LANES), row),
            pl.BlockSpec((tm, d), row),
            pl.BlockSpec((tm, d), row),
        ],
        out_shape=[
            jax.ShapeDtypeStruct((t, d), BF16),
            jax.ShapeDtypeStruct((t, nzx - d), BF16),
            jax.ShapeDtypeStruct((t, LANES), F32),
            jax.ShapeDtypeStruct((t, d), BF16),
            jax.ShapeDtypeStruct((t, d), BF16),
        ],
        compiler_params=_cparams(("arbitrary",)),
        name="in_proj",
    )(x2, g, wzx, wdt_hilo, wdt_hi, wxr, wgr)


def _shift_matrix(n):
    return jnp.concatenate([jnp.eye(n, k=-(CONV_K - 1 - j), dtype=BF16) for j in range(CONV_K - 1)], axis=0)


def _causal_conv(halo_ref, x, shift_ref, w_ref, b_ref, first):
    n = x.shape[0]
    pad = SUBLANES
    k1 = CONV_K - 1

    @pl.when(first)
    def _():
        halo_ref[...] = jnp.zeros(halo_ref.shape, F32)

    xf = x.astype(F32)
    taps = _dot(shift_ref[...], x)
    acc = b_ref[...] + w_ref[k1:k1 + 1, :] * xf
    for j in range(k1):
        acc = acc + w_ref[j:j + 1, :] * taps[j * n:(j + 1) * n]
    head = acc[:pad]
    for j in range(k1):
        head = head + w_ref[j:j + 1, :] * halo_ref[pad - k1 + j:2 * pad - k1 + j, :]
    halo_ref[0:pad, :] = xf[n - pad:n]
    return jnp.concatenate([head, acc[pad:]], axis=0)


def _ssd_kernel(xbc_ref, dt_ref, z_ref, sh_ref, cw_ref, cb_ref, dtb_ref, alog_ref, dskip_ref, gn_ref, e_ref,
                y_ref, halo_ref, st_ref):
    n = xbc_ref.shape[1]
    w = z_ref.shape[2]
    gw = w // SSD_GROUPS
    first = pl.program_id(1) == 0

    @pl.when(first)
    def _():
        st_ref[...] = jnp.zeros(st_ref.shape, F32)

    conv = _causal_conv(halo_ref, xbc_ref[0], sh_ref, cw_ref, cb_ref, first)
    xc = conv * _sigmoid(conv)
    xs = xc[:, :w]

    dt = _softplus(dt_ref[0] + dtb_ref[...])
    a = -jnp.exp(alog_ref[...])
    da = dt * a
    ri = lax.broadcasted_iota(jnp.int32, (n, n), 0)
    ci = lax.broadcasted_iota(jnp.int32, (n, n), 1)
    causal = ri >= ci
    tril = jnp.where(causal, 1.0, 0.0).astype(BF16)
    a_cs = _dot01_left(tril, da)
    a_cs_t = a_cs.T

    e01 = e_ref[...]
    dt_x = _dot01_right(dt, e01)
    acs_x = _dot01_right(a_cs, e01)
    last_x = acs_x[n - 1:n, :]
    xdt = xs * dt_x
    xdt_b = xdt.astype(BF16)
    xdt_end = (xdt * jnp.exp(last_x - acs_x)).astype(BF16)
    exp_acs = jnp.exp(acs_x)
    chunk_decay = jnp.exp(last_x)
    lane = lax.broadcasted_iota(jnp.int32, (n, gw), 1)

    for g in range(SSD_GROUPS):
        lo = g * gw
        bg = xc[:, w + g * SSD_STATE:w + (g + 1) * SSD_STATE].astype(BF16)
        cg = xc[:, w + (SSD_GROUPS + g) * SSD_STATE:w + (SSD_GROUPS + g + 1) * SSD_STATE].astype(BF16)
        cb = _dot_nt(cg, bg)
        prev = st_ref[g]
        acc = _dot(cg, prev.astype(BF16)) * exp_acs[:, lo:lo + gw]
        new = lax.dot_general(bg, xdt_end[:, lo:lo + gw], (((0,), (0,)), ((), ())),
                              preferred_element_type=F32)
        st_ref[g] = chunk_decay[:, lo:lo + gw] * prev + new
        xg = xdt_b[:, lo:lo + gw]
        for k in range(SSD_HEADS // SSD_GROUPS):
            h = g * (SSD_HEADS // SSD_GROUPS) + k
            seg = a_cs[:, h:h + 1] - a_cs_t[h:h + 1, :]
            dec = jnp.exp(jnp.where(causal, seg, -jnp.inf))
            m = (cb * dec).astype(BF16)
            in_head = (lane >= k * SSD_HEAD_DIM) & (lane < (k + 1) * SSD_HEAD_DIM)
            acc = acc + _dot(m, jnp.where(in_head, xg, jnp.zeros_like(xg)))
        yg = acc + xs[:, lo:lo + gw] * dskip_ref[:, lo:lo + gw]
        zg = z_ref[0, :, lo:lo + gw].astype(F32)
        u = yg * (zg * _sigmoid(zg))
        u = u * lax.rsqrt(jnp.mean(u * u, axis=-1, keepdims=True) + NORM_EPS)
        y_ref[0, :, lo:lo + gw] = (u * gn_ref[:, lo:lo + gw]).astype(BF16)


def _ssd(xbc, dt, z, shift, cw, cb, dtb, alog, dskip, gn, e01):
    b, s, cdim = xbc.shape
    w = z.shape[2]
    n = SSD_CHUNK
    tile = lambda i, j: (i, j, 0)
    const = lambda i, j: (0, 0)
    return pl.pallas_call(
        _ssd_kernel,
        grid=(b, s // n),
        in_specs=[
            pl.BlockSpec((1, n, cdim), tile),
            pl.BlockSpec((1, n, LANES), tile),
            pl.BlockSpec((1, n, w), tile),
            pl.BlockSpec(shift.shape, const),
            pl.BlockSpec((CONV_K, cdim), const),
            pl.BlockSpec((1, cdim), const),
            pl.BlockSpec((1, LANES), const),
            pl.BlockSpec((1, LANES), const),
            pl.BlockSpec((1, w), const),
            pl.BlockSpec((1, w), const),
            pl.BlockSpec((LANES, w), const),
        ],
        out_specs=pl.BlockSpec((1, n, w), tile),
        out_shape=jax.ShapeDtypeStruct((b, s, w), BF16),
        scratch_shapes=[
            pltpu.VMEM((2 * SUBLANES, cdim), F32),
            pltpu.VMEM((SSD_GROUPS, SSD_STATE, w // SSD_GROUPS), F32),
        ],
        compiler_params=_cparams(("arbitrary", "arbitrary")),
        name="ssd",
    )(xbc, dt, z, shift, cw, cb, dtb, alog, dskip, gn, e01)


def _gelu_tanh(x):
    c = 0.7978845608028654
    return 0.5 * x * (1.0 + jnp.tanh(c * (x + 0.044715 * (x * x * x))))


def _lru_kernel(xr_ref, gr_ref, sh_ref, cw_ref, cb_ref, wa_ref, ba_ref, wx_ref, bx_ref, lam_ref,
                y_ref, halo_ref, car_ref, h_ref):
    n = xr_ref.shape[1]
    w = xr_ref.shape[2]
    first = pl.program_id(1) == 0

    @pl.when(first)
    def _():
        car_ref[...] = jnp.zeros(car_ref.shape, F32)

    xc = _causal_conv(halo_ref, xr_ref[0], sh_ref, cw_ref, cb_ref, first)
    xb = xc.astype(BF16)
    nq = wa_ref.shape[0]
    qw = w // nq
    r_parts, i_parts = [], []
    for q in range(nq):
        xq = xb[:, q * qw:(q + 1) * qw]
        r_parts.append(_dot(xq, wa_ref[q]))
        i_parts.append(_dot(xq, wx_ref[q]))
    r = _sigmoid(jnp.concatenate(r_parts, axis=1) + ba_ref[...])
    gi = _sigmoid(jnp.concatenate(i_parts, axis=1) + bx_ref[...])
    log_a = (-RG_C) * r * _softplus(-lam_ref[...])
    a = jnp.exp(log_a)
    u = xc * gi * jnp.sqrt(1.0 - a * a)

    groups = n // SUBLANES
    sub = lax.broadcasted_iota(jnp.int32, (groups, SUBLANES, w), 1)
    ap = a.reshape(groups, SUBLANES, w)
    bp = u.reshape(groups, SUBLANES, w)
    for d in (1, 2, 4):
        m = sub >= d
        bp = jnp.where(m, ap * pltpu.roll(bp, d, 1) + bp, bp)
        ap = jnp.where(m, ap * pltpu.roll(ap, d, 1), ap)
    carry = car_ref[...]
    for g in range(groups):
        hb = bp[g] + ap[g] * carry
        h_ref[g * SUBLANES:(g + 1) * SUBLANES, :] = hb
        carry = jnp.broadcast_to(hb[SUBLANES - 1:SUBLANES, :], (SUBLANES, w))
    car_ref[...] = carry
    y_ref[0] = (h_ref[...] * _gelu_tanh(gr_ref[0].astype(F32))).astype(BF16)


def _lru(xr, gr, shift, cw, cb, wa, ba, wx, bx, lam):
    b, s, w = xr.shape
    n = LRU_TILE
    nq, qw, _ = wa.shape
    tile = lambda i, j: (i, j, 0)
    const = lambda i, j: (0, 0)
    const3 = lambda i, j: (0, 0, 0)
    return pl.pallas_call(
        _lru_kernel,
        grid=(b, s // n),
        in_specs=[
            pl.BlockSpec((1, n, w), tile),
            pl.BlockSpec((1, n, w), tile),
            pl.BlockSpec(shift.shape, const),
            pl.BlockSpec((CONV_K, w), const),
            pl.BlockSpec((1, w), const),
            pl.BlockSpec((nq, qw, qw), const3),
            pl.BlockSpec((1, w), const),
            pl.BlockSpec((nq, qw, qw), const3),
            pl.BlockSpec((1, w), const),
            pl.BlockSpec((1, w), const),
        ],
        out_specs=pl.BlockSpec((1, n, w), tile),
        out_shape=jax.ShapeDtypeStruct((b, s, w), BF16),
        scratch_shapes=[
            pltpu.VMEM((2 * SUBLANES, w), F32),
            pltpu.VMEM((SUBLANES, w), F32),
            pltpu.VMEM((n, w), F32),
        ],
        compiler_params=_cparams(("arbitrary", "arbitrary")),
        name="lru",
    )(xr, gr, shift, cw, cb, wa, ba, wx, bx, lam)


def _kv_kernel(m_ref, g_ref, w_ref, k_ref, v_ref):
    d = m_ref.shape[1]
    mn = _rms(m_ref[...], g_ref[...]).astype(BF16)
    k_ref[...] = _dot(mn, w_ref[:, :d]).astype(BF16)
    v_ref[...] = _dot(mn, w_ref[:, d:]).astype(BF16)


def _kv(mem2, g, wkv):
    t, d = mem2.shape
    tm = min(t, 512)
    row = lambda i: (i, 0)
    const = lambda i: (0, 0)
    return pl.pallas_call(
        _kv_kernel,
        grid=(t // tm,),
        in_specs=[pl.BlockSpec((tm, d), row), pl.BlockSpec((1, d), const), pl.BlockSpec((d, 2 * d), const)],
        out_specs=[pl.BlockSpec((tm, d), row), pl.BlockSpec((tm, d), row)],
        out_shape=[jax.ShapeDtypeStruct((t, d), BF16), jax.ShapeDtypeStruct((t, d), BF16)],
        compiler_params=_cparams(("arbitrary",)),
        name="kv",
    )(mem2, g, wkv)


def _mid_kernel(x_ref, ys_ref, yl_ref, k_ref, v_ref, wo1_ref, wo2_ref, gx_ref, wq_ref, wo_ref,
                gm_ref, wrh_ref, wrl_ref, br_ref,
                h_ref, hn_ref, idx_ref, rank_ref, gate_ref, cnt_ref, car_ref):
    tm, d = x_ref.shape
    hd = d // X_HEADS

    @pl.when(pl.program_id(0) == 0)
    def _():
        car_ref[...] = jnp.zeros(car_ref.shape, F32)

    h1 = x_ref[...] + _dot(ys_ref[...], wo1_ref[...]) + _dot(yl_ref[...], wo2_ref[...])

    q = _dot(_rms(h1, gx_ref[...]).astype(BF16), wq_ref[...]).astype(BF16)
    o_parts = []
    for hh in range(X_HEADS):
        sl = slice(hh * hd, (hh + 1) * hd)
        sc = _dot_nt(q[:, sl], k_ref[0, :, sl]) * (hd ** -0.5)
        sc = sc - jnp.max(sc, axis=-1, keepdims=True)
        p = jnp.exp(sc)
        p = p * (1.0 / jnp.sum(p, axis=-1, keepdims=True))
        o_parts.append(_dot(p.astype(BF16), v_ref[0, :, sl]))
    o = jnp.concatenate(o_parts, axis=1).astype(BF16)
    h2 = h1 + _dot(o, wo_ref[...])
    h_ref[...] = h2

    hn = _rms(h2, gm_ref[...])
    _store_token_major(hn_ref, hn)
    logits = _dot_hilo(hn, wrh_ref[...], wrl_ref[...]) + br_ref[...]

    l = logits.T[:N_EXPERTS, :]
    row = lax.broadcasted_iota(jnp.int32, (N_EXPERTS, tm), 0)
    picked = jnp.zeros((N_EXPERTS, tm), F32)
    vals, idxs = [], []
    for _ in range(TOP_K):
        m = jnp.max(l, axis=0, keepdims=True)
        idx = jnp.min(jnp.where(l == m, row, N_EXPERTS), axis=0, keepdims=True)
        sel = row == idx
        vals.append(m)
        idxs.append(idx)
        picked = jnp.where(sel, 1.0, picked)
        l = jnp.where(sel, -jnp.inf, l)
    ex = [jnp.exp(v - vals[0]) for v in vals]
    den = ex[0] + ex[1] + ex[2] + ex[3]

    ri = lax.broadcasted_iota(jnp.int32, (tm, tm), 0)
    ci = lax.broadcasted_iota(jnp.int32, (tm, tm), 1)
    earlier = jnp.where(ri < ci, 1.0, 0.0).astype(BF16)
    before = _dot(picked.astype(BF16), earlier) + car_ref[:, 0:1]
    out_row = lax.broadcasted_iota(jnp.int32, (SUBLANES, tm), 0)
    idx_out = jnp.zeros((SUBLANES, tm), jnp.int32)
    rank_out = jnp.zeros((SUBLANES, tm), jnp.int32)
    gate_out = jnp.zeros((SUBLANES, tm), F32)
    for k in range(TOP_K):
        rk = jnp.sum(jnp.where(row == idxs[k], before, 0.0), axis=0, keepdims=True)
        at_k = out_row == k
        idx_out = jnp.where(at_k, idxs[k], idx_out)
        rank_out = jnp.where(at_k, rk.astype(jnp.int32), rank_out)
        gate_out = jnp.where(at_k, ex[k] / den, gate_out)
    idx_ref[...] = idx_out
    rank_ref[...] = rank_out
    gate_ref[...] = gate_out
    total = car_ref[...] + jnp.sum(picked, axis=1, keepdims=True)
    car_ref[...] = total
    cnt_ref[...] = total.astype(jnp.int32)


def _mid(x2, ys, yl, kk, vv, wo1, wo2, gx, wq, wo, gm, wr_hi, wr_lo, br, seq):
    t, d = x2.shape
    tm = MID_TILE
    m = kk.shape[1]
    per_b = seq // tm
    row = lambda i: (i, 0)
    col = lambda i: (0, i)
    const = lambda i: (0, 0)
    kvmap = lambda i: (i // per_b, 0, 0)
    wspec = pl.BlockSpec((d, d), const)
    vspec = pl.BlockSpec((1, d), const)
    return pl.pallas_call(
        _mid_kernel,
        grid=(t // tm,),
        in_specs=[
            pl.BlockSpec((tm, d), row), pl.BlockSpec((tm, d), row), pl.BlockSpec((tm, d), row),
            pl.BlockSpec((1, m, d), kvmap), pl.BlockSpec((1, m, d), kvmap),
            wspec, wspec, vspec, wspec, wspec, vspec,
            pl.BlockSpec((d, LANES), const), pl.BlockSpec((d, LANES), const), pl.BlockSpec((1, LANES), const),
        ],
        out_specs=[
            pl.BlockSpec((tm, d), row), pl.BlockSpec((tm * d // LANES, LANES), row),
            pl.BlockSpec((SUBLANES, tm), col), pl.BlockSpec((SUBLANES, tm), col), pl.BlockSpec((SUBLANES, tm), col),
            pl.BlockSpec((N_EXPERTS, LANES), const),
        ],
        out_shape=[
            jax.ShapeDtypeStruct((t, d), F32), jax.ShapeDtypeStruct((t * d // LANES, LANES), F32),
            jax.ShapeDtypeStruct((SUBLANES, t), jnp.int32), jax.ShapeDtypeStruct((SUBLANES, t), jnp.int32),
            jax.ShapeDtypeStruct((SUBLANES, t), F32),
            jax.ShapeDtypeStruct((N_EXPERTS, LANES), jnp.int32),
        ],
        scratch_shapes=[pltpu.VMEM((N_EXPERTS, LANES), F32)],
        compiler_params=_cparams(("arbitrary",)),
        name="mid",
    )(x2, ys, yl, kk, vv, wo1, wo2, gx, wq, wo, gm, wr_hi, wr_lo, br)


def _dispatch_kernel(pend_ref, dst_ref, hn_ref, x_hbm, zero_ref, sem, *, parts):
    i = pl.program_id(0)
    tc = hn_ref.shape[0] // parts
    bm = zero_ref.shape[0] // parts
    rows = TOP_K * tc

    def tile(ref, n, count=1):
        return ref.at[pl.ds(pl.multiple_of(n * parts, parts), count * parts), :]

    @pl.when(i == 0)
    def _():
        zero_ref[...] = jnp.zeros(zero_ref.shape, F32)

        def fill(start):
            return pltpu.make_async_copy(zero_ref, tile(x_hbm, start, bm), sem.at[1])

        for e in range(N_EXPERTS):
            fill(jnp.maximum(pend_ref[e] - bm, 0)).start()
        for e in range(N_EXPERTS):
            fill(0).wait()

        def tail(j, c):
            fill(j * bm).start()
            fill(0).wait()
            return c

        lax.fori_loop(pend_ref[N_EXPERTS - 1] // bm, x_hbm.shape[0] // (bm * parts), tail, 0)

    def body(j, c):
        t0 = pl.multiple_of(j * ISSUE_UNROLL, ISSUE_UNROLL)
        for u in range(ISSUE_UNROLL):
            for k in range(TOP_K):
                dst = dst_ref[0, 0, k * tc + t0 + u]
                pltpu.make_async_copy(tile(hn_ref, t0 + u), tile(x_hbm, dst), sem.at[0]).start(priority=k % 2)
        return c

    lax.fori_loop(0, tc // ISSUE_UNROLL, body, 0)
    pltpu.make_async_copy(tile(x_hbm, 0, rows), tile(x_hbm, 0, rows), sem.at[0]).wait()


def _dispatch(pend, dest3, hn_tm, n_rows, d):
    parts = d // LANES
    t = hn_tm.shape[0] // parts
    tc = DISP_TILE
    grid_spec = pltpu.PrefetchScalarGridSpec(
        num_scalar_prefetch=1,
        grid=(t // tc,),
        in_specs=[
            pl.BlockSpec((1, 1, TOP_K * tc), lambda i, pe: (i, 0, 0), memory_space=pltpu.SMEM),
            pl.BlockSpec((tc * parts, LANES), lambda i, pe: (i, 0)),
        ],
        out_specs=pl.BlockSpec(memory_space=pl.ANY),
        scratch_shapes=[pltpu.VMEM((MOE_BLOCK * parts, LANES), F32), pltpu.SemaphoreType.DMA((2,))],
    )
    return pl.pallas_call(
        functools.partial(_dispatch_kernel, parts=parts),
        grid_spec=grid_spec,
        out_shape=jax.ShapeDtypeStruct((n_rows * parts, LANES), F32),
        compiler_params=_cparams(("arbitrary",)),
        name="dispatch",
    )(pend, dest3, hn_tm)


def _moe_kernel(be_ref, nxt_ref, nvb_ref, x_ref, wgu_hbm, bg_ref, bu_ref, wd_hbm, bd_ref, perm_ref,
                y_ref, wgu_f, wd_f, wgu_s, wd_s, act_s, sem):
    i = pl.program_id(0)
    nvb = nvb_ref[0]
    f, d = wd_f.shape
    bm = act_s.shape[0]
    pw = perm_ref.shape[0]
    half = pw // 2

    def fetch(e):
        return (pltpu.make_async_copy(wgu_hbm.at[e], wgu_f, sem.at[0]),
                pltpu.make_async_copy(wd_hbm.at[e], wd_f, sem.at[1]))

    @pl.when(jnp.logical_and(i == 0, nvb > 0))
    def _():
        for c in fetch(be_ref[0]):
            c.start()

    changed = jnp.logical_or(i == 0, be_ref[i] != be_ref[jnp.maximum(i - 1, 0)])

    @pl.when(jnp.logical_and(changed, i < nvb))
    def _():
        for c in fetch(be_ref[i]):
            c.wait()
        for c in range(2 * f // pw):
            wc = wgu_f[:, c * pw:(c + 1) * pw].astype(BF16)
            wgu_s[:, c * pw:(c + 1) * pw] = _dot(wc, perm_ref[...]).astype(BF16)
        wd_s[...] = wd_f[...].astype(BF16)

        @pl.when(nxt_ref[i] >= 0)
        def _():
            for c in fetch(nxt_ref[i]):
                c.start()

    @pl.when(i < nvb)
    def _():
        e = be_ref[i]
        xb = jnp.concatenate([p.astype(BF16) for p in _load_token_major(x_ref, bm, d)], axis=1)
        for c in range(2 * f // pw):
            gu = _dot(xb, wgu_s[:, c * pw:(c + 1) * pw])
            g = gu[:, :half] + bg_ref[e, :, c * half:(c + 1) * half]
            u = gu[:, half:] + bu_ref[e, :, c * half:(c + 1) * half]
            g = jnp.minimum(g, SWIGLU_LIMIT)
            u = jnp.clip(u, -SWIGLU_LIMIT, SWIGLU_LIMIT)
            act = (u + 1.0) * (g * _sigmoid(SWIGLU_ALPHA * g))
            act_s[:, c * half:(c + 1) * half] = act.astype(BF16)
        _store_token_major(y_ref, _dot(act_s[...], wd_s[...]) + bd_ref[e])

    @pl.when(i >= nvb)
    def _():
        y_ref[...] = jnp.zeros(y_ref.shape, F32)


def _moe(block_e, next_e, nvb, xrows_tm, wgu, bg, bu, wd, bd, perm):
    f, d = wd.shape[1], wd.shape[2]
    parts = d // LANES
    bm = MOE_BLOCK
    nb = xrows_tm.shape[0] // (bm * parts)

    def whole(a):
        return pl.BlockSpec(a.shape, lambda i, be, nx, nv, nd=a.ndim: (0,) * nd)

    grid_spec = pltpu.PrefetchScalarGridSpec(
        num_scalar_prefetch=3,
        grid=(nb,),
        in_specs=[
            pl.BlockSpec((bm * parts, LANES),
                         lambda i, be, nx, nv: (jnp.minimum(i, jnp.maximum(nv[0] - 1, 0)), 0)),
            pl.BlockSpec(memory_space=pl.ANY),
            whole(bg), whole(bu),
            pl.BlockSpec(memory_space=pl.ANY),
            whole(bd), whole(perm),
        ],
        out_specs=pl.BlockSpec((bm * parts, LANES), lambda i, be, nx, nv: (i, 0)),
        scratch_shapes=[
            pltpu.VMEM((d, 2 * f), F32), pltpu.VMEM((f, d), F32),
            pltpu.VMEM((d, 2 * f), BF16), pltpu.VMEM((f, d), BF16),
            pltpu.VMEM((bm, f), BF16),
            pltpu.SemaphoreType.DMA((2,)),
        ],
    )
    return pl.pallas_call(
        _moe_kernel,
        grid_spec=grid_spec,
        out_shape=jax.ShapeDtypeStruct(xrows_tm.shape, F32),
        compiler_params=_cparams(("arbitrary",)),
        name="moe",
    )(block_e, next_e, nvb, xrows_tm, wgu, bg, bu, wd, bd, perm)


def _comb_kernel(dst_ref, dstn_ref, h_ref, gate_ref, y_hbm, g_ref, o_ref, ybuf, sem, *, parts):
    i = pl.program_id(0)
    n = pl.num_programs(0)
    tc, d = h_ref.shape
    rows = TOP_K * tc
    slot = i % 2

    def tile(ref, n, count=1):
        return ref.at[pl.ds(pl.multiple_of(n * parts, parts), count * parts), :]

    def start_rows(idx_ref, s):
        def body(j, c):
            t0 = pl.multiple_of(j * ISSUE_UNROLL, ISSUE_UNROLL)
            for u in range(ISSUE_UNROLL):
                for k in range(TOP_K):
                    r = k * tc + t0 + u
                    pltpu.make_async_copy(tile(y_hbm, idx_ref[0, 0, r]), tile(ybuf.at[s], r),
                                          sem.at[s]).start(priority=k % 2)
            return c

        lax.fori_loop(0, tc // ISSUE_UNROLL, body, 0)

    @pl.when(i == 0)
    def _():
        start_rows(dst_ref, 0)

    @pl.when(i + 1 < n)
    def _():
        start_rows(dstn_ref, 1 - slot)

    pltpu.make_async_copy(tile(y_hbm, 0, rows), ybuf.at[slot], sem.at[slot]).wait()
    acc = [h_ref[:, s * LANES:(s + 1) * LANES] for s in range(parts)]
    for k in range(TOP_K):
        yk = _load_token_major(ybuf.at[slot], tc, d, base=k * tc * parts)
        gk = gate_ref[:, k:k + 1]
        acc = [a + p * gk for a, p in zip(acc, yk)]
    o_ref[...] = _rms(jnp.concatenate(acc, axis=1), g_ref[...])


def _combine(dest3, h2, gate, y_tm, g):
    t, d = h2.shape
    parts = d // LANES
    tc = COMB_TILE
    nt = t // tc
    return pl.pallas_call(
        functools.partial(_comb_kernel, parts=parts),
        grid=(nt,),
        in_specs=[
            pl.BlockSpec((1, 1, TOP_K * tc), lambda i: (i, 0, 0), memory_space=pltpu.SMEM),
            pl.BlockSpec((1, 1, TOP_K * tc), lambda i: (jnp.minimum(i + 1, nt - 1), 0, 0),
                         memory_space=pltpu.SMEM),
            pl.BlockSpec((tc, d), lambda i: (i, 0)),
            pl.BlockSpec((tc, SUBLANES), lambda i: (i, 0)),
            pl.BlockSpec(memory_space=pl.ANY),
            pl.BlockSpec((1, d), lambda i: (0, 0)),
        ],
        out_specs=pl.BlockSpec((tc, d), lambda i: (i, 0)),
        out_shape=jax.ShapeDtypeStruct((t, d), F32),
        scratch_shapes=[pltpu.VMEM((2, TOP_K * tc * parts, LANES), F32), pltpu.SemaphoreType.DMA((2,))],
        compiler_params=_cparams(("arbitrary",)),
        name="combine",
    )(dest3, dest3, h2, gate, y_tm, g)


def _block_diag(wb, per):
    nb, bw, _ = wb.shape
    wq = wb.reshape(nb // per, per, bw, bw)
    eye = jnp.eye(per, dtype=wb.dtype)
    out = jnp.einsum('qaij,ab->qaibj', wq, eye)
    return out.reshape(nb // per, per * bw, per * bw)


def _pad_lanes(v, fill=0.0):
    return jnp.pad(v, (0, LANES - v.shape[0]), constant_values=fill).reshape(1, LANES)


def kernel(x, mem, norm_mix, w_in, ssd_conv_w, ssd_conv_b, ssd_dt_bias, ssd_a_log, ssd_d, ssd_norm, lru_conv_w, lru_conv_b, lru_wa, lru_ba, lru_wx, lru_bx, lru_lambda, w_out, norm_xattn, norm_mem, w_q, w_kv, w_o, norm_moe, w_router, b_router, w_gate_up, b_gate_up, w_down, b_down, norm_final):
    b, s, d = x.shape
    t = b * s
    n_mem = mem.shape[1]
    w = d
    cdim = w + 2 * SSD_GROUPS * SSD_STATE
    o1, o2, o3, o4 = w, w + cdim, w + cdim + SSD_HEADS, w + cdim + SSD_HEADS + w

    wi = jnp.swapaxes(w_in[0], 0, 1)
    wzx = wi[:o2].astype(BF16)
    wdt = jnp.pad(wi[o2:o3], ((0, LANES - SSD_HEADS), (0, 0)))
    wdt_hi, wdt_lo = _hilo(wdt)
    wxr = wi[o3:o4].astype(BF16)
    wgr = wi[o4:].astype(BF16)
    e01 = (jnp.arange(LANES)[:, None] == (jnp.arange(w)[None, :] // SSD_HEAD_DIM)).astype(BF16)
    dskip = jnp.repeat(ssd_d[0], SSD_HEAD_DIM).reshape(1, w)
    per = MXU_DIM // (w // LRU_BLOCKS)
    wa_bd = _block_diag(lru_wa[0], per).astype(BF16)
    wx_bd = _block_diag(lru_wx[0], per).astype(BF16)
    wr = jnp.pad(w_router[0], ((0, 0), (0, LANES - N_EXPERTS)))
    wr_hi, wr_lo = _hilo(wr)
    br = _pad_lanes(b_router[0], fill=-1e30)
    bgu = b_gate_up[0]
    bg = bgu[:, None, 0::2]
    bu = bgu[:, None, 1::2]
    bd = b_down[0][:, None, :]
    pw = 2 * LANES
    col = jnp.arange(pw)
    src_col = jnp.where(col < LANES, 2 * col, 2 * (col - LANES) + 1)
    perm = (jnp.arange(pw)[:, None] == src_col[None, :]).astype(BF16)

    x2 = x.reshape(t, d)
    z, xbc, dt, xr, gr = _in_proj(x2, norm_mix[0].reshape(1, d), wzx, jnp.concatenate([wdt_hi, wdt_lo], axis=0),
                                  wdt_hi, wxr, wgr)

    y_ssd = _ssd(xbc.reshape(b, s, cdim), dt.reshape(b, s, LANES), z.reshape(b, s, w),
                 _shift_matrix(SSD_CHUNK), ssd_conv_w[0], ssd_conv_b[0].reshape(1, cdim), _pad_lanes(ssd_dt_bias[0]),
                 _pad_lanes(ssd_a_log[0]), dskip, ssd_norm[0].reshape(1, w), e01)
    y_lru = _lru(xr.reshape(b, s, w), gr.reshape(b, s, w), _shift_matrix(LRU_TILE), lru_conv_w[0], lru_conv_b[0].reshape(1, w),
                 wa_bd, lru_ba[0].reshape(1, w), wx_bd, lru_bx[0].reshape(1, w), lru_lambda[0].reshape(1, w))

    kk, vv = _kv(mem.reshape(b * n_mem, d), norm_mem[0].reshape(1, d), w_kv[0].astype(BF16))
    wo_mix = w_out[0].astype(BF16)
    h2, hn, idx_m, rank_m, gate_m, cnt = _mid(
        x2, y_ssd.reshape(t, w), y_lru.reshape(t, w), kk.reshape(b, n_mem, d), vv.reshape(b, n_mem, d),
        wo_mix[:w], wo_mix[w:], norm_xattn[0].reshape(1, d), w_q[0].astype(BF16), w_o[0].astype(BF16),
        norm_moe[0].reshape(1, d), wr_hi, wr_lo, br, s)

    bm = MOE_BLOCK
    counts = cnt[:, 0]
    pcounts = (counts + bm - 1) // bm * bm
    pend = jnp.cumsum(pcounts).astype(jnp.int32)
    pstart = pend - pcounts
    idx = idx_m[:TOP_K, :]
    onehot = idx[:, :, None] == jnp.arange(N_EXPERTS, dtype=jnp.int32)[None, None, :]
    dest = jnp.sum(jnp.where(onehot, pstart[None, None, :], 0), axis=-1) + rank_m[:TOP_K, :]
    n_pairs = t * TOP_K
    nb = (n_pairs + N_EXPERTS * (bm - 1) + bm - 1) // bm
    blk0 = jnp.arange(nb, dtype=jnp.int32) * bm
    block_e = jnp.minimum(jnp.sum(pend[None, :] <= blk0[:, None], axis=1), N_EXPERTS - 1).astype(jnp.int32)
    nvb = (pend[-1] // bm).astype(jnp.int32).reshape(1)
    experts = jnp.arange(N_EXPERTS, dtype=jnp.int32)
    later = (pcounts > 0)[None, :] & (experts[None, :] > experts[:, None])
    follower = jnp.min(jnp.where(later, experts[None, :], N_EXPERTS), axis=1)
    follower = jnp.where(follower == N_EXPERTS, -1, follower)
    next_e = jnp.sum(jnp.where(block_e[:, None] == experts[None, :], follower[None, :], 0), axis=1).astype(jnp.int32)
    dest = dest.astype(jnp.int32)

    def by_tile(tc):
        return dest.reshape(TOP_K, t // tc, tc).transpose(1, 0, 2).reshape(t // tc, 1, TOP_K * tc)

    xrows = _dispatch(pend, by_tile(DISP_TILE), hn, nb * bm, d)
    y = _moe(block_e, next_e, nvb, xrows, w_gate_up[0], bg, bu, w_down[0], bd, perm)
    out = _combine(by_tile(COMB_TILE), h2, gate_m.T, y, norm_final.reshape(1, d))
    return out.reshape(b, s, d)
```

```python
import functools

import jax
import jax.numpy as jnp
from jax import lax
from jax.experimental import pallas as pl
from jax.experimental.pallas import tpu as pltpu

F32 = jnp.float32
BF16 = jnp.bfloat16

NORM_EPS = 1e-6
LANES = 128
SUBLANES = 8
MXU_DIM = 256
SSD_HEAD_DIM = 64
SSD_HEADS = 16
SSD_GROUPS = 4
SSD_STATE = 128
CONV_K = 4
LRU_BLOCKS = 16
RG_C = 8.0
X_HEADS = 4
N_EXPERTS = 32
TOP_K = 4
SWIGLU_LIMIT = 7.0
SWIGLU_ALPHA = 1.702

VMEM_LIMIT = 56 * 1024 * 1024

IN_TILE = 1024
SSD_CHUNK = 256
LRU_TILE = 256
MID_TILE = 1024
MOE_BLOCK = 512
DISP_TILE = 2048
COMB_TILE = 256
ISSUE_UNROLL = 8


def _cparams(sem):
    return pltpu.CompilerParams(dimension_semantics=sem, vmem_limit_bytes=VMEM_LIMIT)


def _rms(x, g):
    ms = jnp.mean(x * x, axis=-1, keepdims=True)
    return x * lax.rsqrt(ms + NORM_EPS) * g


def _sigmoid(x):
    return 0.5 * jnp.tanh(0.5 * x) + 0.5


def _softplus(x):
    return jnp.maximum(x, 0.0) + jnp.log(1.0 + jnp.exp(-jnp.abs(x)))


def _split3(x):
    a = x.astype(BF16)
    r = x - a.astype(F32)
    b = r.astype(BF16)
    c = (r - b.astype(F32)).astype(BF16)
    return a, b, c


def _dot(a, b):
    return jnp.dot(a, b, preferred_element_type=F32)


def _dot_nt(a, b):
    return lax.dot_general(a, b, (((1,), (1,)), ((), ())), preferred_element_type=F32)


def _dot01_right(x, m01):
    a, b, c = _split3(x)
    return _dot(a, m01) + _dot(b, m01) + _dot(c, m01)


def _dot01_left(m01, x):
    a, b, c = _split3(x)
    return _dot(m01, a) + _dot(m01, b) + _dot(m01, c)


def _dot_hilo(x, w_hi, w_lo):
    xh = x.astype(BF16)
    xl = (x - xh.astype(F32)).astype(BF16)
    return _dot(xh, w_hi) + _dot(xl, w_hi) + _dot(xh, w_lo)


def _hilo(w):
    hi = w.astype(BF16)
    lo = (w - hi.astype(F32)).astype(BF16)
    return hi, lo


def _store_token_major(ref, val, base=0):
    n, d = val.shape
    parts = d // LANES
    for s in range(parts):
        ref[pl.ds(base + s, n, stride=parts), :] = val[:, s * LANES:(s + 1) * LANES]


def _load_token_major(ref, n, d, base=0):
    parts = d // LANES
    return [ref[pl.ds(base + s, n, stride=parts), :] for s in range(parts)]


def _in_proj_kernel(x_ref, g_ref, wzx_ref, wdthl_ref, wdth_ref, wxr_ref, wgr_ref,
                    z_ref, xbc_ref, dt_ref, xr_ref, gr_ref):
    d = x_ref.shape[1]
    hn = _rms(x_ref[...], g_ref[...])
    hb = hn.astype(BF16)
    z_ref[...] = _dot_nt(hb, wzx_ref[:d, :]).astype(BF16)
    xbc_ref[...] = _dot_nt(hb, wzx_ref[d:, :]).astype(BF16)
    hl = _dot_nt(hb, wdthl_ref[...])
    hn_lo = (hn - hb.astype(F32)).astype(BF16)
    dt_ref[...] = hl[:, :LANES] + hl[:, LANES:] + _dot_nt(hn_lo, wdth_ref[...])
    xr_ref[...] = _dot_nt(hb, wxr_ref[...]).astype(BF16)
    gr_ref[...] = _dot_nt(hb, wgr_ref[...]).astype(BF16)


def _in_proj(x2, g, wzx, wdt_hilo, wdt_hi, wxr, wgr):
    t, d = x2.shape
    tm = IN_TILE
    nzx = wzx.shape[0]
    const = lambda i: (0, 0)
    row = lambda i: (i, 0)
    return pl.pallas_call(
        _in_proj_kernel,
        grid=(t // tm,),
        in_specs=[
            pl.BlockSpec((tm, d), row),
            pl.BlockSpec((1, d), const),
            pl.BlockSpec((nzx, d), const),
            pl.BlockSpec((2 * LANES, d), const),
            pl.BlockSpec((LANES, d), const),
            pl.BlockSpec((d, d), const),
            pl.BlockSpec((d, d), const),
        ],
        out_specs=[
            pl.BlockSpec((tm, d), row),
            pl.BlockSpec((tm, nzx - d), row),
            pl.BlockSpec((tm, LANES), row),
            pl.BlockSpec((tm, d), row),
            pl.BlockSpec((tm, d), row),
        ],
        out_shape=[
            jax.ShapeDtypeStruct((t, d), BF16),
            jax.ShapeDtypeStruct((t, nzx - d), BF16),
            jax.ShapeDtypeStruct((t, LANES), F32),
            jax.ShapeDtypeStruct((t, d), BF16),
            jax.ShapeDtypeStruct((t, d), BF16),
        ],
        compiler_params=_cparams(("arbitrary",)),
        name="in_proj",
    )(x2, g, wzx, wdt_hilo, wdt_hi, wxr, wgr)


def _shift_matrix(n):
    return jnp.concatenate([jnp.eye(n, k=-(CONV_K - 1 - j), dtype=BF16) for j in range(CONV_K - 1)], axis=0)


def _causal_conv(halo_ref, x, shift_ref, w_ref, b_ref, first):
    n = x.shape[0]
    pad = SUBLANES
    k1 = CONV_K - 1

    @pl.when(first)
    def _():
        halo_ref[...] = jnp.zeros(halo_ref.shape, F32)

    xf = x.astype(F32)
    taps = _dot(shift_ref[...], x)
    acc = b_ref[...] + w_ref[k1:k1 + 1, :] * xf
    for j in range(k1):
        acc = acc + w_ref[j:j + 1, :] * taps[j * n:(j + 1) * n]
    head = acc[:pad]
    for j in range(k1):
        head = head + w_ref[j:j + 1, :] * halo_ref[pad - k1 + j:2 * pad - k1 + j, :]
    halo_ref[0:pad, :] = xf[n - pad:n]
    return jnp.concatenate([head, acc[pad:]], axis=0)


def _ssd_kernel(xbc_ref, dt_ref, z_ref, sh_ref, cw_ref, cb_ref, dtb_ref, alog_ref, dskip_ref, gn_ref, e_ref,
                y_ref, halo_ref, st_ref):
    n = xbc_ref.shape[1]
    w = z_ref.shape[2]
    gw = w // SSD_GROUPS
    first = pl.program_id(1) == 0

    @pl.when(first)
    def _():
        st_ref[...] = jnp.zeros(st_ref.shape, F32)

    conv = _causal_conv(halo_ref, xbc_ref[0], sh_ref, cw_ref, cb_ref, first)
    xc = conv * _sigmoid(conv)
    xs = xc[:, :w]

    dt = _softplus(dt_ref[0] + dtb_ref[...])
    a = -jnp.exp(alog_ref[...])
    da = dt * a
    ri = lax.broadcasted_iota(jnp.int32, (n, n), 0)
    ci = lax.broadcasted_iota(jnp.int32, (n, n), 1)
    causal = ri >= ci
    tril = jnp.where(causal, 1.0, 0.0).astype(BF16)
    a_cs = _dot01_left(tril, da)
    a_cs_t = a_cs.T

    e01 = e_ref[...]
    dt_x = _dot01_right(dt, e01)
    acs_x = _dot01_right(a_cs, e01)
    last_x = acs_x[n - 1:n, :]
    xdt = xs * dt_x
    xdt_b = xdt.astype(BF16)
    xdt_end = (xdt * jnp.exp(last_x - acs_x)).astype(BF16)
    exp_acs = jnp.exp(acs_x)
    chunk_decay = jnp.exp(last_x)
    lane = lax.broadcasted_iota(jnp.int32, (n, gw), 1)

    for g in range(SSD_GROUPS):
        lo = g * gw
        bg = xc[:, w + g * SSD_STATE:w + (g + 1) * SSD_STATE].astype(BF16)
        cg = xc[:, w + (SSD_GROUPS + g) * SSD_STATE:w + (SSD_GROUPS + g + 1) * SSD_STATE].astype(BF16)
        cb = _dot_nt(cg, bg)
        prev = st_ref[g]
        acc = _dot(cg, prev.astype(BF16)) * exp_acs[:, lo:lo + gw]
        new = lax.dot_general(bg, xdt_end[:, lo:lo + gw], (((0,), (0,)), ((), ())),
                              preferred_element_type=F32)
        st_ref[g] = chunk_decay[:, lo:lo + gw] * prev + new
        xg = xdt_b[:, lo:lo + gw]
        for k in range(SSD_HEADS // SSD_GROUPS):
            h = g * (SSD_HEADS // SSD_GROUPS) + k
            seg = a_cs[:, h:h + 1] - a_cs_t[h:h + 1, :]
            dec = jnp.exp(jnp.where(causal, seg, -jnp.inf))
            m = (cb * dec).astype(BF16)
            in_head = (lane >= k * SSD_HEAD_DIM) & (lane < (k + 1) * SSD_HEAD_DIM)
            acc = acc + _dot(m, jnp.where(in_head, xg, jnp.zeros_like(xg)))
        yg = acc + xs[:, lo:lo + gw] * dskip_ref[:, lo:lo + gw]
        zg = z_ref[0, :, lo:lo + gw].astype(F32)
        u = yg * (zg * _sigmoid(zg))
        u = u * lax.rsqrt(jnp.mean(u * u, axis=-1, keepdims=True) + NORM_EPS)
        y_ref[0, :, lo:lo + gw] = (u * gn_ref[:, lo:lo + gw]).astype(BF16)


def _ssd(xbc, dt, z, shift, cw, cb, dtb, alog, dskip, gn, e01):
    b, s, cdim = xbc.shape
    w = z.shape[2]
    n = SSD_CHUNK
    tile = lambda i, j: (i, j, 0)
    const = lambda i, j: (0, 0)
    return pl.pallas_call(
        _ssd_kernel,
        grid=(b, s // n),
        in_specs=[
            pl.BlockSpec((1, n, cdim), tile),
            pl.BlockSpec((1, n, LANES), tile),
            pl.BlockSpec((1, n, w), tile),
            pl.BlockSpec(shift.shape, const),
            pl.BlockSpec((CONV_K, cdim), const),
            pl.BlockSpec((1, cdim), const),
            pl.BlockSpec((1, LANES), const),
            pl.BlockSpec((1, LANES), const),
            pl.BlockSpec((1, w), const),
            pl.BlockSpec((1, w), const),
            pl.BlockSpec((LANES, w), const),
        ],
        out_specs=pl.BlockSpec((1, n, w), tile),
        out_shape=jax.ShapeDtypeStruct((b, s, w), BF16),
        scratch_shapes=[
            pltpu.VMEM((2 * SUBLANES, cdim), F32),
            pltpu.VMEM((SSD_GROUPS, SSD_STATE, w // SSD_GROUPS), F32),
        ],
        compiler_params=_cparams(("arbitrary", "arbitrary")),
        name="ssd",
    )(xbc, dt, z, shift, cw, cb, dtb, alog, dskip, gn, e01)


def _gelu_tanh(x):
    c = 0.7978845608028654
    return 0.5 * x * (1.0 + jnp.tanh(c * (x + 0.044715 * (x * x * x))))


def _lru_kernel(xr_ref, gr_ref, sh_ref, cw_ref, cb_ref, wa_ref, ba_ref, wx_ref, bx_ref, lam_ref,
                y_ref, halo_ref, car_ref, h_ref):
    n = xr_ref.shape[1]
    w = xr_ref.shape[2]
    first = pl.program_id(1) == 0

    @pl.when(first)
    def _():
        car_ref[...] = jnp.zeros(car_ref.shape, F32)

    xc = _causal_conv(halo_ref, xr_ref[0], sh_ref, cw_ref, cb_ref, first)
    xb = xc.astype(BF16)
    nq = wa_ref.shape[0]
    qw = w // nq
    r_parts, i_parts = [], []
    for q in range(nq):
        xq = xb[:, q * qw:(q + 1) * qw]
        r_parts.append(_dot(xq, wa_ref[q]))
        i_parts.append(_dot(xq, wx_ref[q]))
    r = _sigmoid(jnp.concatenate(r_parts, axis=1) + ba_ref[...])
    gi = _sigmoid(jnp.concatenate(i_parts, axis=1) + bx_ref[...])
    log_a = (-RG_C) * r * _softplus(-lam_ref[...])
    a = jnp.exp(log_a)
    u = xc * gi * jnp.sqrt(1.0 - a * a)

    groups = n // SUBLANES
    sub = lax.broadcasted_iota(jnp.int32, (groups, SUBLANES, w), 1)
    ap = a.reshape(groups, SUBLANES, w)
    bp = u.reshape(groups, SUBLANES, w)
    for d in (1, 2, 4):
        m = sub >= d
        bp = jnp.where(m, ap * pltpu.roll(bp, d, 1) + bp, bp)
        ap = jnp.where(m, ap * pltpu.roll(ap, d, 1), ap)
    carry = car_ref[...]
    for g in range(groups):
        hb = bp[g] + ap[g] * carry
        h_ref[g * SUBLANES:(g + 1) * SUBLANES, :] = hb
        carry = jnp.broadcast_to(hb[SUBLANES - 1:SUBLANES, :], (SUBLANES, w))
    car_ref[...] = carry
    y_ref[0] = (h_ref[...] * _gelu_tanh(gr_ref[0].astype(F32))).astype(BF16)


def _lru(xr, gr, shift, cw, cb, wa, ba, wx, bx, lam):
    b, s, w = xr.shape
    n = LRU_TILE
    nq, qw, _ = wa.shape
    tile = lambda i, j: (i, j, 0)
    const = lambda i, j: (0, 0)
    const3 = lambda i, j: (0, 0, 0)
    return pl.pallas_call(
        _lru_kernel,
        grid=(b, s // n),
        in_specs=[
            pl.BlockSpec((1, n, w), tile),
            pl.BlockSpec((1, n, w), tile),
            pl.BlockSpec(shift.shape, const),
            pl.BlockSpec((CONV_K, w), const),
            pl.BlockSpec((1, w), const),
            pl.BlockSpec((nq, qw, qw), const3),
            pl.BlockSpec((1, w), const),
            pl.BlockSpec((nq, qw, qw), const3),
            pl.BlockSpec((1, w), const),
            pl.BlockSpec((1, w), const),
        ],
        out_specs=pl.BlockSpec((1, n, w), tile),
        out_shape=jax.ShapeDtypeStruct((b, s, w), BF16),
        scratch_shapes=[
            pltpu.VMEM((2 * SUBLANES, w), F32),
            pltpu.VMEM((SUBLANES, w), F32),
            pltpu.VMEM((n, w), F32),
        ],
        compiler_params=_cparams(("arbitrary", "arbitrary")),
        name="lru",
    )(xr, gr, shift, cw, cb, wa, ba, wx, bx, lam)


def _kv_kernel(m_ref, g_ref, w_ref, k_ref, v_ref):
    d = m_ref.shape[1]
    mn = _rms(m_ref[...], g_ref[...]).astype(BF16)
    k_ref[...] = _dot(mn, w_ref[:, :d]).astype(BF16)
    v_ref[...] = _dot(mn, w_ref[:, d:]).astype(BF16)


def _kv(mem2, g, wkv):
    t, d = mem2.shape
    tm = min(t, 512)
    row = lambda i: (i, 0)
    const = lambda i: (0, 0)
    return pl.pallas_call(
        _kv_kernel,
        grid=(t // tm,),
        in_specs=[pl.BlockSpec((tm, d), row), pl.BlockSpec((1, d), const), pl.BlockSpec((d, 2 * d), const)],
        out_specs=[pl.BlockSpec((tm, d), row), pl.BlockSpec((tm, d), row)],
        out_shape=[jax.ShapeDtypeStruct((t, d), BF16), jax.ShapeDtypeStruct((t, d), BF16)],
        compiler_params=_cparams(("arbitrary",)),
        name="kv",
    )(mem2, g, wkv)


def _mid_kernel(x_ref, ys_ref, yl_ref, k_ref, v_ref, wo1_ref, wo2_ref, gx_ref, wq_ref, wo_ref,
                gm_ref, wrh_ref, wrl_ref, br_ref,
                h_ref, hn_ref, idx_ref, rank_ref, gate_ref, cnt_ref, car_ref):
    tm, d = x_ref.shape
    hd = d // X_HEADS

    @pl.when(pl.program_id(0) == 0)
    def _():
        car_ref[...] = jnp.zeros(car_ref.shape, F32)

    h1 = x_ref[...] + _dot(ys_ref[...], wo1_ref[...]) + _dot(yl_ref[...], wo2_ref[...])

    q = _dot(_rms(h1, gx_ref[...]).astype(BF16), wq_ref[...]).astype(BF16)
    o_parts = []
    for hh in range(X_HEADS):
        sl = slice(hh * hd, (hh + 1) * hd)
        sc = _dot_nt(q[:, sl], k_ref[0, :, sl]) * (hd ** -0.5)
        sc = sc - jnp.max(sc, axis=-1, keepdims=True)
        p = jnp.exp(sc)
        p = p * (1.0 / jnp.sum(p, axis=-1, keepdims=True))
        o_parts.append(_dot(p.astype(BF16), v_ref[0, :, sl]))
    o = jnp.concatenate(o_parts, axis=1).astype(BF16)
    h2 = h1 + _dot(o, wo_ref[...])
    h_ref[...] = h2

    hn = _rms(h2, gm_ref[...])
    _store_token_major(hn_ref, hn)
    logits = _dot_hilo(hn, wrh_ref[...], wrl_ref[...]) + br_ref[...]

    l = logits.T[:N_EXPERTS, :]
    row = lax.broadcasted_iota(jnp.int32, (N_EXPERTS, tm), 0)
    picked = jnp.zeros((N_EXPERTS, tm), F32)
    vals, idxs = [], []
    for _ in range(TOP_K):
        m = jnp.max(l, axis=0, keepdims=True)
        idx = jnp.min(jnp.where(l == m, row, N_EXPERTS), axis=0, keepdims=True)
        sel = row == idx
        vals.append(m)
        idxs.append(idx)
        picked = jnp.where(sel, 1.0, picked)
        l = jnp.where(sel, -jnp.inf, l)
    ex = [jnp.exp(v - vals[0]) for v in vals]
    den = ex[0] + ex[1] + ex[2] + ex[3]

    ri = lax.broadcasted_iota(jnp.int32, (tm, tm), 0)
    ci = lax.broadcasted_iota(jnp.int32, (tm, tm), 1)
    earlier = jnp.where(ri < ci, 1.0, 0.0).astype(BF16)
    before = _dot(picked.astype(BF16), earlier) + car_ref[:, 0:1]
    out_row = lax.broadcasted_iota(jnp.int32, (SUBLANES, tm), 0)
    idx_out = jnp.zeros((SUBLANES, tm), jnp.int32)
    rank_out = jnp.zeros((SUBLANES, tm), jnp.int32)
    gate_out = jnp.zeros((SUBLANES, tm), F32)
    for k in range(TOP_K):
        rk = jnp.sum(jnp.where(row == idxs[k], before, 0.0), axis=0, keepdims=True)
        at_k = out_row == k
        idx_out = jnp.where(at_k, idxs[k], idx_out)
        rank_out = jnp.where(at_k, rk.astype(jnp.int32), rank_out)
        gate_out = jnp.where(at_k, ex[k] / den, gate_out)
    idx_ref[...] = idx_out
    rank_ref[...] = rank_out
    gate_ref[...] = gate_out
    total = car_ref[...] + jnp.sum(picked, axis=1, keepdims=True)
    car_ref[...] = total
    cnt_ref[...] = total.astype(jnp.int32)


def _mid(x2, ys, yl, kk, vv, wo1, wo2, gx, wq, wo, gm, wr_hi, wr_lo, br, seq):
    t, d = x2.shape
    tm = MID_TILE
    m = kk.shape[1]
    per_b = seq // tm
    row = lambda i: (i, 0)
    col = lambda i: (0, i)
    const = lambda i: (0, 0)
    kvmap = lambda i: (i // per_b, 0, 0)
    wspec = pl.BlockSpec((d, d), const)
    vspec = pl.BlockSpec((1, d), const)
    return pl.pallas_call(
        _mid_kernel,
        grid=(t // tm,),
        in_specs=[
            pl.BlockSpec((tm, d), row), pl.BlockSpec((tm, d), row), pl.BlockSpec((tm, d), row),
            pl.BlockSpec((1, m, d), kvmap), pl.BlockSpec((1, m, d), kvmap),
            wspec, wspec, vspec, wspec, wspec, vspec,
            pl.BlockSpec((d, LANES), const), pl.BlockSpec((d, LANES), const), pl.BlockSpec((1, LANES), const),
        ],
        out_specs=[
            pl.BlockSpec((tm, d), row), pl.BlockSpec((tm * d // LANES, LANES), row),
            pl.BlockSpec((SUBLANES, tm), col), pl.BlockSpec((SUBLANES, tm), col), pl.BlockSpec((SUBLANES, tm), col),
            pl.BlockSpec((N_EXPERTS, LANES), const),
        ],
        out_shape=[
            jax.ShapeDtypeStruct((t, d), F32), jax.ShapeDtypeStruct((t * d // LANES, LANES), F32),
            jax.ShapeDtypeStruct((SUBLANES, t), jnp.int32), jax.ShapeDtypeStruct((SUBLANES, t), jnp.int32),
            jax.ShapeDtypeStruct((SUBLANES, t), F32),
            jax.ShapeDtypeStruct((N_EXPERTS, LANES), jnp.int32),
        ],
        scratch_shapes=[pltpu.VMEM((N_EXPERTS, LANES), F32)],
        compiler_params=_cparams(("arbitrary",)),
        name="mid",
    )(x2, ys, yl, kk, vv, wo1, wo2, gx, wq, wo, gm, wr_hi, wr_lo, br)


def _dispatch_kernel(pend_ref, dst_ref, hn_ref, x_hbm, zero_ref, sem, *, parts):
    i = pl.program_id(0)
    tc = hn_ref.shape[0] // parts
    bm = zero_ref.shape[0] // parts
    rows = TOP_K * tc

    def tile(ref, n, count=1):
        return ref.at[pl.ds(pl.multiple_of(n * parts, parts), count * parts), :]

    @pl.when(i == 0)
    def _():
        zero_ref[...] = jnp.zeros(zero_ref.shape, F32)

        def fill(start):
            return pltpu.make_async_copy(zero_ref, tile(x_hbm, start, bm), sem.at[1])

        for e in range(N_EXPERTS):
            fill(jnp.maximum(pend_ref[e] - bm, 0)).start()
        for e in range(N_EXPERTS):
            fill(0).wait()

        def tail(j, c):
            fill(j * bm).start()
            fill(0).wait()
            return c

        lax.fori_loop(pend_ref[N_EXPERTS - 1] // bm, x_hbm.shape[0] // (bm * parts), tail, 0)

    def body(j, c):
        t0 = pl.multiple_of(j * ISSUE_UNROLL, ISSUE_UNROLL)
        for u in range(ISSUE_UNROLL):
            for k in range(TOP_K):
                dst = dst_ref[0, 0, k * tc + t0 + u]
                pltpu.make_async_copy(tile(hn_ref, t0 + u), tile(x_hbm, dst), sem.at[0]).start(priority=k % 2)
        return c

    lax.fori_loop(0, tc // ISSUE_UNROLL, body, 0)
    pltpu.make_async_copy(tile(x_hbm, 0, rows), tile(x_hbm, 0, rows), sem.at[0]).wait()


def _dispatch(pend, dest3, hn_tm, n_rows, d):
    parts = d // LANES
    t = hn_tm.shape[0] // parts
    tc = DISP_TILE
    grid_spec = pltpu.PrefetchScalarGridSpec(
        num_scalar_prefetch=1,
        grid=(t // tc,),
        in_specs=[
            pl.BlockSpec((1, 1, TOP_K * tc), lambda i, pe: (i, 0, 0), memory_space=pltpu.SMEM),
            pl.BlockSpec((tc * parts, LANES), lambda i, pe: (i, 0)),
        ],
        out_specs=pl.BlockSpec(memory_space=pl.ANY),
        scratch_shapes=[pltpu.VMEM((MOE_BLOCK * parts, LANES), F32), pltpu.SemaphoreType.DMA((2,))],
    )
    return pl.pallas_call(
        functools.partial(_dispatch_kernel, parts=parts),
        grid_spec=grid_spec,
        out_shape=jax.ShapeDtypeStruct((n_rows * parts, LANES), F32),
        compiler_params=_cparams(("arbitrary",)),
        name="dispatch",
    )(pend, dest3, hn_tm)


def _moe_kernel(be_ref, nxt_ref, nvb_ref, x_ref, wgu_hbm, bg_ref, bu_ref, wd_hbm, bd_ref, perm_ref,
                y_ref, wgu_f, wd_f, wgu_s, wd_s, act_s, sem):
    i = pl.program_id(0)
    nvb = nvb_ref[0]
    f, d = wd_f.shape
    bm = act_s.shape[0]
    pw = perm_ref.shape[0]
    half = pw // 2

    def fetch(e):
        return (pltpu.make_async_copy(wgu_hbm.at[e], wgu_f, sem.at[0]),
                pltpu.make_async_copy(wd_hbm.at[e], wd_f, sem.at[1]))

    @pl.when(jnp.logical_and(i == 0, nvb > 0))
    def _():
        for c in fetch(be_ref[0]):
            c.start()

    changed = jnp.logical_or(i == 0, be_ref[i] != be_ref[jnp.maximum(i - 1, 0)])

    @pl.when(jnp.logical_and(changed, i < nvb))
    def _():
        for c in fetch(be_ref[i]):
            c.wait()
        for c in range(2 * f // pw):
            wc = wgu_f[:, c * pw:(c + 1) * pw].astype(BF16)
            wgu_s[:, c * pw:(c + 1) * pw] = _dot(wc, perm_ref[...]).astype(BF16)
        wd_s[...] = wd_f[...].astype(BF16)

        @pl.when(nxt_ref[i] >= 0)
        def _():
            for c in fetch(nxt_ref[i]):
                c.start()

    @pl.when(i < nvb)
    def _():
        e = be_ref[i]
        xb = jnp.concatenate([p.astype(BF16) for p in _load_token_major(x_ref, bm, d)], axis=1)
        for c in range(2 * f // pw):
            gu = _dot(xb, wgu_s[:, c * pw:(c + 1) * pw])
            g = gu[:, :half] + bg_ref[e, :, c * half:(c + 1) * half]
            u = gu[:, half:] + bu_ref[e, :, c * half:(c + 1) * half]
            g = jnp.minimum(g, SWIGLU_LIMIT)
            u = jnp.clip(u, -SWIGLU_LIMIT, SWIGLU_LIMIT)
            act = (u + 1.0) * (g * _sigmoid(SWIGLU_ALPHA * g))
            act_s[:, c * half:(c + 1) * half] = act.astype(BF16)
        _store_token_major(y_ref, _dot(act_s[...], wd_s[...]) + bd_ref[e])

    @pl.when(i >= nvb)
    def _():
        y_ref[...] = jnp.zeros(y_ref.shape, F32)


def _moe(block_e, next_e, nvb, xrows_tm, wgu, bg, bu, wd, bd, perm):
    f, d = wd.shape[1], wd.shape[2]
    parts = d // LANES
    bm = MOE_BLOCK
    nb = xrows_tm.shape[0] // (bm * parts)

    def whole(a):
        return pl.BlockSpec(a.shape, lambda i, be, nx, nv, nd=a.ndim: (0,) * nd)

    grid_spec = pltpu.PrefetchScalarGridSpec(
        num_scalar_prefetch=3,
        grid=(nb,),
        in_specs=[
            pl.BlockSpec((bm * parts, LANES),
                         lambda i, be, nx, nv: (jnp.minimum(i, jnp.maximum(nv[0] - 1, 0)), 0)),
            pl.BlockSpec(memory_space=pl.ANY),
            whole(bg), whole(bu),
            pl.BlockSpec(memory_space=pl.ANY),
            whole(bd), whole(perm),
        ],
        out_specs=pl.BlockSpec((bm * parts, LANES), lambda i, be, nx, nv: (i, 0)),
        scratch_shapes=[
            pltpu.VMEM((d, 2 * f), F32), pltpu.VMEM((f, d), F32),
            pltpu.VMEM((d, 2 * f), BF16), pltpu.VMEM((f, d), BF16),
            pltpu.VMEM((bm, f), BF16),
            pltpu.SemaphoreType.DMA((2,)),
        ],
    )
    return pl.pallas_call(
        _moe_kernel,
        grid_spec=grid_spec,
        out_shape=jax.ShapeDtypeStruct(xrows_tm.shape, F32),
        compiler_params=_cparams(("arbitrary",)),
        name="moe",
    )(block_e, next_e, nvb, xrows_tm, wgu, bg, bu, wd, bd, perm)


def _comb_kernel(dst_ref, dstn_ref, h_ref, gate_ref, y_hbm, g_ref, o_ref, ybuf, sem, *, parts):
    i = pl.program_id(0)
    n = pl.num_programs(0)
    tc, d = h_ref.shape
    rows = TOP_K * tc
    slot = i % 2

    def tile(ref, n, count=1):
        return ref.at[pl.ds(pl.multiple_of(n * parts, parts), count * parts), :]

    def start_rows(idx_ref, s):
        def body(j, c):
            t0 = pl.multiple_of(j * ISSUE_UNROLL, ISSUE_UNROLL)
            for u in range(ISSUE_UNROLL):
                for k in range(TOP_K):
                    r = k * tc + t0 + u
                    pltpu.make_async_copy(tile(y_hbm, idx_ref[0, 0, r]), tile(ybuf.at[s], r),
                                          sem.at[s]).start(priority=k % 2)
            return c

        lax.fori_loop(0, tc // ISSUE_UNROLL, body, 0)

    @pl.when(i == 0)
    def _():
        start_rows(dst_ref, 0)

    @pl.when(i + 1 < n)
    def _():
        start_rows(dstn_ref, 1 - slot)

    pltpu.make_async_copy(tile(y_hbm, 0, rows), ybuf.at[slot], sem.at[slot]).wait()
    acc = [h_ref[:, s * LANES:(s + 1) * LANES] for s in range(parts)]
    for k in range(TOP_K):
        yk = _load_token_major(ybuf.at[slot], tc, d, base=k * tc * parts)
        gk = gate_ref[:, k:k + 1]
        acc = [a + p * gk for a, p in zip(acc, yk)]
    o_ref[...] = _rms(jnp.concatenate(acc, axis=1), g_ref[...])


def _combine(dest3, h2, gate, y_tm, g):
    t, d = h2.shape
    parts = d // LANES
    tc = COMB_TILE
    nt = t // tc
    return pl.pallas_call(
        functools.partial(_comb_kernel, parts=parts),
        grid=(nt,),
        in_specs=[
            pl.BlockSpec((1, 1, TOP_K * tc), lambda i: (i, 0, 0), memory_space=pltpu.SMEM),
            pl.BlockSpec((1, 1, TOP_K * tc), lambda i: (jnp.minimum(i + 1, nt - 1), 0, 0),
                         memory_space=pltpu.SMEM),
            pl.BlockSpec((tc, d), lambda i: (i, 0)),
            pl.BlockSpec((tc, SUBLANES), lambda i: (i, 0)),
            pl.BlockSpec(memory_space=pl.ANY),
            pl.BlockSpec((1, d), lambda i: (0, 0)),
        ],
        out_specs=pl.BlockSpec((tc, d), lambda i: (i, 0)),
        out_shape=jax.ShapeDtypeStruct((t, d), F32),
        scratch_shapes=[pltpu.VMEM((2, TOP_K * tc * parts, LANES), F32), pltpu.SemaphoreType.DMA((2,))],
        compiler_params=_cparams(("arbitrary",)),
        name="combine",
    )(dest3, dest3, h2, gate, y_tm, g)


def _block_diag(wb, per):
    nb, bw, _ = wb.shape
    wq = wb.reshape(nb // per, per, bw, bw)
    eye = jnp.eye(per, dtype=wb.dtype)
    out = jnp.einsum('qaij,ab->qaibj', wq, eye)
    return out.reshape(nb // per, per * bw, per * bw)


def _pad_lanes(v, fill=0.0):
    return jnp.pad(v, (0, LANES - v.shape[0]), constant_values=fill).reshape(1, LANES)


def kernel(x, mem, norm_mix, w_in, ssd_conv_w, ssd_conv_b, ssd_dt_bias, ssd_a_log, ssd_d, ssd_norm, lru_conv_w, lru_conv_b, lru_wa, lru_ba, lru_wx, lru_bx, lru_lambda, w_out, norm_xattn, norm_mem, w_q, w_kv, w_o, norm_moe, w_router, b_router, w_gate_up, b_gate_up, w_down, b_down, norm_final):
    b, s, d = x.shape
    t = b * s
    n_mem = mem.shape[1]
    w = d
    cdim = w + 2 * SSD_GROUPS * SSD_STATE
    o1, o2, o3, o4 = w, w + cdim, w + cdim + SSD_HEADS, w + cdim + SSD_HEADS + w

    wi = jnp.swapaxes(w_in[0], 0, 1)
    wzx = wi[:o2].astype(BF16)
    wdt = jnp.pad(wi[o2:o3], ((0, LANES - SSD_HEADS), (0, 0)))
    wdt_hi, wdt_lo = _hilo(wdt)
    wxr = wi[o3:o4].astype(BF16)
    wgr = wi[o4:].astype(BF16)
    e01 = (jnp.arange(LANES)[:, None] == (jnp.arange(w)[None, :] // SSD_HEAD_DIM)).astype(BF16)
    dskip = jnp.repeat(ssd_d[0], SSD_HEAD_DIM).reshape(1, w)
    per = MXU_DIM // (w // LRU_BLOCKS)
    wa_bd = _block_diag(lru_wa[0], per).astype(BF16)
    wx_bd = _block_diag(lru_wx[0], per).astype(BF16)
    wr = jnp.pad(w_router[0], ((0, 0), (0, LANES - N_EXPERTS)))
    wr_hi, wr_lo = _hilo(wr)
    br = _pad_lanes(b_router[0], fill=-1e30)
    bgu = b_gate_up[0]
    bg = bgu[:, None, 0::2]
    bu = bgu[:, None, 1::2]
    bd = b_down[0][:, None, :]
    pw = 2 * LANES
    col = jnp.arange(pw)
    src_col = jnp.where(col < LANES, 2 * col, 2 * (col - LANES) + 1)
    perm = (jnp.arange(pw)[:, None] == src_col[None, :]).astype(BF16)

    x2 = x.reshape(t, d)
    z, xbc, dt, xr, gr = _in_proj(x2, norm_mix[0].reshape(1, d), wzx, jnp.concatenate([wdt_hi, wdt_lo], axis=0),
                                  wdt_hi, wxr, wgr)

    y_ssd = _ssd(xbc.reshape(b, s, cdim), dt.reshape(b, s, LANES), z.reshape(b, s, w),
                 _shift_matrix(SSD_CHUNK), ssd_conv_w[0], ssd_conv_b[0].reshape(1, cdim), _pad_lanes(ssd_dt_bias[0]),
                 _pad_lanes(ssd_a_log[0]), dskip, ssd_norm[0].reshape(1, w), e01)
    y_lru = _lru(xr.reshape(b, s, w), gr.reshape(b, s, w), _shift_matrix(LRU_TILE), lru_conv_w[0], lru_conv_b[0].reshape(1, w),
                 wa_bd, lru_ba[0].reshape(1, w), wx_bd, lru_bx[0].reshape(1, w), lru_lambda[0].reshape(1, w))

    kk, vv = _kv(mem.reshape(b * n_mem, d), norm_mem[0].reshape(1, d), w_kv[0].astype(BF16))
    wo_mix = w_out[0].astype(BF16)
    h2, hn, idx_m, rank_m, gate_m, cnt = _mid(
        x2, y_ssd.reshape(t, w), y_lru.reshape(t, w), kk.reshape(b, n_mem, d), vv.reshape(b, n_mem, d),
        wo_mix[:w], wo_mix[w:], norm_xattn[0].reshape(1, d), w_q[0].astype(BF16), w_o[0].astype(BF16),
        norm_moe[0].reshape(1, d), wr_hi, wr_lo, br, s)

    bm = MOE_BLOCK
    counts = cnt[:, 0]
    pcounts = (counts + bm - 1) // bm * bm
    pend = jnp.cumsum(pcounts).astype(jnp.int32)
    pstart = pend - pcounts
    idx = idx_m[:TOP_K, :]
    onehot = idx[:, :, None] == jnp.arange(N_EXPERTS, dtype=jnp.int32)[None, None, :]
    dest = jnp.sum(jnp.where(onehot, pstart[None, None, :], 0), axis=-1) + rank_m[:TOP_K, :]
    n_pairs = t * TOP_K
    nb = (n_pairs + N_EXPERTS * (bm - 1) + bm - 1) // bm
    blk0 = jnp.arange(nb, dtype=jnp.int32) * bm
    block_e = jnp.minimum(jnp.sum(pend[None, :] <= blk0[:, None], axis=1), N_EXPERTS - 1).astype(jnp.int32)
    nvb = (pend[-1] // bm).astype(jnp.int32).reshape(1)
    experts = jnp.arange(N_EXPERTS, dtype=jnp.int32)
    later = (pcounts > 0)[None, :] & (experts[None, :] > experts[:, None])
    follower = jnp.min(jnp.where(later, experts[None, :], N_EXPERTS), axis=1)
    follower = jnp.where(follower == N_EXPERTS, -1, follower)
    next_e = jnp.sum(jnp.where(block_e[:, None] == experts[None, :], follower[None, :], 0), axis=1).astype(jnp.int32)
    dest = dest.astype(jnp.int32)

    def by_tile(tc):
        return dest.reshape(TOP_K, t // tc, tc).transpose(1, 0, 2).reshape(t // tc, 1, TOP_K * tc)

    xrows = _dispatch(pend, by_tile(DISP_TILE), hn, nb * bm, d)
    y = _moe(block_e, next_e, nvb, xrows, w_gate_up[0], bg, bu, w_down[0], bd, perm)
    out = _combine(by_tile(COMB_TILE), h2, gate_m.T, y, norm_final.reshape(1, d))
    return out.reshape(b, s, d)
```

```python
import functools

import jax
import jax.numpy as jnp
from jax import lax
from jax.experimental import pallas as pl
from jax.experimental.pallas import tpu as pltpu

F32 = jnp.float32
BF16 = jnp.bfloat16

NORM_EPS = 1e-6
LANES = 128
SUBLANES = 8
MXU_DIM = 256
SSD_HEAD_DIM = 64
SSD_HEADS = 16
SSD_GROUPS = 4
SSD_STATE = 128
CONV_K = 4
LRU_BLOCKS = 16
RG_C = 8.0
X_HEADS = 4
N_EXPERTS = 32
TOP_K = 4
SWIGLU_LIMIT = 7.0
SWIGLU_ALPHA = 1.702

VMEM_LIMIT = 56 * 1024 * 1024

IN_TILE = 1024
SSD_CHUNK = 256
LRU_TILE = 256
MID_TILE = 1024
MOE_BLOCK = 512
DISP_TILE = 2048
COMB_TILE = 256
ISSUE_UNROLL = 8


def _cparams(sem, flags=None):
    return pltpu.CompilerParams(dimension_semantics=sem, vmem_limit_bytes=VMEM_LIMIT, flags=flags)


def _rms(x, g):
    ms = jnp.mean(x * x, axis=-1, keepdims=True)
    return x * lax.rsqrt(ms + NORM_EPS) * g


def _sigmoid(x):
    return 0.5 * jnp.tanh(0.5 * x) + 0.5


def _softplus(x):
    return jnp.maximum(x, 0.0) + jnp.log(1.0 + jnp.exp(-jnp.abs(x)))


def _split3(x):
    a = x.astype(BF16)
    r = x - a.astype(F32)
    b = r.astype(BF16)
    c = (r - b.astype(F32)).astype(BF16)
    return a, b, c


def _dot(a, b):
    return jnp.dot(a, b, preferred_element_type=F32)


def _dot_nt(a, b):
    return lax.dot_general(a, b, (((1,), (1,)), ((), ())), preferred_element_type=F32)


def _dot01_right(x, m01):
    a, b, c = _split3(x)
    return _dot(a, m01) + _dot(b, m01) + _dot(c, m01)


def _dot01_left(m01, x):
    a, b, c = _split3(x)
    return _dot(m01, a) + _dot(m01, b) + _dot(m01, c)


def _dot_hilo(x, w_hi, w_lo):
    xh = x.astype(BF16)
    xl = (x - xh.astype(F32)).astype(BF16)
    return _dot(xh, w_hi) + _dot(xl, w_hi) + _dot(xh, w_lo)


def _hilo(w):
    hi = w.astype(BF16)
    lo = (w - hi.astype(F32)).astype(BF16)
    return hi, lo


def _store_token_major(ref, val, base=0):
    n, d = val.shape
    parts = d // LANES
    for s in range(parts):
        ref[pl.ds(base + s, n, stride=parts), :] = val[:, s * LANES:(s + 1) * LANES]


def _load_token_major(ref, n, d, base=0):
    parts = d // LANES
    return [ref[pl.ds(base + s, n, stride=parts), :] for s in range(parts)]


def _in_proj_kernel(x_ref, g_ref, wzx_ref, wdthl_ref, wdth_ref, wxr_ref, wgr_ref,
                    z_ref, xbc_ref, dt_ref, xr_ref, gr_ref):
    d = x_ref.shape[1]
    hn = _rms(x_ref[...], g_ref[...])
    hb = hn.astype(BF16)
    z_ref[...] = _dot_nt(hb, wzx_ref[:d, :]).astype(BF16)
    xbc_ref[...] = _dot_nt(hb, wzx_ref[d:, :]).astype(BF16)
    hl = _dot_nt(hb, wdthl_ref[...])
    hn_lo = (hn - hb.astype(F32)).astype(BF16)
    dt_ref[...] = hl[:, :LANES] + hl[:, LANES:] + _dot_nt(hn_lo, wdth_ref[...])
    xr_ref[...] = _dot_nt(hb, wxr_ref[...]).astype(BF16)
    gr_ref[...] = _dot_nt(hb, wgr_ref[...]).astype(BF16)


def _in_proj(x2, g, wzx, wdt_hilo, wdt_hi, wxr, wgr):
    t, d = x2.shape
    tm = IN_TILE
    nzx = wzx.shape[0]
    const = lambda i: (0, 0)
    row = lambda i: (i, 0)
    return pl.pallas_call(
        _in_proj_kernel,
        grid=(t // tm,),
        in_specs=[
            pl.BlockSpec((tm, d), row),
            pl.BlockSpec((1, d), const),
            pl.BlockSpec((nzx, d), const),
            pl.BlockSpec((2 * LANES, d), const),
            pl.BlockSpec((LANES, d), const),
            pl.BlockSpec((d, d), const),
            pl.BlockSpec((d, d), const),
        ],
        out_specs=[
            pl.BlockSpec((tm, d), row),
            pl.BlockSpec((tm, nzx - d), row),
            pl.BlockSpec((tm, LANES), row),
            pl.BlockSpec((tm, d), row),
            pl.BlockSpec((tm, d), row),
        ],
        out_shape=[
            jax.ShapeDtypeStruct((t, d), BF16),
            jax.ShapeDtypeStruct((t, nzx - d), BF16),
            jax.ShapeDtypeStruct((t, LANES), F32),
            jax.ShapeDtypeStruct((t, d), BF16),
            jax.ShapeDtypeStruct((t, d), BF16),
        ],
        compiler_params=_cparams(("arbitrary",)),
        name="in_proj",
    )(x2, g, wzx, wdt_hilo, wdt_hi, wxr, wgr)


def _shift_matrix(n):
    return jnp.concatenate([jnp.eye(n, k=-(CONV_K - 1 - j), dtype=BF16) for j in range(CONV_K - 1)], axis=0)


def _causal_conv(halo_ref, x, shift_ref, w_ref, b_ref, first):
    n = x.shape[0]
    pad = SUBLANES
    k1 = CONV_K - 1

    @pl.when(first)
    def _():
        halo_ref[...] = jnp.zeros(halo_ref.shape, F32)

    xf = x.astype(F32)
    taps = _dot(shift_ref[...], x)
    acc = b_ref[...] + w_ref[k1:k1 + 1, :] * xf
    for j in range(k1):
        acc = acc + w_ref[j:j + 1, :] * taps[j * n:(j + 1) * n]
    head = acc[:pad]
    for j in range(k1):
        head = head + w_ref[j:j + 1, :] * halo_ref[pad - k1 + j:2 * pad - k1 + j, :]
    halo_ref[0:pad, :] = xf[n - pad:n]
    return jnp.concatenate([head, acc[pad:]], axis=0)


def _ssd_kernel(xbc_ref, dt_ref, z_ref, sh_ref, cw_ref, cb_ref, dtb_ref, alog_ref, dskip_ref, gn_ref, e_ref,
                y_ref, halo_ref, st_ref):
    n = xbc_ref.shape[1]
    w = z_ref.shape[2]
    gw = w // SSD_GROUPS
    first = pl.program_id(1) == 0

    @pl.when(first)
    def _():
        st_ref[...] = jnp.zeros(st_ref.shape, F32)

    conv = _causal_conv(halo_ref, xbc_ref[0], sh_ref, cw_ref, cb_ref, first)
    xc = conv * _sigmoid(conv)
    xs = xc[:, :w]

    dt = _softplus(dt_ref[0] + dtb_ref[...])
    a = -jnp.exp(alog_ref[...])
    da = dt * a
    ri = lax.broadcasted_iota(jnp.int32, (n, n), 0)
    ci = lax.broadcasted_iota(jnp.int32, (n, n), 1)
    causal = ri >= ci
    tril = jnp.where(causal, 1.0, 0.0).astype(BF16)
    a_cs = _dot01_left(tril, da)
    a_cs_t = a_cs.T

    e01 = e_ref[...]
    dt_x = _dot01_right(dt, e01)
    acs_x = _dot01_right(a_cs, e01)
    last_x = acs_x[n - 1:n, :]
    xdt = xs * dt_x
    xdt_b = xdt.astype(BF16)
    xdt_end = (xdt * jnp.exp(last_x - acs_x)).astype(BF16)
    exp_acs = jnp.exp(acs_x)
    chunk_decay = jnp.exp(last_x)
    lane = lax.broadcasted_iota(jnp.int32, (n, gw), 1)

    for g in range(SSD_GROUPS):
        lo = g * gw
        bg = xc[:, w + g * SSD_STATE:w + (g + 1) * SSD_STATE].astype(BF16)
        cg = xc[:, w + (SSD_GROUPS + g) * SSD_STATE:w + (SSD_GROUPS + g + 1) * SSD_STATE].astype(BF16)
        cb = _dot_nt(cg, bg)
        prev = st_ref[g]
        acc = _dot(cg, prev.astype(BF16)) * exp_acs[:, lo:lo + gw]
        new = lax.dot_general(bg, xdt_end[:, lo:lo + gw], (((0,), (0,)), ((), ())),
                              preferred_element_type=F32)
        st_ref[g] = chunk_decay[:, lo:lo + gw] * prev + new
        xg = xdt_b[:, lo:lo + gw]
        for k in range(SSD_HEADS // SSD_GROUPS):
            h = g * (SSD_HEADS // SSD_GROUPS) + k
            seg = a_cs[:, h:h + 1] - a_cs_t[h:h + 1, :]
            dec = jnp.exp(jnp.where(causal, seg, -jnp.inf))
            m = (cb * dec).astype(BF16)
            in_head = (lane >= k * SSD_HEAD_DIM) & (lane < (k + 1) * SSD_HEAD_DIM)
            acc = acc + _dot(m, jnp.where(in_head, xg, jnp.zeros_like(xg)))
        yg = acc + xs[:, lo:lo + gw] * dskip_ref[:, lo:lo + gw]
        zg = z_ref[0, :, lo:lo + gw].astype(F32)
        u = yg * (zg * _sigmoid(zg))
        u = u * lax.rsqrt(jnp.mean(u * u, axis=-1, keepdims=True) + NORM_EPS)
        y_ref[0, :, lo:lo + gw] = (u * gn_ref[:, lo:lo + gw]).astype(BF16)


def _ssd(xbc, dt, z, shift, cw, cb, dtb, alog, dskip, gn, e01):
    b, s, cdim = xbc.shape
    w = z.shape[2]
    n = SSD_CHUNK
    tile = lambda i, j: (i, j, 0)
    const = lambda i, j: (0, 0)
    return pl.pallas_call(
        _ssd_kernel,
        grid=(b, s // n),
        in_specs=[
            pl.BlockSpec((1, n, cdim), tile),
            pl.BlockSpec((1, n, LANES), tile),
            pl.BlockSpec((1, n, w), tile),
            pl.BlockSpec(shift.shape, const),
            pl.BlockSpec((CONV_K, cdim), const),
            pl.BlockSpec((1, cdim), const),
            pl.BlockSpec((1, LANES), const),
            pl.BlockSpec((1, LANES), const),
            pl.BlockSpec((1, w), const),
            pl.BlockSpec((1, w), const),
            pl.BlockSpec((LANES, w), const),
        ],
        out_specs=pl.BlockSpec((1, n, w), tile),
        out_shape=jax.ShapeDtypeStruct((b, s, w), BF16),
        scratch_shapes=[
            pltpu.VMEM((2 * SUBLANES, cdim), F32),
            pltpu.VMEM((SSD_GROUPS, SSD_STATE, w // SSD_GROUPS), F32),
        ],
        compiler_params=_cparams(("arbitrary", "arbitrary")),
        name="ssd",
    )(xbc, dt, z, shift, cw, cb, dtb, alog, dskip, gn, e01)


def _gelu_tanh(x):
    c = 0.7978845608028654
    return 0.5 * x * (1.0 + jnp.tanh(c * (x + 0.044715 * (x * x * x))))


def _lru_kernel(xr_ref, gr_ref, sh_ref, cw_ref, cb_ref, wa_ref, ba_ref, wx_ref, bx_ref, lam_ref,
                y_ref, halo_ref, car_ref, h_ref):
    n = xr_ref.shape[1]
    w = xr_ref.shape[2]
    first = pl.program_id(1) == 0

    @pl.when(first)
    def _():
        car_ref[...] = jnp.zeros(car_ref.shape, F32)

    xc = _causal_conv(halo_ref, xr_ref[0], sh_ref, cw_ref, cb_ref, first)
    xb = xc.astype(BF16)
    nq = wa_ref.shape[0]
    qw = w // nq
    r_parts, i_parts = [], []
    for q in range(nq):
        xq = xb[:, q * qw:(q + 1) * qw]
        r_parts.append(_dot(xq, wa_ref[q]))
        i_parts.append(_dot(xq, wx_ref[q]))
    r = _sigmoid(jnp.concatenate(r_parts, axis=1) + ba_ref[...])
    gi = _sigmoid(jnp.concatenate(i_parts, axis=1) + bx_ref[...])
    log_a = (-RG_C) * r * _softplus(-lam_ref[...])
    a = jnp.exp(log_a)
    u = xc * gi * jnp.sqrt(1.0 - a * a)

    groups = n // SUBLANES
    sub = lax.broadcasted_iota(jnp.int32, (groups, SUBLANES, w), 1)
    ap = a.reshape(groups, SUBLANES, w)
    bp = u.reshape(groups, SUBLANES, w)
    for d in (1, 2, 4):
        m = sub >= d
        bp = jnp.where(m, ap * pltpu.roll(bp, d, 1) + bp, bp)
        ap = jnp.where(m, ap * pltpu.roll(ap, d, 1), ap)
    carry = car_ref[...]
    for g in range(groups):
        hb = bp[g] + ap[g] * carry
        h_ref[g * SUBLANES:(g + 1) * SUBLANES, :] = hb
        carry = jnp.broadcast_to(hb[SUBLANES - 1:SUBLANES, :], (SUBLANES, w))
    car_ref[...] = carry
    y_ref[0] = (h_ref[...] * _gelu_tanh(gr_ref[0].astype(F32))).astype(BF16)


def _lru(xr, gr, shift, cw, cb, wa, ba, wx, bx, lam):
    b, s, w = xr.shape
    n = LRU_TILE
    nq, qw, _ = wa.shape
    tile = lambda i, j: (i, j, 0)
    const = lambda i, j: (0, 0)
    const3 = lambda i, j: (0, 0, 0)
    return pl.pallas_call(
        _lru_kernel,
        grid=(b, s // n),
        in_specs=[
            pl.BlockSpec((1, n, w), tile),
            pl.BlockSpec((1, n, w), tile),
            pl.BlockSpec(shift.shape, const),
            pl.BlockSpec((CONV_K, w), const),
            pl.BlockSpec((1, w), const),
            pl.BlockSpec((nq, qw, qw), const3),
            pl.BlockSpec((1, w), const),
            pl.BlockSpec((nq, qw, qw), const3),
            pl.BlockSpec((1, w), const),
            pl.BlockSpec((1, w), const),
        ],
        out_specs=pl.BlockSpec((1, n, w), tile),
        out_shape=jax.ShapeDtypeStruct((b, s, w), BF16),
        scratch_shapes=[
            pltpu.VMEM((2 * SUBLANES, w), F32),
            pltpu.VMEM((SUBLANES, w), F32),
            pltpu.VMEM((n, w), F32),
        ],
        compiler_params=_cparams(("arbitrary", "arbitrary")),
        name="lru",
    )(xr, gr, shift, cw, cb, wa, ba, wx, bx, lam)


def _kv_kernel(m_ref, g_ref, w_ref, k_ref, v_ref):
    d = m_ref.shape[1]
    mn = _rms(m_ref[...], g_ref[...]).astype(BF16)
    k_ref[...] = _dot(mn, w_ref[:, :d]).astype(BF16)
    v_ref[...] = _dot(mn, w_ref[:, d:]).astype(BF16)


def _kv(mem2, g, wkv):
    t, d = mem2.shape
    tm = min(t, 512)
    row = lambda i: (i, 0)
    const = lambda i: (0, 0)
    return pl.pallas_call(
        _kv_kernel,
        grid=(t // tm,),
        in_specs=[pl.BlockSpec((tm, d), row), pl.BlockSpec((1, d), const), pl.BlockSpec((d, 2 * d), const)],
        out_specs=[pl.BlockSpec((tm, d), row), pl.BlockSpec((tm, d), row)],
        out_shape=[jax.ShapeDtypeStruct((t, d), BF16), jax.ShapeDtypeStruct((t, d), BF16)],
        compiler_params=_cparams(("arbitrary",)),
        name="kv",
    )(mem2, g, wkv)


def _mid_kernel(x_ref, ys_ref, yl_ref, k_ref, v_ref, wo1_ref, wo2_ref, gx_ref, wq_ref, wo_ref,
                gm_ref, wrh_ref, wrl_ref, br_ref,
                h_ref, hn_ref, idx_ref, rank_ref, gate_ref, cnt_ref, car_ref):
    tm, d = x_ref.shape
    hd = d // X_HEADS

    @pl.when(pl.program_id(0) == 0)
    def _():
        car_ref[...] = jnp.zeros(car_ref.shape, F32)

    h1 = x_ref[...] + _dot(ys_ref[...], wo1_ref[...]) + _dot(yl_ref[...], wo2_ref[...])

    q = _dot(_rms(h1, gx_ref[...]).astype(BF16), wq_ref[...]).astype(BF16)
    o_parts = []
    for hh in range(X_HEADS):
        sl = slice(hh * hd, (hh + 1) * hd)
        sc = _dot_nt(q[:, sl], k_ref[0, :, sl]) * (hd ** -0.5)
        sc = sc - jnp.max(sc, axis=-1, keepdims=True)
        p = jnp.exp(sc)
        p = p * (1.0 / jnp.sum(p, axis=-1, keepdims=True))
        o_parts.append(_dot(p.astype(BF16), v_ref[0, :, sl]))
    o = jnp.concatenate(o_parts, axis=1).astype(BF16)
    h2 = h1 + _dot(o, wo_ref[...])
    h_ref[...] = h2

    hn = _rms(h2, gm_ref[...])
    _store_token_major(hn_ref, hn)
    logits = _dot_hilo(hn, wrh_ref[...], wrl_ref[...]) + br_ref[...]

    l = logits.T[:N_EXPERTS, :]
    row = lax.broadcasted_iota(jnp.int32, (N_EXPERTS, tm), 0)
    picked = jnp.zeros((N_EXPERTS, tm), F32)
    vals, idxs = [], []
    for _ in range(TOP_K):
        m = jnp.max(l, axis=0, keepdims=True)
        idx = jnp.min(jnp.where(l == m, row, N_EXPERTS), axis=0, keepdims=True)
        sel = row == idx
        vals.append(m)
        idxs.append(idx)
        picked = jnp.where(sel, 1.0, picked)
        l = jnp.where(sel, -jnp.inf, l)
    ex = [jnp.exp(v - vals[0]) for v in vals]
    den = ex[0] + ex[1] + ex[2] + ex[3]

    ri = lax.broadcasted_iota(jnp.int32, (tm, tm), 0)
    ci = lax.broadcasted_iota(jnp.int32, (tm, tm), 1)
    earlier = jnp.where(ri < ci, 1.0, 0.0).astype(BF16)
    before = _dot(picked.astype(BF16), earlier) + car_ref[:, 0:1]
    out_row = lax.broadcasted_iota(jnp.int32, (SUBLANES, tm), 0)
    idx_out = jnp.zeros((SUBLANES, tm), jnp.int32)
    rank_out = jnp.zeros((SUBLANES, tm), jnp.int32)
    gate_out = jnp.zeros((SUBLANES, tm), F32)
    for k in range(TOP_K):
        rk = jnp.sum(jnp.where(row == idxs[k], before, 0.0), axis=0, keepdims=True)
        at_k = out_row == k
        idx_out = jnp.where(at_k, idxs[k], idx_out)
        rank_out = jnp.where(at_k, rk.astype(jnp.int32), rank_out)
        gate_out = jnp.where(at_k, ex[k] / den, gate_out)
    idx_ref[...] = idx_out
    rank_ref[...] = rank_out
    gate_ref[...] = gate_out
    total = car_ref[...] + jnp.sum(picked, axis=1, keepdims=True)
    car_ref[...] = total
    cnt_ref[...] = total.astype(jnp.int32)


def _mid(x2, ys, yl, kk, vv, wo1, wo2, gx, wq, wo, gm, wr_hi, wr_lo, br, seq):
    t, d = x2.shape
    tm = MID_TILE
    m = kk.shape[1]
    per_b = seq // tm
    row = lambda i: (i, 0)
    col = lambda i: (0, i)
    const = lambda i: (0, 0)
    kvmap = lambda i: (i // per_b, 0, 0)
    wspec = pl.BlockSpec((d, d), const)
    vspec = pl.BlockSpec((1, d), const)
    return pl.pallas_call(
        _mid_kernel,
        grid=(t // tm,),
        in_specs=[
            pl.BlockSpec((tm, d), row), pl.BlockSpec((tm, d), row), pl.BlockSpec((tm, d), row),
            pl.BlockSpec((1, m, d), kvmap), pl.BlockSpec((1, m, d), kvmap),
            wspec, wspec, vspec, wspec, wspec, vspec,
            pl.BlockSpec((d, LANES), const), pl.BlockSpec((d, LANES), const), pl.BlockSpec((1, LANES), const),
        ],
        out_specs=[
            pl.BlockSpec((tm, d), row), pl.BlockSpec((tm * d // LANES, LANES), row),
            pl.BlockSpec((SUBLANES, tm), col), pl.BlockSpec((SUBLANES, tm), col), pl.BlockSpec((SUBLANES, tm), col),
            pl.BlockSpec((N_EXPERTS, LANES), const),
        ],
        out_shape=[
            jax.ShapeDtypeStruct((t, d), F32), jax.ShapeDtypeStruct((t * d // LANES, LANES), F32),
            jax.ShapeDtypeStruct((SUBLANES, t), jnp.int32), jax.ShapeDtypeStruct((SUBLANES, t), jnp.int32),
            jax.ShapeDtypeStruct((SUBLANES, t), F32),
            jax.ShapeDtypeStruct((N_EXPERTS, LANES), jnp.int32),
        ],
        scratch_shapes=[pltpu.VMEM((N_EXPERTS, LANES), F32)],
        compiler_params=_cparams(("arbitrary",)),
        name="mid",
    )(x2, ys, yl, kk, vv, wo1, wo2, gx, wq, wo, gm, wr_hi, wr_lo, br)


def _dispatch_kernel(pend_ref, dst_ref, hn_ref, x_hbm, zero_ref, sem, *, parts):
    i = pl.program_id(0)
    tc = hn_ref.shape[0] // parts
    bm = zero_ref.shape[0] // parts
    rows = TOP_K * tc

    def tile(ref, n, count=1):
        return ref.at[pl.ds(pl.multiple_of(n * parts, parts), count * parts), :]

    @pl.when(i == 0)
    def _():
        zero_ref[...] = jnp.zeros(zero_ref.shape, F32)

        def fill(start):
            return pltpu.make_async_copy(zero_ref, tile(x_hbm, start, bm), sem.at[1])

        for e in range(N_EXPERTS):
            fill(jnp.maximum(pend_ref[e] - bm, 0)).start()
        for e in range(N_EXPERTS):
            fill(0).wait()

        def tail(j, c):
            fill(j * bm).start()
            fill(0).wait()
            return c

        lax.fori_loop(pend_ref[N_EXPERTS - 1] // bm, x_hbm.shape[0] // (bm * parts), tail, 0)

    def body(j, c):
        t0 = pl.multiple_of(j * ISSUE_UNROLL, ISSUE_UNROLL)
        for u in range(ISSUE_UNROLL):
            for k in range(TOP_K):
                dst = dst_ref[0, 0, k * tc + t0 + u]
                pltpu.make_async_copy(tile(hn_ref, t0 + u), tile(x_hbm, dst), sem.at[0]).start(priority=k % 2)
        return c

    lax.fori_loop(0, tc // ISSUE_UNROLL, body, 0)
    pltpu.make_async_copy(tile(x_hbm, 0, rows), tile(x_hbm, 0, rows), sem.at[0]).wait()


def _dispatch(pend, dest3, hn_tm, n_rows, d):
    parts = d // LANES
    t = hn_tm.shape[0] // parts
    tc = DISP_TILE
    grid_spec = pltpu.PrefetchScalarGridSpec(
        num_scalar_prefetch=1,
        grid=(t // tc,),
        in_specs=[
            pl.BlockSpec((1, 1, TOP_K * tc), lambda i, pe: (i, 0, 0), memory_space=pltpu.SMEM),
            pl.BlockSpec((tc * parts, LANES), lambda i, pe: (i, 0)),
        ],
        out_specs=pl.BlockSpec(memory_space=pl.ANY),
        scratch_shapes=[pltpu.VMEM((MOE_BLOCK * parts, LANES), F32), pltpu.SemaphoreType.DMA((2,))],
    )
    return pl.pallas_call(
        functools.partial(_dispatch_kernel, parts=parts),
        grid_spec=grid_spec,
        out_shape=jax.ShapeDtypeStruct((n_rows * parts, LANES), F32),
        compiler_params=_cparams(("arbitrary",)),
        name="dispatch",
    )(pend, dest3, hn_tm)


def _moe_kernel(be_ref, nxt_ref, nvb_ref, x_ref, wgu_hbm, bg_ref, bu_ref, wd_hbm, bd_ref, perm_ref,
                y_ref, wgu_f, wd_f, wgu_s, wd_s, act_s, sem):
    i = pl.program_id(0)
    nvb = nvb_ref[0]
    f, d = wd_f.shape
    bm = act_s.shape[0]
    pw = perm_ref.shape[0]
    half = pw // 2

    def fetch(e):
        return (pltpu.make_async_copy(wgu_hbm.at[e], wgu_f, sem.at[0]),
                pltpu.make_async_copy(wd_hbm.at[e], wd_f, sem.at[1]))

    @pl.when(jnp.logical_and(i == 0, nvb > 0))
    def _():
        for c in fetch(be_ref[0]):
            c.start()

    changed = jnp.logical_or(i == 0, be_ref[i] != be_ref[jnp.maximum(i - 1, 0)])

    @pl.when(jnp.logical_and(changed, i < nvb))
    def _():
        for c in fetch(be_ref[i]):
            c.wait()
        for c in range(2 * f // pw):
            wc = wgu_f[:, c * pw:(c + 1) * pw].astype(BF16)
            wgu_s[:, c * pw:(c + 1) * pw] = _dot(wc, perm_ref[...]).astype(BF16)
        wd_s[...] = wd_f[...].astype(BF16)

        @pl.when(nxt_ref[i] >= 0)
        def _():
            for c in fetch(nxt_ref[i]):
                c.start()

    @pl.when(i < nvb)
    def _():
        e = be_ref[i]
        xb = jnp.concatenate([p.astype(BF16) for p in _load_token_major(x_ref, bm, d)], axis=1)
        for c in range(2 * f // pw):
            gu = _dot(xb, wgu_s[:, c * pw:(c + 1) * pw])
            g = gu[:, :half] + bg_ref[e, :, c * half:(c + 1) * half]
            u = gu[:, half:] + bu_ref[e, :, c * half:(c + 1) * half]
            g = jnp.minimum(g, SWIGLU_LIMIT)
            u = jnp.clip(u, -SWIGLU_LIMIT, SWIGLU_LIMIT)
            act = (u + 1.0) * (g * _sigmoid(SWIGLU_ALPHA * g))
            act_s[:, c * half:(c + 1) * half] = act.astype(BF16)
        parts = d // LANES
        hm = bm // 2
        for r in range(2):
            yr = _dot(act_s[r * hm:(r + 1) * hm, :], wd_s[...]) + bd_ref[e]
            _store_token_major(y_ref, yr, base=r * hm * parts)

    @pl.when(i >= nvb)
    def _():
        y_ref[...] = jnp.zeros(y_ref.shape, F32)


def _moe(block_e, next_e, nvb, xrows_tm, wgu, bg, bu, wd, bd, perm):
    f, d = wd.shape[1], wd.shape[2]
    parts = d // LANES
    bm = MOE_BLOCK
    nb = xrows_tm.shape[0] // (bm * parts)

    def whole(a):
        return pl.BlockSpec(a.shape, lambda i, be, nx, nv, nd=a.ndim: (0,) * nd)

    grid_spec = pltpu.PrefetchScalarGridSpec(
        num_scalar_prefetch=3,
        grid=(nb,),
        in_specs=[
            pl.BlockSpec((bm * parts, LANES),
                         lambda i, be, nx, nv: (jnp.minimum(i, jnp.maximum(nv[0] - 1, 0)), 0)),
            pl.BlockSpec(memory_space=pl.ANY),
            whole(bg), whole(bu),
            pl.BlockSpec(memory_space=pl.ANY),
            whole(bd), whole(perm),
        ],
        out_specs=pl.BlockSpec((bm * parts, LANES), lambda i, be, nx, nv: (i, 0)),
        scratch_shapes=[
            pltpu.VMEM((d, 2 * f), F32), pltpu.VMEM((f, d), F32),
            pltpu.VMEM((d, 2 * f), BF16), pltpu.VMEM((f, d), BF16),
            pltpu.VMEM((bm, f), BF16),
            pltpu.SemaphoreType.DMA((2,)),
        ],
    )
    return pl.pallas_call(
        _moe_kernel,
        grid_spec=grid_spec,
        out_shape=jax.ShapeDtypeStruct(xrows_tm.shape, F32),
        compiler_params=_cparams(("arbitrary",)),
        name="moe",
    )(block_e, next_e, nvb, xrows_tm, wgu, bg, bu, wd, bd, perm)


def _comb_kernel(dst_ref, dstn_ref, h_ref, gate_ref, y_hbm, g_ref, o_ref, ybuf, sem, *, parts):
    i = pl.program_id(0)
    n = pl.num_programs(0)
    tc, d = h_ref.shape
    rows = TOP_K * tc
    slot = i % 2

    def tile(ref, n, count=1):
        return ref.at[pl.ds(pl.multiple_of(n * parts, parts), count * parts), :]

    def start_rows(idx_ref, s):
        def body(j, c):
            t0 = pl.multiple_of(j * ISSUE_UNROLL, ISSUE_UNROLL)
            for u in range(ISSUE_UNROLL):
                for k in range(TOP_K):
                    r = k * tc + t0 + u
                    pltpu.make_async_copy(tile(y_hbm, idx_ref[0, 0, r]), tile(ybuf.at[s], r),
                                          sem.at[s]).start(priority=k % 2)
            return c

        lax.fori_loop(0, tc // ISSUE_UNROLL, body, 0)

    @pl.when(i == 0)
    def _():
        start_rows(dst_ref, 0)

    @pl.when(i + 1 < n)
    def _():
        start_rows(dstn_ref, 1 - slot)

    pltpu.make_async_copy(tile(y_hbm, 0, rows), ybuf.at[slot], sem.at[slot]).wait()
    acc = [h_ref[:, s * LANES:(s + 1) * LANES] for s in range(parts)]
    for k in range(TOP_K):
        yk = _load_token_major(ybuf.at[slot], tc, d, base=k * tc * parts)
        gk = gate_ref[:, k:k + 1]
        acc = [a + p * gk for a, p in zip(acc, yk)]
    o_ref[...] = _rms(jnp.concatenate(acc, axis=1), g_ref[...])


def _combine(dest3, h2, gate, y_tm, g):
    t, d = h2.shape
    parts = d // LANES
    tc = COMB_TILE
    nt = t // tc
    return pl.pallas_call(
        functools.partial(_comb_kernel, parts=parts),
        grid=(nt,),
        in_specs=[
            pl.BlockSpec((1, 1, TOP_K * tc), lambda i: (i, 0, 0), memory_space=pltpu.SMEM),
            pl.BlockSpec((1, 1, TOP_K * tc), lambda i: (jnp.minimum(i + 1, nt - 1), 0, 0),
                         memory_space=pltpu.SMEM),
            pl.BlockSpec((tc, d), lambda i: (i, 0)),
            pl.BlockSpec((tc, SUBLANES), lambda i: (i, 0)),
            pl.BlockSpec(memory_space=pl.ANY),
            pl.BlockSpec((1, d), lambda i: (0, 0)),
        ],
        out_specs=pl.BlockSpec((tc, d), lambda i: (i, 0)),
        out_shape=jax.ShapeDtypeStruct((t, d), F32),
        scratch_shapes=[pltpu.VMEM((2, TOP_K * tc * parts, LANES), F32), pltpu.SemaphoreType.DMA((2,))],
        compiler_params=_cparams(("arbitrary",)),
        name="combine",
    )(dest3, dest3, h2, gate, y_tm, g)


def _block_diag(wb, per):
    nb, bw, _ = wb.shape
    wq = wb.reshape(nb // per, per, bw, bw)
    eye = jnp.eye(per, dtype=wb.dtype)
    out = jnp.einsum('qaij,ab->qaibj', wq, eye)
    return out.reshape(nb // per, per * bw, per * bw)


def _pad_lanes(v, fill=0.0):
    return jnp.pad(v, (0, LANES - v.shape[0]), constant_values=fill).reshape(1, LANES)


def kernel(x, mem, norm_mix, w_in, ssd_conv_w, ssd_conv_b, ssd_dt_bias, ssd_a_log, ssd_d, ssd_norm, lru_conv_w, lru_conv_b, lru_wa, lru_ba, lru_wx, lru_bx, lru_lambda, w_out, norm_xattn, norm_mem, w_q, w_kv, w_o, norm_moe, w_router, b_router, w_gate_up, b_gate_up, w_down, b_down, norm_final):
    b, s, d = x.shape
    t = b * s
    n_mem = mem.shape[1]
    w = d
    cdim = w + 2 * SSD_GROUPS * SSD_STATE
    o1, o2, o3, o4 = w, w + cdim, w + cdim + SSD_HEADS, w + cdim + SSD_HEADS + w

    wi = jnp.swapaxes(w_in[0], 0, 1)
    wzx = wi[:o2].astype(BF16)
    wdt = jnp.pad(wi[o2:o3], ((0, LANES - SSD_HEADS), (0, 0)))
    wdt_hi, wdt_lo = _hilo(wdt)
    wxr = wi[o3:o4].astype(BF16)
    wgr = wi[o4:].astype(BF16)
    e01 = (jnp.arange(LANES)[:, None] == (jnp.arange(w)[None, :] // SSD_HEAD_DIM)).astype(BF16)
    dskip = jnp.repeat(ssd_d[0], SSD_HEAD_DIM).reshape(1, w)
    per = MXU_DIM // (w // LRU_BLOCKS)
    wa_bd = _block_diag(lru_wa[0], per).astype(BF16)
    wx_bd = _block_diag(lru_wx[0], per).astype(BF16)
    wr = jnp.pad(w_router[0], ((0, 0), (0, LANES - N_EXPERTS)))
    wr_hi, wr_lo = _hilo(wr)
    br = _pad_lanes(b_router[0], fill=-1e30)
    bgu = b_gate_up[0]
    bg = bgu[:, None, 0::2]
    bu = bgu[:, None, 1::2]
    bd = b_down[0][:, None, :]
    pw = 2 * LANES
    col = jnp.arange(pw)
    src_col = jnp.where(col < LANES, 2 * col, 2 * (col - LANES) + 1)
    perm = (jnp.arange(pw)[:, None] == src_col[None, :]).astype(BF16)

    x2 = x.reshape(t, d)
    z, xbc, dt, xr, gr = _in_proj(x2, norm_mix[0].reshape(1, d), wzx, jnp.concatenate([wdt_hi, wdt_lo], axis=0),
                                  wdt_hi, wxr, wgr)

    y_ssd = _ssd(xbc.reshape(b, s, cdim), dt.reshape(b, s, LANES), z.reshape(b, s, w),
                 _shift_matrix(SSD_CHUNK), ssd_conv_w[0], ssd_conv_b[0].reshape(1, cdim), _pad_lanes(ssd_dt_bias[0]),
                 _pad_lanes(ssd_a_log[0]), dskip, ssd_norm[0].reshape(1, w), e01)
    y_lru = _lru(xr.reshape(b, s, w), gr.reshape(b, s, w), _shift_matrix(LRU_TILE), lru_conv_w[0], lru_conv_b[0].reshape(1, w),
                 wa_bd, lru_ba[0].reshape(1, w), wx_bd, lru_bx[0].reshape(1, w), lru_lambda[0].reshape(1, w))

    kk, vv = _kv(mem.reshape(b * n_mem, d), norm_mem[0].reshape(1, d), w_kv[0].astype(BF16))
    wo_mix = w_out[0].astype(BF16)
    h2, hn, idx_m, rank_m, gate_m, cnt = _mid(
        x2, y_ssd.reshape(t, w), y_lru.reshape(t, w), kk.reshape(b, n_mem, d), vv.reshape(b, n_mem, d),
        wo_mix[:w], wo_mix[w:], norm_xattn[0].reshape(1, d), w_q[0].astype(BF16), w_o[0].astype(BF16),
        norm_moe[0].reshape(1, d), wr_hi, wr_lo, br, s)

    bm = MOE_BLOCK
    counts = cnt[:, 0]
    pcounts = (counts + bm - 1) // bm * bm
    pend = jnp.cumsum(pcounts).astype(jnp.int32)
    pstart = pend - pcounts
    idx = idx_m[:TOP_K, :]
    onehot = idx[:, :, None] == jnp.arange(N_EXPERTS, dtype=jnp.int32)[None, None, :]
    dest = jnp.sum(jnp.where(onehot, pstart[None, None, :], 0), axis=-1) + rank_m[:TOP_K, :]
    n_pairs = t * TOP_K
    nb = (n_pairs + N_EXPERTS * (bm - 1) + bm - 1) // bm
    blk0 = jnp.arange(nb, dtype=jnp.int32) * bm
    block_e = jnp.minimum(jnp.sum(pend[None, :] <= blk0[:, None], axis=1), N_EXPERTS - 1).astype(jnp.int32)
    nvb = (pend[-1] // bm).astype(jnp.int32).reshape(1)
    experts = jnp.arange(N_EXPERTS, dtype=jnp.int32)
    later = (pcounts > 0)[None, :] & (experts[None, :] > experts[:, None])
    follower = jnp.min(jnp.where(later, experts[None, :], N_EXPERTS), axis=1)
    follower = jnp.where(follower == N_EXPERTS, -1, follower)
    next_e = jnp.sum(jnp.where(block_e[:, None] == experts[None, :], follower[None, :], 0), axis=1).astype(jnp.int32)
    dest = dest.astype(jnp.int32)

    def by_tile(tc):
        return dest.reshape(TOP_K, t // tc, tc).transpose(1, 0, 2).reshape(t // tc, 1, TOP_K * tc)

    xrows = _dispatch(pend, by_tile(DISP_TILE), hn, nb * bm, d)
    y = _moe(block_e, next_e, nvb, xrows, w_gate_up[0], bg, bu, w_down[0], bd, perm)
    out = _combine(by_tile(COMB_TILE), h2, gate_m.T, y, norm_final.reshape(1, d))
    return out.reshape(b, s, d)
```
